```python
import jax
import jax.numpy as jnp
from jax import lax
import numpy as np

D_MODEL = 2048
BATCH = 4
SEQ = 4096
DEPTH = 1
DEC_BATCH = 32
DEC_SEQ = 1
PAST_LEN = 16384
PAGE_SIZE = 128

ATT_GROUPS = ((128, 1), (512, 4), (2048, 16))
N_ATT_GROUPS = len(ATT_GROUPS)
HEADS_PER_GROUP = 4
HEAD_DIM = 128
ATT_WIDTH = N_ATT_GROUPS * HEADS_PER_GROUP * HEAD_DIM
ATT_OUT = HEADS_PER_GROUP * HEAD_DIM
D_CONV = 1536
CONV_K = 31
N_MEM = 256
X_HEADS = 4
X_HEAD_DIM = D_MODEL // X_HEADS
N_EXPERT_GROUPS = 4
EXPERTS_PER_GROUP = 4
N_EXPERTS = N_EXPERT_GROUPS * EXPERTS_PER_GROUP
TOP_K_EXPERT = 2
D_EXPERT = 512
D_IN_PROJ = 3 * ATT_WIDTH + 2 * D_CONV + 2 * D_MODEL
RMS_EPS = 1e-6
LN_EPS = 1e-5
NEG_INF = -1e30

kernel_name = "hybrid_dilated_attn_conformer_hmoe_step"


def rmsnorm(x, g):
    xf = x.astype(jnp.float32)
    y = xf * lax.rsqrt(jnp.mean(xf * xf, axis=-1, keepdims=True) + RMS_EPS)
    return (y * g.astype(jnp.float32)).astype(x.dtype)


def project_in(xn, w_in):
    z = xn @ w_in
    qkv, u2, gates = jnp.split(z, [3 * ATT_WIDTH, 3 * ATT_WIDTH + 2 * D_CONV], axis=-1)
    qkv = qkv.reshape(*xn.shape[:-1], 3, N_ATT_GROUPS, HEADS_PER_GROUP, HEAD_DIM)
    u = u2[..., :D_CONV] * jax.nn.sigmoid(u2[..., D_CONV:])
    return qkv, u, gates[..., :D_MODEL], gates[..., D_MODEL:]


def band_attention(q, k, v, dil, steps):
    B, S, H, Dh = q.shape
    L = S // dil
    blk = steps
    nb = -(-L // blk)
    Lp = nb * blk

    def sub(t):
        t = t.reshape(B, L, dil, H, Dh).transpose(0, 2, 1, 3, 4)
        t = jnp.pad(t, ((0, 0), (0, 0), (0, Lp - L), (0, 0), (0, 0)))
        return t.reshape(B, dil, nb, blk, H, Dh)

    def band(t):
        prev = jnp.pad(t[:, :, :-1], ((0, 0), (0, 0), (1, 0), (0, 0), (0, 0), (0, 0)))
        return jnp.concatenate([prev, t], axis=3)

    qs = sub(q)
    kb = band(sub(k))
    vb = band(sub(v))
    s = jnp.einsum('brnqhd,brnkhd->brnhqk', qs, kb, preferred_element_type=jnp.float32) * (HEAD_DIM ** -0.5)
    qi = jnp.arange(blk)[:, None]
    kj = jnp.arange(2 * blk)[None, :] - blk
    dist = qi - kj
    absk = jnp.arange(nb)[:, None, None] * blk + kj[None]
    valid = (dist >= 0)[None] & (dist <= steps)[None] & (absk >= 0)
    s = jnp.where(valid[None, None, :, None], s, NEG_INF)
    m = jnp.max(s, axis=-1)
    p = jnp.exp(s - m[..., None])
    den = jnp.sum(p, axis=-1)
    o = jnp.einsum('brnhqk,brnkhd->brnqhd', p, vb.astype(jnp.float32))
    o = o.reshape(B, dil, Lp, H, Dh)[:, :, :L].transpose(0, 2, 1, 3, 4).reshape(B, S, H, Dh)

    def unsub_stat(t):
        t = t.transpose(0, 1, 2, 4, 3).reshape(B, dil, Lp, H)[:, :, :L]
        return t.transpose(0, 2, 1, 3).reshape(B, S, H)

    return o, unsub_stat(m), unsub_stat(den)


def window_decode_attention(q, kv_ext, n_buf, dil, steps):
    T = q.shape[1]
    idx = n_buf + jnp.arange(T)[:, None] - dil * jnp.arange(steps + 1)[None, :]
    valid = idx >= 0
    kvg = jnp.take(kv_ext, jnp.maximum(idx, 0), axis=1)
    s = jnp.einsum('bthd,btkhd->bthk', q, kvg[:, :, :, 0], preferred_element_type=jnp.float32) * (HEAD_DIM ** -0.5)
    s = jnp.where(valid[None, :, None, :], s, NEG_INF)
    m = jnp.max(s, axis=-1)
    p = jnp.exp(s - m[..., None])
    den = jnp.sum(p, axis=-1)
    o = jnp.einsum('bthk,btkhd->bthd', p, kvg[:, :, :, 1].astype(jnp.float32))
    return o, m, den


def merge_by_denominator(parts):
    o_all = jnp.stack([p[0] for p in parts])
    m_all = jnp.stack([p[1] for p in parts])
    d_all = jnp.stack([p[2] for p in parts])
    w = jnp.exp(m_all - jnp.max(m_all, axis=0))
    num = jnp.sum(w[..., None] * o_all, axis=0)
    den = jnp.sum(w * d_all, axis=0)
    return num / den[..., None]


def causal_dwconv(u_ext, w, b):
    y = lax.conv_general_dilated(u_ext, w[:, None, :].astype(u_ext.dtype), (1,), 'VALID',
                                 dimension_numbers=('NWC', 'WIO', 'NWC'), feature_group_count=D_CONV)
    return y + b


def conv_tail(y, g, b):
    yf = y.astype(jnp.float32)
    mu = jnp.mean(yf, axis=-1, keepdims=True)
    var = jnp.mean(jnp.square(yf - mu), axis=-1, keepdims=True)
    yn = (yf - mu) * lax.rsqrt(var + LN_EPS) * g.astype(jnp.float32) + b.astype(jnp.float32)
    return jax.nn.silu(yn).astype(y.dtype)


def merge_branches(o_a, c, ga, gb, w_pa, w_pb, w_out):
    h = jax.nn.sigmoid(ga) * (o_a @ w_pa) + jax.nn.sigmoid(gb) * (c @ w_pb)
    return h @ w_out


def memory_kv(mem, g, w_xkv):
    mn = rmsnorm(mem, g)
    return (mn @ w_xkv).reshape(mem.shape[0], mem.shape[1], 2, X_HEADS, X_HEAD_DIM)


def cross_attention(xn, mkv, w_xq, w_xo):
    B, T, _ = xn.shape
    q = (xn @ w_xq).reshape(B, T, X_HEADS, X_HEAD_DIM)
    s = jnp.einsum('bthd,bmhd->bhtm', q, mkv[:, :, 0], preferred_element_type=jnp.float32) * (X_HEAD_DIM ** -0.5)
    p = jax.nn.softmax(s, axis=-1)
    o = jnp.einsum('bhtm,bmhd->bthd', p, mkv[:, :, 1].astype(jnp.float32)).astype(xn.dtype)
    return o.reshape(B, T, D_MODEL) @ w_xo


def hierarchical_moe(xn, w_rg, b_rg, w_re, b_re, w_ein, w_eout):
    shp = xn.shape
    xt = xn.reshape(-1, D_MODEL)
    lg = (xt @ w_rg).astype(jnp.float32) + b_rg.astype(jnp.float32)
    pg = jax.nn.softmax(lg, axis=-1)
    gsel = jnp.argmax(lg, axis=-1)
    p_sel = jnp.take_along_axis(pg, gsel[:, None], axis=-1)
    le = ((xt @ w_re).astype(jnp.float32) + b_re.astype(jnp.float32)).reshape(-1, N_EXPERT_GROUPS, EXPERTS_PER_GROUP)
    le_sel = jnp.take_along_axis(le, gsel[:, None, None], axis=1)[:, 0]
    tv, ti = lax.top_k(le_sel, TOP_K_EXPERT)
    pe = jax.nn.softmax(tv, axis=-1) * p_sel
    eid = gsel[:, None] * EXPERTS_PER_GROUP + ti
    gate = jnp.sum(jax.nn.one_hot(eid, N_EXPERTS, dtype=jnp.float32) * pe[..., None], axis=1)
    hu = jnp.einsum('nd,edf->nef', xt, w_ein)
    h = jax.nn.silu(hu[..., :D_EXPERT]) * hu[..., D_EXPERT:]
    h = h * gate[..., None].astype(h.dtype)
    y = jnp.einsum('nef,efd->nd', h, w_eout)
    return y.reshape(shp)


def setup_inputs(seed: int = 0) -> dict:
    key = jax.random.key(seed)
    ks = jax.random.split(key, 32)

    def nrm(k, shape, scale):
        return jax.random.normal(k, shape, jnp.float32) * scale

    def gain(k, shape):
        return 1.0 + nrm(k, shape, 0.02)

    bufs = [min(w, PAST_LEN) for (w, _) in ATT_GROUPS]
    return {
        "x_prompt": nrm(ks[0], (BATCH, SEQ, D_MODEL), 1.0),
        "x_sample": nrm(ks[1], (DEC_BATCH, DEC_SEQ, D_MODEL), 1.0),
        "mem_prompt": nrm(ks[2], (BATCH, N_MEM, D_MODEL), 1.0),
        "cache_kv_g1": nrm(ks[3], (DEPTH, DEC_BATCH, bufs[0], 2, HEADS_PER_GROUP, HEAD_DIM), 1.0),
        "cache_kv_g2": nrm(ks[4], (DEPTH, DEC_BATCH, bufs[1], 2, HEADS_PER_GROUP, HEAD_DIM), 1.0),
        "cache_kv_g3": nrm(ks[5], (DEPTH, DEC_BATCH, bufs[2], 2, HEADS_PER_GROUP, HEAD_DIM), 1.0),
        "state_conv": nrm(ks[6], (DEPTH, DEC_BATCH, CONV_K - 1, D_CONV), 0.5),
        "cache_mem_kv": nrm(ks[7], (DEPTH, DEC_BATCH, N_MEM, 2, X_HEADS, X_HEAD_DIM), 1.0),
        "norm_mix_g": gain(ks[8], (DEPTH, D_MODEL)),
        "w_in": nrm(ks[9], (DEPTH, D_MODEL, D_IN_PROJ), D_MODEL ** -0.5),
        "conv_w": nrm(ks[10], (DEPTH, CONV_K, D_CONV), CONV_K ** -0.5),
        "conv_b": nrm(ks[11], (DEPTH, D_CONV), 0.02),
        "conv_ln_g": gain(ks[12], (DEPTH, D_CONV)),
        "conv_ln_b": nrm(ks[13], (DEPTH, D_CONV), 0.02),
        "w_proj_a": nrm(ks[14], (DEPTH, ATT_OUT, D_MODEL), ATT_OUT ** -0.5),
        "w_proj_b": nrm(ks[15], (DEPTH, D_CONV, D_MODEL), D_CONV ** -0.5),
        "w_out": nrm(ks[16], (DEPTH, D_MODEL, D_MODEL), D_MODEL ** -0.5),
        "norm_xattn_g": gain(ks[17], (DEPTH, D_MODEL)),
        "norm_mem_g": gain(ks[18], (DEPTH, D_MODEL)),
        "w_xq": nrm(ks[19], (DEPTH, D_MODEL, D_MODEL), D_MODEL ** -0.5),
        "w_xkv": nrm(ks[20], (DEPTH, D_MODEL, 2 * D_MODEL), D_MODEL ** -0.5),
        "w_xo": nrm(ks[21], (DEPTH, D_MODEL, D_MODEL), D_MODEL ** -0.5),
        "norm_ffn_g": gain(ks[22], (DEPTH, D_MODEL)),
        "w_router_group": nrm(ks[23], (DEPTH, D_MODEL, N_EXPERT_GROUPS), D_MODEL ** -0.5),
        "b_router_group": nrm(ks[24], (DEPTH, N_EXPERT_GROUPS), 0.01),
        "w_router_expert": nrm(ks[25], (DEPTH, D_MODEL, N_EXPERTS), D_MODEL ** -0.5),
        "b_router_expert": nrm(ks[26], (DEPTH, N_EXPERTS), 0.01),
        "w_expert_in": nrm(ks[27], (DEPTH, N_EXPERTS, D_MODEL, 2 * D_EXPERT), D_MODEL ** -0.5),
        "w_expert_out": nrm(ks[28], (DEPTH, N_EXPERTS, D_EXPERT, D_MODEL), D_EXPERT ** -0.5),
        "norm_final_g": gain(ks[29], (D_MODEL,)),
    }


def reference(x_prompt, x_sample, mem_prompt, cache_kv_g1, cache_kv_g2, cache_kv_g3, state_conv, cache_mem_kv,
              norm_mix_g, w_in, conv_w, conv_b, conv_ln_g, conv_ln_b, w_proj_a, w_proj_b, w_out,
              norm_xattn_g, norm_mem_g, w_xq, w_xkv, w_xo, norm_ffn_g,
              w_router_group, b_router_group, w_router_expert, b_router_expert, w_expert_in, w_expert_out,
              norm_final_g):
    caches = (cache_kv_g1, cache_kv_g2, cache_kv_g3)
    xp = x_prompt
    xs = x_sample
    kv_p = [[] for _ in range(N_ATT_GROUPS)]
    kv_s = [[] for _ in range(N_ATT_GROUPS)]
    conv_p, conv_s, mem_p = [], [], []
    for l in range(DEPTH):
        xn = rmsnorm(xp, norm_mix_g[l])
        qkv, u, ga, gb = project_in(xn, w_in[l])
        parts = []
        for g, (win, dil) in enumerate(ATT_GROUPS):
            parts.append(band_attention(qkv[:, :, 0, g], qkv[:, :, 1, g], qkv[:, :, 2, g], dil, win // dil))
            kv_p[g].append(qkv[:, SEQ - min(win, SEQ):, 1:, g])
        o_a = merge_by_denominator(parts).astype(xp.dtype).reshape(BATCH, SEQ, ATT_OUT)
        u_ext = jnp.pad(u, ((0, 0), (CONV_K - 1, 0), (0, 0)))
        c = conv_tail(causal_dwconv(u_ext, conv_w[l], conv_b[l]), conv_ln_g[l], conv_ln_b[l])
        conv_p.append(u[:, SEQ - (CONV_K - 1):])
        xp = xp + merge_branches(o_a, c, ga, gb, w_proj_a[l], w_proj_b[l], w_out[l])
        mkv_p = memory_kv(mem_prompt, norm_mem_g[l], w_xkv[l])
        mem_p.append(mkv_p)
        xp = xp + cross_attention(rmsnorm(xp, norm_xattn_g[l]), mkv_p, w_xq[l], w_xo[l])
        xp = xp + hierarchical_moe(rmsnorm(xp, norm_ffn_g[l]), w_router_group[l], b_router_group[l],
                                   w_router_expert[l], b_router_expert[l], w_expert_in[l], w_expert_out[l])

        xn = rmsnorm(xs, norm_mix_g[l])
        qkv, u, ga, gb = project_in(xn, w_in[l])
        parts = []
        for g, (win, dil) in enumerate(ATT_GROUPS):
            buf = caches[g][l]
            n_buf = buf.shape[1]
            kv_ext = jnp.concatenate([buf, qkv[:, :, 1:, g]], axis=1)
            parts.append(window_decode_attention(qkv[:, :, 0, g], kv_ext, n_buf, dil, win // dil))
            n_keep = min(win, kv_ext.shape[1])
            kv_s[g].append(kv_ext[:, kv_ext.shape[1] - n_keep:])
        o_a = merge_by_denominator(parts).astype(xs.dtype).reshape(DEC_BATCH, DEC_SEQ, ATT_OUT)
        u_ext = jnp.concatenate([state_conv[l].astype(u.dtype), u], axis=1)
        c = conv_tail(causal_dwconv(u_ext, conv_w[l], conv_b[l]), conv_ln_g[l], conv_ln_b[l])
        conv_s.append(u_ext[:, u_ext.shape[1] - (CONV_K - 1):])
        xs = xs + merge_branches(o_a, c, ga, gb, w_proj_a[l], w_proj_b[l], w_out[l])
        xs = xs + cross_attention(rmsnorm(xs, norm_xattn_g[l]), cache_mem_kv[l], w_xq[l], w_xo[l])
        xs = xs + hierarchical_moe(rmsnorm(xs, norm_ffn_g[l]), w_router_group[l], b_router_group[l],
                                   w_router_expert[l], b_router_expert[l], w_expert_in[l], w_expert_out[l])

    y_prompt = rmsnorm(xp, norm_final_g)
    y_sample = rmsnorm(xs, norm_final_g)
    kv_g1_prompt = jnp.stack(kv_p[0])
    kv_g2_prompt = jnp.stack(kv_p[1])
    kv_g3_prompt = jnp.stack(kv_p[2])
    conv_prompt = jnp.stack(conv_p)
    mem_kv_prompt = jnp.stack(mem_p)
    kv_g1_sample = jnp.stack(kv_s[0])
    kv_g2_sample = jnp.stack(kv_s[1])
    kv_g3_sample = jnp.stack(kv_s[2])
    conv_sample = jnp.stack(conv_s)
    return (y_prompt, y_sample, kv_g1_prompt, kv_g2_prompt, kv_g3_prompt, conv_prompt, mem_kv_prompt,
            kv_g1_sample, kv_g2_sample, kv_g3_sample, conv_sample)
```

```python
import functools

import jax
import jax.numpy as jnp
from jax import lax
from jax.experimental import pallas as pl
from jax.experimental.pallas import tpu as pltpu

F32 = jnp.float32
BF16 = jnp.bfloat16

D_MODEL = 2048
ATT_GROUPS = ((128, 1), (512, 4), (2048, 16))
HEADS = 4
HEAD_DIM = 128
GROUP_W = HEADS * HEAD_DIM
ATT_WIDTH = len(ATT_GROUPS) * GROUP_W
D_CONV = 1536
CONV_K = 31
N_MEM = 256
X_HEADS = 4
X_HEAD_DIM = D_MODEL // X_HEADS
N_EXPERT_GROUPS = 4
EXPERTS_PER_GROUP = 4
N_EXPERTS = 16
D_EXPERT = 512
ZA_W = 3 * ATT_WIDTH + 2 * D_CONV
GATE_W = 2 * D_MODEL
RMS_EPS = 1e-6
LN_EPS = 1e-5
NEG_INF = -1e30
BAND = 128
LANES = 128
ROUTER_GROUP_LANE = N_EXPERTS

VMEM_LIMIT = 56 * 1024 * 1024


def _params(*sem):
    return pltpu.CompilerParams(dimension_semantics=sem, vmem_limit_bytes=VMEM_LIMIT)


def _rms_rows(x, g):
    ms = jnp.mean(x * x, axis=-1, keepdims=True)
    return x * lax.rsqrt(ms + RMS_EPS) * g


def _store_normed(x_ref, g_ref, xn_ref, chunk=256):
    rows = x_ref.shape[0]
    step = min(chunk, rows)
    for r0 in range(0, rows, step):
        xn_ref[r0:r0 + step, :] = _rms_rows(x_ref[r0:r0 + step, :], g_ref[...]).astype(xn_ref.dtype)


def _norm_matmul_kernel(x_ref, g_ref, w_ref, o_ref, xn_ref):
    @pl.when(pl.program_id(1) == 0)
    def _():
        _store_normed(x_ref, g_ref, xn_ref)

    o_ref[...] = jnp.dot(xn_ref[...], w_ref[...].astype(BF16),
                         preferred_element_type=F32).astype(o_ref.dtype)


def norm_matmul(x, g, w, out_dtype, bm, bn=512):
    m, k = x.shape
    n = w.shape[1]
    bm = min(bm, m)
    return pl.pallas_call(
        _norm_matmul_kernel,
        grid=(m // bm, n // bn),
        in_specs=[pl.BlockSpec((bm, k), lambda i, j: (i, 0)),
                  pl.BlockSpec((1, k), lambda i, j: (0, 0)),
                  pl.BlockSpec((k, bn), lambda i, j: (0, j))],
        out_specs=pl.BlockSpec((bm, bn), lambda i, j: (i, j)),
        out_shape=jax.ShapeDtypeStruct((m, n), out_dtype),
        scratch_shapes=[pltpu.VMEM((bm, k), BF16)],
        compiler_params=_params("parallel", "arbitrary"),
        name="norm_matmul",
    )(x, g.reshape(1, k), w)


def _in_proj_kernel(x_ref, g_ref, w_ref, za_ref, gt_ref, xn_ref, *, n_za):
    j = pl.program_id(1)

    @pl.when(j == 0)
    def _():
        _store_normed(x_ref, g_ref, xn_ref)

    r = jnp.dot(xn_ref[...], w_ref[...].astype(BF16), preferred_element_type=F32)

    @pl.when(j < n_za)
    def _():
        za_ref[...] = r

    @pl.when(j >= n_za)
    def _():
        gt_ref[...] = r


def in_proj(x, g, w, bm, bn=512):
    m, k = x.shape
    bm = min(bm, m)
    n_za = ZA_W // bn
    n_gt = GATE_W // bn
    return pl.pallas_call(
        functools.partial(_in_proj_kernel, n_za=n_za),
        grid=(m // bm, n_za + n_gt),
        in_specs=[pl.BlockSpec((bm, k), lambda i, j: (i, 0)),
                  pl.BlockSpec((1, k), lambda i, j: (0, 0)),
                  pl.BlockSpec((k, bn), lambda i, j: (0, j))],
        out_specs=[pl.BlockSpec((bm, bn), lambda i, j: (i, jnp.minimum(j, n_za - 1))),
                   pl.BlockSpec((bm, bn), lambda i, j: (i, jnp.maximum(j - n_za, 0)))],
        out_shape=[jax.ShapeDtypeStruct((m, ZA_W), F32), jax.ShapeDtypeStruct((m, GATE_W), F32)],
        scratch_shapes=[pltpu.VMEM((bm, k), BF16)],
        compiler_params=_params("parallel", "arbitrary"),
        name="in_proj",
    )(x, g.reshape(1, k), w)


def _matmul_res_kernel(x_ref, w_ref, r_ref, o_ref):
    o_ref[...] = r_ref[...] + jnp.dot(x_ref[...], w_ref[...].astype(BF16), preferred_element_type=F32)


def matmul_residual(x, w, res, bm, bn=512):
    m, k = x.shape
    n = w.shape[1]
    bm = min(bm, m)
    return pl.pallas_call(
        _matmul_res_kernel,
        grid=(m // bm, n // bn),
        in_specs=[pl.BlockSpec((bm, k), lambda i, j: (i, 0)),
                  pl.BlockSpec((k, bn), lambda i, j: (0, j)),
                  pl.BlockSpec((bm, bn), lambda i, j: (i, j))],
        out_specs=pl.BlockSpec((bm, bn), lambda i, j: (i, j)),
        out_shape=jax.ShapeDtypeStruct((m, n), F32),
        compiler_params=_params("parallel", "arbitrary"),
        name="matmul_residual",
    )(x, w, res)


def _band_attn_kernel(q_ref, kp_ref, kc_ref, vp_ref, vc_ref, o_ref, st_ref):
    has_prev = pl.program_id(2) > 0
    row = lax.broadcasted_iota(jnp.int32, (BAND, BAND), 0)
    col = lax.broadcasted_iota(jnp.int32, (BAND, BAND), 1)
    valid_prev = (col >= row) & has_prev
    valid_cur = col <= row
    lane = lax.broadcasted_iota(jnp.int32, (BAND, LANES), 1)
    scale = HEAD_DIM ** -0.5
    nt = (((1,), (1,)), ((), ()))
    stats = jnp.zeros((BAND, LANES), F32)
    for h in range(HEADS):
        hs = slice(h * HEAD_DIM, (h + 1) * HEAD_DIM)
        q = q_ref[0, :, hs].astype(BF16)
        s_p = lax.dot_general(q, kp_ref[0, :, hs].astype(BF16), nt, preferred_element_type=F32) * scale
        s_c = lax.dot_general(q, kc_ref[0, :, hs].astype(BF16), nt, preferred_element_type=F32) * scale
        s_p = jnp.where(valid_prev, s_p, NEG_INF)
        s_c = jnp.where(valid_cur, s_c, NEG_INF)
        m = jnp.maximum(jnp.max(s_p, axis=-1, keepdims=True), jnp.max(s_c, axis=-1, keepdims=True))
        p_p = jnp.exp(s_p - m)
        p_c = jnp.exp(s_c - m)
        den = jnp.sum(p_p, axis=-1, keepdims=True) + jnp.sum(p_c, axis=-1, keepdims=True)
        o = jnp.dot(p_p.astype(BF16), vp_ref[0, :, hs].astype(BF16), preferred_element_type=F32)
        o = o + jnp.dot(p_c.astype(BF16), vc_ref[0, :, hs].astype(BF16), preferred_element_type=F32)
        o_ref[0, :, hs] = o
        stats = jnp.where(lane == h, m, stats)
        stats = jnp.where(lane == HEADS + h, den, stats)
    st_ref[0] = stats


def band_attention(za, batch, seq, group):
    win, dil = ATT_GROUPS[group]
    assert win // dil == BAND
    length = seq // dil
    nb = length // BAND
    za3 = za.reshape(batch, length, dil * ZA_W)
    cpr = ZA_W // GROUP_W
    qb, kb, vb = group, 3 + group, 6 + group

    def spec(cb, prev):
        if prev:
            return pl.BlockSpec((1, BAND, GROUP_W), lambda b, r, n: (b, jnp.maximum(n - 1, 0), r * cpr + cb))
        return pl.BlockSpec((1, BAND, GROUP_W), lambda b, r, n: (b, n, r * cpr + cb))

    o, st = pl.pallas_call(
        _band_attn_kernel,
        grid=(batch, dil, nb),
        in_specs=[spec(qb, False), spec(kb, True), spec(kb, False), spec(vb, True), spec(vb, False)],
        out_specs=[pl.BlockSpec((1, BAND, GROUP_W), lambda b, r, n: (b, n, r)),
                   pl.BlockSpec((1, BAND, LANES), lambda b, r, n: (b, n, r))],
        out_shape=[jax.ShapeDtypeStruct((batch, length, dil * GROUP_W), F32),
                   jax.ShapeDtypeStruct((batch, length, dil * LANES), F32)],
        compiler_params=_params("parallel", "parallel", "arbitrary"),
        name=f"band_attention_g{group}",
    )(za3, za3, za3, za3, za3)
    return o.reshape(batch * seq, GROUP_W), st.reshape(batch * seq, LANES)


CONV_T = 256
CONV_HIST = 32
CONV_RC = 64
CONV_CC = 256


def _layernorm_silu(y, g, b):
    mu = jnp.mean(y, axis=-1, keepdims=True)
    yc = y - mu
    var = jnp.mean(yc * yc, axis=-1, keepdims=True)
    yn = yc * lax.rsqrt(var + LN_EPS) * g + b
    return yn * jax.nn.sigmoid(yn)


def _conv_kernel(a_ref, b_ref, w_ref, cb_ref, lg_ref, lb_ref, c_ref, tail_ref, ubuf, ybuf):
    t = pl.program_id(1)

    @pl.when(t == 0)
    def _():
        ubuf[0:CONV_HIST, :] = jnp.zeros((CONV_HIST, D_CONV), F32)

    @pl.when(t > 0)
    def _():
        ubuf[0:CONV_HIST, :] = ubuf[CONV_T:CONV_T + CONV_HIST, :]

    u = a_ref[...] * jax.nn.sigmoid(b_ref[...])
    ubuf[CONV_HIST:CONV_HIST + CONV_T, :] = u
    tail_ref[0] = ubuf[CONV_T:CONV_T + CONV_HIST, :]

    first = CONV_HIST - (CONV_K - 1)
    for c0 in range(0, D_CONV, CONV_CC):
        cs = slice(c0, c0 + CONV_CC)
        for r0 in range(0, CONV_T, CONV_RC):
            acc = jnp.broadcast_to(cb_ref[:, cs], (CONV_RC, CONV_CC))
            for k in range(CONV_K):
                acc = acc + w_ref[k:k + 1, cs] * ubuf[r0 + first + k:r0 + first + k + CONV_RC, cs]
            ybuf[r0:r0 + CONV_RC, cs] = acc
    c_ref[...] = _layernorm_silu(ybuf[...], lg_ref[...], lb_ref[...]).astype(c_ref.dtype)


def conformer_conv(za, batch, seq, conv_w, conv_b, ln_g, ln_b):
    nt = seq // CONV_T
    row = lambda b, t: (b * nt + t, 0)
    const = lambda b, t: (0, 0)
    c, tail = pl.pallas_call(
        _conv_kernel,
        grid=(batch, nt),
        in_specs=[pl.BlockSpec((CONV_T, D_CONV), lambda b, t: (b * nt + t, 3 * ATT_WIDTH // D_CONV)),
                  pl.BlockSpec((CONV_T, D_CONV), lambda b, t: (b * nt + t, 3 * ATT_WIDTH // D_CONV + 1)),
                  pl.BlockSpec((CONV_K, D_CONV), const),
                  pl.BlockSpec((1, D_CONV), const),
                  pl.BlockSpec((1, D_CONV), const),
                  pl.BlockSpec((1, D_CONV), const)],
        out_specs=[pl.BlockSpec((CONV_T, D_CONV), row),
                   pl.BlockSpec((1, CONV_HIST, D_CONV), lambda b, t: (b, 0, 0))],
        out_shape=[jax.ShapeDtypeStruct((batch * seq, D_CONV), BF16),
                   jax.ShapeDtypeStruct((batch, CONV_HIST, D_CONV), F32)],
        scratch_shapes=[pltpu.VMEM((CONV_HIST + CONV_T, D_CONV), F32),
                        pltpu.VMEM((CONV_T, D_CONV), F32)],
        compiler_params=_params("parallel", "arbitrary"),
        name="conformer_conv",
    )(za, za, conv_w, conv_b.reshape(1, D_CONV), ln_g.reshape(1, D_CONV), ln_b.reshape(1, D_CONV))
    return c, tail[:, CONV_HIST - (CONV_K - 1):]


def _merge_kernel(o1_ref, o2_ref, o3_ref, s1_ref, s2_ref, s3_ref, c_ref, ga_ref, gb_ref, x_ref,
                  wpa_ref, wpb_ref, wo_ref, y_ref, oa_ref):
    o_refs = (o1_ref, o2_ref, o3_ref)
    s_refs = (s1_ref, s2_ref, s3_ref)
    for h in range(HEADS):
        hs = slice(h * HEAD_DIM, (h + 1) * HEAD_DIM)
        ms = [s[:, h:h + 1] for s in s_refs]
        ds = [s[:, HEADS + h:HEADS + h + 1] for s in s_refs]
        m_all = jnp.maximum(jnp.maximum(ms[0], ms[1]), ms[2])
        ws = [jnp.exp(m - m_all) for m in ms]
        num = ws[0] * o_refs[0][:, hs] + ws[1] * o_refs[1][:, hs] + ws[2] * o_refs[2][:, hs]
        den = ws[0] * ds[0] + ws[1] * ds[1] + ws[2] * ds[2]
        oa_ref[:, hs] = (num / den).astype(BF16)
    ta = jnp.dot(oa_ref[...], wpa_ref[...], preferred_element_type=F32)
    tb = jnp.dot(c_ref[...], wpb_ref[...], preferred_element_type=F32)
    hmix = jax.nn.sigmoid(ga_ref[...]) * ta + jax.nn.sigmoid(gb_ref[...]) * tb
    y_ref[...] = x_ref[...] + jnp.dot(hmix.astype(BF16), wo_ref[...], preferred_element_type=F32)


def merge_branches(parts, c, gates, x, w_pa, w_pb, w_out, tm):
    m = x.shape[0]
    tm = min(tm, m)
    row = lambda i: (i, 0)
    const = lambda i: (0, 0)
    once = dict(pipeline_mode=pl.Buffered(1))
    (o1, s1), (o2, s2), (o3, s3) = parts
    return pl.pallas_call(
        _merge_kernel,
        grid=(m // tm,),
        in_specs=[pl.BlockSpec((tm, GROUP_W), row)] * 3 + [pl.BlockSpec((tm, LANES), row)] * 3 + [
            pl.BlockSpec((tm, D_CONV), row),
            pl.BlockSpec((tm, D_MODEL), lambda i: (i, 0)),
            pl.BlockSpec((tm, D_MODEL), lambda i: (i, 1)),
            pl.BlockSpec((tm, D_MODEL), row),
            pl.BlockSpec((GROUP_W, D_MODEL), const, **once),
            pl.BlockSpec((D_CONV, D_MODEL), const, **once),
            pl.BlockSpec((D_MODEL, D_MODEL), const, **once)],
        out_specs=pl.BlockSpec((tm, D_MODEL), row),
        out_shape=jax.ShapeDtypeStruct((m, D_MODEL), F32),
        scratch_shapes=[pltpu.VMEM((tm, GROUP_W), BF16)],
        compiler_params=_params("parallel"),
        name="merge_branches",
    )(o1, o2, o3, s1, s2, s3, c, gates, gates, x, w_pa, w_pb, w_out)


def _xattn_kernel(q_ref, kv_ref, o_ref):
    scale = X_HEAD_DIM ** -0.5
    nt = (((1,), (1,)), ((), ()))
    for h in range(X_HEADS):
        hs = slice(h * X_HEAD_DIM, (h + 1) * X_HEAD_DIM)
        vs = slice(D_MODEL + h * X_HEAD_DIM, D_MODEL + (h + 1) * X_HEAD_DIM)
        s = lax.dot_general(q_ref[:, hs], kv_ref[:, hs].astype(BF16), nt, preferred_element_type=F32) * scale
        m = jnp.max(s, axis=-1, keepdims=True)
        p = jnp.exp(s - m)
        p = p / jnp.sum(p, axis=-1, keepdims=True)
        o = jnp.dot(p.astype(BF16), kv_ref[:, vs].astype(BF16), preferred_element_type=F32)
        o_ref[:, hs] = o.astype(o_ref.dtype)


def cross_attention_prompt(q, mkv, batch, seq, tm=512):
    nt = seq // tm
    return pl.pallas_call(
        _xattn_kernel,
        grid=(batch, nt),
        in_specs=[pl.BlockSpec((tm, D_MODEL), lambda b, t: (b * nt + t, 0)),
                  pl.BlockSpec((N_MEM, 2 * D_MODEL), lambda b, t: (b, 0))],
        out_specs=pl.BlockSpec((tm, D_MODEL), lambda b, t: (b * nt + t, 0)),
        out_shape=jax.ShapeDtypeStruct((batch * seq, D_MODEL), BF16),
        compiler_params=_params("parallel", "arbitrary"),
        name="cross_attention_prompt",
    )(q, mkv)


def _xattn_decode_kernel(q_ref, kv_ref, o_ref):
    scale = X_HEAD_DIM ** -0.5
    q = q_ref[0].astype(F32)
    k = kv_ref[0, 0, :, 0]
    v = kv_ref[0, 0, :, 1]
    s = jnp.sum(k * q, axis=-1, keepdims=True) * scale
    m = jnp.max(s, axis=0, keepdims=True)
    p = jnp.exp(s - m)
    p = p / jnp.sum(p, axis=0, keepdims=True)
    o_ref[0] = jnp.sum(p * v, axis=0, keepdims=True).astype(o_ref.dtype)


def cross_attention_decode(q, cache_mem_kv):
    b = q.shape[0]
    q4 = q.reshape(b, 1, X_HEADS, X_HEAD_DIM)
    o = pl.pallas_call(
        _xattn_decode_kernel,
        grid=(b,),
        in_specs=[pl.BlockSpec((1, 1, X_HEADS, X_HEAD_DIM), lambda i: (i, 0, 0, 0)),
                  pl.BlockSpec((1, 1, N_MEM, 2, X_HEADS, X_HEAD_DIM), lambda i: (0, i, 0, 0, 0, 0))],
        out_specs=pl.BlockSpec((1, 1, X_HEADS, X_HEAD_DIM), lambda i: (i, 0, 0, 0)),
        out_shape=jax.ShapeDtypeStruct((b, 1, X_HEADS, X_HEAD_DIM), BF16),
        compiler_params=_params("parallel"),
        name="cross_attention_decode",
    )(q4, cache_mem_kv)
    return o.reshape(b, D_MODEL)


def _router_kernel(x_ref, g_ref, w_ref, b_ref, xn_ref, gate_ref):
    xn = _rms_rows(x_ref[...], g_ref[...])
    xn_ref[...] = xn.astype(BF16)
    logits = jnp.dot(xn, w_ref[...], preferred_element_type=F32, precision=lax.Precision.HIGHEST) + b_ref[...]
    lane = lax.broadcasted_iota(jnp.int32, logits.shape, 1)
    lanef = lane.astype(F32)
    big = float(LANES)
    is_group = (lane >= ROUTER_GROUP_LANE) & (lane < ROUTER_GROUP_LANE + N_EXPERT_GROUPS)
    lg = jnp.where(is_group, logits, -jnp.inf)
    mg = jnp.max(lg, axis=-1, keepdims=True)
    p_sel = 1.0 / jnp.sum(jnp.exp(lg - mg), axis=-1, keepdims=True)
    gsel = jnp.min(jnp.where(lg == mg, lanef, big), axis=-1, keepdims=True) - ROUTER_GROUP_LANE
    group_of_lane = lax.shift_right_logical(lane, 2)
    in_group = (lane < N_EXPERTS) & (group_of_lane == gsel.astype(jnp.int32))
    le = jnp.where(in_group, logits, -jnp.inf)
    v1 = jnp.max(le, axis=-1, keepdims=True)
    i1 = jnp.min(jnp.where(le == v1, lanef, big), axis=-1, keepdims=True)
    le2 = jnp.where(lanef == i1, -jnp.inf, le)
    v2 = jnp.max(le2, axis=-1, keepdims=True)
    i2 = jnp.min(jnp.where(le2 == v2, lanef, big), axis=-1, keepdims=True)
    t = jnp.exp(v2 - v1)
    tot = 1.0 + t
    gate_ref[...] = (jnp.where(lanef == i1, (1.0 / tot) * p_sel, 0.0)
                     + jnp.where(lanef == i2, (t / tot) * p_sel, 0.0))


def moe_router(x, g, w_rg, b_rg, w_re, b_re, tm):
    m = x.shape[0]
    tm = min(tm, m)
    pad = LANES - N_EXPERTS - N_EXPERT_GROUPS
    w = jnp.concatenate([w_re, w_rg, jnp.zeros((D_MODEL, pad), F32)], axis=1)
    b = jnp.concatenate([b_re, b_rg, jnp.zeros((pad,), F32)]).reshape(1, LANES)
    row = lambda i: (i, 0)
    const = lambda i: (0, 0)
    return pl.pallas_call(
        _router_kernel,
        grid=(m // tm,),
        in_specs=[pl.BlockSpec((tm, D_MODEL), row), pl.BlockSpec((1, D_MODEL), const),
                  pl.BlockSpec((D_MODEL, LANES), const), pl.BlockSpec((1, LANES), const)],
        out_specs=[pl.BlockSpec((tm, D_MODEL), row), pl.BlockSpec((tm, LANES), row)],
        out_shape=[jax.ShapeDtypeStruct((m, D_MODEL), BF16), jax.ShapeDtypeStruct((m, LANES), F32)],
        compiler_params=_params("parallel"),
        name="moe_router",
    )(x, g.reshape(1, D_MODEL), w, b)


def _moe_dense_kernel(xn_ref, gate_ref, w1_ref, w2_ref, res_ref, gf_ref, y_ref):
    e = pl.program_id(1)

    @pl.when(e == 0)
    def _():
        y_ref[...] = res_ref[...]

    hu = jnp.dot(xn_ref[...], w1_ref[0].astype(BF16), preferred_element_type=F32)
    h = jax.nn.silu(hu[:, :D_EXPERT]) * hu[:, D_EXPERT:]
    lane = lax.broadcasted_iota(jnp.int32, gate_ref.shape, 1)
    gate = jnp.sum(jnp.where(lane == e, gate_ref[...], 0.0), axis=-1, keepdims=True)
    h = h * gate
    y_ref[...] += jnp.dot(h.astype(BF16), w2_ref[0].astype(BF16), preferred_element_type=F32)

    @pl.when(e == N_EXPERTS - 1)
    def _():
        y_ref[...] = _rms_rows(y_ref[...], gf_ref[...])


def moe_experts_final(xn, gate, w_ein, w_eout, res, g_final, tm):
    m = xn.shape[0]
    tm = min(tm, m)
    row = lambda i, e: (i, 0)
    return pl.pallas_call(
        _moe_dense_kernel,
        grid=(m // tm, N_EXPERTS),
        in_specs=[pl.BlockSpec((tm, D_MODEL), row), pl.BlockSpec((tm, LANES), row),
                  pl.BlockSpec((1, D_MODEL, 2 * D_EXPERT), lambda i, e: (e, 0, 0)),
                  pl.BlockSpec((1, D_EXPERT, D_MODEL), lambda i, e: (e, 0, 0)),
                  pl.BlockSpec((tm, D_MODEL), row),
                  pl.BlockSpec((1, D_MODEL), lambda i, e: (0, 0))],
        out_specs=pl.BlockSpec((tm, D_MODEL), row),
        out_shape=jax.ShapeDtypeStruct((m, D_MODEL), F32),
        compiler_params=_params("parallel", "arbitrary"),
        name="moe_experts_final",
    )(xn, gate, w_ein, w_eout, res, g_final.reshape(1, D_MODEL))


def _window_decode_kernel(z_ref, c1_ref, c2_ref, c3_ref, o_ref, m_ref, d_ref):
    scale = HEAD_DIM ** -0.5
    for g, c_ref in enumerate((c1_ref, c2_ref, c3_ref)):
        q = z_ref[0, g:g + 1]
        k_new = z_ref[0, 3 + g:4 + g]
        v_new = z_ref[0, 6 + g:7 + g]
        k = c_ref[0, 0, :, 0, 0]
        v = c_ref[0, 0, :, 0, 1]
        s = jnp.sum(k * q, axis=-1, keepdims=True) * scale
        s_new = jnp.sum(k_new * q, axis=-1, keepdims=True) * scale
        m = jnp.maximum(jnp.max(s, axis=0, keepdims=True), s_new)
        p = jnp.exp(s - m)
        p_new = jnp.exp(s_new - m)
        den = jnp.sum(p, axis=0, keepdims=True) + p_new
        o = jnp.sum(p * v, axis=0, keepdims=True) + p_new * v_new
        o_ref[0, g:g + 1] = o
        m_ref[0, g:g + 1] = jnp.broadcast_to(m, (1, HEADS, HEAD_DIM))
        d_ref[0, g:g + 1] = jnp.broadcast_to(den, (1, HEADS, HEAD_DIM))


def window_decode(z4, caches):
    b = z4.shape[0]
    views, specs = [], []
    for cache, (win, dil) in zip(caches, ATT_GROUPS):
        n = cache.shape[2]
        assert n == win and n // dil == BAND
        views.append(cache.reshape(1, b, BAND, dil, 2, HEADS, HEAD_DIM))
        specs.append(pl.BlockSpec((1, 1, BAND, 1, 2, HEADS, HEAD_DIM), lambda i: (0, i, 0, 0, 0, 0, 0)))
    out_spec = pl.BlockSpec((1, 3, HEADS, HEAD_DIM), lambda i: (i, 0, 0, 0))
    out_shape = jax.ShapeDtypeStruct((b, 3, HEADS, HEAD_DIM), F32)
    return pl.pallas_call(
        _window_decode_kernel,
        grid=(b,),
        in_specs=[pl.BlockSpec((1,) + z4.shape[1:], lambda i: (i, 0, 0, 0))] + specs,
        out_specs=[out_spec] * 3,
        out_shape=[out_shape] * 3,
        compiler_params=_params("parallel"),
        name="window_decode",
    )(z4, *views)


N_SHIFT_CHUNKS = 4


def _shift_kernel(c1, c2, c3, n1, n2, n3, o1, o2, o3, sems):
    copies = []
    for g, (c, nw, o) in enumerate(((c1, n1, o1), (c2, n2, o2), (c3, n3, o3))):
        n = c.shape[2]
        bc = c.shape[1] // N_SHIFT_CHUNKS
        for j in range(N_SHIFT_CHUNKS):
            bs = pl.ds(j * bc, bc)
            copies.append(pltpu.make_async_copy(c.at[:, bs, pl.ds(1, n - 1)], o.at[:, bs, pl.ds(0, n - 1)],
                                                sems.at[g, j]))
        copies.append(pltpu.make_async_copy(nw, o.at[:, :, pl.ds(n - 1, 1)], sems.at[g, N_SHIFT_CHUNKS]))
    for cp in copies:
        cp.start()
    for cp in copies:
        cp.wait()


def shift_caches(caches, new_rows):
    any_spec = pl.BlockSpec(memory_space=pl.ANY)
    return pl.pallas_call(
        _shift_kernel,
        in_specs=[any_spec] * 6,
        out_specs=[any_spec] * 3,
        out_shape=[jax.ShapeDtypeStruct(c.shape, c.dtype) for c in caches],
        scratch_shapes=[pltpu.SemaphoreType.DMA((3, N_SHIFT_CHUNKS + 1))],
        name="shift_caches",
    )(*caches, *new_rows)


def _conv_step_kernel(a_ref, b_ref, s_ref, w_ref, cb_ref, lg_ref, lb_ref, c_ref, so_ref):
    hist = CONV_K - 1
    u = a_ref[...] * jax.nn.sigmoid(b_ref[...])
    y = (jnp.sum(s_ref[0] * w_ref[0:hist, :], axis=1, keepdims=True)
         + u * w_ref[hist:hist + 1, :] + cb_ref[...])
    c_ref[...] = _layernorm_silu(y, lg_ref[...], lb_ref[...]).astype(c_ref.dtype)
    so_ref[0, :, pl.ds(0, hist - 1), :] = s_ref[0, :, pl.ds(1, hist - 1), :]
    so_ref[0, :, pl.ds(hist - 1, 1), :] = u


def conv_step(a, b, state, conv_w, conv_b, ln_g, ln_b):
    bsz = a.shape[0]
    return pl.pallas_call(
        _conv_step_kernel,
        out_shape=[jax.ShapeDtypeStruct((bsz, 1, D_CONV), BF16), jax.ShapeDtypeStruct(state.shape, F32)],
        compiler_params=pltpu.CompilerParams(vmem_limit_bytes=VMEM_LIMIT),
        name="conv_step",
    )(a, b, state, conv_w, conv_b.reshape(1, D_CONV), ln_g.reshape(1, D_CONV), ln_b.reshape(1, D_CONV))


def _trunk_tail(x1, xo_in, w_xo, norm_ffn_g, w_rg, b_rg, w_re, b_re, w_ein, w_eout, norm_final_g, bm, tm):
    x2 = matmul_residual(xo_in, w_xo, x1, bm)
    xn3, gate = moe_router(x2, norm_ffn_g, w_rg, b_rg, w_re, b_re, tm)
    return moe_experts_final(xn3, gate, w_ein, w_eout, x2, norm_final_g, tm)


def kernel(x_prompt, x_sample, mem_prompt, cache_kv_g1, cache_kv_g2, cache_kv_g3, state_conv, cache_mem_kv,
           norm_mix_g, w_in, conv_w, conv_b, conv_ln_g, conv_ln_b, w_proj_a, w_proj_b, w_out,
           norm_xattn_g, norm_mem_g, w_xq, w_xkv, w_xo, norm_ffn_g,
           w_router_group, b_router_group, w_router_expert, b_router_expert, w_expert_in, w_expert_out,
           norm_final_g):
    depth = norm_mix_g.shape[0]
    assert depth == 1, "single-layer trunk"
    batch, seq, _ = x_prompt.shape
    dec_b, dec_t, _ = x_sample.shape
    assert dec_t == 1
    (g_mix, w_in, conv_w, conv_b, ln_g, ln_b, w_pa, w_pb, w_o, g_x, g_mem, w_xq, w_xkv, w_xo, g_ffn,
     w_rg, b_rg, w_re, b_re, w_ein, w_eout) = [t[0] for t in (
         norm_mix_g, w_in, conv_w, conv_b, conv_ln_g, conv_ln_b, w_proj_a, w_proj_b, w_out, norm_xattn_g,
         norm_mem_g, w_xq, w_xkv, w_xo, norm_ffn_g, w_router_group, b_router_group, w_router_expert,
         b_router_expert, w_expert_in, w_expert_out)]
    w_pa16, w_pb16, w_o16 = w_pa.astype(BF16), w_pb.astype(BF16), w_o.astype(BF16)
    w_ein, w_eout = w_ein.astype(BF16), w_eout.astype(BF16)
    caches = (cache_kv_g1, cache_kv_g2, cache_kv_g3)

    m_p = batch * seq
    xp = x_prompt.reshape(m_p, D_MODEL)
    za, gates = in_proj(xp, g_mix, w_in, bm=1024)
    parts = [band_attention(za, batch, seq, g) for g in range(len(ATT_GROUPS))]
    c, conv_prompt = conformer_conv(za, batch, seq, conv_w, conv_b, ln_g, ln_b)
    x1 = merge_branches(parts, c, gates, xp, w_pa16, w_pb16, w_o16, tm=256)
    mkv = norm_matmul(mem_prompt.reshape(batch * N_MEM, D_MODEL), g_mem, w_xkv, F32, bm=1024)
    q = norm_matmul(x1, g_x, w_xq, BF16, bm=1024)
    xo_in = cross_attention_prompt(q, mkv, batch, seq)
    y_prompt = _trunk_tail(x1, xo_in, w_xo, g_ffn, w_rg, b_rg, w_re, b_re, w_ein, w_eout, norm_final_g,
                           bm=1024, tm=512)

    za3 = za.reshape(batch, seq, ZA_W)
    kv_prompt = []
    for g, (win, _) in enumerate(ATT_GROUPS):
        keep = min(win, seq)
        k = za3[:, seq - keep:, ATT_WIDTH + g * GROUP_W:ATT_WIDTH + (g + 1) * GROUP_W]
        v = za3[:, seq - keep:, 2 * ATT_WIDTH + g * GROUP_W:2 * ATT_WIDTH + (g + 1) * GROUP_W]
        kv_prompt.append(jnp.stack([k, v], axis=2).reshape(1, batch, keep, 2, HEADS, HEAD_DIM))
    conv_prompt = conv_prompt[None]
    mem_kv_prompt = mkv.reshape(1, batch, N_MEM, 2, X_HEADS, X_HEAD_DIM)

    xs = x_sample.reshape(dec_b, D_MODEL)
    zs, gates_s = in_proj(xs, g_mix, w_in, bm=dec_b)
    z4 = zs.reshape(dec_b, ZA_W // HEAD_DIM // HEADS, HEADS, HEAD_DIM)
    o_s, m_s, d_s = window_decode(z4, caches)
    zero = jnp.zeros((dec_b, LANES - 2 * HEADS), F32)
    parts_s = [(o_s[:, g].reshape(dec_b, GROUP_W),
                jnp.concatenate([m_s[:, g, :, 0], d_s[:, g, :, 0], zero], axis=1)) for g in range(3)]
    new_rows = [jnp.stack([z4[:, 3 + g], z4[:, 6 + g]], axis=1).reshape(1, dec_b, 1, 2, HEADS, HEAD_DIM)
                for g in range(3)]
    kv_sample = shift_caches(caches, new_rows)
    a_s = zs[:, 3 * ATT_WIDTH:3 * ATT_WIDTH + D_CONV].reshape(dec_b, 1, D_CONV)
    b_s = zs[:, 3 * ATT_WIDTH + D_CONV:].reshape(dec_b, 1, D_CONV)
    c_s, conv_sample = conv_step(a_s, b_s, state_conv, conv_w, conv_b, ln_g, ln_b)
    x1s = merge_branches(parts_s, c_s.reshape(dec_b, D_CONV), gates_s, xs, w_pa16, w_pb16, w_o16, tm=dec_b)
    q_s = norm_matmul(x1s, g_x, w_xq, BF16, bm=dec_b)
    xo_s = cross_attention_decode(q_s, cache_mem_kv)
    y_sample = _trunk_tail(x1s, xo_s, w_xo, g_ffn, w_rg, b_rg, w_re, b_re, w_ein, w_eout, norm_final_g,
                           bm=dec_b, tm=dec_b)

    return (y_prompt.reshape(batch, seq, D_MODEL), y_sample.reshape(dec_b, 1, D_MODEL),
            kv_prompt[0], kv_prompt[1], kv_prompt[2], conv_prompt, mem_kv_prompt,
            kv_sample[0], kv_sample[1], kv_sample[2], conv_sample)
```

```python
import functools

import jax
import jax.numpy as jnp
from jax import lax
from jax.experimental import pallas as pl
from jax.experimental.pallas import tpu as pltpu

F32 = jnp.float32
BF16 = jnp.bfloat16

D_MODEL = 2048
ATT_GROUPS = ((128, 1), (512, 4), (2048, 16))
HEADS = 4
HEAD_DIM = 128
GROUP_W = HEADS * HEAD_DIM
ATT_WIDTH = len(ATT_GROUPS) * GROUP_W
D_CONV = 1536
CONV_K = 31
N_MEM = 256
X_HEADS = 4
X_HEAD_DIM = D_MODEL // X_HEADS
N_EXPERT_GROUPS = 4
EXPERTS_PER_GROUP = 4
N_EXPERTS = 16
D_EXPERT = 512
ZA_W = 3 * ATT_WIDTH + 2 * D_CONV
GATE_W = 2 * D_MODEL
RMS_EPS = 1e-6
LN_EPS = 1e-5
NEG_INF = -1e30
BAND = 128
LANES = 128
ROUTER_GROUP_LANE = N_EXPERTS

VMEM_LIMIT = 56 * 1024 * 1024


def _params(*sem):
    return pltpu.CompilerParams(dimension_semantics=sem, vmem_limit_bytes=VMEM_LIMIT)


def _rms_rows(x, g):
    ms = jnp.mean(x * x, axis=-1, keepdims=True)
    return x * lax.rsqrt(ms + RMS_EPS) * g


def _store_normed(x_ref, g_ref, xn_ref, chunk=256):
    rows = x_ref.shape[0]
    step = min(chunk, rows)
    for r0 in range(0, rows, step):
        xn_ref[r0:r0 + step, :] = _rms_rows(x_ref[r0:r0 + step, :], g_ref[...]).astype(xn_ref.dtype)


def _norm_matmul_kernel(x_ref, g_ref, w_ref, o_ref, xn_ref):
    @pl.when(pl.program_id(1) == 0)
    def _():
        _store_normed(x_ref, g_ref, xn_ref)

    o_ref[...] = jnp.dot(xn_ref[...], w_ref[...].astype(BF16),
                         preferred_element_type=F32).astype(o_ref.dtype)


def norm_matmul(x, g, w, out_dtype, bm, bn=512):
    m, k = x.shape
    n = w.shape[1]
    bm = min(bm, m)
    return pl.pallas_call(
        _norm_matmul_kernel,
        grid=(m // bm, n // bn),
        in_specs=[pl.BlockSpec((bm, k), lambda i, j: (i, 0)),
                  pl.BlockSpec((1, k), lambda i, j: (0, 0)),
                  pl.BlockSpec((k, bn), lambda i, j: (0, j))],
        out_specs=pl.BlockSpec((bm, bn), lambda i, j: (i, j)),
        out_shape=jax.ShapeDtypeStruct((m, n), out_dtype),
        scratch_shapes=[pltpu.VMEM((bm, k), BF16)],
        compiler_params=_params("parallel", "arbitrary"),
        name="norm_matmul",
    )(x, g.reshape(1, k), w)


def _in_proj_kernel(x_ref, g_ref, w_ref, za_ref, gt_ref, xn_ref, *, n_za):
    j = pl.program_id(1)

    @pl.when(j == 0)
    def _():
        _store_normed(x_ref, g_ref, xn_ref)

    r = jnp.dot(xn_ref[...], w_ref[...].astype(BF16), preferred_element_type=F32)

    @pl.when(j < n_za)
    def _():
        za_ref[...] = r

    @pl.when(j >= n_za)
    def _():
        gt_ref[...] = r


def in_proj(x, g, w, bm, bn=512):
    m, k = x.shape
    bm = min(bm, m)
    n_za = ZA_W // bn
    n_gt = GATE_W // bn
    return pl.pallas_call(
        functools.partial(_in_proj_kernel, n_za=n_za),
        grid=(m // bm, n_za + n_gt),
        in_specs=[pl.BlockSpec((bm, k), lambda i, j: (i, 0)),
                  pl.BlockSpec((1, k), lambda i, j: (0, 0)),
                  pl.BlockSpec((k, bn), lambda i, j: (0, j))],
        out_specs=[pl.BlockSpec((bm, bn), lambda i, j: (i, jnp.minimum(j, n_za - 1))),
                   pl.BlockSpec((bm, bn), lambda i, j: (i, jnp.maximum(j - n_za, 0)))],
        out_shape=[jax.ShapeDtypeStruct((m, ZA_W), F32), jax.ShapeDtypeStruct((m, GATE_W), F32)],
        scratch_shapes=[pltpu.VMEM((bm, k), BF16)],
        compiler_params=_params("parallel", "arbitrary"),
        name="in_proj",
    )(x, g.reshape(1, k), w)


def _matmul_res_kernel(x_ref, w_ref, r_ref, o_ref):
    o_ref[...] = r_ref[...] + jnp.dot(x_ref[...], w_ref[...].astype(BF16), preferred_element_type=F32)


def matmul_residual(x, w, res, bm, bn=512):
    m, k = x.shape
    n = w.shape[1]
    bm = min(bm, m)
    return pl.pallas_call(
        _matmul_res_kernel,
        grid=(m // bm, n // bn),
        in_specs=[pl.BlockSpec((bm, k), lambda i, j: (i, 0)),
                  pl.BlockSpec((k, bn), lambda i, j: (0, j)),
                  pl.BlockSpec((bm, bn), lambda i, j: (i, j))],
        out_specs=pl.BlockSpec((bm, bn), lambda i, j: (i, j)),
        out_shape=jax.ShapeDtypeStruct((m, n), F32),
        compiler_params=_params("parallel", "arbitrary"),
        name="matmul_residual",
    )(x, w, res)


ATT_R = 2048


def _band_attn_kernel(*refs):
    n_g = len(ATT_GROUPS)
    in_refs = refs[:5 * n_g]
    o_ref = refs[5 * n_g]
    scr = refs[5 * n_g + 1:]
    kbufs, vbufs = scr[:n_g], scr[n_g:2 * n_g]
    obufs, mbufs, dbufs = scr[2 * n_g:3 * n_g], scr[3 * n_g:4 * n_g], scr[4 * n_g:5 * n_g]
    first_chunk = pl.program_id(1) == 0
    row = lax.broadcasted_iota(jnp.int32, (BAND, 2 * BAND), 0)
    col = lax.broadcasted_iota(jnp.int32, (BAND, 2 * BAND), 1)
    in_band = (col >= row) & (col <= row + BAND)
    in_cur = col >= BAND
    scale = HEAD_DIM ** -0.5
    nt = (((1,), (1,)), ((), ()))

    for g, (win, dil) in enumerate(ATT_GROUPS):
        q_ref, kc_ref, vc_ref, kp_ref, vp_ref = in_refs[5 * g:5 * g + 5]
        kbuf, vbuf, obuf, mbuf, dbuf = kbufs[g], vbufs[g], obufs[g], mbufs[g], dbufs[g]
        kbuf[0:win, :] = kp_ref[...]
        kbuf[win:win + ATT_R, :] = kc_ref[...]
        vbuf[0:win, :] = vp_ref[...]
        vbuf[win:win + ATT_R, :] = vc_ref[...]
        shift = dil.bit_length() - 1

        def sub_block(i, carry, q_ref=q_ref, kbuf=kbuf, vbuf=vbuf, obuf=obuf, mbuf=mbuf, dbuf=dbuf,
                      win=win, dil=dil, shift=shift):
            span = lax.shift_right_logical(i, shift)
            base = span * win + (i & (dil - 1))
            if dil == 1:
                q_rows, kv_rows = pl.ds(base, BAND), pl.ds(base, 2 * BAND)
            else:
                q_rows, kv_rows = pl.ds(base, BAND, stride=dil), pl.ds(base, 2 * BAND, stride=dil)
            q = q_ref[q_rows, :].astype(BF16)
            k = kbuf[kv_rows, :].astype(BF16)
            v = vbuf[kv_rows, :].astype(BF16)
            s = lax.dot_general(q, k, nt, preferred_element_type=F32) * scale
            has_prev = jnp.logical_not(first_chunk & (span == 0))
            s = jnp.where(in_band & (in_cur | has_prev), s, NEG_INF)
            m = jnp.max(s, axis=-1, keepdims=True)
            p = jnp.exp(s - m)
            den = jnp.sum(p, axis=-1, keepdims=True)
            obuf[q_rows, :] = jnp.dot(p.astype(BF16), v, preferred_element_type=F32)
            mbuf[q_rows, :] = jnp.broadcast_to(m, (BAND, HEAD_DIM))
            dbuf[q_rows, :] = jnp.broadcast_to(den, (BAND, HEAD_DIM))
            return carry

        lax.fori_loop(0, ATT_R // BAND, sub_block, 0)

    chunk = 256
    for r0 in range(0, ATT_R, chunk):
        rs = slice(r0, r0 + chunk)
        ms = [mb[rs, :] for mb in mbufs]
        m_all = functools.reduce(jnp.maximum, ms)
        ws = [jnp.exp(m - m_all) for m in ms]
        num = sum(w * ob[rs, :] for w, ob in zip(ws, obufs))
        den = sum(w * db[rs, :] for w, db in zip(ws, dbufs))
        o_ref[rs, :] = (num / den).astype(o_ref.dtype)


def band_attention(za, batch, seq):
    assert seq % ATT_R == 0
    nch = seq // ATT_R
    in_specs, scratch = [], []
    for g, (win, dil) in enumerate(ATT_GROUPS):
        assert win // dil == BAND and ATT_R % win == 0
        cols = [(part * len(ATT_GROUPS) + g) * HEADS for part in range(3)]
        cur = lambda c: pl.BlockSpec((ATT_R, HEAD_DIM), lambda b, ch, h, c=c: (b * nch + ch, c + h))
        prev = lambda c, win=win: pl.BlockSpec(
            (win, HEAD_DIM), lambda b, ch, h, c=c, win=win: (jnp.maximum((b * seq + ch * ATT_R) // win - 1, 0), c + h))
        in_specs += [cur(cols[0]), cur(cols[1]), cur(cols[2]), prev(cols[1]), prev(cols[2])]
    for _ in range(2):
        scratch += [pltpu.VMEM((win + ATT_R, HEAD_DIM), F32) for win, _ in ATT_GROUPS]
    scratch += [pltpu.VMEM((ATT_R, HEAD_DIM), F32)] * (3 * len(ATT_GROUPS))
    return pl.pallas_call(
        _band_attn_kernel,
        grid=(batch, nch, HEADS),
        in_specs=in_specs,
        out_specs=pl.BlockSpec((ATT_R, HEAD_DIM), lambda b, ch, h: (b * nch + ch, h)),
        out_shape=jax.ShapeDtypeStruct((batch * seq, GROUP_W), BF16),
        scratch_shapes=scratch,
        compiler_params=_params("parallel", "parallel", "parallel"),
        name="band_attention",
    )(*([za] * (5 * len(ATT_GROUPS))))


CONV_T = 256
CONV_HIST = 32
CONV_RC = 64
CONV_CC = 256


def _layernorm_silu(y, g, b):
    mu = jnp.mean(y, axis=-1, keepdims=True)
    yc = y - mu
    var = jnp.mean(yc * yc, axis=-1, keepdims=True)
    yn = yc * lax.rsqrt(var + LN_EPS) * g + b
    return yn * jax.nn.sigmoid(yn)


def _conv_kernel(a_ref, b_ref, w_ref, cb_ref, lg_ref, lb_ref, c_ref, tail_ref, ubuf, ybuf):
    t = pl.program_id(1)

    @pl.when(t == 0)
    def _():
        ubuf[0:CONV_HIST, :] = jnp.zeros((CONV_HIST, D_CONV), F32)

    @pl.when(t > 0)
    def _():
        ubuf[0:CONV_HIST, :] = ubuf[CONV_T:CONV_T + CONV_HIST, :]

    u = a_ref[...] * jax.nn.sigmoid(b_ref[...])
    ubuf[CONV_HIST:CONV_HIST + CONV_T, :] = u
    tail_ref[0] = ubuf[CONV_T:CONV_T + CONV_HIST, :]

    first = CONV_HIST - (CONV_K - 1)
    for c0 in range(0, D_CONV, CONV_CC):
        cs = slice(c0, c0 + CONV_CC)
        for r0 in range(0, CONV_T, CONV_RC):
            acc = jnp.broadcast_to(cb_ref[:, cs], (CONV_RC, CONV_CC))
            for k in range(CONV_K):
                acc = acc + w_ref[k:k + 1, cs] * ubuf[r0 + first + k:r0 + first + k + CONV_RC, cs]
            ybuf[r0:r0 + CONV_RC, cs] = acc
    c_ref[...] = _layernorm_silu(ybuf[...], lg_ref[...], lb_ref[...]).astype(c_ref.dtype)


def conformer_conv(za, batch, seq, conv_w, conv_b, ln_g, ln_b):
    nt = seq // CONV_T
    row = lambda b, t: (b * nt + t, 0)
    const = lambda b, t: (0, 0)
    c, tail = pl.pallas_call(
        _conv_kernel,
        grid=(batch, nt),
        in_specs=[pl.BlockSpec((CONV_T, D_CONV), lambda b, t: (b * nt + t, 3 * ATT_WIDTH // D_CONV)),
                  pl.BlockSpec((CONV_T, D_CONV), lambda b, t: (b * nt + t, 3 * ATT_WIDTH // D_CONV + 1)),
                  pl.BlockSpec((CONV_K, D_CONV), const),
                  pl.BlockSpec((1, D_CONV), const),
                  pl.BlockSpec((1, D_CONV), const),
                  pl.BlockSpec((1, D_CONV), const)],
        out_specs=[pl.BlockSpec((CONV_T, D_CONV), row),
                   pl.BlockSpec((1, CONV_HIST, D_CONV), lambda b, t: (b, 0, 0))],
        out_shape=[jax.ShapeDtypeStruct((batch * seq, D_CONV), BF16),
                   jax.ShapeDtypeStruct((batch, CONV_HIST, D_CONV), F32)],
        scratch_shapes=[pltpu.VMEM((CONV_HIST + CONV_T, D_CONV), F32),
                        pltpu.VMEM((CONV_T, D_CONV), F32)],
        compiler_params=_params("parallel", "arbitrary"),
        name="conformer_conv",
    )(za, za, conv_w, conv_b.reshape(1, D_CONV), ln_g.reshape(1, D_CONV), ln_b.reshape(1, D_CONV))
    return c, tail[:, CONV_HIST - (CONV_K - 1):]


def _merge_kernel(oa_ref, c_ref, ga_ref, gb_ref, x_ref, wpa_ref, wpb_ref, wo_ref, y_ref):
    ta = jnp.dot(oa_ref[...], wpa_ref[...], preferred_element_type=F32)
    tb = jnp.dot(c_ref[...], wpb_ref[...], preferred_element_type=F32)
    hmix = jax.nn.sigmoid(ga_ref[...]) * ta + jax.nn.sigmoid(gb_ref[...]) * tb
    y_ref[...] = x_ref[...] + jnp.dot(hmix.astype(BF16), wo_ref[...], preferred_element_type=F32)


def merge_branches(o_a, c, gates, x, w_pa, w_pb, w_out, tm):
    m = x.shape[0]
    tm = min(tm, m)
    row = lambda i: (i, 0)
    const = lambda i: (0, 0)
    once = dict(pipeline_mode=pl.Buffered(1))
    return pl.pallas_call(
        _merge_kernel,
        grid=(m // tm,),
        in_specs=[
            pl.BlockSpec((tm, GROUP_W), row),
            pl.BlockSpec((tm, D_CONV), row),
            pl.BlockSpec((tm, D_MODEL), lambda i: (i, 0)),
            pl.BlockSpec((tm, D_MODEL), lambda i: (i, 1)),
            pl.BlockSpec((tm, D_MODEL), row),
            pl.BlockSpec((GROUP_W, D_MODEL), const, **once),
            pl.BlockSpec((D_CONV, D_MODEL), const, **once),
            pl.BlockSpec((D_MODEL, D_MODEL), const, **once)],
        out_specs=pl.BlockSpec((tm, D_MODEL), row),
        out_shape=jax.ShapeDtypeStruct((m, D_MODEL), F32),
        compiler_params=_params("parallel"),
        name="merge_branches",
    )(o_a, c, gates, gates, x, w_pa, w_pb, w_out)


def _xattn_kernel(q_ref, kv_ref, o_ref):
    scale = X_HEAD_DIM ** -0.5
    nt = (((1,), (1,)), ((), ()))
    for h in range(X_HEADS):
        hs = slice(h * X_HEAD_DIM, (h + 1) * X_HEAD_DIM)
        vs = slice(D_MODEL + h * X_HEAD_DIM, D_MODEL + (h + 1) * X_HEAD_DIM)
        s = lax.dot_general(q_ref[:, hs], kv_ref[:, hs].astype(BF16), nt, preferred_element_type=F32) * scale
        m = jnp.max(s, axis=-1, keepdims=True)
        p = jnp.exp(s - m)
        p = p / jnp.sum(p, axis=-1, keepdims=True)
        o = jnp.dot(p.astype(BF16), kv_ref[:, vs].astype(BF16), preferred_element_type=F32)
        o_ref[:, hs] = o.astype(o_ref.dtype)


def cross_attention_prompt(q, mkv, batch, seq, tm=512):
    nt = seq // tm
    return pl.pallas_call(
        _xattn_kernel,
        grid=(batch, nt),
        in_specs=[pl.BlockSpec((tm, D_MODEL), lambda b, t: (b * nt + t, 0)),
                  pl.BlockSpec((N_MEM, 2 * D_MODEL), lambda b, t: (b, 0))],
        out_specs=pl.BlockSpec((tm, D_MODEL), lambda b, t: (b * nt + t, 0)),
        out_shape=jax.ShapeDtypeStruct((batch * seq, D_MODEL), BF16),
        compiler_params=_params("parallel", "arbitrary"),
        name="cross_attention_prompt",
    )(q, mkv)


def _xattn_decode_kernel(q_ref, kv_ref, o_ref):
    scale = X_HEAD_DIM ** -0.5
    q = q_ref[0].astype(F32)
    k = kv_ref[0, 0, :, 0]
    v = kv_ref[0, 0, :, 1]
    s = jnp.sum(k * q, axis=-1, keepdims=True) * scale
    m = jnp.max(s, axis=0, keepdims=True)
    p = jnp.exp(s - m)
    p = p / jnp.sum(p, axis=0, keepdims=True)
    o_ref[0] = jnp.sum(p * v, axis=0, keepdims=True).astype(o_ref.dtype)


def cross_attention_decode(q, cache_mem_kv):
    b = q.shape[0]
    q4 = q.reshape(b, 1, X_HEADS, X_HEAD_DIM)
    o = pl.pallas_call(
        _xattn_decode_kernel,
        grid=(b,),
        in_specs=[pl.BlockSpec((1, 1, X_HEADS, X_HEAD_DIM), lambda i: (i, 0, 0, 0)),
                  pl.BlockSpec((1, 1, N_MEM, 2, X_HEADS, X_HEAD_DIM), lambda i: (0, i, 0, 0, 0, 0))],
        out_specs=pl.BlockSpec((1, 1, X_HEADS, X_HEAD_DIM), lambda i: (i, 0, 0, 0)),
        out_shape=jax.ShapeDtypeStruct((b, 1, X_HEADS, X_HEAD_DIM), BF16),
        compiler_params=_params("parallel"),
        name="cross_attention_decode",
    )(q4, cache_mem_kv)
    return o.reshape(b, D_MODEL)


def _router_kernel(x_ref, g_ref, w_ref, b_ref, xn_ref, gate_ref):
    xn = _rms_rows(x_ref[...], g_ref[...])
    xn_ref[...] = xn.astype(BF16)
    logits = jnp.dot(xn, w_ref[...], preferred_element_type=F32, precision=lax.Precision.HIGHEST) + b_ref[...]
    lane = lax.broadcasted_iota(jnp.int32, logits.shape, 1)
    lanef = lane.astype(F32)
    big = float(LANES)
    is_group = (lane >= ROUTER_GROUP_LANE) & (lane < ROUTER_GROUP_LANE + N_EXPERT_GROUPS)
    lg = jnp.where(is_group, logits, -jnp.inf)
    mg = jnp.max(lg, axis=-1, keepdims=True)
    p_sel = 1.0 / jnp.sum(jnp.exp(lg - mg), axis=-1, keepdims=True)
    gsel = jnp.min(jnp.where(lg == mg, lanef, big), axis=-1, keepdims=True) - ROUTER_GROUP_LANE
    group_of_lane = lax.shift_right_logical(lane, 2)
    in_group = (lane < N_EXPERTS) & (group_of_lane == gsel.astype(jnp.int32))
    le = jnp.where(in_group, logits, -jnp.inf)
    v1 = jnp.max(le, axis=-1, keepdims=True)
    i1 = jnp.min(jnp.where(le == v1, lanef, big), axis=-1, keepdims=True)
    le2 = jnp.where(lanef == i1, -jnp.inf, le)
    v2 = jnp.max(le2, axis=-1, keepdims=True)
    i2 = jnp.min(jnp.where(le2 == v2, lanef, big), axis=-1, keepdims=True)
    t = jnp.exp(v2 - v1)
    tot = 1.0 + t
    gate_ref[...] = (jnp.where(lanef == i1, (1.0 / tot) * p_sel, 0.0)
                     + jnp.where(lanef == i2, (t / tot) * p_sel, 0.0))


def moe_router(x, g, w_rg, b_rg, w_re, b_re, tm):
    m = x.shape[0]
    tm = min(tm, m)
    pad = LANES - N_EXPERTS - N_EXPERT_GROUPS
    w = jnp.concatenate([w_re, w_rg, jnp.zeros((D_MODEL, pad), F32)], axis=1)
    b = jnp.concatenate([b_re, b_rg, jnp.zeros((pad,), F32)]).reshape(1, LANES)
    row = lambda i: (i, 0)
    const = lambda i: (0, 0)
    return pl.pallas_call(
        _router_kernel,
        grid=(m // tm,),
        in_specs=[pl.BlockSpec((tm, D_MODEL), row), pl.BlockSpec((1, D_MODEL), const),
                  pl.BlockSpec((D_MODEL, LANES), const), pl.BlockSpec((1, LANES), const)],
        out_specs=[pl.BlockSpec((tm, D_MODEL), row), pl.BlockSpec((tm, LANES), row)],
        out_shape=[jax.ShapeDtypeStruct((m, D_MODEL), BF16), jax.ShapeDtypeStruct((m, LANES), F32)],
        compiler_params=_params("parallel"),
        name="moe_router",
    )(x, g.reshape(1, D_MODEL), w, b)


def _moe_dense_kernel(xn_ref, gate_ref, w1_ref, w2_ref, res_ref, gf_ref, y_ref):
    e = pl.program_id(1)

    @pl.when(e == 0)
    def _():
        y_ref[...] = res_ref[...]

    hu = jnp.dot(xn_ref[...], w1_ref[0].astype(BF16), preferred_element_type=F32)
    h = jax.nn.silu(hu[:, :D_EXPERT]) * hu[:, D_EXPERT:]
    lane = lax.broadcasted_iota(jnp.int32, gate_ref.shape, 1)
    gate = jnp.sum(jnp.where(lane == e, gate_ref[...], 0.0), axis=-1, keepdims=True)
    h = h * gate
    y_ref[...] += jnp.dot(h.astype(BF16), w2_ref[0].astype(BF16), preferred_element_type=F32)

    @pl.when(e == N_EXPERTS - 1)
    def _():
        y_ref[...] = _rms_rows(y_ref[...], gf_ref[...])


def moe_experts_final(xn, gate, w_ein, w_eout, res, g_final, tm):
    m = xn.shape[0]
    tm = min(tm, m)
    row = lambda i, e: (i, 0)
    return pl.pallas_call(
        _moe_dense_kernel,
        grid=(m // tm, N_EXPERTS),
        in_specs=[pl.BlockSpec((tm, D_MODEL), row), pl.BlockSpec((tm, LANES), row),
                  pl.BlockSpec((1, D_MODEL, 2 * D_EXPERT), lambda i, e: (e, 0, 0)),
                  pl.BlockSpec((1, D_EXPERT, D_MODEL), lambda i, e: (e, 0, 0)),
                  pl.BlockSpec((tm, D_MODEL), row),
                  pl.BlockSpec((1, D_MODEL), lambda i, e: (0, 0))],
        out_specs=pl.BlockSpec((tm, D_MODEL), row),
        out_shape=jax.ShapeDtypeStruct((m, D_MODEL), F32),
        compiler_params=_params("parallel", "arbitrary"),
        name="moe_experts_final",
    )(xn, gate, w_ein, w_eout, res, g_final.reshape(1, D_MODEL))


def _window_decode_kernel(z_ref, c1_ref, c2_ref, c3_ref, o_ref):
    scale = HEAD_DIM ** -0.5
    os, ms, ds = [], [], []
    for g, c_ref in enumerate((c1_ref, c2_ref, c3_ref)):
        q = z_ref[0, g:g + 1]
        k_new = z_ref[0, 3 + g:4 + g]
        v_new = z_ref[0, 6 + g:7 + g]
        k = c_ref[0, 0, :, 0, 0]
        v = c_ref[0, 0, :, 0, 1]
        s = jnp.sum(k * q, axis=-1, keepdims=True) * scale
        s_new = jnp.sum(k_new * q, axis=-1, keepdims=True) * scale
        m = jnp.maximum(jnp.max(s, axis=0, keepdims=True), s_new)
        p = jnp.exp(s - m)
        p_new = jnp.exp(s_new - m)
        den = jnp.sum(p, axis=0, keepdims=True) + p_new
        os.append(jnp.sum(p * v, axis=0, keepdims=True) + p_new * v_new)
        ms.append(m)
        ds.append(den)
    m_all = functools.reduce(jnp.maximum, ms)
    ws = [jnp.exp(m - m_all) for m in ms]
    num = sum(w * o for w, o in zip(ws, os))
    den = sum(w * d for w, d in zip(ws, ds))
    o_ref[0] = (num / den).astype(o_ref.dtype)


def window_decode(z4, caches):
    b = z4.shape[0]
    views, specs = [], []
    for cache, (win, dil) in zip(caches, ATT_GROUPS):
        n = cache.shape[2]
        assert n == win and n // dil == BAND
        views.append(cache.reshape(1, b, BAND, dil, 2, HEADS, HEAD_DIM))
        specs.append(pl.BlockSpec((1, 1, BAND, 1, 2, HEADS, HEAD_DIM), lambda i: (0, i, 0, 0, 0, 0, 0)))
    return pl.pallas_call(
        _window_decode_kernel,
        grid=(b,),
        in_specs=[pl.BlockSpec((1,) + z4.shape[1:], lambda i: (i, 0, 0, 0))] + specs,
        out_specs=pl.BlockSpec((1, 1, HEADS, HEAD_DIM), lambda i: (i, 0, 0, 0)),
        out_shape=jax.ShapeDtypeStruct((b, 1, HEADS, HEAD_DIM), BF16),
        compiler_params=_params("parallel"),
        name="window_decode",
    )(z4, *views)


SHIFT_ROWS = 64


def _shift_kernel(z_ref, c1_ref, c2_ref, c3_ref, o1_ref, o2_ref, o3_ref):
    for g, (c_ref, o_ref) in enumerate(((c1_ref, o1_ref), (c2_ref, o2_ref), (c3_ref, o3_ref))):
        n = c_ref.shape[2]
        full, rem = divmod(n - 1, SHIFT_ROWS)

        def move(j, carry, c_ref=c_ref, o_ref=o_ref):
            o_ref[0, 0, pl.ds(j * SHIFT_ROWS, SHIFT_ROWS)] = c_ref[0, 0, pl.ds(j * SHIFT_ROWS + 1, SHIFT_ROWS)]
            return carry

        lax.fori_loop(0, full, move, 0)
        if rem:
            o_ref[0, 0, pl.ds(full * SHIFT_ROWS, rem)] = c_ref[0, 0, pl.ds(full * SHIFT_ROWS + 1, rem)]
        o_ref[0, 0, n - 1, 0] = z_ref[0, 3 + g]
        o_ref[0, 0, n - 1, 1] = z_ref[0, 6 + g]


def shift_caches(z4, caches):
    b = z4.shape[0]
    specs = [pl.BlockSpec((1, 1) + c.shape[2:], lambda i: (0, i, 0, 0, 0, 0)) for c in caches]
    return pl.pallas_call(
        _shift_kernel,
        grid=(b,),
        in_specs=[pl.BlockSpec((1,) + z4.shape[1:], lambda i: (i, 0, 0, 0))] + specs,
        out_specs=specs,
        out_shape=[jax.ShapeDtypeStruct(c.shape, c.dtype) for c in caches],
        compiler_params=_params("parallel"),
        name="shift_caches",
    )(z4, *caches)


def _conv_step_kernel(a_ref, b_ref, s_ref, w_ref, cb_ref, lg_ref, lb_ref, c_ref, so_ref):
    hist = CONV_K - 1
    u = a_ref[...] * jax.nn.sigmoid(b_ref[...])
    y = (jnp.sum(s_ref[0] * w_ref[0:hist, :], axis=1, keepdims=True)
         + u * w_ref[hist:hist + 1, :] + cb_ref[...])
    c_ref[...] = _layernorm_silu(y, lg_ref[...], lb_ref[...]).astype(c_ref.dtype)
    so_ref[0, :, pl.ds(0, hist - 1), :] = s_ref[0, :, pl.ds(1, hist - 1), :]
    so_ref[0, :, pl.ds(hist - 1, 1), :] = u


def conv_step(a, b, state, conv_w, conv_b, ln_g, ln_b):
    bsz = a.shape[0]
    return pl.pallas_call(
        _conv_step_kernel,
        out_shape=[jax.ShapeDtypeStruct((bsz, 1, D_CONV), BF16), jax.ShapeDtypeStruct(state.shape, F32)],
        compiler_params=pltpu.CompilerParams(vmem_limit_bytes=VMEM_LIMIT),
        name="conv_step",
    )(a, b, state, conv_w, conv_b.reshape(1, D_CONV), ln_g.reshape(1, D_CONV), ln_b.reshape(1, D_CONV))


def _trunk_tail(x1, xo_in, w_xo, norm_ffn_g, w_rg, b_rg, w_re, b_re, w_ein, w_eout, norm_final_g, bm, tm):
    x2 = matmul_residual(xo_in, w_xo, x1, bm)
    xn3, gate = moe_router(x2, norm_ffn_g, w_rg, b_rg, w_re, b_re, tm)
    return moe_experts_final(xn3, gate, w_ein, w_eout, x2, norm_final_g, tm)


def kernel(x_prompt, x_sample, mem_prompt, cache_kv_g1, cache_kv_g2, cache_kv_g3, state_conv, cache_mem_kv,
           norm_mix_g, w_in, conv_w, conv_b, conv_ln_g, conv_ln_b, w_proj_a, w_proj_b, w_out,
           norm_xattn_g, norm_mem_g, w_xq, w_xkv, w_xo, norm_ffn_g,
           w_router_group, b_router_group, w_router_expert, b_router_expert, w_expert_in, w_expert_out,
           norm_final_g):
    depth = norm_mix_g.shape[0]
    assert depth == 1, "single-layer trunk"
    batch, seq, _ = x_prompt.shape
    dec_b, dec_t, _ = x_sample.shape
    assert dec_t == 1
    (g_mix, w_in, conv_w, conv_b, ln_g, ln_b, w_pa, w_pb, w_o, g_x, g_mem, w_xq, w_xkv, w_xo, g_ffn,
     w_rg, b_rg, w_re, b_re, w_ein, w_eout) = [t[0] for t in (
         norm_mix_g, w_in, conv_w, conv_b, conv_ln_g, conv_ln_b, w_proj_a, w_proj_b, w_out, norm_xattn_g,
         norm_mem_g, w_xq, w_xkv, w_xo, norm_ffn_g, w_router_group, b_router_group, w_router_expert,
         b_router_expert, w_expert_in, w_expert_out)]
    w_pa16, w_pb16, w_o16 = w_pa.astype(BF16), w_pb.astype(BF16), w_o.astype(BF16)
    w_ein, w_eout = w_ein.astype(BF16), w_eout.astype(BF16)
    caches = (cache_kv_g1, cache_kv_g2, cache_kv_g3)

    m_p = batch * seq
    xp = x_prompt.reshape(m_p, D_MODEL)
    za, gates = in_proj(xp, g_mix, w_in, bm=1024)
    o_a = band_attention(za, batch, seq)
    c, conv_prompt = conformer_conv(za, batch, seq, conv_w, conv_b, ln_g, ln_b)
    x1 = merge_branches(o_a, c, gates, xp, w_pa16, w_pb16, w_o16, tm=256)
    mkv = norm_matmul(mem_prompt.reshape(batch * N_MEM, D_MODEL), g_mem, w_xkv, F32, bm=1024)
    q = norm_matmul(x1, g_x, w_xq, BF16, bm=1024)
    xo_in = cross_attention_prompt(q, mkv, batch, seq)
    y_prompt = _trunk_tail(x1, xo_in, w_xo, g_ffn, w_rg, b_rg, w_re, b_re, w_ein, w_eout, norm_final_g,
                           bm=1024, tm=512)

    za3 = za.reshape(batch, seq, ZA_W)
    kv_prompt = []
    for g, (win, _) in enumerate(ATT_GROUPS):
        keep = min(win, seq)
        k = za3[:, seq - keep:, ATT_WIDTH + g * GROUP_W:ATT_WIDTH + (g + 1) * GROUP_W]
        v = za3[:, seq - keep:, 2 * ATT_WIDTH + g * GROUP_W:2 * ATT_WIDTH + (g + 1) * GROUP_W]
        kv_prompt.append(jnp.stack([k, v], axis=2).reshape(1, batch, keep, 2, HEADS, HEAD_DIM))
    conv_prompt = conv_prompt[None]
    mem_kv_prompt = mkv.reshape(1, batch, N_MEM, 2, X_HEADS, X_HEAD_DIM)

    xs = x_sample.reshape(dec_b, D_MODEL)
    zs, gates_s = in_proj(xs, g_mix, w_in, bm=dec_b)
    z4 = zs.reshape(dec_b, ZA_W // HEAD_DIM // HEADS, HEADS, HEAD_DIM)
    o_as = window_decode(z4, caches).reshape(dec_b, GROUP_W)
    kv_sample = shift_caches(z4, caches)
    a_s = zs[:, 3 * ATT_WIDTH:3 * ATT_WIDTH + D_CONV].reshape(dec_b, 1, D_CONV)
    b_s = zs[:, 3 * ATT_WIDTH + D_CONV:].reshape(dec_b, 1, D_CONV)
    c_s, conv_sample = conv_step(a_s, b_s, state_conv, conv_w, conv_b, ln_g, ln_b)
    x1s = merge_branches(o_as, c_s.reshape(dec_b, D_CONV), gates_s, xs, w_pa16, w_pb16, w_o16, tm=dec_b)
    q_s = norm_matmul(x1s, g_x, w_xq, BF16, bm=dec_b)
    xo_s = cross_attention_decode(q_s, cache_mem_kv)
    y_sample = _trunk_tail(x1s, xo_s, w_xo, g_ffn, w_rg, b_rg, w_re, b_re, w_ein, w_eout, norm_final_g,
                           bm=dec_b, tm=dec_b)

    return (y_prompt.reshape(batch, seq, D_MODEL), y_sample.reshape(dec_b, 1, D_MODEL),
            kv_prompt[0], kv_prompt[1], kv_prompt[2], conv_prompt, mem_kv_prompt,
            kv_sample[0], kv_sample[1], kv_sample[2], conv_sample)
```

```python
import functools

import jax
import jax.numpy as jnp
from jax import lax
from jax.experimental import pallas as pl
from jax.experimental.pallas import tpu as pltpu

F32 = jnp.float32
BF16 = jnp.bfloat16

D_MODEL = 2048
ATT_GROUPS = ((128, 1), (512, 4), (2048, 16))
HEADS = 4
HEAD_DIM = 128
GROUP_W = HEADS * HEAD_DIM
ATT_WIDTH = len(ATT_GROUPS) * GROUP_W
D_CONV = 1536
CONV_K = 31
N_MEM = 256
X_HEADS = 4
X_HEAD_DIM = D_MODEL // X_HEADS
N_EXPERT_GROUPS = 4
EXPERTS_PER_GROUP = 4
N_EXPERTS = 16
D_EXPERT = 512
ZA_W = 3 * ATT_WIDTH + 2 * D_CONV
GATE_W = 2 * D_MODEL
RMS_EPS = 1e-6
LN_EPS = 1e-5
NEG_INF = -1e30
BAND = 128
LANES = 128
ROUTER_GROUP_LANE = N_EXPERTS

VMEM_LIMIT = 56 * 1024 * 1024


def _params(*sem):
    return pltpu.CompilerParams(dimension_semantics=sem, vmem_limit_bytes=VMEM_LIMIT)


def _rms_rows(x, g):
    ms = jnp.mean(x * x, axis=-1, keepdims=True)
    return x * lax.rsqrt(ms + RMS_EPS) * g


def _store_normed(x_ref, g_ref, xn_ref, chunk=256):
    rows = x_ref.shape[0]
    step = min(chunk, rows)
    for r0 in range(0, rows, step):
        xn_ref[r0:r0 + step, :] = _rms_rows(x_ref[r0:r0 + step, :], g_ref[...]).astype(xn_ref.dtype)


def _norm_matmul_kernel(x_ref, g_ref, w_ref, o_ref, xn_ref):
    @pl.when(pl.program_id(1) == 0)
    def _():
        _store_normed(x_ref, g_ref, xn_ref)

    o_ref[...] = jnp.dot(xn_ref[...], w_ref[...].astype(BF16),
                         preferred_element_type=F32).astype(o_ref.dtype)


def norm_matmul(x, g, w, out_dtype, bm, bn=512):
    m, k = x.shape
    n = w.shape[1]
    bm = min(bm, m)
    return pl.pallas_call(
        _norm_matmul_kernel,
        grid=(m // bm, n // bn),
        in_specs=[pl.BlockSpec((bm, k), lambda i, j: (i, 0)),
                  pl.BlockSpec((1, k), lambda i, j: (0, 0)),
                  pl.BlockSpec((k, bn), lambda i, j: (0, j))],
        out_specs=pl.BlockSpec((bm, bn), lambda i, j: (i, j)),
        out_shape=jax.ShapeDtypeStruct((m, n), out_dtype),
        scratch_shapes=[pltpu.VMEM((bm, k), BF16)],
        compiler_params=_params("parallel", "arbitrary"),
        name="norm_matmul",
    )(x, g.reshape(1, k), w)


def _in_proj_kernel(x_ref, g_ref, w_ref, za_ref, gt_ref, xn_ref, *, n_za):
    j = pl.program_id(1)

    @pl.when(j == 0)
    def _():
        _store_normed(x_ref, g_ref, xn_ref)

    r = jnp.dot(xn_ref[...], w_ref[...].astype(BF16), preferred_element_type=F32)

    @pl.when(j < n_za)
    def _():
        za_ref[...] = r

    @pl.when(j >= n_za)
    def _():
        gt_ref[...] = r


def in_proj(x, g, w, bm, bn=512):
    m, k = x.shape
    bm = min(bm, m)
    n_za = ZA_W // bn
    n_gt = GATE_W // bn
    return pl.pallas_call(
        functools.partial(_in_proj_kernel, n_za=n_za),
        grid=(m // bm, n_za + n_gt),
        in_specs=[pl.BlockSpec((bm, k), lambda i, j: (i, 0)),
                  pl.BlockSpec((1, k), lambda i, j: (0, 0)),
                  pl.BlockSpec((k, bn), lambda i, j: (0, j))],
        out_specs=[pl.BlockSpec((bm, bn), lambda i, j: (i, jnp.minimum(j, n_za - 1))),
                   pl.BlockSpec((bm, bn), lambda i, j: (i, jnp.maximum(j - n_za, 0)))],
        out_shape=[jax.ShapeDtypeStruct((m, ZA_W), F32), jax.ShapeDtypeStruct((m, GATE_W), F32)],
        scratch_shapes=[pltpu.VMEM((bm, k), BF16)],
        compiler_params=_params("parallel", "arbitrary"),
        name="in_proj",
    )(x, g.reshape(1, k), w)


def _matmul_res_kernel(x_ref, w_ref, r_ref, o_ref):
    o_ref[...] = r_ref[...] + jnp.dot(x_ref[...], w_ref[...].astype(BF16), preferred_element_type=F32)


def matmul_residual(x, w, res, bm, bn=512):
    m, k = x.shape
    n = w.shape[1]
    bm = min(bm, m)
    return pl.pallas_call(
        _matmul_res_kernel,
        grid=(m // bm, n // bn),
        in_specs=[pl.BlockSpec((bm, k), lambda i, j: (i, 0)),
                  pl.BlockSpec((k, bn), lambda i, j: (0, j)),
                  pl.BlockSpec((bm, bn), lambda i, j: (i, j))],
        out_specs=pl.BlockSpec((bm, bn), lambda i, j: (i, j)),
        out_shape=jax.ShapeDtypeStruct((m, n), F32),
        compiler_params=_params("parallel", "arbitrary"),
        name="matmul_residual",
    )(x, w, res)


ATT_R = 2048


def _band_attn_kernel(*refs):
    n_g = len(ATT_GROUPS)
    in_refs = refs[:5 * n_g]
    o_ref = refs[5 * n_g]
    scr = refs[5 * n_g + 1:]
    kbufs, vbufs = scr[:n_g], scr[n_g:2 * n_g]
    obufs, mbufs, dbufs = scr[2 * n_g:3 * n_g], scr[3 * n_g:4 * n_g], scr[4 * n_g:5 * n_g]
    first_chunk = pl.program_id(1) == 0
    row = lax.broadcasted_iota(jnp.int32, (BAND, 2 * BAND), 0)
    col = lax.broadcasted_iota(jnp.int32, (BAND, 2 * BAND), 1)
    in_band = (col >= row) & (col <= row + BAND)
    in_cur = col >= BAND
    scale = HEAD_DIM ** -0.5
    nt = (((1,), (1,)), ((), ()))

    for g, (win, dil) in enumerate(ATT_GROUPS):
        q_ref, kc_ref, vc_ref, kp_ref, vp_ref = in_refs[5 * g:5 * g + 5]
        kbuf, vbuf, obuf, mbuf, dbuf = kbufs[g], vbufs[g], obufs[g], mbufs[g], dbufs[g]
        kbuf[0:win, :] = kp_ref[...]
        kbuf[win:win + ATT_R, :] = kc_ref[...]
        vbuf[0:win, :] = vp_ref[...]
        vbuf[win:win + ATT_R, :] = vc_ref[...]
        shift = dil.bit_length() - 1

        def sub_block(i, carry, q_ref=q_ref, kbuf=kbuf, vbuf=vbuf, obuf=obuf, mbuf=mbuf, dbuf=dbuf,
                      win=win, dil=dil, shift=shift):
            span = lax.shift_right_logical(i, shift)
            base = span * win + (i & (dil - 1))
            if dil == 1:
                q_rows, kv_rows = pl.ds(base, BAND), pl.ds(base, 2 * BAND)
            else:
                q_rows, kv_rows = pl.ds(base, BAND, stride=dil), pl.ds(base, 2 * BAND, stride=dil)
            q = q_ref[q_rows, :].astype(BF16)
            k = kbuf[kv_rows, :].astype(BF16)
            v = vbuf[kv_rows, :].astype(BF16)
            s = lax.dot_general(q, k, nt, preferred_element_type=F32) * scale
            has_prev = jnp.logical_not(first_chunk & (span == 0))
            s = jnp.where(in_band & (in_cur | has_prev), s, NEG_INF)
            m = jnp.max(s, axis=-1, keepdims=True)
            p = jnp.exp(s - m)
            den = jnp.sum(p, axis=-1, keepdims=True)
            obuf[q_rows, :] = jnp.dot(p.astype(BF16), v, preferred_element_type=F32)
            mbuf[q_rows, :] = jnp.broadcast_to(m, (BAND, HEAD_DIM))
            dbuf[q_rows, :] = jnp.broadcast_to(den, (BAND, HEAD_DIM))
            return carry

        lax.fori_loop(0, ATT_R // BAND, sub_block, 0)

    chunk = 256
    for r0 in range(0, ATT_R, chunk):
        rs = slice(r0, r0 + chunk)
        ms = [mb[rs, :] for mb in mbufs]
        m_all = functools.reduce(jnp.maximum, ms)
        ws = [jnp.exp(m - m_all) for m in ms]
        num = sum(w * ob[rs, :] for w, ob in zip(ws, obufs))
        den = sum(w * db[rs, :] for w, db in zip(ws, dbufs))
        o_ref[rs, :] = (num / den).astype(o_ref.dtype)


def band_attention(za, batch, seq):
    assert seq % ATT_R == 0
    nch = seq // ATT_R
    in_specs, scratch = [], []
    for g, (win, dil) in enumerate(ATT_GROUPS):
        assert win // dil == BAND and ATT_R % win == 0
        cols = [(part * len(ATT_GROUPS) + g) * HEADS for part in range(3)]
        cur = lambda c: pl.BlockSpec((ATT_R, HEAD_DIM), lambda b, ch, h, c=c: (b * nch + ch, c + h))
        prev = lambda c, win=win: pl.BlockSpec(
            (win, HEAD_DIM), lambda b, ch, h, c=c, win=win: (jnp.maximum((b * seq + ch * ATT_R) // win - 1, 0), c + h))
        in_specs += [cur(cols[0]), cur(cols[1]), cur(cols[2]), prev(cols[1]), prev(cols[2])]
    for _ in range(2):
        scratch += [pltpu.VMEM((win + ATT_R, HEAD_DIM), F32) for win, _ in ATT_GROUPS]
    scratch += [pltpu.VMEM((ATT_R, HEAD_DIM), F32)] * (3 * len(ATT_GROUPS))
    return pl.pallas_call(
        _band_attn_kernel,
        grid=(batch, nch, HEADS),
        in_specs=in_specs,
        out_specs=pl.BlockSpec((ATT_R, HEAD_DIM), lambda b, ch, h: (b * nch + ch, h)),
        out_shape=jax.ShapeDtypeStruct((batch * seq, GROUP_W), BF16),
        scratch_shapes=scratch,
        compiler_params=_params("parallel", "parallel", "parallel"),
        name="band_attention",
    )(*([za] * (5 * len(ATT_GROUPS))))


CONV_T = 256
CONV_HIST = 32
CONV_RC = 64
CONV_CC = 256


def _layernorm_silu(y, g, b):
    mu = jnp.mean(y, axis=-1, keepdims=True)
    yc = y - mu
    var = jnp.mean(yc * yc, axis=-1, keepdims=True)
    yn = yc * lax.rsqrt(var + LN_EPS) * g + b
    return yn * jax.nn.sigmoid(yn)


def _conv_kernel(a_ref, b_ref, w_ref, cb_ref, lg_ref, lb_ref, c_ref, tail_ref, ubuf, ybuf):
    t = pl.program_id(1)

    @pl.when(t == 0)
    def _():
        ubuf[0:CONV_HIST, :] = jnp.zeros((CONV_HIST, D_CONV), F32)

    @pl.when(t > 0)
    def _():
        ubuf[0:CONV_HIST, :] = ubuf[CONV_T:CONV_T + CONV_HIST, :]

    u = a_ref[...] * jax.nn.sigmoid(b_ref[...])
    ubuf[CONV_HIST:CONV_HIST + CONV_T, :] = u
    tail_ref[0] = ubuf[CONV_T:CONV_T + CONV_HIST, :]

    first = CONV_HIST - (CONV_K - 1)
    for c0 in range(0, D_CONV, CONV_CC):
        cs = slice(c0, c0 + CONV_CC)
        for r0 in range(0, CONV_T, CONV_RC):
            acc = jnp.broadcast_to(cb_ref[:, cs], (CONV_RC, CONV_CC))
            for k in range(CONV_K):
                acc = acc + w_ref[k:k + 1, cs] * ubuf[r0 + first + k:r0 + first + k + CONV_RC, cs]
            ybuf[r0:r0 + CONV_RC, cs] = acc
    c_ref[...] = _layernorm_silu(ybuf[...], lg_ref[...], lb_ref[...]).astype(c_ref.dtype)


def conformer_conv(za, batch, seq, conv_w, conv_b, ln_g, ln_b):
    nt = seq // CONV_T
    row = lambda b, t: (b * nt + t, 0)
    const = lambda b, t: (0, 0)
    c, tail = pl.pallas_call(
        _conv_kernel,
        grid=(batch, nt),
        in_specs=[pl.BlockSpec((CONV_T, D_CONV), lambda b, t: (b * nt + t, 3 * ATT_WIDTH // D_CONV)),
                  pl.BlockSpec((CONV_T, D_CONV), lambda b, t: (b * nt + t, 3 * ATT_WIDTH // D_CONV + 1)),
                  pl.BlockSpec((CONV_K, D_CONV), const),
                  pl.BlockSpec((1, D_CONV), const),
                  pl.BlockSpec((1, D_CONV), const),
                  pl.BlockSpec((1, D_CONV), const)],
        out_specs=[pl.BlockSpec((CONV_T, D_CONV), row),
                   pl.BlockSpec((1, CONV_HIST, D_CONV), lambda b, t: (b, 0, 0))],
        out_shape=[jax.ShapeDtypeStruct((batch * seq, D_CONV), BF16),
                   jax.ShapeDtypeStruct((batch, CONV_HIST, D_CONV), F32)],
        scratch_shapes=[pltpu.VMEM((CONV_HIST + CONV_T, D_CONV), F32),
                        pltpu.VMEM((CONV_T, D_CONV), F32)],
        compiler_params=_params("parallel", "arbitrary"),
        name="conformer_conv",
    )(za, za, conv_w, conv_b.reshape(1, D_CONV), ln_g.reshape(1, D_CONV), ln_b.reshape(1, D_CONV))
    return c, tail[:, CONV_HIST - (CONV_K - 1):]


def _merge_kernel(oa_ref, c_ref, ga_ref, gb_ref, x_ref, wpa_ref, wpb_ref, wo_ref, y_ref):
    ta = jnp.dot(oa_ref[...], wpa_ref[...], preferred_element_type=F32)
    tb = jnp.dot(c_ref[...], wpb_ref[...], preferred_element_type=F32)
    hmix = jax.nn.sigmoid(ga_ref[...]) * ta + jax.nn.sigmoid(gb_ref[...]) * tb
    y_ref[...] = x_ref[...] + jnp.dot(hmix.astype(BF16), wo_ref[...], preferred_element_type=F32)


def merge_branches(o_a, c, gates, x, w_pa, w_pb, w_out, tm):
    m = x.shape[0]
    tm = min(tm, m)
    row = lambda i: (i, 0)
    const = lambda i: (0, 0)
    once = dict(pipeline_mode=pl.Buffered(1))
    return pl.pallas_call(
        _merge_kernel,
        grid=(m // tm,),
        in_specs=[
            pl.BlockSpec((tm, GROUP_W), row),
            pl.BlockSpec((tm, D_CONV), row),
            pl.BlockSpec((tm, D_MODEL), lambda i: (i, 0)),
            pl.BlockSpec((tm, D_MODEL), lambda i: (i, 1)),
            pl.BlockSpec((tm, D_MODEL), row),
            pl.BlockSpec((GROUP_W, D_MODEL), const, **once),
            pl.BlockSpec((D_CONV, D_MODEL), const, **once),
            pl.BlockSpec((D_MODEL, D_MODEL), const, **once)],
        out_specs=pl.BlockSpec((tm, D_MODEL), row),
        out_shape=jax.ShapeDtypeStruct((m, D_MODEL), F32),
        compiler_params=_params("parallel"),
        name="merge_branches",
    )(o_a, c, gates, gates, x, w_pa, w_pb, w_out)


def _xattn_kernel(q_ref, kv_ref, o_ref):
    scale = X_HEAD_DIM ** -0.5
    nt = (((1,), (1,)), ((), ()))
    for h in range(X_HEADS):
        hs = slice(h * X_HEAD_DIM, (h + 1) * X_HEAD_DIM)
        vs = slice(D_MODEL + h * X_HEAD_DIM, D_MODEL + (h + 1) * X_HEAD_DIM)
        s = lax.dot_general(q_ref[:, hs], kv_ref[:, hs].astype(BF16), nt, preferred_element_type=F32) * scale
        m = jnp.max(s, axis=-1, keepdims=True)
        p = jnp.exp(s - m)
        p = p / jnp.sum(p, axis=-1, keepdims=True)
        o = jnp.dot(p.astype(BF16), kv_ref[:, vs].astype(BF16), preferred_element_type=F32)
        o_ref[:, hs] = o.astype(o_ref.dtype)


def cross_attention_prompt(q, mkv, batch, seq, tm=512):
    nt = seq // tm
    return pl.pallas_call(
        _xattn_kernel,
        grid=(batch, nt),
        in_specs=[pl.BlockSpec((tm, D_MODEL), lambda b, t: (b * nt + t, 0)),
                  pl.BlockSpec((N_MEM, 2 * D_MODEL), lambda b, t: (b, 0))],
        out_specs=pl.BlockSpec((tm, D_MODEL), lambda b, t: (b * nt + t, 0)),
        out_shape=jax.ShapeDtypeStruct((batch * seq, D_MODEL), BF16),
        compiler_params=_params("parallel", "arbitrary"),
        name="cross_attention_prompt",
    )(q, mkv)


def _xattn_decode_kernel(q_ref, kv_ref, o_ref):
    scale = X_HEAD_DIM ** -0.5
    q = q_ref[0].astype(F32)
    k = kv_ref[0, 0, :, 0]
    v = kv_ref[0, 0, :, 1]
    s = jnp.sum(k * q, axis=-1, keepdims=True) * scale
    m = jnp.max(s, axis=0, keepdims=True)
    p = jnp.exp(s - m)
    p = p / jnp.sum(p, axis=0, keepdims=True)
    o_ref[0] = jnp.sum(p * v, axis=0, keepdims=True).astype(o_ref.dtype)


def cross_attention_decode(q, cache_mem_kv):
    b = q.shape[0]
    q4 = q.reshape(b, 1, X_HEADS, X_HEAD_DIM)
    o = pl.pallas_call(
        _xattn_decode_kernel,
        grid=(b,),
        in_specs=[pl.BlockSpec((1, 1, X_HEADS, X_HEAD_DIM), lambda i: (i, 0, 0, 0)),
                  pl.BlockSpec((1, 1, N_MEM, 2, X_HEADS, X_HEAD_DIM), lambda i: (0, i, 0, 0, 0, 0))],
        out_specs=pl.BlockSpec((1, 1, X_HEADS, X_HEAD_DIM), lambda i: (i, 0, 0, 0)),
        out_shape=jax.ShapeDtypeStruct((b, 1, X_HEADS, X_HEAD_DIM), BF16),
        compiler_params=_params("parallel"),
        name="cross_attention_decode",
    )(q4, cache_mem_kv)
    return o.reshape(b, D_MODEL)


N_BUCKETS = N_EXPERT_GROUPS * 6
PAIR_LO = (0, 0, 0, 1, 1, 2)
PAIR_HI = (1, 2, 3, 2, 3, 3)
ROW_TILES = D_MODEL // LANES
PAYLOAD_ROWS = ROW_TILES + 8
MOE_TM = 256


def _route(logits):
    lane = lax.broadcasted_iota(jnp.int32, logits.shape, 1)
    lanef = lane.astype(F32)
    big = float(LANES)
    is_group = (lane >= ROUTER_GROUP_LANE) & (lane < ROUTER_GROUP_LANE + N_EXPERT_GROUPS)
    lg = jnp.where(is_group, logits, -jnp.inf)
    mg = jnp.max(lg, axis=-1, keepdims=True)
    p_sel = 1.0 / jnp.sum(jnp.exp(lg - mg), axis=-1, keepdims=True)
    gsel = jnp.min(jnp.where(lg == mg, lanef, big), axis=-1, keepdims=True) - ROUTER_GROUP_LANE
    group_of_lane = lax.shift_right_logical(lane, 2)
    in_group = (lane < N_EXPERTS) & (group_of_lane == gsel.astype(jnp.int32))
    le = jnp.where(in_group, logits, -jnp.inf)
    v1 = jnp.max(le, axis=-1, keepdims=True)
    i1 = jnp.min(jnp.where(le == v1, lanef, big), axis=-1, keepdims=True)
    le2 = jnp.where(lanef == i1, -jnp.inf, le)
    v2 = jnp.max(le2, axis=-1, keepdims=True)
    i2 = jnp.min(jnp.where(le2 == v2, lanef, big), axis=-1, keepdims=True)
    t = jnp.exp(v2 - v1)
    tot = 1.0 + t
    return gsel, i1, i2, (1.0 / tot) * p_sel, (t / tot) * p_sel


def _router_routed_kernel(x_ref, g_ref, w_ref, b_ref, xd_ref, meta_ref, cnt_ref, carry_ref):
    tm = x_ref.shape[0]

    @pl.when(pl.program_id(0) == 0)
    def _():
        carry_ref[...] = jnp.zeros_like(carry_ref)

    xn = _rms_rows(x_ref[...], g_ref[...])
    logits = jnp.dot(xn, w_ref[...], preferred_element_type=F32, precision=lax.Precision.HIGHEST) + b_ref[...]
    gsel, i1, i2, p1, p2 = _route(logits)
    lo = jnp.minimum(i1, i2) - EXPERTS_PER_GROUP * gsel
    hi = jnp.maximum(i1, i2) - EXPERTS_PER_GROUP * gsel
    bucket = gsel * 6.0 + lo * (7.0 - lo) * 0.5 + (hi - lo - 1.0)
    gate_lo = jnp.where(i1 < i2, p1, p2)
    gate_hi = jnp.where(i1 < i2, p2, p1)

    lane = lax.broadcasted_iota(jnp.int32, (tm, LANES), 1)
    onehot = (lane.astype(F32) == bucket).astype(F32)
    r_i = lax.broadcasted_iota(jnp.int32, (tm, tm), 0)
    c_i = lax.broadcasted_iota(jnp.int32, (tm, tm), 1)
    before = (c_i < r_i).astype(BF16)
    rank_local = jnp.dot(before, onehot.astype(BF16), preferred_element_type=F32)
    rank = jnp.sum(onehot * (rank_local + carry_ref[0:1, :]), axis=-1, keepdims=True)
    carry_ref[0:1, :] = carry_ref[0:1, :] + jnp.sum(onehot, axis=0, keepdims=True)
    cnt_ref[...] = jnp.broadcast_to(carry_ref[0:1, :], cnt_ref.shape)
    meta_ref[...] = jnp.where(lane == 0, bucket, jnp.where(lane == 1, rank, 0.0))

    for j in range(ROW_TILES):
        xd_ref[pl.ds(j, tm, stride=PAYLOAD_ROWS), :] = xn[:, j * LANES:(j + 1) * LANES]
    xd_ref[pl.ds(ROW_TILES, tm, stride=PAYLOAD_ROWS), :] = jnp.where(
        lane == 0, gate_lo, jnp.where(lane == 1, gate_hi, 0.0))
    for j in range(ROW_TILES + 1, PAYLOAD_ROWS):
        xd_ref[pl.ds(j, tm, stride=PAYLOAD_ROWS), :] = jnp.zeros((tm, LANES), F32)


def _router_weights(w_rg, b_rg, w_re, b_re):
    pad = LANES - N_EXPERTS - N_EXPERT_GROUPS
    w = jnp.concatenate([w_re, w_rg, jnp.zeros((D_MODEL, pad), F32)], axis=1)
    b = jnp.concatenate([b_re, b_rg, jnp.zeros((pad,), F32)]).reshape(1, LANES)
    return w, b


def moe_router_routed(x, g, w_rg, b_rg, w_re, b_re, tm=512):
    m = x.shape[0]
    w, b = _router_weights(w_rg, b_rg, w_re, b_re)
    row = lambda i: (i, 0)
    const = lambda i: (0, 0)
    return pl.pallas_call(
        _router_routed_kernel,
        grid=(m // tm,),
        in_specs=[pl.BlockSpec((tm, D_MODEL), row), pl.BlockSpec((1, D_MODEL), const),
                  pl.BlockSpec((D_MODEL, LANES), const), pl.BlockSpec((1, LANES), const)],
        out_specs=[pl.BlockSpec((tm * PAYLOAD_ROWS, LANES), row), pl.BlockSpec((tm, LANES), row),
                   pl.BlockSpec((8, LANES), const)],
        out_shape=[jax.ShapeDtypeStruct((m * PAYLOAD_ROWS, LANES), F32), jax.ShapeDtypeStruct((m, LANES), F32),
                   jax.ShapeDtypeStruct((8, LANES), F32)],
        scratch_shapes=[pltpu.VMEM((8, LANES), F32)],
        compiler_params=_params("arbitrary"),
        name="moe_router_routed",
    )(x, g.reshape(1, D_MODEL), w, b)


INV_CHUNK = 2048


def _inverse_kernel(dest_ref, inv_ref):
    step = pl.program_id(0)

    @pl.when(step == 0)
    def _():
        def clear(k, c):
            inv_ref[k] = 0
            return c
        lax.fori_loop(0, inv_ref.shape[0], clear, 0, unroll=8)

    def put(t, c):
        inv_ref[dest_ref[0, 0, t]] = step * INV_CHUNK + t
        return c
    lax.fori_loop(0, INV_CHUNK, put, 0, unroll=8)


def invert_slots(dest, n_slots):
    m = dest.shape[0]
    return pl.pallas_call(
        _inverse_kernel,
        grid=(m // INV_CHUNK,),
        in_specs=[pl.BlockSpec((1, 1, INV_CHUNK), lambda i: (i, 0, 0), memory_space=pltpu.SMEM)],
        out_specs=pl.BlockSpec(memory_space=pltpu.SMEM),
        out_shape=jax.ShapeDtypeStruct((n_slots,), jnp.int32),
        compiler_params=_params("arbitrary"),
        name="invert_slots",
    )(dest.reshape(m // INV_CHUNK, 1, INV_CHUNK))


def _start_row_gather(idx_ref, src_hbm, dst, sem, n_items, rows):
    def start(r, c):
        first = pl.multiple_of(idx_ref[0, 0, r] * rows, 8)
        pltpu.make_async_copy(src_hbm.at[pl.ds(first, rows)], dst.at[pl.ds(r * rows, rows)], sem).start()
        return c
    lax.fori_loop(0, n_items, start, 0, unroll=8)


def _wait_row_gather(src_hbm, dst, sem, n_items, rows):
    def wait(r, c):
        pltpu.make_async_copy(src_hbm.at[pl.ds(0, rows)], dst.at[pl.ds(r * rows, rows)], sem).wait()
        return c
    lax.fori_loop(0, n_items, wait, 0, unroll=8)


def _moe_routed_kernel(ea_ref, eb_ref, nv_ref, inv_ref, inv_next_ref, xd_hbm, w1a_ref, w1b_ref, w2a_ref, w2b_ref,
                       ys_ref, xbuf, sems, x_scr):
    del ea_ref, eb_ref
    i = pl.program_id(0)
    n_valid = nv_ref[0]
    slot = i % 2
    tm = MOE_TM

    @pl.when(i == 0)
    def _():
        _start_row_gather(inv_ref, xd_hbm, xbuf.at[0], sems.at[0], tm, PAYLOAD_ROWS)

    @pl.when(i + 1 < n_valid)
    def _():
        _start_row_gather(inv_next_ref, xd_hbm, xbuf.at[1 - slot], sems.at[1 - slot], tm, PAYLOAD_ROWS)

    @pl.when(i < n_valid)
    def _():
        buf = xbuf.at[slot]
        _wait_row_gather(xd_hbm, buf, sems.at[slot], tm, PAYLOAD_ROWS)
        for j in range(ROW_TILES):
            x_scr[:, j * LANES:(j + 1) * LANES] = buf[pl.ds(j, tm, stride=PAYLOAD_ROWS), :].astype(BF16)
        gates = buf[pl.ds(ROW_TILES, tm, stride=PAYLOAD_ROWS), :]
        x = x_scr[...]
        hu = jnp.dot(x, w1a_ref[0], preferred_element_type=F32)
        ha = jax.nn.silu(hu[:, :D_EXPERT]) * hu[:, D_EXPERT:] * gates[:, 0:1]
        hu = jnp.dot(x, w1b_ref[0], preferred_element_type=F32)
        hb = jax.nn.silu(hu[:, :D_EXPERT]) * hu[:, D_EXPERT:] * gates[:, 1:2]
        y = (jnp.dot(ha.astype(BF16), w2a_ref[0], preferred_element_type=F32)
             + jnp.dot(hb.astype(BF16), w2b_ref[0], preferred_element_type=F32))
        for j in range(ROW_TILES):
            ys_ref[pl.ds(j, tm, stride=ROW_TILES), :] = y[:, j * LANES:(j + 1) * LANES]

    @pl.when(i >= n_valid)
    def _():
        ys_ref[...] = jnp.zeros_like(ys_ref)


def moe_experts_routed(xd, inv, tile_ea, tile_eb, n_valid, w_ein, w_eout):
    n_tiles = tile_ea.shape[0]
    tm = MOE_TM
    inv3 = inv.reshape(n_tiles, 1, tm)
    smem = dict(memory_space=pltpu.SMEM)
    grid_spec = pltpu.PrefetchScalarGridSpec(
        num_scalar_prefetch=3,
        grid=(n_tiles,),
        in_specs=[
            pl.BlockSpec((1, 1, tm), lambda i, ea, eb, nv: (i, 0, 0), **smem),
            pl.BlockSpec((1, 1, tm), lambda i, ea, eb, nv: (jnp.minimum(i + 1, n_tiles - 1), 0, 0), **smem),
            pl.BlockSpec(memory_space=pl.ANY),
            pl.BlockSpec((1, D_MODEL, 2 * D_EXPERT), lambda i, ea, eb, nv: (ea[i], 0, 0)),
            pl.BlockSpec((1, D_MODEL, 2 * D_EXPERT), lambda i, ea, eb, nv: (eb[i], 0, 0)),
            pl.BlockSpec((1, D_EXPERT, D_MODEL), lambda i, ea, eb, nv: (ea[i], 0, 0)),
            pl.BlockSpec((1, D_EXPERT, D_MODEL), lambda i, ea, eb, nv: (eb[i], 0, 0))],
        out_specs=pl.BlockSpec((tm * ROW_TILES, LANES), lambda i, ea, eb, nv: (i, 0)),
        scratch_shapes=[pltpu.VMEM((2, tm * PAYLOAD_ROWS, LANES), F32), pltpu.SemaphoreType.DMA((2,)),
                        pltpu.VMEM((tm, D_MODEL), BF16)])
    return pl.pallas_call(
        _moe_routed_kernel,
        grid_spec=grid_spec,
        out_shape=jax.ShapeDtypeStruct((n_tiles * tm * ROW_TILES, LANES), F32),
        compiler_params=_params("arbitrary"),
        name="moe_experts_routed",
    )(tile_ea, tile_eb, n_valid, inv3, inv3, xd, w_ein, w_ein, w_eout, w_eout)


def _combine_kernel(dest_ref, dest_next_ref, ys_hbm, res_ref, gf_ref, y_ref, buf, sems):
    i = pl.program_id(0)
    slot = i % 2
    tm = res_ref.shape[0]

    @pl.when(i == 0)
    def _():
        _start_row_gather(dest_ref, ys_hbm, buf.at[0], sems.at[0], tm, ROW_TILES)

    @pl.when(i + 1 < pl.num_programs(0))
    def _():
        _start_row_gather(dest_next_ref, ys_hbm, buf.at[1 - slot], sems.at[1 - slot], tm, ROW_TILES)

    cur = buf.at[slot]
    _wait_row_gather(ys_hbm, cur, sems.at[slot], tm, ROW_TILES)
    for j in range(ROW_TILES):
        js = slice(j * LANES, (j + 1) * LANES)
        y_ref[:, js] = res_ref[:, js] + cur[pl.ds(j, tm, stride=ROW_TILES), :]
    y_ref[...] = _rms_rows(y_ref[...], gf_ref[...])


def moe_combine_final(ys, dest, res, g_final, tm=256):
    m = res.shape[0]
    nt = m // tm
    dest3 = dest.reshape(nt, 1, tm)
    smem = dict(memory_space=pltpu.SMEM)
    return pl.pallas_call(
        _combine_kernel,
        grid=(nt,),
        in_specs=[pl.BlockSpec((1, 1, tm), lambda i: (i, 0, 0), **smem),
                  pl.BlockSpec((1, 1, tm), lambda i: (jnp.minimum(i + 1, nt - 1), 0, 0), **smem),
                  pl.BlockSpec(memory_space=pl.ANY),
                  pl.BlockSpec((tm, D_MODEL), lambda i: (i, 0)),
                  pl.BlockSpec((1, D_MODEL), lambda i: (0, 0))],
        out_specs=pl.BlockSpec((tm, D_MODEL), lambda i: (i, 0)),
        out_shape=jax.ShapeDtypeStruct((m, D_MODEL), F32),
        scratch_shapes=[pltpu.VMEM((2, tm * ROW_TILES, LANES), F32), pltpu.SemaphoreType.DMA((2,))],
        compiler_params=_params("arbitrary"),
        name="moe_combine_final",
    )(dest3, dest3, ys, res, g_final.reshape(1, D_MODEL))


def moe_routed(x, g_ffn, w_rg, b_rg, w_re, b_re, w_ein, w_eout, g_final):
    m = x.shape[0]
    tm = MOE_TM
    n_tiles = m // tm + N_BUCKETS
    xd, meta, cnt = moe_router_routed(x, g_ffn, w_rg, b_rg, w_re, b_re)
    bucket = meta[:, 0].astype(jnp.int32)
    rank = meta[:, 1].astype(jnp.int32)
    counts = cnt[0, :N_BUCKETS].astype(jnp.int32)
    tiles_per_bucket = (counts + tm - 1) // tm
    tile_end = jnp.cumsum(tiles_per_bucket)
    tile_start = tile_end - tiles_per_bucket
    dest = jnp.take(tile_start, bucket) * tm + rank
    n_valid = tile_end[-1]
    tile_id = jnp.arange(n_tiles, dtype=jnp.int32)
    tile_bucket = jnp.searchsorted(tile_end, jnp.minimum(tile_id, n_valid - 1), side="right").astype(jnp.int32)
    group = tile_bucket // 6
    tile_ea = group * EXPERTS_PER_GROUP + jnp.take(jnp.array(PAIR_LO, jnp.int32), tile_bucket % 6)
    tile_eb = group * EXPERTS_PER_GROUP + jnp.take(jnp.array(PAIR_HI, jnp.int32), tile_bucket % 6)
    inv = invert_slots(dest, n_tiles * tm)
    ys = moe_experts_routed(xd, inv, tile_ea, tile_eb, n_valid.reshape(1), w_ein, w_eout)
    return moe_combine_final(ys, dest, x, g_final)


def _router_kernel(x_ref, g_ref, w_ref, b_ref, xn_ref, gate_ref):
    xn = _rms_rows(x_ref[...], g_ref[...])
    xn_ref[...] = xn.astype(BF16)
    logits = jnp.dot(xn, w_ref[...], preferred_element_type=F32, precision=lax.Precision.HIGHEST) + b_ref[...]
    _, i1, i2, p1, p2 = _route(logits)
    lanef = lax.broadcasted_iota(jnp.int32, logits.shape, 1).astype(F32)
    gate_ref[...] = jnp.where(lanef == i1, p1, 0.0) + jnp.where(lanef == i2, p2, 0.0)


def moe_router(x, g, w_rg, b_rg, w_re, b_re, tm):
    m = x.shape[0]
    tm = min(tm, m)
    w, b = _router_weights(w_rg, b_rg, w_re, b_re)
    row = lambda i: (i, 0)
    const = lambda i: (0, 0)
    return pl.pallas_call(
        _router_kernel,
        grid=(m // tm,),
        in_specs=[pl.BlockSpec((tm, D_MODEL), row), pl.BlockSpec((1, D_MODEL), const),
                  pl.BlockSpec((D_MODEL, LANES), const), pl.BlockSpec((1, LANES), const)],
        out_specs=[pl.BlockSpec((tm, D_MODEL), row), pl.BlockSpec((tm, LANES), row)],
        out_shape=[jax.ShapeDtypeStruct((m, D_MODEL), BF16), jax.ShapeDtypeStruct((m, LANES), F32)],
        compiler_params=_params("parallel"),
        name="moe_router",
    )(x, g.reshape(1, D_MODEL), w, b)


def _moe_dense_kernel(xn_ref, gate_ref, w1_ref, w2_ref, res_ref, gf_ref, y_ref):
    e = pl.program_id(1)

    @pl.when(e == 0)
    def _():
        y_ref[...] = res_ref[...]

    hu = jnp.dot(xn_ref[...], w1_ref[0].astype(BF16), preferred_element_type=F32)
    h = jax.nn.silu(hu[:, :D_EXPERT]) * hu[:, D_EXPERT:]
    lane = lax.broadcasted_iota(jnp.int32, gate_ref.shape, 1)
    gate = jnp.sum(jnp.where(lane == e, gate_ref[...], 0.0), axis=-1, keepdims=True)
    h = h * gate
    y_ref[...] += jnp.dot(h.astype(BF16), w2_ref[0].astype(BF16), preferred_element_type=F32)

    @pl.when(e == N_EXPERTS - 1)
    def _():
        y_ref[...] = _rms_rows(y_ref[...], gf_ref[...])


def moe_experts_final(xn, gate, w_ein, w_eout, res, g_final, tm):
    m = xn.shape[0]
    tm = min(tm, m)
    row = lambda i, e: (i, 0)
    return pl.pallas_call(
        _moe_dense_kernel,
        grid=(m // tm, N_EXPERTS),
        in_specs=[pl.BlockSpec((tm, D_MODEL), row), pl.BlockSpec((tm, LANES), row),
                  pl.BlockSpec((1, D_MODEL, 2 * D_EXPERT), lambda i, e: (e, 0, 0)),
                  pl.BlockSpec((1, D_EXPERT, D_MODEL), lambda i, e: (e, 0, 0)),
                  pl.BlockSpec((tm, D_MODEL), row),
                  pl.BlockSpec((1, D_MODEL), lambda i, e: (0, 0))],
        out_specs=pl.BlockSpec((tm, D_MODEL), row),
        out_shape=jax.ShapeDtypeStruct((m, D_MODEL), F32),
        compiler_params=_params("parallel", "arbitrary"),
        name="moe_experts_final",
    )(xn, gate, w_ein, w_eout, res, g_final.reshape(1, D_MODEL))


def _window_decode_kernel(z_ref, c1_ref, c2_ref, c3_ref, o_ref):
    scale = HEAD_DIM ** -0.5
    os, ms, ds = [], [], []
    for g, c_ref in enumerate((c1_ref, c2_ref, c3_ref)):
        q = z_ref[0, g:g + 1]
        k_new = z_ref[0, 3 + g:4 + g]
        v_new = z_ref[0, 6 + g:7 + g]
        k = c_ref[0, 0, :, 0, 0]
        v = c_ref[0, 0, :, 0, 1]
        s = jnp.sum(k * q, axis=-1, keepdims=True) * scale
        s_new = jnp.sum(k_new * q, axis=-1, keepdims=True) * scale
        m = jnp.maximum(jnp.max(s, axis=0, keepdims=True), s_new)
        p = jnp.exp(s - m)
        p_new = jnp.exp(s_new - m)
        den = jnp.sum(p, axis=0, keepdims=True) + p_new
        os.append(jnp.sum(p * v, axis=0, keepdims=True) + p_new * v_new)
        ms.append(m)
        ds.append(den)
    m_all = functools.reduce(jnp.maximum, ms)
    ws = [jnp.exp(m - m_all) for m in ms]
    num = sum(w * o for w, o in zip(ws, os))
    den = sum(w * d for w, d in zip(ws, ds))
    o_ref[0] = (num / den).astype(o_ref.dtype)


def window_decode(z4, caches):
    b = z4.shape[0]
    views, specs = [], []
    for cache, (win, dil) in zip(caches, ATT_GROUPS):
        n = cache.shape[2]
        assert n == win and n // dil == BAND
        views.append(cache.reshape(1, b, BAND, dil, 2, HEADS, HEAD_DIM))
        specs.append(pl.BlockSpec((1, 1, BAND, 1, 2, HEADS, HEAD_DIM), lambda i: (0, i, 0, 0, 0, 0, 0)))
    return pl.pallas_call(
        _window_decode_kernel,
        grid=(b,),
        in_specs=[pl.BlockSpec((1,) + z4.shape[1:], lambda i: (i, 0, 0, 0))] + specs,
        out_specs=pl.BlockSpec((1, 1, HEADS, HEAD_DIM), lambda i: (i, 0, 0, 0)),
        out_shape=jax.ShapeDtypeStruct((b, 1, HEADS, HEAD_DIM), BF16),
        compiler_params=_params("parallel"),
        name="window_decode",
    )(z4, *views)


SHIFT_ROWS = 64


def _shift_kernel(z_ref, c1_ref, c2_ref, c3_ref, o1_ref, o2_ref, o3_ref):
    for g, (c_ref, o_ref) in enumerate(((c1_ref, o1_ref), (c2_ref, o2_ref), (c3_ref, o3_ref))):
        n = c_ref.shape[2]
        full, rem = divmod(n - 1, SHIFT_ROWS)

        def move(j, carry, c_ref=c_ref, o_ref=o_ref):
            o_ref[0, 0, pl.ds(j * SHIFT_ROWS, SHIFT_ROWS)] = c_ref[0, 0, pl.ds(j * SHIFT_ROWS + 1, SHIFT_ROWS)]
            return carry

        lax.fori_loop(0, full, move, 0)
        if rem:
            o_ref[0, 0, pl.ds(full * SHIFT_ROWS, rem)] = c_ref[0, 0, pl.ds(full * SHIFT_ROWS + 1, rem)]
        o_ref[0, 0, n - 1, 0] = z_ref[0, 3 + g]
        o_ref[0, 0, n - 1, 1] = z_ref[0, 6 + g]


def shift_caches(z4, caches):
    b = z4.shape[0]
    specs = [pl.BlockSpec((1, 1) + c.shape[2:], lambda i: (0, i, 0, 0, 0, 0)) for c in caches]
    return pl.pallas_call(
        _shift_kernel,
        grid=(b,),
        in_specs=[pl.BlockSpec((1,) + z4.shape[1:], lambda i: (i, 0, 0, 0))] + specs,
        out_specs=specs,
        out_shape=[jax.ShapeDtypeStruct(c.shape, c.dtype) for c in caches],
        compiler_params=_params("parallel"),
        name="shift_caches",
    )(z4, *caches)


def _conv_step_kernel(a_ref, b_ref, s_ref, w_ref, cb_ref, lg_ref, lb_ref, c_ref, so_ref):
    hist = CONV_K - 1
    u = a_ref[...] * jax.nn.sigmoid(b_ref[...])
    y = (jnp.sum(s_ref[0] * w_ref[0:hist, :], axis=1, keepdims=True)
         + u * w_ref[hist:hist + 1, :] + cb_ref[...])
    c_ref[...] = _layernorm_silu(y, lg_ref[...], lb_ref[...]).astype(c_ref.dtype)
    so_ref[0, :, pl.ds(0, hist - 1), :] = s_ref[0, :, pl.ds(1, hist - 1), :]
    so_ref[0, :, pl.ds(hist - 1, 1), :] = u


def conv_step(a, b, state, conv_w, conv_b, ln_g, ln_b):
    bsz = a.shape[0]
    return pl.pallas_call(
        _conv_step_kernel,
        out_shape=[jax.ShapeDtypeStruct((bsz, 1, D_CONV), BF16), jax.ShapeDtypeStruct(state.shape, F32)],
        compiler_params=pltpu.CompilerParams(vmem_limit_bytes=VMEM_LIMIT),
        name="conv_step",
    )(a, b, state, conv_w, conv_b.reshape(1, D_CONV), ln_g.reshape(1, D_CONV), ln_b.reshape(1, D_CONV))


def _trunk_tail(x1, xo_in, w_xo, norm_ffn_g, w_rg, b_rg, w_re, b_re, w_ein, w_eout, norm_final_g, bm, routed):
    x2 = matmul_residual(xo_in, w_xo, x1, bm)
    if routed:
        return moe_routed(x2, norm_ffn_g, w_rg, b_rg, w_re, b_re, w_ein, w_eout, norm_final_g)
    xn3, gate = moe_router(x2, norm_ffn_g, w_rg, b_rg, w_re, b_re, bm)
    return moe_experts_final(xn3, gate, w_ein, w_eout, x2, norm_final_g, bm)


def kernel(x_prompt, x_sample, mem_prompt, cache_kv_g1, cache_kv_g2, cache_kv_g3, state_conv, cache_mem_kv,
           norm_mix_g, w_in, conv_w, conv_b, conv_ln_g, conv_ln_b, w_proj_a, w_proj_b, w_out,
           norm_xattn_g, norm_mem_g, w_xq, w_xkv, w_xo, norm_ffn_g,
           w_router_group, b_router_group, w_router_expert, b_router_expert, w_expert_in, w_expert_out,
           norm_final_g):
    depth = norm_mix_g.shape[0]
    assert depth == 1, "single-layer trunk"
    batch, seq, _ = x_prompt.shape
    dec_b, dec_t, _ = x_sample.shape
    assert dec_t == 1
    (g_mix, w_in, conv_w, conv_b, ln_g, ln_b, w_pa, w_pb, w_o, g_x, g_mem, w_xq, w_xkv, w_xo, g_ffn,
     w_rg, b_rg, w_re, b_re, w_ein, w_eout) = [t[0] for t in (
         norm_mix_g, w_in, conv_w, conv_b, conv_ln_g, conv_ln_b, w_proj_a, w_proj_b, w_out, norm_xattn_g,
         norm_mem_g, w_xq, w_xkv, w_xo, norm_ffn_g, w_router_group, b_router_group, w_router_expert,
         b_router_expert, w_expert_in, w_expert_out)]
    w_pa16, w_pb16, w_o16 = w_pa.astype(BF16), w_pb.astype(BF16), w_o.astype(BF16)
    w_ein, w_eout = w_ein.astype(BF16), w_eout.astype(BF16)
    caches = (cache_kv_g1, cache_kv_g2, cache_kv_g3)

    m_p = batch * seq
    xp = x_prompt.reshape(m_p, D_MODEL)
    za, gates = in_proj(xp, g_mix, w_in, bm=1024)
    o_a = band_attention(za, batch, seq)
    c, conv_prompt = conformer_conv(za, batch, seq, conv_w, conv_b, ln_g, ln_b)
    x1 = merge_branches(o_a, c, gates, xp, w_pa16, w_pb16, w_o16, tm=256)
    mkv = norm_matmul(mem_prompt.reshape(batch * N_MEM, D_MODEL), g_mem, w_xkv, F32, bm=1024)
    q = norm_matmul(x1, g_x, w_xq, BF16, bm=1024)
    xo_in = cross_attention_prompt(q, mkv, batch, seq)
    y_prompt = _trunk_tail(x1, xo_in, w_xo, g_ffn, w_rg, b_rg, w_re, b_re, w_ein, w_eout, norm_final_g,
                           bm=1024, routed=True)

    za3 = za.reshape(batch, seq, ZA_W)
    kv_prompt = []
    for g, (win, _) in enumerate(ATT_GROUPS):
        keep = min(win, seq)
        k = za3[:, seq - keep:, ATT_WIDTH + g * GROUP_W:ATT_WIDTH + (g + 1) * GROUP_W]
        v = za3[:, seq - keep:, 2 * ATT_WIDTH + g * GROUP_W:2 * ATT_WIDTH + (g + 1) * GROUP_W]
        kv_prompt.append(jnp.stack([k, v], axis=2).reshape(1, batch, keep, 2, HEADS, HEAD_DIM))
    conv_prompt = conv_prompt[None]
    mem_kv_prompt = mkv.reshape(1, batch, N_MEM, 2, X_HEADS, X_HEAD_DIM)

    xs = x_sample.reshape(dec_b, D_MODEL)
    zs, gates_s = in_proj(xs, g_mix, w_in, bm=dec_b)
    z4 = zs.reshape(dec_b, ZA_W // HEAD_DIM // HEADS, HEADS, HEAD_DIM)
    o_as = window_decode(z4, caches).reshape(dec_b, GROUP_W)
    kv_sample = shift_caches(z4, caches)
    a_s = zs[:, 3 * ATT_WIDTH:3 * ATT_WIDTH + D_CONV].reshape(dec_b, 1, D_CONV)
    b_s = zs[:, 3 * ATT_WIDTH + D_CONV:].reshape(dec_b, 1, D_CONV)
    c_s, conv_sample = conv_step(a_s, b_s, state_conv, conv_w, conv_b, ln_g, ln_b)
    x1s = merge_branches(o_as, c_s.reshape(dec_b, D_CONV), gates_s, xs, w_pa16, w_pb16, w_o16, tm=dec_b)
    q_s = norm_matmul(x1s, g_x, w_xq, BF16, bm=dec_b)
    xo_s = cross_attention_decode(q_s, cache_mem_kv)
    y_sample = _trunk_tail(x1s, xo_s, w_xo, g_ffn, w_rg, b_rg, w_re, b_re, w_ein, w_eout, norm_final_g,
                           bm=dec_b, routed=False)

    return (y_prompt.reshape(batch, seq, D_MODEL), y_sample.reshape(dec_b, 1, D_MODEL),
            kv_prompt[0], kv_prompt[1], kv_prompt[2], conv_prompt, mem_kv_prompt,
            kv_sample[0], kv_sample[1], kv_sample[2], conv_sample)
```

```python
import functools

import jax
import jax.numpy as jnp
from jax import lax
from jax.experimental import pallas as pl
from jax.experimental.pallas import tpu as pltpu

F32 = jnp.float32
BF16 = jnp.bfloat16

D_MODEL = 2048
ATT_GROUPS = ((128, 1), (512, 4), (2048, 16))
HEADS = 4
HEAD_DIM = 128
GROUP_W = HEADS * HEAD_DIM
ATT_WIDTH = len(ATT_GROUPS) * GROUP_W
D_CONV = 1536
CONV_K = 31
N_MEM = 256
X_HEADS = 4
X_HEAD_DIM = D_MODEL // X_HEADS
N_EXPERT_GROUPS = 4
EXPERTS_PER_GROUP = 4
N_EXPERTS = 16
D_EXPERT = 512
ZA_W = 3 * ATT_WIDTH + 2 * D_CONV
GATE_W = 2 * D_MODEL
RMS_EPS = 1e-6
LN_EPS = 1e-5
NEG_INF = -1e30
BAND = 128
LANES = 128
ROUTER_GROUP_LANE = N_EXPERTS

VMEM_LIMIT = 56 * 1024 * 1024


def _params(*sem):
    return pltpu.CompilerParams(dimension_semantics=sem, vmem_limit_bytes=VMEM_LIMIT)


def _rms_rows(x, g):
    ms = jnp.mean(x * x, axis=-1, keepdims=True)
    return x * lax.rsqrt(ms + RMS_EPS) * g


def _store_normed(x_ref, g_ref, xn_ref, chunk=256):
    rows = x_ref.shape[0]
    step = min(chunk, rows)
    for r0 in range(0, rows, step):
        xn_ref[r0:r0 + step, :] = _rms_rows(x_ref[r0:r0 + step, :], g_ref[...]).astype(xn_ref.dtype)


def _norm_matmul_kernel(x_ref, g_ref, w_ref, o_ref, xn_ref):
    @pl.when(pl.program_id(1) == 0)
    def _():
        _store_normed(x_ref, g_ref, xn_ref)

    o_ref[...] = jnp.dot(xn_ref[...], w_ref[...], preferred_element_type=F32).astype(o_ref.dtype)


def norm_matmul(x, g, w, out_dtype, bm, bn=512):
    m, k = x.shape
    n = w.shape[1]
    bm = min(bm, m)
    return pl.pallas_call(
        _norm_matmul_kernel,
        grid=(m // bm, n // bn),
        in_specs=[pl.BlockSpec((bm, k), lambda i, j: (i, 0)),
                  pl.BlockSpec((1, k), lambda i, j: (0, 0)),
                  pl.BlockSpec((k, bn), lambda i, j: (0, j))],
        out_specs=pl.BlockSpec((bm, bn), lambda i, j: (i, j)),
        out_shape=jax.ShapeDtypeStruct((m, n), out_dtype),
        scratch_shapes=[pltpu.VMEM((bm, k), BF16)],
        compiler_params=_params("parallel", "arbitrary"),
        name="norm_matmul",
    )(x, g.reshape(1, k), w)


def _in_proj_kernel(x_ref, g_ref, w_ref, za_ref, gt_ref, xn_ref, *, n_za):
    j = pl.program_id(1)

    @pl.when(j == 0)
    def _():
        _store_normed(x_ref, g_ref, xn_ref)

    r = jnp.dot(xn_ref[...], w_ref[...], preferred_element_type=F32)

    @pl.when(j < n_za)
    def _():
        za_ref[...] = r

    @pl.when(j >= n_za)
    def _():
        gt_ref[...] = r


def in_proj(x, g, w, bm, bn=512):
    m, k = x.shape
    bm = min(bm, m)
    n_za = ZA_W // bn
    n_gt = GATE_W // bn
    return pl.pallas_call(
        functools.partial(_in_proj_kernel, n_za=n_za),
        grid=(m // bm, n_za + n_gt),
        in_specs=[pl.BlockSpec((bm, k), lambda i, j: (i, 0)),
                  pl.BlockSpec((1, k), lambda i, j: (0, 0)),
                  pl.BlockSpec((k, bn), lambda i, j: (0, j))],
        out_specs=[pl.BlockSpec((bm, bn), lambda i, j: (i, jnp.minimum(j, n_za - 1))),
                   pl.BlockSpec((bm, bn), lambda i, j: (i, jnp.maximum(j - n_za, 0)))],
        out_shape=[jax.ShapeDtypeStruct((m, ZA_W), F32), jax.ShapeDtypeStruct((m, GATE_W), F32)],
        scratch_shapes=[pltpu.VMEM((bm, k), BF16)],
        compiler_params=_params("parallel", "arbitrary"),
        name="in_proj",
    )(x, g.reshape(1, k), w)


def _matmul_res_kernel(x_ref, w_ref, r_ref, o_ref):
    o_ref[...] = r_ref[...] + jnp.dot(x_ref[...], w_ref[...], preferred_element_type=F32)


def matmul_residual(x, w, res, bm, bn=512):
    m, k = x.shape
    n = w.shape[1]
    bm = min(bm, m)
    return pl.pallas_call(
        _matmul_res_kernel,
        grid=(m // bm, n // bn),
        in_specs=[pl.BlockSpec((bm, k), lambda i, j: (i, 0)),
                  pl.BlockSpec((k, bn), lambda i, j: (0, j)),
                  pl.BlockSpec((bm, bn), lambda i, j: (i, j))],
        out_specs=pl.BlockSpec((bm, bn), lambda i, j: (i, j)),
        out_shape=jax.ShapeDtypeStruct((m, n), F32),
        compiler_params=_params("parallel", "arbitrary"),
        name="matmul_residual",
    )(x, w, res)


ATT_R = 2048
ATT_UNROLL = 8


def _band_attn_kernel(*refs):
    n_g = len(ATT_GROUPS)
    in_refs = refs[:5 * n_g]
    o_ref = refs[5 * n_g]
    scr = refs[5 * n_g + 1:]
    kbufs, vbufs = scr[:n_g], scr[n_g:2 * n_g]
    obufs, mbufs, dbufs = scr[2 * n_g:3 * n_g], scr[3 * n_g:4 * n_g], scr[4 * n_g:5 * n_g]
    first_chunk = pl.program_id(1) == 0
    row = lax.broadcasted_iota(jnp.int32, (BAND, 2 * BAND), 0)
    col = lax.broadcasted_iota(jnp.int32, (BAND, 2 * BAND), 1)
    in_band = (col >= row) & (col <= row + BAND)
    in_cur = col >= BAND
    scale = HEAD_DIM ** -0.5
    nt = (((1,), (1,)), ((), ()))

    for g, (win, dil) in enumerate(ATT_GROUPS):
        q_ref, kc_ref, vc_ref, kp_ref, vp_ref = in_refs[5 * g:5 * g + 5]
        kbuf, vbuf, obuf, mbuf, dbuf = kbufs[g], vbufs[g], obufs[g], mbufs[g], dbufs[g]
        kbuf[0:win, :] = kp_ref[...]
        kbuf[win:win + ATT_R, :] = kc_ref[...]
        vbuf[0:win, :] = vp_ref[...]
        vbuf[win:win + ATT_R, :] = vc_ref[...]
        shift = dil.bit_length() - 1

        def sub_blocks(i4, carry, q_ref=q_ref, kbuf=kbuf, vbuf=vbuf, obuf=obuf, mbuf=mbuf, dbuf=dbuf,
                       win=win, dil=dil, shift=shift):
            done = []
            for u in range(ATT_UNROLL):
                i = i4 * ATT_UNROLL + u
                span = lax.shift_right_logical(i, shift)
                base = span * win + (i & (dil - 1))
                if dil == 1:
                    q_rows, kv_rows = pl.ds(base, BAND), pl.ds(base, 2 * BAND)
                else:
                    q_rows, kv_rows = pl.ds(base, BAND, stride=dil), pl.ds(base, 2 * BAND, stride=dil)
                q = q_ref[q_rows, :].astype(BF16)
                k = kbuf[kv_rows, :].astype(BF16)
                v = vbuf[kv_rows, :].astype(BF16)
                s = lax.dot_general(q, k, nt, preferred_element_type=F32) * scale
                has_prev = jnp.logical_not(first_chunk & (span == 0))
                s = jnp.where(in_band & (in_cur | has_prev), s, NEG_INF)
                m = jnp.max(s, axis=-1, keepdims=True)
                p = jnp.exp(s - m)
                den = jnp.sum(p, axis=-1, keepdims=True)
                done.append((q_rows, jnp.dot(p.astype(BF16), v, preferred_element_type=F32), m, den))
            for q_rows, o, m, den in done:
                obuf[q_rows, :] = o
                mbuf[q_rows, :] = jnp.broadcast_to(m, (BAND, HEAD_DIM))
                dbuf[q_rows, :] = jnp.broadcast_to(den, (BAND, HEAD_DIM))
            return carry

        lax.fori_loop(0, ATT_R // BAND // ATT_UNROLL, sub_blocks, 0)

    chunk = 256
    for r0 in range(0, ATT_R, chunk):
        rs = slice(r0, r0 + chunk)
        ms = [mb[rs, :] for mb in mbufs]
        m_all = functools.reduce(jnp.maximum, ms)
        ws = [jnp.exp(m - m_all) for m in ms]
        num = sum(w * ob[rs, :] for w, ob in zip(ws, obufs))
        den = sum(w * db[rs, :] for w, db in zip(ws, dbufs))
        o_ref[rs, :] = (num / den).astype(o_ref.dtype)


def band_attention(za, batch, seq):
    assert seq % ATT_R == 0
    nch = seq // ATT_R
    in_specs, scratch = [], []
    for g, (win, dil) in enumerate(ATT_GROUPS):
        assert win // dil == BAND and ATT_R % win == 0
        cols = [(part * len(ATT_GROUPS) + g) * HEADS for part in range(3)]
        cur = lambda c: pl.BlockSpec((ATT_R, HEAD_DIM), lambda b, ch, h, c=c: (b * nch + ch, c + h))
        prev = lambda c, win=win: pl.BlockSpec(
            (win, HEAD_DIM), lambda b, ch, h, c=c, win=win: (jnp.maximum((b * seq + ch * ATT_R) // win - 1, 0), c + h))
        in_specs += [cur(cols[0]), cur(cols[1]), cur(cols[2]), prev(cols[1]), prev(cols[2])]
    for _ in range(2):
        scratch += [pltpu.VMEM((win + ATT_R, HEAD_DIM), F32) for win, _ in ATT_GROUPS]
    scratch += [pltpu.VMEM((ATT_R, HEAD_DIM), F32)] * (3 * len(ATT_GROUPS))
    return pl.pallas_call(
        _band_attn_kernel,
        grid=(batch, nch, HEADS),
        in_specs=in_specs,
        out_specs=pl.BlockSpec((ATT_R, HEAD_DIM), lambda b, ch, h: (b * nch + ch, h)),
        out_shape=jax.ShapeDtypeStruct((batch * seq, GROUP_W), BF16),
        scratch_shapes=scratch,
        compiler_params=_params("parallel", "parallel", "parallel"),
        name="band_attention",
    )(*([za] * (5 * len(ATT_GROUPS))))


CONV_T = 256
CONV_HIST = 32
CONV_RC = 128
CONV_PHASES = 4
CONV_SLABS = D_CONV // LANES


def _layernorm_silu(y, g, b):
    mu = jnp.mean(y, axis=-1, keepdims=True)
    yc = y - mu
    var = jnp.mean(yc * yc, axis=-1, keepdims=True)
    yn = yc * lax.rsqrt(var + LN_EPS) * g + b
    return yn * jax.nn.sigmoid(yn)


def _conv_kernel(a_ref, b_ref, w_ref, cb_ref, lg_ref, lb_ref, c_ref, tail_ref, ubuf, ybuf):
    t = pl.program_id(1)

    @pl.when(t == 0)
    def _():
        ubuf[:, 0:CONV_HIST, :] = jnp.zeros((CONV_SLABS, CONV_HIST, LANES), F32)

    @pl.when(t > 0)
    def _():
        ubuf[:, 0:CONV_HIST, :] = ubuf[:, CONV_T:CONV_T + CONV_HIST, :]

    first = CONV_HIST - (CONV_K - 1)
    n = CONV_RC // CONV_PHASES
    total = jnp.zeros((CONV_T, 1), F32)
    for j in range(CONV_SLABS):
        js = slice(j * LANES, (j + 1) * LANES)
        u = a_ref[:, js] * jax.nn.sigmoid(b_ref[:, js])
        ubuf[j, CONV_HIST:CONV_HIST + CONV_T, :] = u
        tail_ref[0, :, js] = u[CONV_T - CONV_HIST:, :]
        for r0 in range(0, CONV_T, CONV_RC):
            accs = [jnp.broadcast_to(cb_ref[:, js], (n, LANES))] * CONV_PHASES
            for s in range(first, first + CONV_K + CONV_PHASES - 1):
                rows = ubuf[j, pl.ds(r0 + s, n, stride=CONV_PHASES), :]
                for p in range(CONV_PHASES):
                    k = s - first - p
                    if 0 <= k < CONV_K:
                        accs[p] = accs[p] + w_ref[k:k + 1, js] * rows
            for p in range(CONV_PHASES):
                ybuf[j, pl.ds(r0 + p, n, stride=CONV_PHASES), :] = accs[p]
        total = total + jnp.sum(ybuf[j], axis=-1, keepdims=True)
    mu = total * (1.0 / D_CONV)
    sq = jnp.zeros((CONV_T, 1), F32)
    for j in range(CONV_SLABS):
        yc = ybuf[j] - mu
        sq = sq + jnp.sum(yc * yc, axis=-1, keepdims=True)
    inv = lax.rsqrt(sq * (1.0 / D_CONV) + LN_EPS)
    for j in range(CONV_SLABS):
        js = slice(j * LANES, (j + 1) * LANES)
        yn = (ybuf[j] - mu) * inv * lg_ref[:, js] + lb_ref[:, js]
        c_ref[:, js] = (yn * jax.nn.sigmoid(yn)).astype(c_ref.dtype)


def conformer_conv(za, batch, seq, conv_w, conv_b, ln_g, ln_b):
    nt = seq // CONV_T
    row = lambda b, t: (b * nt + t, 0)
    const = lambda b, t: (0, 0)
    c, tail = pl.pallas_call(
        _conv_kernel,
        grid=(batch, nt),
        in_specs=[pl.BlockSpec((CONV_T, D_CONV), lambda b, t: (b * nt + t, 3 * ATT_WIDTH // D_CONV)),
                  pl.BlockSpec((CONV_T, D_CONV), lambda b, t: (b * nt + t, 3 * ATT_WIDTH // D_CONV + 1)),
                  pl.BlockSpec((CONV_K, D_CONV), const),
                  pl.BlockSpec((1, D_CONV), const),
                  pl.BlockSpec((1, D_CONV), const),
                  pl.BlockSpec((1, D_CONV), const)],
        out_specs=[pl.BlockSpec((CONV_T, D_CONV), row),
                   pl.BlockSpec((1, CONV_HIST, D_CONV), lambda b, t: (b, 0, 0))],
        out_shape=[jax.ShapeDtypeStruct((batch * seq, D_CONV), BF16),
                   jax.ShapeDtypeStruct((batch, CONV_HIST, D_CONV), F32)],
        scratch_shapes=[pltpu.VMEM((CONV_SLABS, CONV_HIST + CONV_T, LANES), F32),
                        pltpu.VMEM((CONV_SLABS, CONV_T, LANES), F32)],
        compiler_params=_params("parallel", "arbitrary"),
        name="conformer_conv",
    )(za, za, conv_w, conv_b.reshape(1, D_CONV), ln_g.reshape(1, D_CONV), ln_b.reshape(1, D_CONV))
    return c, tail[:, CONV_HIST - (CONV_K - 1):]


def _merge_kernel(oa_ref, c_ref, ga_ref, gb_ref, x_ref, wpa_ref, wpb_ref, wo_ref, y_ref):
    ta = jnp.dot(oa_ref[...], wpa_ref[...], preferred_element_type=F32)
    tb = jnp.dot(c_ref[...], wpb_ref[...], preferred_element_type=F32)
    hmix = jax.nn.sigmoid(ga_ref[...]) * ta + jax.nn.sigmoid(gb_ref[...]) * tb
    y_ref[...] = x_ref[...] + jnp.dot(hmix.astype(BF16), wo_ref[...], preferred_element_type=F32)


def merge_branches(o_a, c, gates, x, w_pa, w_pb, w_out, tm):
    m = x.shape[0]
    tm = min(tm, m)
    row = lambda i: (i, 0)
    const = lambda i: (0, 0)
    once = dict(pipeline_mode=pl.Buffered(1))
    return pl.pallas_call(
        _merge_kernel,
        grid=(m // tm,),
        in_specs=[
            pl.BlockSpec((tm, GROUP_W), row),
            pl.BlockSpec((tm, D_CONV), row),
            pl.BlockSpec((tm, D_MODEL), lambda i: (i, 0)),
            pl.BlockSpec((tm, D_MODEL), lambda i: (i, 1)),
            pl.BlockSpec((tm, D_MODEL), row),
            pl.BlockSpec((GROUP_W, D_MODEL), const, **once),
            pl.BlockSpec((D_CONV, D_MODEL), const, **once),
            pl.BlockSpec((D_MODEL, D_MODEL), const, **once)],
        out_specs=pl.BlockSpec((tm, D_MODEL), row),
        out_shape=jax.ShapeDtypeStruct((m, D_MODEL), F32),
        compiler_params=_params("parallel"),
        name="merge_branches",
    )(o_a, c, gates, gates, x, w_pa, w_pb, w_out)


def _xattn_kernel(q_ref, kv_ref, o_ref):
    scale = X_HEAD_DIM ** -0.5
    nt = (((1,), (1,)), ((), ()))
    for h in range(X_HEADS):
        hs = slice(h * X_HEAD_DIM, (h + 1) * X_HEAD_DIM)
        vs = slice(D_MODEL + h * X_HEAD_DIM, D_MODEL + (h + 1) * X_HEAD_DIM)
        s = lax.dot_general(q_ref[:, hs], kv_ref[:, hs].astype(BF16), nt, preferred_element_type=F32) * scale
        m = jnp.max(s, axis=-1, keepdims=True)
        p = jnp.exp(s - m)
        p = p / jnp.sum(p, axis=-1, keepdims=True)
        o = jnp.dot(p.astype(BF16), kv_ref[:, vs].astype(BF16), preferred_element_type=F32)
        o_ref[:, hs] = o.astype(o_ref.dtype)


def cross_attention_prompt(q, mkv, batch, seq, tm=512):
    nt = seq // tm
    return pl.pallas_call(
        _xattn_kernel,
        grid=(batch, nt),
        in_specs=[pl.BlockSpec((tm, D_MODEL), lambda b, t: (b * nt + t, 0)),
                  pl.BlockSpec((N_MEM, 2 * D_MODEL), lambda b, t: (b, 0))],
        out_specs=pl.BlockSpec((tm, D_MODEL), lambda b, t: (b * nt + t, 0)),
        out_shape=jax.ShapeDtypeStruct((batch * seq, D_MODEL), BF16),
        compiler_params=_params("parallel", "arbitrary"),
        name="cross_attention_prompt",
    )(q, mkv)


def _xattn_decode_kernel(q_ref, kv_ref, o_ref):
    scale = X_HEAD_DIM ** -0.5
    q = q_ref[0].astype(F32)
    k = kv_ref[0, 0, :, 0]
    v = kv_ref[0, 0, :, 1]
    s = jnp.sum(k * q, axis=-1, keepdims=True) * scale
    m = jnp.max(s, axis=0, keepdims=True)
    p = jnp.exp(s - m)
    p = p / jnp.sum(p, axis=0, keepdims=True)
    o_ref[0] = jnp.sum(p * v, axis=0, keepdims=True).astype(o_ref.dtype)


def cross_attention_decode(q, cache_mem_kv):
    b = q.shape[0]
    q4 = q.reshape(b, 1, X_HEADS, X_HEAD_DIM)
    o = pl.pallas_call(
        _xattn_decode_kernel,
        grid=(b,),
        in_specs=[pl.BlockSpec((1, 1, X_HEADS, X_HEAD_DIM), lambda i: (i, 0, 0, 0)),
                  pl.BlockSpec((1, 1, N_MEM, 2, X_HEADS, X_HEAD_DIM), lambda i: (0, i, 0, 0, 0, 0))],
        out_specs=pl.BlockSpec((1, 1, X_HEADS, X_HEAD_DIM), lambda i: (i, 0, 0, 0)),
        out_shape=jax.ShapeDtypeStruct((b, 1, X_HEADS, X_HEAD_DIM), BF16),
        compiler_params=_params("parallel"),
        name="cross_attention_decode",
    )(q4, cache_mem_kv)
    return o.reshape(b, D_MODEL)


N_BUCKETS = N_EXPERT_GROUPS * 6
PAIR_LO = (0, 0, 0, 1, 1, 2)
PAIR_HI = (1, 2, 3, 2, 3, 3)
ROW_TILES = D_MODEL // LANES
PAYLOAD_ROWS = ROW_TILES + 8
MOE_TM = 256


def _route(logits):
    lane = lax.broadcasted_iota(jnp.int32, logits.shape, 1)
    lanef = lane.astype(F32)
    big = float(LANES)
    is_group = (lane >= ROUTER_GROUP_LANE) & (lane < ROUTER_GROUP_LANE + N_EXPERT_GROUPS)
    lg = jnp.where(is_group, logits, -jnp.inf)
    mg = jnp.max(lg, axis=-1, keepdims=True)
    p_sel = 1.0 / jnp.sum(jnp.exp(lg - mg), axis=-1, keepdims=True)
    gsel = jnp.min(jnp.where(lg == mg, lanef, big), axis=-1, keepdims=True) - ROUTER_GROUP_LANE
    group_of_lane = lax.shift_right_logical(lane, 2)
    in_group = (lane < N_EXPERTS) & (group_of_lane == gsel.astype(jnp.int32))
    le = jnp.where(in_group, logits, -jnp.inf)
    v1 = jnp.max(le, axis=-1, keepdims=True)
    i1 = jnp.min(jnp.where(le == v1, lanef, big), axis=-1, keepdims=True)
    le2 = jnp.where(lanef == i1, -jnp.inf, le)
    v2 = jnp.max(le2, axis=-1, keepdims=True)
    i2 = jnp.min(jnp.where(le2 == v2, lanef, big), axis=-1, keepdims=True)
    t = jnp.exp(v2 - v1)
    tot = 1.0 + t
    return gsel, i1, i2, (1.0 / tot) * p_sel, (t / tot) * p_sel


def _router_routed_kernel(x_ref, g_ref, w_ref, b_ref, xd_ref, meta_ref, cnt_ref, carry_ref):
    tm = x_ref.shape[0]

    @pl.when(pl.program_id(0) == 0)
    def _():
        carry_ref[...] = jnp.zeros_like(carry_ref)

    xn = _rms_rows(x_ref[...], g_ref[...])
    logits = jnp.dot(xn, w_ref[...], preferred_element_type=F32, precision=lax.Precision.HIGHEST) + b_ref[...]
    gsel, i1, i2, p1, p2 = _route(logits)
    lo = jnp.minimum(i1, i2) - EXPERTS_PER_GROUP * gsel
    hi = jnp.maximum(i1, i2) - EXPERTS_PER_GROUP * gsel
    bucket = gsel * 6.0 + lo * (7.0 - lo) * 0.5 + (hi - lo - 1.0)
    gate_lo = jnp.where(i1 < i2, p1, p2)
    gate_hi = jnp.where(i1 < i2, p2, p1)

    lane = lax.broadcasted_iota(jnp.int32, (tm, LANES), 1)
    onehot = (lane.astype(F32) == bucket).astype(F32)
    r_i = lax.broadcasted_iota(jnp.int32, (tm, tm), 0)
    c_i = lax.broadcasted_iota(jnp.int32, (tm, tm), 1)
    before = (c_i < r_i).astype(BF16)
    rank_local = jnp.dot(before, onehot.astype(BF16), preferred_element_type=F32)
    rank = jnp.sum(onehot * (rank_local + carry_ref[0:1, :]), axis=-1, keepdims=True)
    carry_ref[0:1, :] = carry_ref[0:1, :] + jnp.sum(onehot, axis=0, keepdims=True)
    cnt_ref[...] = jnp.broadcast_to(carry_ref[0:1, :], cnt_ref.shape)
    meta_ref[...] = jnp.where(lane == 0, bucket, jnp.where(lane == 1, rank, 0.0))

    for j in range(ROW_TILES):
        xd_ref[pl.ds(j, tm, stride=PAYLOAD_ROWS), :] = xn[:, j * LANES:(j + 1) * LANES]
    xd_ref[pl.ds(ROW_TILES, tm, stride=PAYLOAD_ROWS), :] = jnp.where(
        lane == 0, gate_lo, jnp.where(lane == 1, gate_hi, 0.0))
    for j in range(ROW_TILES + 1, PAYLOAD_ROWS):
        xd_ref[pl.ds(j, tm, stride=PAYLOAD_ROWS), :] = jnp.zeros((tm, LANES), F32)


def _router_weights(w_rg, b_rg, w_re, b_re):
    pad = LANES - N_EXPERTS - N_EXPERT_GROUPS
    w = jnp.concatenate([w_re, w_rg, jnp.zeros((D_MODEL, pad), F32)], axis=1)
    b = jnp.concatenate([b_re, b_rg, jnp.zeros((pad,), F32)]).reshape(1, LANES)
    return w, b


def moe_router_routed(x, g, w_rg, b_rg, w_re, b_re, tm=512):
    m = x.shape[0]
    w, b = _router_weights(w_rg, b_rg, w_re, b_re)
    row = lambda i: (i, 0)
    const = lambda i: (0, 0)
    return pl.pallas_call(
        _router_routed_kernel,
        grid=(m // tm,),
        in_specs=[pl.BlockSpec((tm, D_MODEL), row), pl.BlockSpec((1, D_MODEL), const),
                  pl.BlockSpec((D_MODEL, LANES), const), pl.BlockSpec((1, LANES), const)],
        out_specs=[pl.BlockSpec((tm * PAYLOAD_ROWS, LANES), row), pl.BlockSpec((tm, LANES), row),
                   pl.BlockSpec((8, LANES), const)],
        out_shape=[jax.ShapeDtypeStruct((m * PAYLOAD_ROWS, LANES), F32), jax.ShapeDtypeStruct((m, LANES), F32),
                   jax.ShapeDtypeStruct((8, LANES), F32)],
        scratch_shapes=[pltpu.VMEM((8, LANES), F32)],
        compiler_params=_params("arbitrary"),
        name="moe_router_routed",
    )(x, g.reshape(1, D_MODEL), w, b)


INV_CHUNK = 2048


def _inverse_kernel(dest_ref, inv_ref):
    step = pl.program_id(0)

    @pl.when(step == 0)
    def _():
        def clear(k, c):
            inv_ref[k] = 0
            return c
        lax.fori_loop(0, inv_ref.shape[0], clear, 0, unroll=8)

    def put(t, c):
        inv_ref[dest_ref[0, 0, t]] = step * INV_CHUNK + t
        return c
    lax.fori_loop(0, INV_CHUNK, put, 0, unroll=8)


def invert_slots(dest, n_slots):
    m = dest.shape[0]
    return pl.pallas_call(
        _inverse_kernel,
        grid=(m // INV_CHUNK,),
        in_specs=[pl.BlockSpec((1, 1, INV_CHUNK), lambda i: (i, 0, 0), memory_space=pltpu.SMEM)],
        out_specs=pl.BlockSpec(memory_space=pltpu.SMEM),
        out_shape=jax.ShapeDtypeStruct((n_slots,), jnp.int32),
        compiler_params=_params("arbitrary"),
        name="invert_slots",
    )(dest.reshape(m // INV_CHUNK, 1, INV_CHUNK))


def _start_row_gather(idx_ref, src_hbm, dst, sem, n_items, rows):
    def start(r, c):
        first = pl.multiple_of(idx_ref[0, 0, r] * rows, 8)
        pltpu.make_async_copy(src_hbm.at[pl.ds(first, rows)], dst.at[pl.ds(r * rows, rows)], sem).start()
        return c
    lax.fori_loop(0, n_items, start, 0, unroll=8)


def _wait_row_gather(src_hbm, dst, sem, n_items, rows):
    def wait(r, c):
        pltpu.make_async_copy(src_hbm.at[pl.ds(0, rows)], dst.at[pl.ds(r * rows, rows)], sem).wait()
        return c
    lax.fori_loop(0, n_items, wait, 0, unroll=8)


def _moe_routed_kernel(ea_ref, eb_ref, nv_ref, inv_ref, inv_next_ref, xd_hbm, w1a_ref, w1b_ref, w2a_ref, w2b_ref,
                       ys_ref, xbuf, sems, x_scr):
    del ea_ref, eb_ref
    i = pl.program_id(0)
    n_valid = nv_ref[0]
    slot = i % 2
    tm = MOE_TM

    @pl.when(i == 0)
    def _():
        _start_row_gather(inv_ref, xd_hbm, xbuf.at[0], sems.at[0], tm, PAYLOAD_ROWS)

    @pl.when(i + 1 < n_valid)
    def _():
        _start_row_gather(inv_next_ref, xd_hbm, xbuf.at[1 - slot], sems.at[1 - slot], tm, PAYLOAD_ROWS)

    @pl.when(i < n_valid)
    def _():
        buf = xbuf.at[slot]
        _wait_row_gather(xd_hbm, buf, sems.at[slot], tm, PAYLOAD_ROWS)
        for j in range(ROW_TILES):
            x_scr[:, j * LANES:(j + 1) * LANES] = buf[pl.ds(j, tm, stride=PAYLOAD_ROWS), :].astype(BF16)
        gates = buf[pl.ds(ROW_TILES, tm, stride=PAYLOAD_ROWS), :]
        x = x_scr[...]
        hu = jnp.dot(x, w1a_ref[0], preferred_element_type=F32)
        ha = jax.nn.silu(hu[:, :D_EXPERT]) * hu[:, D_EXPERT:] * gates[:, 0:1]
        hu = jnp.dot(x, w1b_ref[0], preferred_element_type=F32)
        hb = jax.nn.silu(hu[:, :D_EXPERT]) * hu[:, D_EXPERT:] * gates[:, 1:2]
        y = (jnp.dot(ha.astype(BF16), w2a_ref[0], preferred_element_type=F32)
             + jnp.dot(hb.astype(BF16), w2b_ref[0], preferred_element_type=F32))
        for j in range(ROW_TILES):
            ys_ref[pl.ds(j, tm, stride=ROW_TILES), :] = y[:, j * LANES:(j + 1) * LANES]

    @pl.when(i >= n_valid)
    def _():
        ys_ref[...] = jnp.zeros_like(ys_ref)


def moe_experts_routed(xd, inv, tile_ea, tile_eb, n_valid, w_ein, w_eout):
    n_tiles = tile_ea.shape[0]
    tm = MOE_TM
    inv3 = inv.reshape(n_tiles, 1, tm)
    smem = dict(memory_space=pltpu.SMEM)
    grid_spec = pltpu.PrefetchScalarGridSpec(
        num_scalar_prefetch=3,
        grid=(n_tiles,),
        in_specs=[
            pl.BlockSpec((1, 1, tm), lambda i, ea, eb, nv: (i, 0, 0), **smem),
            pl.BlockSpec((1, 1, tm), lambda i, ea, eb, nv: (jnp.minimum(i + 1, n_tiles - 1), 0, 0), **smem),
            pl.BlockSpec(memory_space=pl.ANY),
            pl.BlockSpec((1, D_MODEL, 2 * D_EXPERT), lambda i, ea, eb, nv: (ea[i], 0, 0)),
            pl.BlockSpec((1, D_MODEL, 2 * D_EXPERT), lambda i, ea, eb, nv: (eb[i], 0, 0)),
            pl.BlockSpec((1, D_EXPERT, D_MODEL), lambda i, ea, eb, nv: (ea[i], 0, 0)),
            pl.BlockSpec((1, D_EXPERT, D_MODEL), lambda i, ea, eb, nv: (eb[i], 0, 0))],
        out_specs=pl.BlockSpec((tm * ROW_TILES, LANES), lambda i, ea, eb, nv: (i, 0)),
        scratch_shapes=[pltpu.VMEM((2, tm * PAYLOAD_ROWS, LANES), F32), pltpu.SemaphoreType.DMA((2,)),
                        pltpu.VMEM((tm, D_MODEL), BF16)])
    return pl.pallas_call(
        _moe_routed_kernel,
        grid_spec=grid_spec,
        out_shape=jax.ShapeDtypeStruct((n_tiles * tm * ROW_TILES, LANES), F32),
        compiler_params=_params("arbitrary"),
        name="moe_experts_routed",
    )(tile_ea, tile_eb, n_valid, inv3, inv3, xd, w_ein, w_ein, w_eout, w_eout)


def _combine_kernel(dest_ref, dest_next_ref, ys_hbm, res_ref, gf_ref, y_ref, buf, sems):
    i = pl.program_id(0)
    slot = i % 2
    tm = res_ref.shape[0]

    @pl.when(i == 0)
    def _():
        _start_row_gather(dest_ref, ys_hbm, buf.at[0], sems.at[0], tm, ROW_TILES)

    @pl.when(i + 1 < pl.num_programs(0))
    def _():
        _start_row_gather(dest_next_ref, ys_hbm, buf.at[1 - slot], sems.at[1 - slot], tm, ROW_TILES)

    cur = buf.at[slot]
    _wait_row_gather(ys_hbm, cur, sems.at[slot], tm, ROW_TILES)
    for j in range(ROW_TILES):
        js = slice(j * LANES, (j + 1) * LANES)
        y_ref[:, js] = res_ref[:, js] + cur[pl.ds(j, tm, stride=ROW_TILES), :]
    y_ref[...] = _rms_rows(y_ref[...], gf_ref[...])


def moe_combine_final(ys, dest, res, g_final, tm=256):
    m = res.shape[0]
    nt = m // tm
    dest3 = dest.reshape(nt, 1, tm)
    smem = dict(memory_space=pltpu.SMEM)
    return pl.pallas_call(
        _combine_kernel,
        grid=(nt,),
        in_specs=[pl.BlockSpec((1, 1, tm), lambda i: (i, 0, 0), **smem),
                  pl.BlockSpec((1, 1, tm), lambda i: (jnp.minimum(i + 1, nt - 1), 0, 0), **smem),
                  pl.BlockSpec(memory_space=pl.ANY),
                  pl.BlockSpec((tm, D_MODEL), lambda i: (i, 0)),
                  pl.BlockSpec((1, D_MODEL), lambda i: (0, 0))],
        out_specs=pl.BlockSpec((tm, D_MODEL), lambda i: (i, 0)),
        out_shape=jax.ShapeDtypeStruct((m, D_MODEL), F32),
        scratch_shapes=[pltpu.VMEM((2, tm * ROW_TILES, LANES), F32), pltpu.SemaphoreType.DMA((2,))],
        compiler_params=_params("arbitrary"),
        name="moe_combine_final",
    )(dest3, dest3, ys, res, g_final.reshape(1, D_MODEL))


def moe_routed(x, g_ffn, w_rg, b_rg, w_re, b_re, w_ein, w_eout, g_final):
    m = x.shape[0]
    tm = MOE_TM
    n_tiles = m // tm + N_BUCKETS
    xd, meta, cnt = moe_router_routed(x, g_ffn, w_rg, b_rg, w_re, b_re)
    bucket = meta[:, 0].astype(jnp.int32)
    rank = meta[:, 1].astype(jnp.int32)
    counts = cnt[0, :N_BUCKETS].astype(jnp.int32)
    tiles_per_bucket = (counts + tm - 1) // tm
    tile_end = jnp.cumsum(tiles_per_bucket)
    tile_start = tile_end - tiles_per_bucket
    dest = jnp.take(tile_start, bucket) * tm + rank
    n_valid = tile_end[-1]
    tile_id = jnp.arange(n_tiles, dtype=jnp.int32)
    tile_bucket = jnp.searchsorted(tile_end, jnp.minimum(tile_id, n_valid - 1), side="right").astype(jnp.int32)
    group = tile_bucket // 6
    tile_ea = group * EXPERTS_PER_GROUP + jnp.take(jnp.array(PAIR_LO, jnp.int32), tile_bucket % 6)
    tile_eb = group * EXPERTS_PER_GROUP + jnp.take(jnp.array(PAIR_HI, jnp.int32), tile_bucket % 6)
    inv = invert_slots(dest, n_tiles * tm)
    ys = moe_experts_routed(xd, inv, tile_ea, tile_eb, n_valid.reshape(1), w_ein, w_eout)
    return moe_combine_final(ys, dest, x, g_final)


def _router_kernel(x_ref, g_ref, w_ref, b_ref, xn_ref, gate_ref):
    xn = _rms_rows(x_ref[...], g_ref[...])
    xn_ref[...] = xn.astype(BF16)
    logits = jnp.dot(xn, w_ref[...], preferred_element_type=F32, precision=lax.Precision.HIGHEST) + b_ref[...]
    _, i1, i2, p1, p2 = _route(logits)
    lanef = lax.broadcasted_iota(jnp.int32, logits.shape, 1).astype(F32)
    gate_ref[...] = jnp.where(lanef == i1, p1, 0.0) + jnp.where(lanef == i2, p2, 0.0)


def moe_router(x, g, w_rg, b_rg, w_re, b_re, tm):
    m = x.shape[0]
    tm = min(tm, m)
    w, b = _router_weights(w_rg, b_rg, w_re, b_re)
    row = lambda i: (i, 0)
    const = lambda i: (0, 0)
    return pl.pallas_call(
        _router_kernel,
        grid=(m // tm,),
        in_specs=[pl.BlockSpec((tm, D_MODEL), row), pl.BlockSpec((1, D_MODEL), const),
                  pl.BlockSpec((D_MODEL, LANES), const), pl.BlockSpec((1, LANES), const)],
        out_specs=[pl.BlockSpec((tm, D_MODEL), row), pl.BlockSpec((tm, LANES), row)],
        out_shape=[jax.ShapeDtypeStruct((m, D_MODEL), BF16), jax.ShapeDtypeStruct((m, LANES), F32)],
        compiler_params=_params("parallel"),
        name="moe_router",
    )(x, g.reshape(1, D_MODEL), w, b)


def _moe_dense_kernel(xn_ref, gate_ref, w1_ref, w2_ref, res_ref, gf_ref, y_ref):
    e = pl.program_id(1)

    @pl.when(e == 0)
    def _():
        y_ref[...] = res_ref[...]

    hu = jnp.dot(xn_ref[...], w1_ref[0], preferred_element_type=F32)
    h = jax.nn.silu(hu[:, :D_EXPERT]) * hu[:, D_EXPERT:]
    lane = lax.broadcasted_iota(jnp.int32, gate_ref.shape, 1)
    gate = jnp.sum(jnp.where(lane == e, gate_ref[...], 0.0), axis=-1, keepdims=True)
    h = h * gate
    y_ref[...] += jnp.dot(h.astype(BF16), w2_ref[0], preferred_element_type=F32)

    @pl.when(e == N_EXPERTS - 1)
    def _():
        y_ref[...] = _rms_rows(y_ref[...], gf_ref[...])


def moe_experts_final(xn, gate, w_ein, w_eout, res, g_final, tm):
    m = xn.shape[0]
    tm = min(tm, m)
    row = lambda i, e: (i, 0)
    return pl.pallas_call(
        _moe_dense_kernel,
        grid=(m // tm, N_EXPERTS),
        in_specs=[pl.BlockSpec((tm, D_MODEL), row), pl.BlockSpec((tm, LANES), row),
                  pl.BlockSpec((1, D_MODEL, 2 * D_EXPERT), lambda i, e: (e, 0, 0)),
                  pl.BlockSpec((1, D_EXPERT, D_MODEL), lambda i, e: (e, 0, 0)),
                  pl.BlockSpec((tm, D_MODEL), row),
                  pl.BlockSpec((1, D_MODEL), lambda i, e: (0, 0))],
        out_specs=pl.BlockSpec((tm, D_MODEL), row),
        out_shape=jax.ShapeDtypeStruct((m, D_MODEL), F32),
        compiler_params=_params("parallel", "arbitrary"),
        name="moe_experts_final",
    )(xn, gate, w_ein, w_eout, res, g_final.reshape(1, D_MODEL))


def _window_decode_kernel(z_ref, c1_ref, c2_ref, c3_ref, o_ref):
    scale = HEAD_DIM ** -0.5
    os, ms, ds = [], [], []
    for g, c_ref in enumerate((c1_ref, c2_ref, c3_ref)):
        q = z_ref[0, g:g + 1]
        k_new = z_ref[0, 3 + g:4 + g]
        v_new = z_ref[0, 6 + g:7 + g]
        k = c_ref[0, 0, :, 0, 0]
        v = c_ref[0, 0, :, 0, 1]
        s = jnp.sum(k * q, axis=-1, keepdims=True) * scale
        s_new = jnp.sum(k_new * q, axis=-1, keepdims=True) * scale
        m = jnp.maximum(jnp.max(s, axis=0, keepdims=True), s_new)
        p = jnp.exp(s - m)
        p_new = jnp.exp(s_new - m)
        den = jnp.sum(p, axis=0, keepdims=True) + p_new
        os.append(jnp.sum(p * v, axis=0, keepdims=True) + p_new * v_new)
        ms.append(m)
        ds.append(den)
    m_all = functools.reduce(jnp.maximum, ms)
    ws = [jnp.exp(m - m_all) for m in ms]
    num = sum(w * o for w, o in zip(ws, os))
    den = sum(w * d for w, d in zip(ws, ds))
    o_ref[0] = (num / den).astype(o_ref.dtype)


def window_decode(z4, caches):
    b = z4.shape[0]
    views, specs = [], []
    for cache, (win, dil) in zip(caches, ATT_GROUPS):
        n = cache.shape[2]
        assert n == win and n // dil == BAND
        views.append(cache.reshape(1, b, BAND, dil, 2, HEADS, HEAD_DIM))
        specs.append(pl.BlockSpec((1, 1, BAND, 1, 2, HEADS, HEAD_DIM), lambda i: (0, i, 0, 0, 0, 0, 0)))
    return pl.pallas_call(
        _window_decode_kernel,
        grid=(b,),
        in_specs=[pl.BlockSpec((1,) + z4.shape[1:], lambda i: (i, 0, 0, 0))] + specs,
        out_specs=pl.BlockSpec((1, 1, HEADS, HEAD_DIM), lambda i: (i, 0, 0, 0)),
        out_shape=jax.ShapeDtypeStruct((b, 1, HEADS, HEAD_DIM), BF16),
        compiler_params=_params("parallel"),
        name="window_decode",
    )(z4, *views)


SHIFT_ROWS = 64


def _shift_kernel(z_ref, c1_ref, c2_ref, c3_ref, o1_ref, o2_ref, o3_ref):
    for g, (c_ref, o_ref) in enumerate(((c1_ref, o1_ref), (c2_ref, o2_ref), (c3_ref, o3_ref))):
        n = c_ref.shape[2]
        full, rem = divmod(n - 1, SHIFT_ROWS)

        def move(j, carry, c_ref=c_ref, o_ref=o_ref):
            o_ref[0, 0, pl.ds(j * SHIFT_ROWS, SHIFT_ROWS)] = c_ref[0, 0, pl.ds(j * SHIFT_ROWS + 1, SHIFT_ROWS)]
            return carry

        lax.fori_loop(0, full, move, 0)
        if rem:
            o_ref[0, 0, pl.ds(full * SHIFT_ROWS, rem)] = c_ref[0, 0, pl.ds(full * SHIFT_ROWS + 1, rem)]
        o_ref[0, 0, n - 1, 0] = z_ref[0, 3 + g]
        o_ref[0, 0, n - 1, 1] = z_ref[0, 6 + g]


def shift_caches(z4, caches):
    b = z4.shape[0]
    specs = [pl.BlockSpec((1, 1) + c.shape[2:], lambda i: (0, i, 0, 0, 0, 0)) for c in caches]
    return pl.pallas_call(
        _shift_kernel,
        grid=(b,),
        in_specs=[pl.BlockSpec((1,) + z4.shape[1:], lambda i: (i, 0, 0, 0))] + specs,
        out_specs=specs,
        out_shape=[jax.ShapeDtypeStruct(c.shape, c.dtype) for c in caches],
        compiler_params=_params("parallel"),
        name="shift_caches",
    )(z4, *caches)


def _conv_step_kernel(a_ref, b_ref, s_ref, w_ref, cb_ref, lg_ref, lb_ref, c_ref, so_ref):
    hist = CONV_K - 1
    u = a_ref[...] * jax.nn.sigmoid(b_ref[...])
    y = (jnp.sum(s_ref[0] * w_ref[0:hist, :], axis=1, keepdims=True)
         + u * w_ref[hist:hist + 1, :] + cb_ref[...])
    c_ref[...] = _layernorm_silu(y, lg_ref[...], lb_ref[...]).astype(c_ref.dtype)
    so_ref[0, :, pl.ds(0, hist - 1), :] = s_ref[0, :, pl.ds(1, hist - 1), :]
    so_ref[0, :, pl.ds(hist - 1, 1), :] = u


def conv_step(a, b, state, conv_w, conv_b, ln_g, ln_b):
    bsz = a.shape[0]
    return pl.pallas_call(
        _conv_step_kernel,
        out_shape=[jax.ShapeDtypeStruct((bsz, 1, D_CONV), BF16), jax.ShapeDtypeStruct(state.shape, F32)],
        compiler_params=pltpu.CompilerParams(vmem_limit_bytes=VMEM_LIMIT),
        name="conv_step",
    )(a, b, state, conv_w, conv_b.reshape(1, D_CONV), ln_g.reshape(1, D_CONV), ln_b.reshape(1, D_CONV))


def _trunk_tail(x1, xo_in, w_xo, norm_ffn_g, w_rg, b_rg, w_re, b_re, w_ein, w_eout, norm_final_g, bm, routed):
    x2 = matmul_residual(xo_in, w_xo, x1, bm)
    if routed:
        return moe_routed(x2, norm_ffn_g, w_rg, b_rg, w_re, b_re, w_ein, w_eout, norm_final_g)
    xn3, gate = moe_router(x2, norm_ffn_g, w_rg, b_rg, w_re, b_re, bm)
    return moe_experts_final(xn3, gate, w_ein, w_eout, x2, norm_final_g, bm)


def kernel(x_prompt, x_sample, mem_prompt, cache_kv_g1, cache_kv_g2, cache_kv_g3, state_conv, cache_mem_kv,
           norm_mix_g, w_in, conv_w, conv_b, conv_ln_g, conv_ln_b, w_proj_a, w_proj_b, w_out,
           norm_xattn_g, norm_mem_g, w_xq, w_xkv, w_xo, norm_ffn_g,
           w_router_group, b_router_group, w_router_expert, b_router_expert, w_expert_in, w_expert_out,
           norm_final_g):
    depth = norm_mix_g.shape[0]
    assert depth == 1, "single-layer trunk"
    batch, seq, _ = x_prompt.shape
    dec_b, dec_t, _ = x_sample.shape
    assert dec_t == 1
    (g_mix, w_in, conv_w, conv_b, ln_g, ln_b, w_pa, w_pb, w_o, g_x, g_mem, w_xq, w_xkv, w_xo, g_ffn,
     w_rg, b_rg, w_re, b_re, w_ein, w_eout) = [t[0] for t in (
         norm_mix_g, w_in, conv_w, conv_b, conv_ln_g, conv_ln_b, w_proj_a, w_proj_b, w_out, norm_xattn_g,
         norm_mem_g, w_xq, w_xkv, w_xo, norm_ffn_g, w_router_group, b_router_group, w_router_expert,
         b_router_expert, w_expert_in, w_expert_out)]
    w_pa16, w_pb16, w_o16 = w_pa.astype(BF16), w_pb.astype(BF16), w_o.astype(BF16)
    w_ein, w_eout = w_ein.astype(BF16), w_eout.astype(BF16)
    w_in, w_xq, w_xkv, w_xo = w_in.astype(BF16), w_xq.astype(BF16), w_xkv.astype(BF16), w_xo.astype(BF16)
    caches = (cache_kv_g1, cache_kv_g2, cache_kv_g3)

    m_p = batch * seq
    xp = x_prompt.reshape(m_p, D_MODEL)
    za, gates = in_proj(xp, g_mix, w_in, bm=1024)
    o_a = band_attention(za, batch, seq)
    c, conv_prompt = conformer_conv(za, batch, seq, conv_w, conv_b, ln_g, ln_b)
    x1 = merge_branches(o_a, c, gates, xp, w_pa16, w_pb16, w_o16, tm=256)
    mkv = norm_matmul(mem_prompt.reshape(batch * N_MEM, D_MODEL), g_mem, w_xkv, F32, bm=1024)
    q = norm_matmul(x1, g_x, w_xq, BF16, bm=1024)
    xo_in = cross_attention_prompt(q, mkv, batch, seq)
    y_prompt = _trunk_tail(x1, xo_in, w_xo, g_ffn, w_rg, b_rg, w_re, b_re, w_ein, w_eout, norm_final_g,
                           bm=1024, routed=True)

    za3 = za.reshape(batch, seq, ZA_W)
    kv_prompt = []
    for g, (win, _) in enumerate(ATT_GROUPS):
        keep = min(win, seq)
        k = za3[:, seq - keep:, ATT_WIDTH + g * GROUP_W:ATT_WIDTH + (g + 1) * GROUP_W]
        v = za3[:, seq - keep:, 2 * ATT_WIDTH + g * GROUP_W:2 * ATT_WIDTH + (g + 1) * GROUP_W]
        kv_prompt.append(jnp.stack([k, v], axis=2).reshape(1, batch, keep, 2, HEADS, HEAD_DIM))
    conv_prompt = conv_prompt[None]
    mem_kv_prompt = mkv.reshape(1, batch, N_MEM, 2, X_HEADS, X_HEAD_DIM)

    xs = x_sample.reshape(dec_b, D_MODEL)
    zs, gates_s = in_proj(xs, g_mix, w_in, bm=dec_b)
    z4 = zs.reshape(dec_b, ZA_W // HEAD_DIM // HEADS, HEADS, HEAD_DIM)
    o_as = window_decode(z4, caches).reshape(dec_b, GROUP_W)
    kv_sample = shift_caches(z4, caches)
    a_s = zs[:, 3 * ATT_WIDTH:3 * ATT_WIDTH + D_CONV].reshape(dec_b, 1, D_CONV)
    b_s = zs[:, 3 * ATT_WIDTH + D_CONV:].reshape(dec_b, 1, D_CONV)
    c_s, conv_sample = conv_step(a_s, b_s, state_conv, conv_w, conv_b, ln_g, ln_b)
    x1s = merge_branches(o_as, c_s.reshape(dec_b, D_CONV), gates_s, xs, w_pa16, w_pb16, w_o16, tm=dec_b)
    q_s = norm_matmul(x1s, g_x, w_xq, BF16, bm=dec_b)
    xo_s = cross_attention_decode(q_s, cache_mem_kv)
    y_sample = _trunk_tail(x1s, xo_s, w_xo, g_ffn, w_rg, b_rg, w_re, b_re, w_ein, w_eout, norm_final_g,
                           bm=dec_b, routed=False)

    return (y_prompt.reshape(batch, seq, D_MODEL), y_sample.reshape(dec_b, 1, D_MODEL),
            kv_prompt[0], kv_prompt[1], kv_prompt[2], conv_prompt, mem_kv_prompt,
            kv_sample[0], kv_sample[1], kv_sample[2], conv_sample)
```

```python
import functools

import jax
import jax.numpy as jnp
from jax import lax
from jax.experimental import pallas as pl
from jax.experimental.pallas import tpu as pltpu

F32 = jnp.float32
BF16 = jnp.bfloat16

D_MODEL = 2048
ATT_GROUPS = ((128, 1), (512, 4), (2048, 16))
HEADS = 4
HEAD_DIM = 128
GROUP_W = HEADS * HEAD_DIM
ATT_WIDTH = len(ATT_GROUPS) * GROUP_W
D_CONV = 1536
CONV_K = 31
N_MEM = 256
X_HEADS = 4
X_HEAD_DIM = D_MODEL // X_HEADS
N_EXPERT_GROUPS = 4
EXPERTS_PER_GROUP = 4
N_EXPERTS = 16
D_EXPERT = 512
ZA_W = 3 * ATT_WIDTH + 2 * D_CONV
GATE_W = 2 * D_MODEL
RMS_EPS = 1e-6
LN_EPS = 1e-5
NEG_INF = -1e30
BAND = 128
LANES = 128
ROUTER_GROUP_LANE = N_EXPERTS

VMEM_LIMIT = 56 * 1024 * 1024


def _params(*sem):
    return pltpu.CompilerParams(dimension_semantics=sem, vmem_limit_bytes=VMEM_LIMIT)


def _rms_rows(x, g):
    ms = jnp.mean(x * x, axis=-1, keepdims=True)
    return x * lax.rsqrt(ms + RMS_EPS) * g


def _store_normed(x_ref, g_ref, xn_ref, chunk=256):
    rows = x_ref.shape[0]
    step = min(chunk, rows)
    for r0 in range(0, rows, step):
        xn_ref[r0:r0 + step, :] = _rms_rows(x_ref[r0:r0 + step, :], g_ref[...]).astype(xn_ref.dtype)


def _norm_matmul_kernel(x_ref, g_ref, w_ref, o_ref, xn_ref):
    @pl.when(pl.program_id(1) == 0)
    def _():
        _store_normed(x_ref, g_ref, xn_ref)

    o_ref[...] = jnp.dot(xn_ref[...], w_ref[...], preferred_element_type=F32).astype(o_ref.dtype)


def norm_matmul(x, g, w, out_dtype, bm, bn=512):
    m, k = x.shape
    n = w.shape[1]
    bm = min(bm, m)
    return pl.pallas_call(
        _norm_matmul_kernel,
        grid=(m // bm, n // bn),
        in_specs=[pl.BlockSpec((bm, k), lambda i, j: (i, 0)),
                  pl.BlockSpec((1, k), lambda i, j: (0, 0)),
                  pl.BlockSpec((k, bn), lambda i, j: (0, j))],
        out_specs=pl.BlockSpec((bm, bn), lambda i, j: (i, j)),
        out_shape=jax.ShapeDtypeStruct((m, n), out_dtype),
        scratch_shapes=[pltpu.VMEM((bm, k), BF16)],
        compiler_params=_params("parallel", "arbitrary"),
        name="norm_matmul",
    )(x, g.reshape(1, k), w)


def _in_proj_kernel(x_ref, g_ref, w_ref, za_ref, gt_ref, xn_ref, *, n_za):
    j = pl.program_id(1)

    @pl.when(j == 0)
    def _():
        _store_normed(x_ref, g_ref, xn_ref)

    r = jnp.dot(xn_ref[...], w_ref[...], preferred_element_type=F32)

    @pl.when(j < n_za)
    def _():
        za_ref[...] = r

    @pl.when(j >= n_za)
    def _():
        gt_ref[...] = r


def in_proj(x, g, w, bm, bn=512):
    m, k = x.shape
    bm = min(bm, m)
    n_za = ZA_W // bn
    n_gt = GATE_W // bn
    return pl.pallas_call(
        functools.partial(_in_proj_kernel, n_za=n_za),
        grid=(m // bm, n_za + n_gt),
        in_specs=[pl.BlockSpec((bm, k), lambda i, j: (i, 0)),
                  pl.BlockSpec((1, k), lambda i, j: (0, 0)),
                  pl.BlockSpec((k, bn), lambda i, j: (0, j))],
        out_specs=[pl.BlockSpec((bm, bn), lambda i, j: (i, jnp.minimum(j, n_za - 1))),
                   pl.BlockSpec((bm, bn), lambda i, j: (i, jnp.maximum(j - n_za, 0)))],
        out_shape=[jax.ShapeDtypeStruct((m, ZA_W), F32), jax.ShapeDtypeStruct((m, GATE_W), F32)],
        scratch_shapes=[pltpu.VMEM((bm, k), BF16)],
        compiler_params=_params("parallel", "arbitrary"),
        name="in_proj",
    )(x, g.reshape(1, k), w)


def _matmul_res_kernel(x_ref, w_ref, r_ref, o_ref):
    o_ref[...] = r_ref[...] + jnp.dot(x_ref[...], w_ref[...], preferred_element_type=F32)


def matmul_residual(x, w, res, bm, bn=512):
    m, k = x.shape
    n = w.shape[1]
    bm = min(bm, m)
    return pl.pallas_call(
        _matmul_res_kernel,
        grid=(m // bm, n // bn),
        in_specs=[pl.BlockSpec((bm, k), lambda i, j: (i, 0)),
                  pl.BlockSpec((k, bn), lambda i, j: (0, j)),
                  pl.BlockSpec((bm, bn), lambda i, j: (i, j))],
        out_specs=pl.BlockSpec((bm, bn), lambda i, j: (i, j)),
        out_shape=jax.ShapeDtypeStruct((m, n), F32),
        compiler_params=_params("parallel", "arbitrary"),
        name="matmul_residual",
    )(x, w, res)


ATT_R = 2048
ATT_UNROLL = 8


def _band_attn_kernel(*refs):
    n_g = len(ATT_GROUPS)
    in_refs = refs[:5 * n_g]
    o_ref = refs[5 * n_g]
    scr = refs[5 * n_g + 1:]
    kbufs, vbufs = scr[:n_g], scr[n_g:2 * n_g]
    obufs, mbufs, dbufs = scr[2 * n_g:3 * n_g], scr[3 * n_g:4 * n_g], scr[4 * n_g:5 * n_g]
    first_chunk = pl.program_id(1) == 0
    row = lax.broadcasted_iota(jnp.int32, (BAND, 2 * BAND), 0)
    col = lax.broadcasted_iota(jnp.int32, (BAND, 2 * BAND), 1)
    in_band = (col >= row) & (col <= row + BAND)
    in_cur = col >= BAND
    scale = HEAD_DIM ** -0.5
    nt = (((1,), (1,)), ((), ()))

    for g, (win, dil) in enumerate(ATT_GROUPS):
        q_ref, kc_ref, vc_ref, kp_ref, vp_ref = in_refs[5 * g:5 * g + 5]
        kbuf, vbuf, obuf, mbuf, dbuf = kbufs[g], vbufs[g], obufs[g], mbufs[g], dbufs[g]
        kbuf[0:win, :] = kp_ref[...]
        kbuf[win:win + ATT_R, :] = kc_ref[...]
        vbuf[0:win, :] = vp_ref[...]
        vbuf[win:win + ATT_R, :] = vc_ref[...]
        shift = dil.bit_length() - 1

        def sub_blocks(i4, carry, q_ref=q_ref, kbuf=kbuf, vbuf=vbuf, obuf=obuf, mbuf=mbuf, dbuf=dbuf,
                       win=win, dil=dil, shift=shift):
            done = []
            for u in range(ATT_UNROLL):
                i = i4 * ATT_UNROLL + u
                span = lax.shift_right_logical(i, shift)
                base = span * win + (i & (dil - 1))
                if dil == 1:
                    q_rows, kv_rows = pl.ds(base, BAND), pl.ds(base, 2 * BAND)
                else:
                    q_rows, kv_rows = pl.ds(base, BAND, stride=dil), pl.ds(base, 2 * BAND, stride=dil)
                q = q_ref[q_rows, :].astype(BF16)
                k = kbuf[kv_rows, :].astype(BF16)
                v = vbuf[kv_rows, :].astype(BF16)
                s = lax.dot_general(q, k, nt, preferred_element_type=F32) * scale
                has_prev = jnp.logical_not(first_chunk & (span == 0))
                s = jnp.where(in_band & (in_cur | has_prev), s, NEG_INF)
                m = jnp.max(s, axis=-1, keepdims=True)
                p = jnp.exp(s - m)
                den = jnp.sum(p, axis=-1, keepdims=True)
                done.append((q_rows, jnp.dot(p.astype(BF16), v, preferred_element_type=F32), m, den))
            for q_rows, o, m, den in done:
                obuf[q_rows, :] = o
                mbuf[q_rows, :] = jnp.broadcast_to(m, (BAND, HEAD_DIM))
                dbuf[q_rows, :] = jnp.broadcast_to(den, (BAND, HEAD_DIM))
            return carry

        lax.fori_loop(0, ATT_R // BAND // ATT_UNROLL, sub_blocks, 0)

    chunk = 256
    for r0 in range(0, ATT_R, chunk):
        rs = slice(r0, r0 + chunk)
        ms = [mb[rs, :] for mb in mbufs]
        m_all = functools.reduce(jnp.maximum, ms)
        ws = [jnp.exp(m - m_all) for m in ms]
        num = sum(w * ob[rs, :] for w, ob in zip(ws, obufs))
        den = sum(w * db[rs, :] for w, db in zip(ws, dbufs))
        o_ref[rs, :] = (num / den).astype(o_ref.dtype)


def band_attention(za, batch, seq):
    assert seq % ATT_R == 0
    nch = seq // ATT_R
    in_specs, scratch = [], []
    for g, (win, dil) in enumerate(ATT_GROUPS):
        assert win // dil == BAND and ATT_R % win == 0
        cols = [(part * len(ATT_GROUPS) + g) * HEADS for part in range(3)]
        cur = lambda c: pl.BlockSpec((ATT_R, HEAD_DIM), lambda b, ch, h, c=c: (b * nch + ch, c + h))
        prev = lambda c, win=win: pl.BlockSpec(
            (win, HEAD_DIM), lambda b, ch, h, c=c, win=win: (jnp.maximum((b * seq + ch * ATT_R) // win - 1, 0), c + h))
        in_specs += [cur(cols[0]), cur(cols[1]), cur(cols[2]), prev(cols[1]), prev(cols[2])]
    for _ in range(2):
        scratch += [pltpu.VMEM((win + ATT_R, HEAD_DIM), F32) for win, _ in ATT_GROUPS]
    scratch += [pltpu.VMEM((ATT_R, HEAD_DIM), F32)] * (3 * len(ATT_GROUPS))
    return pl.pallas_call(
        _band_attn_kernel,
        grid=(batch, nch, HEADS),
        in_specs=in_specs,
        out_specs=pl.BlockSpec((ATT_R, HEAD_DIM), lambda b, ch, h: (b * nch + ch, h)),
        out_shape=jax.ShapeDtypeStruct((batch * seq, GROUP_W), BF16),
        scratch_shapes=scratch,
        compiler_params=_params("parallel", "parallel", "parallel"),
        name="band_attention",
    )(*([za] * (5 * len(ATT_GROUPS))))


CONV_T = 256
CONV_HIST = 32
CONV_RC = 128
CONV_PHASES = 4
CONV_SLABS = D_CONV // LANES


def _layernorm_silu(y, g, b):
    mu = jnp.mean(y, axis=-1, keepdims=True)
    yc = y - mu
    var = jnp.mean(yc * yc, axis=-1, keepdims=True)
    yn = yc * lax.rsqrt(var + LN_EPS) * g + b
    return yn * jax.nn.sigmoid(yn)


def _conv_history(t, ubuf):
    @pl.when(t == 0)
    def _():
        ubuf[:, 0:CONV_HIST, :] = jnp.zeros((CONV_SLABS, CONV_HIST, LANES), F32)

    @pl.when(t > 0)
    def _():
        ubuf[:, 0:CONV_HIST, :] = ubuf[:, CONV_T:CONV_T + CONV_HIST, :]


def _conv_slab(j, a_ref, b_ref, w_ref, cb_ref, tail_ref, ubuf, ybuf):
    first = CONV_HIST - (CONV_K - 1)
    n = CONV_RC // CONV_PHASES
    js = slice(j * LANES, (j + 1) * LANES)
    u = a_ref[:, js] * jax.nn.sigmoid(b_ref[:, js])
    ubuf[j, CONV_HIST:CONV_HIST + CONV_T, :] = u
    tail_ref[0, :, js] = u[CONV_T - CONV_HIST:, :]
    for r0 in range(0, CONV_T, CONV_RC):
        accs = [jnp.broadcast_to(cb_ref[:, js], (n, LANES))] * CONV_PHASES
        for s in range(first, first + CONV_K + CONV_PHASES - 1):
            rows = ubuf[j, pl.ds(r0 + s, n, stride=CONV_PHASES), :]
            for p in range(CONV_PHASES):
                k = s - first - p
                if 0 <= k < CONV_K:
                    accs[p] = accs[p] + w_ref[k:k + 1, js] * rows
        for p in range(CONV_PHASES):
            ybuf[j, pl.ds(r0 + p, n, stride=CONV_PHASES), :] = accs[p]
    return jnp.sum(ybuf[j], axis=-1, keepdims=True)


def _conv_centered_sq(j, mu, ybuf):
    yc = ybuf[j] - mu
    return jnp.sum(yc * yc, axis=-1, keepdims=True)


def _conv_norm_slab(j, mu, inv, lg_ref, lb_ref, c_ref, ybuf):
    js = slice(j * LANES, (j + 1) * LANES)
    yn = (ybuf[j] - mu) * inv * lg_ref[:, js] + lb_ref[:, js]
    c_ref[:, js] = (yn * jax.nn.sigmoid(yn)).astype(c_ref.dtype)


def _merge_kernel(oa_ref, c_ref, ga_ref, gb_ref, x_ref, wpa_ref, wpb_ref, wo_ref, y_ref):
    ta = jnp.dot(oa_ref[...], wpa_ref[...], preferred_element_type=F32)
    tb = jnp.dot(c_ref[...], wpb_ref[...], preferred_element_type=F32)
    hmix = jax.nn.sigmoid(ga_ref[...]) * ta + jax.nn.sigmoid(gb_ref[...]) * tb
    y_ref[...] = x_ref[...] + jnp.dot(hmix.astype(BF16), wo_ref[...], preferred_element_type=F32)


def _conv_merge_kernel(oa_ref, a_ref, b_ref, ga_ref, gb_ref, x_ref, cw_ref, cb_ref, lg_ref, lb_ref,
                       wpa_ref, wpb_ref, wo_ref, y_ref, tail_ref, ubuf, ybuf, c_scr):
    _conv_history(pl.program_id(1), ubuf)
    total = jnp.zeros((CONV_T, 1), F32)
    for j in range(CONV_SLABS):
        total = total + _conv_slab(j, a_ref, b_ref, cw_ref, cb_ref, tail_ref, ubuf, ybuf)
    mu = total * (1.0 / D_CONV)
    sq = jnp.zeros((CONV_T, 1), F32)
    for j in range(CONV_SLABS):
        sq = sq + _conv_centered_sq(j, mu, ybuf)
    inv = lax.rsqrt(sq * (1.0 / D_CONV) + LN_EPS)
    for j in range(CONV_SLABS):
        _conv_norm_slab(j, mu, inv, lg_ref, lb_ref, c_scr, ybuf)
    _merge_kernel(oa_ref, c_scr, ga_ref, gb_ref, x_ref, wpa_ref, wpb_ref, wo_ref, y_ref)


def conv_merge_branches(o_a, za, gates, x, batch, seq, conv_w, conv_b, ln_g, ln_b, w_pa, w_pb, w_out):
    nt = seq // CONV_T
    row = lambda b, t: (b * nt + t, 0)
    const = lambda b, t: (0, 0)
    once = dict(pipeline_mode=pl.Buffered(1))
    glu = 3 * ATT_WIDTH // D_CONV
    y, tail = pl.pallas_call(
        _conv_merge_kernel,
        grid=(batch, nt),
        in_specs=[pl.BlockSpec((CONV_T, GROUP_W), row),
                  pl.BlockSpec((CONV_T, D_CONV), lambda b, t: (b * nt + t, glu)),
                  pl.BlockSpec((CONV_T, D_CONV), lambda b, t: (b * nt + t, glu + 1)),
                  pl.BlockSpec((CONV_T, D_MODEL), lambda b, t: (b * nt + t, 0)),
                  pl.BlockSpec((CONV_T, D_MODEL), lambda b, t: (b * nt + t, 1)),
                  pl.BlockSpec((CONV_T, D_MODEL), row),
                  pl.BlockSpec((CONV_K, D_CONV), const),
                  pl.BlockSpec((1, D_CONV), const),
                  pl.BlockSpec((1, D_CONV), const),
                  pl.BlockSpec((1, D_CONV), const),
                  pl.BlockSpec((GROUP_W, D_MODEL), const, **once),
                  pl.BlockSpec((D_CONV, D_MODEL), const, **once),
                  pl.BlockSpec((D_MODEL, D_MODEL), const, **once)],
        out_specs=[pl.BlockSpec((CONV_T, D_MODEL), row),
                   pl.BlockSpec((1, CONV_HIST, D_CONV), lambda b, t: (b, 0, 0))],
        out_shape=[jax.ShapeDtypeStruct((batch * seq, D_MODEL), F32),
                   jax.ShapeDtypeStruct((batch, CONV_HIST, D_CONV), F32)],
        scratch_shapes=[pltpu.VMEM((CONV_SLABS, CONV_HIST + CONV_T, LANES), F32),
                        pltpu.VMEM((CONV_SLABS, CONV_T, LANES), F32),
                        pltpu.VMEM((CONV_T, D_CONV), BF16)],
        compiler_params=_params("parallel", "arbitrary"),
        name="conv_merge_branches",
    )(o_a, za, za, gates, gates, x, conv_w, conv_b.reshape(1, D_CONV), ln_g.reshape(1, D_CONV),
      ln_b.reshape(1, D_CONV), w_pa, w_pb, w_out)
    return y, tail[:, CONV_HIST - (CONV_K - 1):]


def merge_branches(o_a, c, gates, x, w_pa, w_pb, w_out, tm):
    m = x.shape[0]
    tm = min(tm, m)
    row = lambda i: (i, 0)
    const = lambda i: (0, 0)
    once = dict(pipeline_mode=pl.Buffered(1))
    return pl.pallas_call(
        _merge_kernel,
        grid=(m // tm,),
        in_specs=[
            pl.BlockSpec((tm, GROUP_W), row),
            pl.BlockSpec((tm, D_CONV), row),
            pl.BlockSpec((tm, D_MODEL), lambda i: (i, 0)),
            pl.BlockSpec((tm, D_MODEL), lambda i: (i, 1)),
            pl.BlockSpec((tm, D_MODEL), row),
            pl.BlockSpec((GROUP_W, D_MODEL), const, **once),
            pl.BlockSpec((D_CONV, D_MODEL), const, **once),
            pl.BlockSpec((D_MODEL, D_MODEL), const, **once)],
        out_specs=pl.BlockSpec((tm, D_MODEL), row),
        out_shape=jax.ShapeDtypeStruct((m, D_MODEL), F32),
        compiler_params=_params("parallel"),
        name="merge_branches",
    )(o_a, c, gates, gates, x, w_pa, w_pb, w_out)


def _xattn_kernel(q_ref, kv_ref, o_ref):
    scale = X_HEAD_DIM ** -0.5
    nt = (((1,), (1,)), ((), ()))
    for h in range(X_HEADS):
        hs = slice(h * X_HEAD_DIM, (h + 1) * X_HEAD_DIM)
        vs = slice(D_MODEL + h * X_HEAD_DIM, D_MODEL + (h + 1) * X_HEAD_DIM)
        s = lax.dot_general(q_ref[:, hs], kv_ref[:, hs].astype(BF16), nt, preferred_element_type=F32) * scale
        m = jnp.max(s, axis=-1, keepdims=True)
        p = jnp.exp(s - m)
        p = p / jnp.sum(p, axis=-1, keepdims=True)
        o = jnp.dot(p.astype(BF16), kv_ref[:, vs].astype(BF16), preferred_element_type=F32)
        o_ref[:, hs] = o.astype(o_ref.dtype)


def cross_attention_prompt(q, mkv, batch, seq, tm=512):
    nt = seq // tm
    return pl.pallas_call(
        _xattn_kernel,
        grid=(batch, nt),
        in_specs=[pl.BlockSpec((tm, D_MODEL), lambda b, t: (b * nt + t, 0)),
                  pl.BlockSpec((N_MEM, 2 * D_MODEL), lambda b, t: (b, 0))],
        out_specs=pl.BlockSpec((tm, D_MODEL), lambda b, t: (b * nt + t, 0)),
        out_shape=jax.ShapeDtypeStruct((batch * seq, D_MODEL), BF16),
        compiler_params=_params("parallel", "arbitrary"),
        name="cross_attention_prompt",
    )(q, mkv)


def _xattn_decode_kernel(q_ref, kv_ref, o_ref):
    scale = X_HEAD_DIM ** -0.5
    q = q_ref[0].astype(F32)
    k = kv_ref[0, 0, :, 0]
    v = kv_ref[0, 0, :, 1]
    s = jnp.sum(k * q, axis=-1, keepdims=True) * scale
    m = jnp.max(s, axis=0, keepdims=True)
    p = jnp.exp(s - m)
    p = p / jnp.sum(p, axis=0, keepdims=True)
    o_ref[0] = jnp.sum(p * v, axis=0, keepdims=True).astype(o_ref.dtype)


def cross_attention_decode(q, cache_mem_kv):
    b = q.shape[0]
    q4 = q.reshape(b, 1, X_HEADS, X_HEAD_DIM)
    o = pl.pallas_call(
        _xattn_decode_kernel,
        grid=(b,),
        in_specs=[pl.BlockSpec((1, 1, X_HEADS, X_HEAD_DIM), lambda i: (i, 0, 0, 0)),
                  pl.BlockSpec((1, 1, N_MEM, 2, X_HEADS, X_HEAD_DIM), lambda i: (0, i, 0, 0, 0, 0))],
        out_specs=pl.BlockSpec((1, 1, X_HEADS, X_HEAD_DIM), lambda i: (i, 0, 0, 0)),
        out_shape=jax.ShapeDtypeStruct((b, 1, X_HEADS, X_HEAD_DIM), BF16),
        compiler_params=_params("parallel"),
        name="cross_attention_decode",
    )(q4, cache_mem_kv)
    return o.reshape(b, D_MODEL)


N_BUCKETS = N_EXPERT_GROUPS * 6
PAIR_LO = (0, 0, 0, 1, 1, 2)
PAIR_HI = (1, 2, 3, 2, 3, 3)
ROW_TILES = D_MODEL // LANES
PAYLOAD_ROWS = ROW_TILES + 8
MOE_TM = 256


def _route(logits):
    lane = lax.broadcasted_iota(jnp.int32, logits.shape, 1)
    lanef = lane.astype(F32)
    big = float(LANES)
    is_group = (lane >= ROUTER_GROUP_LANE) & (lane < ROUTER_GROUP_LANE + N_EXPERT_GROUPS)
    lg = jnp.where(is_group, logits, -jnp.inf)
    mg = jnp.max(lg, axis=-1, keepdims=True)
    p_sel = 1.0 / jnp.sum(jnp.exp(lg - mg), axis=-1, keepdims=True)
    gsel = jnp.min(jnp.where(lg == mg, lanef, big), axis=-1, keepdims=True) - ROUTER_GROUP_LANE
    group_of_lane = lax.shift_right_logical(lane, 2)
    in_group = (lane < N_EXPERTS) & (group_of_lane == gsel.astype(jnp.int32))
    le = jnp.where(in_group, logits, -jnp.inf)
    v1 = jnp.max(le, axis=-1, keepdims=True)
    i1 = jnp.min(jnp.where(le == v1, lanef, big), axis=-1, keepdims=True)
    le2 = jnp.where(lanef == i1, -jnp.inf, le)
    v2 = jnp.max(le2, axis=-1, keepdims=True)
    i2 = jnp.min(jnp.where(le2 == v2, lanef, big), axis=-1, keepdims=True)
    t = jnp.exp(v2 - v1)
    tot = 1.0 + t
    return gsel, i1, i2, (1.0 / tot) * p_sel, (t / tot) * p_sel


def _router_routed_kernel(x_ref, g_ref, w_ref, b_ref, xd_ref, meta_ref, cnt_ref, carry_ref):
    tm = x_ref.shape[0]

    @pl.when(pl.program_id(0) == 0)
    def _():
        carry_ref[...] = jnp.zeros_like(carry_ref)

    xn = _rms_rows(x_ref[...], g_ref[...])
    logits = jnp.dot(xn, w_ref[...], preferred_element_type=F32, precision=lax.Precision.HIGHEST) + b_ref[...]
    gsel, i1, i2, p1, p2 = _route(logits)
    lo = jnp.minimum(i1, i2) - EXPERTS_PER_GROUP * gsel
    hi = jnp.maximum(i1, i2) - EXPERTS_PER_GROUP * gsel
    bucket = gsel * 6.0 + lo * (7.0 - lo) * 0.5 + (hi - lo - 1.0)
    gate_lo = jnp.where(i1 < i2, p1, p2)
    gate_hi = jnp.where(i1 < i2, p2, p1)

    lane = lax.broadcasted_iota(jnp.int32, (tm, LANES), 1)
    onehot = (lane.astype(F32) == bucket).astype(F32)
    r_i = lax.broadcasted_iota(jnp.int32, (tm, tm), 0)
    c_i = lax.broadcasted_iota(jnp.int32, (tm, tm), 1)
    before = (c_i < r_i).astype(BF16)
    rank_local = jnp.dot(before, onehot.astype(BF16), preferred_element_type=F32)
    rank = jnp.sum(onehot * (rank_local + carry_ref[0:1, :]), axis=-1, keepdims=True)
    carry_ref[0:1, :] = carry_ref[0:1, :] + jnp.sum(onehot, axis=0, keepdims=True)
    cnt_ref[...] = jnp.broadcast_to(carry_ref[0:1, :], cnt_ref.shape)
    meta_ref[...] = jnp.where(lane == 0, bucket, jnp.where(lane == 1, rank, 0.0))

    for j in range(ROW_TILES):
        xd_ref[pl.ds(j, tm, stride=PAYLOAD_ROWS), :] = xn[:, j * LANES:(j + 1) * LANES]
    xd_ref[pl.ds(ROW_TILES, tm, stride=PAYLOAD_ROWS), :] = jnp.where(
        lane == 0, gate_lo, jnp.where(lane == 1, gate_hi, 0.0))
    for j in range(ROW_TILES + 1, PAYLOAD_ROWS):
        xd_ref[pl.ds(j, tm, stride=PAYLOAD_ROWS), :] = jnp.zeros((tm, LANES), F32)


def _router_weights(w_rg, b_rg, w_re, b_re):
    pad = LANES - N_EXPERTS - N_EXPERT_GROUPS
    w = jnp.concatenate([w_re, w_rg, jnp.zeros((D_MODEL, pad), F32)], axis=1)
    b = jnp.concatenate([b_re, b_rg, jnp.zeros((pad,), F32)]).reshape(1, LANES)
    return w, b


def moe_router_routed(x, g, w_rg, b_rg, w_re, b_re, tm=512):
    m = x.shape[0]
    w, b = _router_weights(w_rg, b_rg, w_re, b_re)
    row = lambda i: (i, 0)
    const = lambda i: (0, 0)
    return pl.pallas_call(
        _router_routed_kernel,
        grid=(m // tm,),
        in_specs=[pl.BlockSpec((tm, D_MODEL), row), pl.BlockSpec((1, D_MODEL), const),
                  pl.BlockSpec((D_MODEL, LANES), const), pl.BlockSpec((1, LANES), const)],
        out_specs=[pl.BlockSpec((tm * PAYLOAD_ROWS, LANES), row), pl.BlockSpec((tm, LANES), row),
                   pl.BlockSpec((8, LANES), const)],
        out_shape=[jax.ShapeDtypeStruct((m * PAYLOAD_ROWS, LANES), F32), jax.ShapeDtypeStruct((m, LANES), F32),
                   jax.ShapeDtypeStruct((8, LANES), F32)],
        scratch_shapes=[pltpu.VMEM((8, LANES), F32)],
        compiler_params=_params("arbitrary"),
        name="moe_router_routed",
    )(x, g.reshape(1, D_MODEL), w, b)


INV_CHUNK = 2048


def _inverse_kernel(dest_ref, inv_ref):
    step = pl.program_id(0)

    @pl.when(step == 0)
    def _():
        def clear(k, c):
            inv_ref[k] = 0
            return c
        lax.fori_loop(0, inv_ref.shape[0], clear, 0, unroll=8)

    def put(t, c):
        inv_ref[dest_ref[0, 0, t]] = step * INV_CHUNK + t
        return c
    lax.fori_loop(0, INV_CHUNK, put, 0, unroll=8)


def invert_slots(dest, n_slots):
    m = dest.shape[0]
    return pl.pallas_call(
        _inverse_kernel,
        grid=(m // INV_CHUNK,),
        in_specs=[pl.BlockSpec((1, 1, INV_CHUNK), lambda i: (i, 0, 0), memory_space=pltpu.SMEM)],
        out_specs=pl.BlockSpec(memory_space=pltpu.SMEM),
        out_shape=jax.ShapeDtypeStruct((n_slots,), jnp.int32),
        compiler_params=_params("arbitrary"),
        name="invert_slots",
    )(dest.reshape(m // INV_CHUNK, 1, INV_CHUNK))


def _start_row_gather(idx_ref, src_hbm, dst, sem, n_items, rows):
    def start(r2, c):
        for queue in range(2):
            r = 2 * r2 + queue
            first = pl.multiple_of(idx_ref[0, 0, r] * rows, 8)
            pltpu.make_async_copy(src_hbm.at[pl.ds(first, rows)], dst.at[pl.ds(r * rows, rows)],
                                  sem).start(priority=queue)
        return c
    lax.fori_loop(0, n_items // 2, start, 0, unroll=4)


def _wait_row_gather(src_hbm, dst, sem, n_items, rows):
    def wait(r, c):
        pltpu.make_async_copy(src_hbm.at[pl.ds(0, rows)], dst.at[pl.ds(r * rows, rows)], sem).wait()
        return c
    lax.fori_loop(0, n_items, wait, 0, unroll=8)


def _moe_routed_kernel(ea_ref, eb_ref, nv_ref, inv_ref, inv_next_ref, xd_hbm, w1a_ref, w1b_ref, w2a_ref, w2b_ref,
                       ys_ref, xbuf, sems, x_scr):
    del ea_ref, eb_ref
    i = pl.program_id(0)
    n_valid = nv_ref[0]
    slot = i % 2
    tm = MOE_TM

    @pl.when(i == 0)
    def _():
        _start_row_gather(inv_ref, xd_hbm, xbuf.at[0], sems.at[0], tm, PAYLOAD_ROWS)

    @pl.when(i + 1 < n_valid)
    def _():
        _start_row_gather(inv_next_ref, xd_hbm, xbuf.at[1 - slot], sems.at[1 - slot], tm, PAYLOAD_ROWS)

    @pl.when(i < n_valid)
    def _():
        buf = xbuf.at[slot]
        _wait_row_gather(xd_hbm, buf, sems.at[slot], tm, PAYLOAD_ROWS)
        for j in range(ROW_TILES):
            x_scr[:, j * LANES:(j + 1) * LANES] = buf[pl.ds(j, tm, stride=PAYLOAD_ROWS), :].astype(BF16)
        gates = buf[pl.ds(ROW_TILES, tm, stride=PAYLOAD_ROWS), :]
        x = x_scr[...]
        hu = jnp.dot(x, w1a_ref[0], preferred_element_type=F32)
        ha = jax.nn.silu(hu[:, :D_EXPERT]) * hu[:, D_EXPERT:] * gates[:, 0:1]
        hu = jnp.dot(x, w1b_ref[0], preferred_element_type=F32)
        hb = jax.nn.silu(hu[:, :D_EXPERT]) * hu[:, D_EXPERT:] * gates[:, 1:2]
        y = (jnp.dot(ha.astype(BF16), w2a_ref[0], preferred_element_type=F32)
             + jnp.dot(hb.astype(BF16), w2b_ref[0], preferred_element_type=F32))
        for j in range(ROW_TILES):
            ys_ref[pl.ds(j, tm, stride=ROW_TILES), :] = y[:, j * LANES:(j + 1) * LANES]

    @pl.when(i >= n_valid)
    def _():
        ys_ref[...] = jnp.zeros_like(ys_ref)


def moe_experts_routed(xd, inv, tile_ea, tile_eb, n_valid, w_ein, w_eout):
    n_tiles = tile_ea.shape[0]
    tm = MOE_TM
    inv3 = inv.reshape(n_tiles, 1, tm)
    smem = dict(memory_space=pltpu.SMEM)
    grid_spec = pltpu.PrefetchScalarGridSpec(
        num_scalar_prefetch=3,
        grid=(n_tiles,),
        in_specs=[
            pl.BlockSpec((1, 1, tm), lambda i, ea, eb, nv: (i, 0, 0), **smem),
            pl.BlockSpec((1, 1, tm), lambda i, ea, eb, nv: (jnp.minimum(i + 1, n_tiles - 1), 0, 0), **smem),
            pl.BlockSpec(memory_space=pl.ANY),
            pl.BlockSpec((1, D_MODEL, 2 * D_EXPERT), lambda i, ea, eb, nv: (ea[i], 0, 0)),
            pl.BlockSpec((1, D_MODEL, 2 * D_EXPERT), lambda i, ea, eb, nv: (eb[i], 0, 0)),
            pl.BlockSpec((1, D_EXPERT, D_MODEL), lambda i, ea, eb, nv: (ea[i], 0, 0)),
            pl.BlockSpec((1, D_EXPERT, D_MODEL), lambda i, ea, eb, nv: (eb[i], 0, 0))],
        out_specs=pl.BlockSpec((tm * ROW_TILES, LANES), lambda i, ea, eb, nv: (i, 0)),
        scratch_shapes=[pltpu.VMEM((2, tm * PAYLOAD_ROWS, LANES), F32), pltpu.SemaphoreType.DMA((2,)),
                        pltpu.VMEM((tm, D_MODEL), BF16)])
    return pl.pallas_call(
        _moe_routed_kernel,
        grid_spec=grid_spec,
        out_shape=jax.ShapeDtypeStruct((n_tiles * tm * ROW_TILES, LANES), F32),
        compiler_params=_params("arbitrary"),
        name="moe_experts_routed",
    )(tile_ea, tile_eb, n_valid, inv3, inv3, xd, w_ein, w_ein, w_eout, w_eout)


def _combine_kernel(dest_ref, dest_next_ref, ys_hbm, res_ref, gf_ref, y_ref, buf, sems):
    i = pl.program_id(0)
    slot = i % 2
    tm = res_ref.shape[0]

    @pl.when(i == 0)
    def _():
        _start_row_gather(dest_ref, ys_hbm, buf.at[0], sems.at[0], tm, ROW_TILES)

    @pl.when(i + 1 < pl.num_programs(0))
    def _():
        _start_row_gather(dest_next_ref, ys_hbm, buf.at[1 - slot], sems.at[1 - slot], tm, ROW_TILES)

    cur = buf.at[slot]
    _wait_row_gather(ys_hbm, cur, sems.at[slot], tm, ROW_TILES)
    for j in range(ROW_TILES):
        js = slice(j * LANES, (j + 1) * LANES)
        y_ref[:, js] = res_ref[:, js] + cur[pl.ds(j, tm, stride=ROW_TILES), :]
    y_ref[...] = _rms_rows(y_ref[...], gf_ref[...])


def moe_combine_final(ys, dest, res, g_final, tm=256):
    m = res.shape[0]
    nt = m // tm
    dest3 = dest.reshape(nt, 1, tm)
    smem = dict(memory_space=pltpu.SMEM)
    return pl.pallas_call(
        _combine_kernel,
        grid=(nt,),
        in_specs=[pl.BlockSpec((1, 1, tm), lambda i: (i, 0, 0), **smem),
                  pl.BlockSpec((1, 1, tm), lambda i: (jnp.minimum(i + 1, nt - 1), 0, 0), **smem),
                  pl.BlockSpec(memory_space=pl.ANY),
                  pl.BlockSpec((tm, D_MODEL), lambda i: (i, 0)),
                  pl.BlockSpec((1, D_MODEL), lambda i: (0, 0))],
        out_specs=pl.BlockSpec((tm, D_MODEL), lambda i: (i, 0)),
        out_shape=jax.ShapeDtypeStruct((m, D_MODEL), F32),
        scratch_shapes=[pltpu.VMEM((2, tm * ROW_TILES, LANES), F32), pltpu.SemaphoreType.DMA((2,))],
        compiler_params=_params("arbitrary"),
        name="moe_combine_final",
    )(dest3, dest3, ys, res, g_final.reshape(1, D_MODEL))


def moe_routed(x, g_ffn, w_rg, b_rg, w_re, b_re, w_ein, w_eout, g_final):
    m = x.shape[0]
    tm = MOE_TM
    n_tiles = m // tm + N_BUCKETS
    xd, meta, cnt = moe_router_routed(x, g_ffn, w_rg, b_rg, w_re, b_re)
    bucket = meta[:, 0].astype(jnp.int32)
    rank = meta[:, 1].astype(jnp.int32)
    counts = cnt[0, :N_BUCKETS].astype(jnp.int32)
    tiles_per_bucket = (counts + tm - 1) // tm
    tile_end = jnp.cumsum(tiles_per_bucket)
    tile_start = tile_end - tiles_per_bucket
    dest = jnp.take(tile_start, bucket) * tm + rank
    n_valid = tile_end[-1]
    tile_id = jnp.arange(n_tiles, dtype=jnp.int32)
    tile_bucket = jnp.searchsorted(tile_end, jnp.minimum(tile_id, n_valid - 1), side="right").astype(jnp.int32)
    group = tile_bucket // 6
    tile_ea = group * EXPERTS_PER_GROUP + jnp.take(jnp.array(PAIR_LO, jnp.int32), tile_bucket % 6)
    tile_eb = group * EXPERTS_PER_GROUP + jnp.take(jnp.array(PAIR_HI, jnp.int32), tile_bucket % 6)
    inv = invert_slots(dest, n_tiles * tm)
    ys = moe_experts_routed(xd, inv, tile_ea, tile_eb, n_valid.reshape(1), w_ein, w_eout)
    return moe_combine_final(ys, dest, x, g_final)


def _router_kernel(x_ref, g_ref, w_ref, b_ref, xn_ref, gate_ref):
    xn = _rms_rows(x_ref[...], g_ref[...])
    xn_ref[...] = xn.astype(BF16)
    logits = jnp.dot(xn, w_ref[...], preferred_element_type=F32, precision=lax.Precision.HIGHEST) + b_ref[...]
    _, i1, i2, p1, p2 = _route(logits)
    lanef = lax.broadcasted_iota(jnp.int32, logits.shape, 1).astype(F32)
    gate_ref[...] = jnp.where(lanef == i1, p1, 0.0) + jnp.where(lanef == i2, p2, 0.0)


def moe_router(x, g, w_rg, b_rg, w_re, b_re, tm):
    m = x.shape[0]
    tm = min(tm, m)
    w, b = _router_weights(w_rg, b_rg, w_re, b_re)
    row = lambda i: (i, 0)
    const = lambda i: (0, 0)
    return pl.pallas_call(
        _router_kernel,
        grid=(m // tm,),
        in_specs=[pl.BlockSpec((tm, D_MODEL), row), pl.BlockSpec((1, D_MODEL), const),
                  pl.BlockSpec((D_MODEL, LANES), const), pl.BlockSpec((1, LANES), const)],
        out_specs=[pl.BlockSpec((tm, D_MODEL), row), pl.BlockSpec((tm, LANES), row)],
        out_shape=[jax.ShapeDtypeStruct((m, D_MODEL), BF16), jax.ShapeDtypeStruct((m, LANES), F32)],
        compiler_params=_params("parallel"),
        name="moe_router",
    )(x, g.reshape(1, D_MODEL), w, b)


def _moe_dense_kernel(xn_ref, gate_ref, w1_ref, w2_ref, res_ref, gf_ref, y_ref):
    e = pl.program_id(1)

    @pl.when(e == 0)
    def _():
        y_ref[...] = res_ref[...]

    hu = jnp.dot(xn_ref[...], w1_ref[0], preferred_element_type=F32)
    h = jax.nn.silu(hu[:, :D_EXPERT]) * hu[:, D_EXPERT:]
    lane = lax.broadcasted_iota(jnp.int32, gate_ref.shape, 1)
    gate = jnp.sum(jnp.where(lane == e, gate_ref[...], 0.0), axis=-1, keepdims=True)
    h = h * gate
    y_ref[...] += jnp.dot(h.astype(BF16), w2_ref[0], preferred_element_type=F32)

    @pl.when(e == N_EXPERTS - 1)
    def _():
        y_ref[...] = _rms_rows(y_ref[...], gf_ref[...])


def moe_experts_final(xn, gate, w_ein, w_eout, res, g_final, tm):
    m = xn.shape[0]
    tm = min(tm, m)
    row = lambda i, e: (i, 0)
    return pl.pallas_call(
        _moe_dense_kernel,
        grid=(m // tm, N_EXPERTS),
        in_specs=[pl.BlockSpec((tm, D_MODEL), row), pl.BlockSpec((tm, LANES), row),
                  pl.BlockSpec((1, D_MODEL, 2 * D_EXPERT), lambda i, e: (e, 0, 0)),
                  pl.BlockSpec((1, D_EXPERT, D_MODEL), lambda i, e: (e, 0, 0)),
                  pl.BlockSpec((tm, D_MODEL), row),
                  pl.BlockSpec((1, D_MODEL), lambda i, e: (0, 0))],
        out_specs=pl.BlockSpec((tm, D_MODEL), row),
        out_shape=jax.ShapeDtypeStruct((m, D_MODEL), F32),
        compiler_params=_params("parallel", "arbitrary"),
        name="moe_experts_final",
    )(xn, gate, w_ein, w_eout, res, g_final.reshape(1, D_MODEL))


def _window_decode_kernel(z_ref, c1_ref, c2_ref, c3_ref, o_ref):
    scale = HEAD_DIM ** -0.5
    os, ms, ds = [], [], []
    for g, c_ref in enumerate((c1_ref, c2_ref, c3_ref)):
        q = z_ref[0, g:g + 1]
        k_new = z_ref[0, 3 + g:4 + g]
        v_new = z_ref[0, 6 + g:7 + g]
        k = c_ref[0, 0, :, 0, 0]
        v = c_ref[0, 0, :, 0, 1]
        s = jnp.sum(k * q, axis=-1, keepdims=True) * scale
        s_new = jnp.sum(k_new * q, axis=-1, keepdims=True) * scale
        m = jnp.maximum(jnp.max(s, axis=0, keepdims=True), s_new)
        p = jnp.exp(s - m)
        p_new = jnp.exp(s_new - m)
        den = jnp.sum(p, axis=0, keepdims=True) + p_new
        os.append(jnp.sum(p * v, axis=0, keepdims=True) + p_new * v_new)
        ms.append(m)
        ds.append(den)
    m_all = functools.reduce(jnp.maximum, ms)
    ws = [jnp.exp(m - m_all) for m in ms]
    num = sum(w * o for w, o in zip(ws, os))
    den = sum(w * d for w, d in zip(ws, ds))
    o_ref[0] = (num / den).astype(o_ref.dtype)


def window_decode(z4, caches):
    b = z4.shape[0]
    views, specs = [], []
    for cache, (win, dil) in zip(caches, ATT_GROUPS):
        n = cache.shape[2]
        assert n == win and n // dil == BAND
        views.append(cache.reshape(1, b, BAND, dil, 2, HEADS, HEAD_DIM))
        specs.append(pl.BlockSpec((1, 1, BAND, 1, 2, HEADS, HEAD_DIM), lambda i: (0, i, 0, 0, 0, 0, 0)))
    return pl.pallas_call(
        _window_decode_kernel,
        grid=(b,),
        in_specs=[pl.BlockSpec((1,) + z4.shape[1:], lambda i: (i, 0, 0, 0))] + specs,
        out_specs=pl.BlockSpec((1, 1, HEADS, HEAD_DIM), lambda i: (i, 0, 0, 0)),
        out_shape=jax.ShapeDtypeStruct((b, 1, HEADS, HEAD_DIM), BF16),
        compiler_params=_params("parallel"),
        name="window_decode",
    )(z4, *views)


SHIFT_ROWS = 64


def _shift_kernel(z_ref, c1_ref, c2_ref, c3_ref, o1_ref, o2_ref, o3_ref):
    for g, (c_ref, o_ref) in enumerate(((c1_ref, o1_ref), (c2_ref, o2_ref), (c3_ref, o3_ref))):
        n = c_ref.shape[2]
        full, rem = divmod(n - 1, SHIFT_ROWS)

        def move(j, carry, c_ref=c_ref, o_ref=o_ref):
            o_ref[0, 0, pl.ds(j * SHIFT_ROWS, SHIFT_ROWS)] = c_ref[0, 0, pl.ds(j * SHIFT_ROWS + 1, SHIFT_ROWS)]
            return carry

        lax.fori_loop(0, full, move, 0)
        if rem:
            o_ref[0, 0, pl.ds(full * SHIFT_ROWS, rem)] = c_ref[0, 0, pl.ds(full * SHIFT_ROWS + 1, rem)]
        o_ref[0, 0, n - 1, 0] = z_ref[0, 3 + g]
        o_ref[0, 0, n - 1, 1] = z_ref[0, 6 + g]


def shift_caches(z4, caches):
    b = z4.shape[0]
    specs = [pl.BlockSpec((1, 1) + c.shape[2:], lambda i: (0, i, 0, 0, 0, 0)) for c in caches]
    return pl.pallas_call(
        _shift_kernel,
        grid=(b,),
        in_specs=[pl.BlockSpec((1,) + z4.shape[1:], lambda i: (i, 0, 0, 0))] + specs,
        out_specs=specs,
        out_shape=[jax.ShapeDtypeStruct(c.shape, c.dtype) for c in caches],
        compiler_params=_params("parallel"),
        name="shift_caches",
    )(z4, *caches)


def _conv_step_kernel(a_ref, b_ref, s_ref, w_ref, cb_ref, lg_ref, lb_ref, c_ref, so_ref):
    hist = CONV_K - 1
    u = a_ref[...] * jax.nn.sigmoid(b_ref[...])
    y = (jnp.sum(s_ref[0] * w_ref[0:hist, :], axis=1, keepdims=True)
         + u * w_ref[hist:hist + 1, :] + cb_ref[...])
    c_ref[...] = _layernorm_silu(y, lg_ref[...], lb_ref[...]).astype(c_ref.dtype)
    so_ref[0, :, pl.ds(0, hist - 1), :] = s_ref[0, :, pl.ds(1, hist - 1), :]
    so_ref[0, :, pl.ds(hist - 1, 1), :] = u


def conv_step(a, b, state, conv_w, conv_b, ln_g, ln_b):
    bsz = a.shape[0]
    return pl.pallas_call(
        _conv_step_kernel,
        out_shape=[jax.ShapeDtypeStruct((bsz, 1, D_CONV), BF16), jax.ShapeDtypeStruct(state.shape, F32)],
        compiler_params=pltpu.CompilerParams(vmem_limit_bytes=VMEM_LIMIT),
        name="conv_step",
    )(a, b, state, conv_w, conv_b.reshape(1, D_CONV), ln_g.reshape(1, D_CONV), ln_b.reshape(1, D_CONV))


def _trunk_tail(x1, xo_in, w_xo, norm_ffn_g, w_rg, b_rg, w_re, b_re, w_ein, w_eout, norm_final_g, bm, routed):
    x2 = matmul_residual(xo_in, w_xo, x1, min(bm, 512), bn=D_MODEL)
    if routed:
        return moe_routed(x2, norm_ffn_g, w_rg, b_rg, w_re, b_re, w_ein, w_eout, norm_final_g)
    xn3, gate = moe_router(x2, norm_ffn_g, w_rg, b_rg, w_re, b_re, bm)
    return moe_experts_final(xn3, gate, w_ein, w_eout, x2, norm_final_g, bm)


def kernel(x_prompt, x_sample, mem_prompt, cache_kv_g1, cache_kv_g2, cache_kv_g3, state_conv, cache_mem_kv,
           norm_mix_g, w_in, conv_w, conv_b, conv_ln_g, conv_ln_b, w_proj_a, w_proj_b, w_out,
           norm_xattn_g, norm_mem_g, w_xq, w_xkv, w_xo, norm_ffn_g,
           w_router_group, b_router_group, w_router_expert, b_router_expert, w_expert_in, w_expert_out,
           norm_final_g):
    depth = norm_mix_g.shape[0]
    assert depth == 1, "single-layer trunk"
    batch, seq, _ = x_prompt.shape
    dec_b, dec_t, _ = x_sample.shape
    assert dec_t == 1
    (g_mix, w_in, conv_w, conv_b, ln_g, ln_b, w_pa, w_pb, w_o, g_x, g_mem, w_xq, w_xkv, w_xo, g_ffn,
     w_rg, b_rg, w_re, b_re, w_ein, w_eout) = [t[0] for t in (
         norm_mix_g, w_in, conv_w, conv_b, conv_ln_g, conv_ln_b, w_proj_a, w_proj_b, w_out, norm_xattn_g,
         norm_mem_g, w_xq, w_xkv, w_xo, norm_ffn_g, w_router_group, b_router_group, w_router_expert,
         b_router_expert, w_expert_in, w_expert_out)]
    w_pa16, w_pb16, w_o16 = w_pa.astype(BF16), w_pb.astype(BF16), w_o.astype(BF16)
    w_ein, w_eout = w_ein.astype(BF16), w_eout.astype(BF16)
    w_in, w_xq, w_xkv, w_xo = w_in.astype(BF16), w_xq.astype(BF16), w_xkv.astype(BF16), w_xo.astype(BF16)
    caches = (cache_kv_g1, cache_kv_g2, cache_kv_g3)

    m_p = batch * seq
    xp = x_prompt.reshape(m_p, D_MODEL)
    za, gates = in_proj(xp, g_mix, w_in, bm=1024)
    o_a = band_attention(za, batch, seq)
    x1, conv_prompt = conv_merge_branches(o_a, za, gates, xp, batch, seq, conv_w, conv_b, ln_g, ln_b,
                                          w_pa16, w_pb16, w_o16)
    mkv = norm_matmul(mem_prompt.reshape(batch * N_MEM, D_MODEL), g_mem, w_xkv, F32, bm=1024)
    q = norm_matmul(x1, g_x, w_xq, BF16, bm=512, bn=D_MODEL)
    xo_in = cross_attention_prompt(q, mkv, batch, seq)
    y_prompt = _trunk_tail(x1, xo_in, w_xo, g_ffn, w_rg, b_rg, w_re, b_re, w_ein, w_eout, norm_final_g,
                           bm=1024, routed=True)

    za3 = za.reshape(batch, seq, ZA_W)
    kv_prompt = []
    for g, (win, _) in enumerate(ATT_GROUPS):
        keep = min(win, seq)
        k = za3[:, seq - keep:, ATT_WIDTH + g * GROUP_W:ATT_WIDTH + (g + 1) * GROUP_W]
        v = za3[:, seq - keep:, 2 * ATT_WIDTH + g * GROUP_W:2 * ATT_WIDTH + (g + 1) * GROUP_W]
        kv_prompt.append(jnp.stack([k, v], axis=2).reshape(1, batch, keep, 2, HEADS, HEAD_DIM))
    conv_prompt = conv_prompt[None]
    mem_kv_prompt = mkv.reshape(1, batch, N_MEM, 2, X_HEADS, X_HEAD_DIM)

    xs = x_sample.reshape(dec_b, D_MODEL)
    zs, gates_s = in_proj(xs, g_mix, w_in, bm=dec_b)
    z4 = zs.reshape(dec_b, ZA_W // HEAD_DIM // HEADS, HEADS, HEAD_DIM)
    o_as = window_decode(z4, caches).reshape(dec_b, GROUP_W)
    kv_sample = shift_caches(z4, caches)
    a_s = zs[:, 3 * ATT_WIDTH:3 * ATT_WIDTH + D_CONV].reshape(dec_b, 1, D_CONV)
    b_s = zs[:, 3 * ATT_WIDTH + D_CONV:].reshape(dec_b, 1, D_CONV)
    c_s, conv_sample = conv_step(a_s, b_s, state_conv, conv_w, conv_b, ln_g, ln_b)
    x1s = merge_branches(o_as, c_s.reshape(dec_b, D_CONV), gates_s, xs, w_pa16, w_pb16, w_o16, tm=dec_b)
    q_s = norm_matmul(x1s, g_x, w_xq, BF16, bm=dec_b)
    xo_s = cross_attention_decode(q_s, cache_mem_kv)
    y_sample = _trunk_tail(x1s, xo_s, w_xo, g_ffn, w_rg, b_rg, w_re, b_re, w_ein, w_eout, norm_final_g,
                           bm=dec_b, routed=False)

    return (y_prompt.reshape(batch, seq, D_MODEL), y_sample.reshape(dec_b, 1, D_MODEL),
            kv_prompt[0], kv_prompt[1], kv_prompt[2], conv_prompt, mem_kv_prompt,
            kv_sample[0], kv_sample[1], kv_sample[2], conv_sample)
```

```python
import functools

import jax
import jax.numpy as jnp
from jax import lax
from jax.experimental import pallas as pl
from jax.experimental.pallas import tpu as pltpu

F32 = jnp.float32
BF16 = jnp.bfloat16

D_MODEL = 2048
ATT_GROUPS = ((128, 1), (512, 4), (2048, 16))
HEADS = 4
HEAD_DIM = 128
GROUP_W = HEADS * HEAD_DIM
ATT_WIDTH = len(ATT_GROUPS) * GROUP_W
D_CONV = 1536
CONV_K = 31
N_MEM = 256
X_HEADS = 4
X_HEAD_DIM = D_MODEL // X_HEADS
N_EXPERT_GROUPS = 4
EXPERTS_PER_GROUP = 4
N_EXPERTS = 16
D_EXPERT = 512
ZA_W = 3 * ATT_WIDTH + 2 * D_CONV
GATE_W = 2 * D_MODEL
RMS_EPS = 1e-6
LN_EPS = 1e-5
NEG_INF = -1e30
BAND = 128
LANES = 128
ROUTER_GROUP_LANE = N_EXPERTS

VMEM_LIMIT = 56 * 1024 * 1024


def _params(*sem):
    return pltpu.CompilerParams(dimension_semantics=sem, vmem_limit_bytes=VMEM_LIMIT)


def _rms_rows(x, g):
    ms = jnp.mean(x * x, axis=-1, keepdims=True)
    return x * lax.rsqrt(ms + RMS_EPS) * g


def _store_normed(x_ref, g_ref, xn_ref, chunk=256):
    rows = x_ref.shape[0]
    step = min(chunk, rows)
    for r0 in range(0, rows, step):
        xn_ref[r0:r0 + step, :] = _rms_rows(x_ref[r0:r0 + step, :], g_ref[...]).astype(xn_ref.dtype)


def _norm_matmul_kernel(x_ref, g_ref, w_ref, o_ref, xn_ref):
    @pl.when(pl.program_id(1) == 0)
    def _():
        _store_normed(x_ref, g_ref, xn_ref)

    o_ref[...] = jnp.dot(xn_ref[...], w_ref[...], preferred_element_type=F32).astype(o_ref.dtype)


def norm_matmul(x, g, w, out_dtype, bm, bn=512):
    m, k = x.shape
    n = w.shape[1]
    bm = min(bm, m)
    return pl.pallas_call(
        _norm_matmul_kernel,
        grid=(m // bm, n // bn),
        in_specs=[pl.BlockSpec((bm, k), lambda i, j: (i, 0)),
                  pl.BlockSpec((1, k), lambda i, j: (0, 0)),
                  pl.BlockSpec((k, bn), lambda i, j: (0, j))],
        out_specs=pl.BlockSpec((bm, bn), lambda i, j: (i, j)),
        out_shape=jax.ShapeDtypeStruct((m, n), out_dtype),
        scratch_shapes=[pltpu.VMEM((bm, k), BF16)],
        compiler_params=_params("parallel", "arbitrary"),
        name="norm_matmul",
    )(x, g.reshape(1, k), w)


def _norm_matmul_keep_kernel(x_ref, g_ref, w_ref, o_ref, xn_ref):
    @pl.when(pl.program_id(1) == 0)
    def _():
        _store_normed(x_ref, g_ref, xn_ref)

    o_ref[...] = jnp.dot(xn_ref[...], w_ref[...], preferred_element_type=F32)


def _matmul_kernel(x_ref, w_ref, o_ref):
    o_ref[...] = jnp.dot(x_ref[...], w_ref[...], preferred_element_type=F32)


ZA_BN = 1280


def in_proj(x, g, w_za, w_gt, bm):
    m, k = x.shape
    bm = min(bm, m)
    za, xn = pl.pallas_call(
        _norm_matmul_keep_kernel,
        grid=(m // bm, ZA_W // ZA_BN),
        in_specs=[pl.BlockSpec((bm, k), lambda i, j: (i, 0)),
                  pl.BlockSpec((1, k), lambda i, j: (0, 0)),
                  pl.BlockSpec((k, ZA_BN), lambda i, j: (0, j))],
        out_specs=[pl.BlockSpec((bm, ZA_BN), lambda i, j: (i, j)),
                   pl.BlockSpec((bm, k), lambda i, j: (i, 0))],
        out_shape=[jax.ShapeDtypeStruct((m, ZA_W), F32), jax.ShapeDtypeStruct((m, k), BF16)],
        compiler_params=_params("parallel", "arbitrary"),
        name="in_proj_za",
    )(x, g.reshape(1, k), w_za)
    bn = GATE_W // 2
    gates = pl.pallas_call(
        _matmul_kernel,
        grid=(m // bm, GATE_W // bn),
        in_specs=[pl.BlockSpec((bm, k), lambda i, j: (i, 0)),
                  pl.BlockSpec((k, bn), lambda i, j: (0, j))],
        out_specs=pl.BlockSpec((bm, bn), lambda i, j: (i, j)),
        out_shape=jax.ShapeDtypeStruct((m, GATE_W), F32),
        compiler_params=_params("parallel", "arbitrary"),
        name="in_proj_gates",
    )(xn, w_gt)
    return za, gates


def _matmul_res_kernel(x_ref, w_ref, r_ref, o_ref):
    o_ref[...] = r_ref[...] + jnp.dot(x_ref[...], w_ref[...], preferred_element_type=F32)


def matmul_residual(x, w, res, bm, bn=512):
    m, k = x.shape
    n = w.shape[1]
    bm = min(bm, m)
    return pl.pallas_call(
        _matmul_res_kernel,
        grid=(m // bm, n // bn),
        in_specs=[pl.BlockSpec((bm, k), lambda i, j: (i, 0)),
                  pl.BlockSpec((k, bn), lambda i, j: (0, j)),
                  pl.BlockSpec((bm, bn), lambda i, j: (i, j))],
        out_specs=pl.BlockSpec((bm, bn), lambda i, j: (i, j)),
        out_shape=jax.ShapeDtypeStruct((m, n), F32),
        compiler_params=_params("parallel", "arbitrary"),
        name="matmul_residual",
    )(x, w, res)


ATT_R = 2048
ATT_UNROLL = 8


def _band_attn_kernel(*refs):
    n_g = len(ATT_GROUPS)
    in_refs = refs[:5 * n_g]
    o_ref = refs[5 * n_g]
    scr = refs[5 * n_g + 1:]
    kbufs, vbufs = scr[:n_g], scr[n_g:2 * n_g]
    obufs, mbufs, dbufs = scr[2 * n_g:3 * n_g], scr[3 * n_g:4 * n_g], scr[4 * n_g:5 * n_g]
    first_chunk = pl.program_id(1) == 0
    row = lax.broadcasted_iota(jnp.int32, (BAND, 2 * BAND), 0)
    col = lax.broadcasted_iota(jnp.int32, (BAND, 2 * BAND), 1)
    in_band = (col >= row) & (col <= row + BAND)
    in_cur = col >= BAND
    scale = HEAD_DIM ** -0.5
    nt = (((1,), (1,)), ((), ()))

    for g, (win, dil) in enumerate(ATT_GROUPS):
        q_ref, kc_ref, vc_ref, kp_ref, vp_ref = in_refs[5 * g:5 * g + 5]
        kbuf, vbuf, obuf, mbuf, dbuf = kbufs[g], vbufs[g], obufs[g], mbufs[g], dbufs[g]
        kbuf[0:win, :] = kp_ref[...]
        kbuf[win:win + ATT_R, :] = kc_ref[...]
        vbuf[0:win, :] = vp_ref[...]
        vbuf[win:win + ATT_R, :] = vc_ref[...]
        shift = dil.bit_length() - 1

        def sub_blocks(i4, carry, q_ref=q_ref, kbuf=kbuf, vbuf=vbuf, obuf=obuf, mbuf=mbuf, dbuf=dbuf,
                       win=win, dil=dil, shift=shift):
            done = []
            for u in range(ATT_UNROLL):
                i = i4 * ATT_UNROLL + u
                span = lax.shift_right_logical(i, shift)
                base = span * win + (i & (dil - 1))
                if dil == 1:
                    q_rows, kv_rows = pl.ds(base, BAND), pl.ds(base, 2 * BAND)
                else:
                    q_rows, kv_rows = pl.ds(base, BAND, stride=dil), pl.ds(base, 2 * BAND, stride=dil)
                q = q_ref[q_rows, :].astype(BF16)
                k = kbuf[kv_rows, :].astype(BF16)
                v = vbuf[kv_rows, :].astype(BF16)
                s = lax.dot_general(q, k, nt, preferred_element_type=F32) * scale
                has_prev = jnp.logical_not(first_chunk & (span == 0))
                s = jnp.where(in_band & (in_cur | has_prev), s, NEG_INF)
                m = jnp.max(s, axis=-1, keepdims=True)
                p = jnp.exp(s - m)
                den = jnp.sum(p, axis=-1, keepdims=True)
                done.append((q_rows, jnp.dot(p.astype(BF16), v, preferred_element_type=F32), m, den))
            for q_rows, o, m, den in done:
                obuf[q_rows, :] = o
                mbuf[q_rows, :] = jnp.broadcast_to(m, (BAND, HEAD_DIM))
                dbuf[q_rows, :] = jnp.broadcast_to(den, (BAND, HEAD_DIM))
            return carry

        lax.fori_loop(0, ATT_R // BAND // ATT_UNROLL, sub_blocks, 0)

    chunk = 256
    for r0 in range(0, ATT_R, chunk):
        rs = slice(r0, r0 + chunk)
        ms = [mb[rs, :] for mb in mbufs]
        m_all = functools.reduce(jnp.maximum, ms)
        ws = [jnp.exp(m - m_all) for m in ms]
        num = sum(w * ob[rs, :] for w, ob in zip(ws, obufs))
        den = sum(w * db[rs, :] for w, db in zip(ws, dbufs))
        o_ref[rs, :] = (num / den).astype(o_ref.dtype)


def band_attention(za, batch, seq):
    assert seq % ATT_R == 0
    nch = seq // ATT_R
    in_specs, scratch = [], []
    for g, (win, dil) in enumerate(ATT_GROUPS):
        assert win // dil == BAND and ATT_R % win == 0
        cols = [(part * len(ATT_GROUPS) + g) * HEADS for part in range(3)]
        cur = lambda c: pl.BlockSpec((ATT_R, HEAD_DIM), lambda b, ch, h, c=c: (b * nch + ch, c + h))
        prev = lambda c, win=win: pl.BlockSpec(
            (win, HEAD_DIM), lambda b, ch, h, c=c, win=win: (jnp.maximum((b * seq + ch * ATT_R) // win - 1, 0), c + h))
        in_specs += [cur(cols[0]), cur(cols[1]), cur(cols[2]), prev(cols[1]), prev(cols[2])]
    for _ in range(2):
        scratch += [pltpu.VMEM((win + ATT_R, HEAD_DIM), F32) for win, _ in ATT_GROUPS]
    scratch += [pltpu.VMEM((ATT_R, HEAD_DIM), F32)] * (3 * len(ATT_GROUPS))
    return pl.pallas_call(
        _band_attn_kernel,
        grid=(batch, nch, HEADS),
        in_specs=in_specs,
        out_specs=pl.BlockSpec((ATT_R, HEAD_DIM), lambda b, ch, h: (b * nch + ch, h)),
        out_shape=jax.ShapeDtypeStruct((batch * seq, GROUP_W), BF16),
        scratch_shapes=scratch,
        compiler_params=_params("parallel", "parallel", "parallel"),
        name="band_attention",
    )(*([za] * (5 * len(ATT_GROUPS))))


CONV_T = 256
CONV_HIST = 32
CONV_RC = 128
CONV_PHASES = 4
CONV_SLABS = D_CONV // LANES


def _layernorm_silu(y, g, b):
    mu = jnp.mean(y, axis=-1, keepdims=True)
    yc = y - mu
    var = jnp.mean(yc * yc, axis=-1, keepdims=True)
    yn = yc * lax.rsqrt(var + LN_EPS) * g + b
    return yn * jax.nn.sigmoid(yn)


def _conv_history(t, ubuf):
    @pl.when(t == 0)
    def _():
        ubuf[:, 0:CONV_HIST, :] = jnp.zeros((CONV_SLABS, CONV_HIST, LANES), F32)

    @pl.when(t > 0)
    def _():
        ubuf[:, 0:CONV_HIST, :] = ubuf[:, CONV_T:CONV_T + CONV_HIST, :]


def _conv_slab(j, a_ref, b_ref, w_ref, cb_ref, tail_ref, ubuf, ybuf):
    first = CONV_HIST - (CONV_K - 1)
    n = CONV_RC // CONV_PHASES
    js = slice(j * LANES, (j + 1) * LANES)
    u = a_ref[:, js] * jax.nn.sigmoid(b_ref[:, js])
    ubuf[j, CONV_HIST:CONV_HIST + CONV_T, :] = u
    tail_ref[0, :, js] = u[CONV_T - CONV_HIST:, :]
    for r0 in range(0, CONV_T, CONV_RC):
        accs = [jnp.broadcast_to(cb_ref[:, js], (n, LANES))] * CONV_PHASES
        for s in range(first, first + CONV_K + CONV_PHASES - 1):
            rows = ubuf[j, pl.ds(r0 + s, n, stride=CONV_PHASES), :]
            for p in range(CONV_PHASES):
                k = s - first - p
                if 0 <= k < CONV_K:
                    accs[p] = accs[p] + w_ref[k:k + 1, js] * rows
        for p in range(CONV_PHASES):
            ybuf[j, pl.ds(r0 + p, n, stride=CONV_PHASES), :] = accs[p]
    return jnp.sum(ybuf[j], axis=-1, keepdims=True)


def _conv_centered_sq(j, mu, ybuf):
    yc = ybuf[j] - mu
    return jnp.sum(yc * yc, axis=-1, keepdims=True)


def _conv_norm_slab(j, mu, inv, lg_ref, lb_ref, c_ref, ybuf):
    js = slice(j * LANES, (j + 1) * LANES)
    yn = (ybuf[j] - mu) * inv * lg_ref[:, js] + lb_ref[:, js]
    c_ref[:, js] = (yn * jax.nn.sigmoid(yn)).astype(c_ref.dtype)


def _merge_kernel(oa_ref, c_ref, ga_ref, gb_ref, x_ref, wpa_ref, wpb_ref, wo_ref, y_ref):
    ta = jnp.dot(oa_ref[...], wpa_ref[...], preferred_element_type=F32)
    tb = jnp.dot(c_ref[...], wpb_ref[...], preferred_element_type=F32)
    hmix = jax.nn.sigmoid(ga_ref[...]) * ta + jax.nn.sigmoid(gb_ref[...]) * tb
    y_ref[...] = x_ref[...] + jnp.dot(hmix.astype(BF16), wo_ref[...], preferred_element_type=F32)


def _conv_merge_kernel(oa_ref, a_ref, b_ref, ga_ref, gb_ref, x_ref, cw_ref, cb_ref, lg_ref, lb_ref,
                       wpa_ref, wpb_ref, wo_ref, y_ref, tail_ref, ubuf, ybuf, c_scr):
    _conv_history(pl.program_id(1), ubuf)
    total = jnp.zeros((CONV_T, 1), F32)
    for j in range(CONV_SLABS):
        total = total + _conv_slab(j, a_ref, b_ref, cw_ref, cb_ref, tail_ref, ubuf, ybuf)
    mu = total * (1.0 / D_CONV)
    sq = jnp.zeros((CONV_T, 1), F32)
    for j in range(CONV_SLABS):
        sq = sq + _conv_centered_sq(j, mu, ybuf)
    inv = lax.rsqrt(sq * (1.0 / D_CONV) + LN_EPS)
    for j in range(CONV_SLABS):
        _conv_norm_slab(j, mu, inv, lg_ref, lb_ref, c_scr, ybuf)
    _merge_kernel(oa_ref, c_scr, ga_ref, gb_ref, x_ref, wpa_ref, wpb_ref, wo_ref, y_ref)


def conv_merge_branches(o_a, za, gates, x, batch, seq, conv_w, conv_b, ln_g, ln_b, w_pa, w_pb, w_out):
    nt = seq // CONV_T
    row = lambda b, t: (b * nt + t, 0)
    const = lambda b, t: (0, 0)
    once = dict(pipeline_mode=pl.Buffered(1))
    glu = 3 * ATT_WIDTH // D_CONV
    y, tail = pl.pallas_call(
        _conv_merge_kernel,
        grid=(batch, nt),
        in_specs=[pl.BlockSpec((CONV_T, GROUP_W), row),
                  pl.BlockSpec((CONV_T, D_CONV), lambda b, t: (b * nt + t, glu)),
                  pl.BlockSpec((CONV_T, D_CONV), lambda b, t: (b * nt + t, glu + 1)),
                  pl.BlockSpec((CONV_T, D_MODEL), lambda b, t: (b * nt + t, 0)),
                  pl.BlockSpec((CONV_T, D_MODEL), lambda b, t: (b * nt + t, 1)),
                  pl.BlockSpec((CONV_T, D_MODEL), row),
                  pl.BlockSpec((CONV_K, D_CONV), const),
                  pl.BlockSpec((1, D_CONV), const),
                  pl.BlockSpec((1, D_CONV), const),
                  pl.BlockSpec((1, D_CONV), const),
                  pl.BlockSpec((GROUP_W, D_MODEL), const, **once),
                  pl.BlockSpec((D_CONV, D_MODEL), const, **once),
                  pl.BlockSpec((D_MODEL, D_MODEL), const, **once)],
        out_specs=[pl.BlockSpec((CONV_T, D_MODEL), row),
                   pl.BlockSpec((1, CONV_HIST, D_CONV), lambda b, t: (b, 0, 0))],
        out_shape=[jax.ShapeDtypeStruct((batch * seq, D_MODEL), F32),
                   jax.ShapeDtypeStruct((batch, CONV_HIST, D_CONV), F32)],
        scratch_shapes=[pltpu.VMEM((CONV_SLABS, CONV_HIST + CONV_T, LANES), F32),
                        pltpu.VMEM((CONV_SLABS, CONV_T, LANES), F32),
                        pltpu.VMEM((CONV_T, D_CONV), BF16)],
        compiler_params=_params("parallel", "arbitrary"),
        name="conv_merge_branches",
    )(o_a, za, za, gates, gates, x, conv_w, conv_b.reshape(1, D_CONV), ln_g.reshape(1, D_CONV),
      ln_b.reshape(1, D_CONV), w_pa, w_pb, w_out)
    return y, tail[:, CONV_HIST - (CONV_K - 1):]


def merge_branches(o_a, c, gates, x, w_pa, w_pb, w_out, tm):
    m = x.shape[0]
    tm = min(tm, m)
    row = lambda i: (i, 0)
    const = lambda i: (0, 0)
    once = dict(pipeline_mode=pl.Buffered(1))
    return pl.pallas_call(
        _merge_kernel,
        grid=(m // tm,),
        in_specs=[
            pl.BlockSpec((tm, GROUP_W), row),
            pl.BlockSpec((tm, D_CONV), row),
            pl.BlockSpec((tm, D_MODEL), lambda i: (i, 0)),
            pl.BlockSpec((tm, D_MODEL), lambda i: (i, 1)),
            pl.BlockSpec((tm, D_MODEL), row),
            pl.BlockSpec((GROUP_W, D_MODEL), const, **once),
            pl.BlockSpec((D_CONV, D_MODEL), const, **once),
            pl.BlockSpec((D_MODEL, D_MODEL), const, **once)],
        out_specs=pl.BlockSpec((tm, D_MODEL), row),
        out_shape=jax.ShapeDtypeStruct((m, D_MODEL), F32),
        compiler_params=_params("parallel"),
        name="merge_branches",
    )(o_a, c, gates, gates, x, w_pa, w_pb, w_out)


def _xattn_kernel(q_ref, kv_ref, o_ref):
    scale = X_HEAD_DIM ** -0.5
    nt = (((1,), (1,)), ((), ()))
    for h in range(X_HEADS):
        hs = slice(h * X_HEAD_DIM, (h + 1) * X_HEAD_DIM)
        vs = slice(D_MODEL + h * X_HEAD_DIM, D_MODEL + (h + 1) * X_HEAD_DIM)
        s = lax.dot_general(q_ref[:, hs], kv_ref[:, hs].astype(BF16), nt, preferred_element_type=F32) * scale
        m = jnp.max(s, axis=-1, keepdims=True)
        p = jnp.exp(s - m)
        p = p / jnp.sum(p, axis=-1, keepdims=True)
        o = jnp.dot(p.astype(BF16), kv_ref[:, vs].astype(BF16), preferred_element_type=F32)
        o_ref[:, hs] = o.astype(o_ref.dtype)


def cross_attention_prompt(q, mkv, batch, seq, tm=512):
    nt = seq // tm
    return pl.pallas_call(
        _xattn_kernel,
        grid=(batch, nt),
        in_specs=[pl.BlockSpec((tm, D_MODEL), lambda b, t: (b * nt + t, 0)),
                  pl.BlockSpec((N_MEM, 2 * D_MODEL), lambda b, t: (b, 0))],
        out_specs=pl.BlockSpec((tm, D_MODEL), lambda b, t: (b * nt + t, 0)),
        out_shape=jax.ShapeDtypeStruct((batch * seq, D_MODEL), BF16),
        compiler_params=_params("parallel", "arbitrary"),
        name="cross_attention_prompt",
    )(q, mkv)


def _xattn_decode_kernel(q_ref, kv_ref, o_ref):
    scale = X_HEAD_DIM ** -0.5
    q = q_ref[0].astype(F32)
    k = kv_ref[0, 0, :, 0]
    v = kv_ref[0, 0, :, 1]
    s = jnp.sum(k * q, axis=-1, keepdims=True) * scale
    m = jnp.max(s, axis=0, keepdims=True)
    p = jnp.exp(s - m)
    p = p / jnp.sum(p, axis=0, keepdims=True)
    o_ref[0] = jnp.sum(p * v, axis=0, keepdims=True).astype(o_ref.dtype)


def cross_attention_decode(q, cache_mem_kv):
    b = q.shape[0]
    q4 = q.reshape(b, 1, X_HEADS, X_HEAD_DIM)
    o = pl.pallas_call(
        _xattn_decode_kernel,
        grid=(b,),
        in_specs=[pl.BlockSpec((1, 1, X_HEADS, X_HEAD_DIM), lambda i: (i, 0, 0, 0)),
                  pl.BlockSpec((1, 1, N_MEM, 2, X_HEADS, X_HEAD_DIM), lambda i: (0, i, 0, 0, 0, 0))],
        out_specs=pl.BlockSpec((1, 1, X_HEADS, X_HEAD_DIM), lambda i: (i, 0, 0, 0)),
        out_shape=jax.ShapeDtypeStruct((b, 1, X_HEADS, X_HEAD_DIM), BF16),
        compiler_params=_params("parallel"),
        name="cross_attention_decode",
    )(q4, cache_mem_kv)
    return o.reshape(b, D_MODEL)


N_BUCKETS = N_EXPERT_GROUPS * 6
PAIR_LO = (0, 0, 0, 1, 1, 2)
PAIR_HI = (1, 2, 3, 2, 3, 3)
ROW_TILES = D_MODEL // LANES
PAYLOAD_ROWS = ROW_TILES + 8
MOE_TM = 256


def _route(logits):
    lane = lax.broadcasted_iota(jnp.int32, logits.shape, 1)
    lanef = lane.astype(F32)
    big = float(LANES)
    is_group = (lane >= ROUTER_GROUP_LANE) & (lane < ROUTER_GROUP_LANE + N_EXPERT_GROUPS)
    lg = jnp.where(is_group, logits, -jnp.inf)
    mg = jnp.max(lg, axis=-1, keepdims=True)
    p_sel = 1.0 / jnp.sum(jnp.exp(lg - mg), axis=-1, keepdims=True)
    gsel = jnp.min(jnp.where(lg == mg, lanef, big), axis=-1, keepdims=True) - ROUTER_GROUP_LANE
    group_of_lane = lax.shift_right_logical(lane, 2)
    in_group = (lane < N_EXPERTS) & (group_of_lane == gsel.astype(jnp.int32))
    le = jnp.where(in_group, logits, -jnp.inf)
    v1 = jnp.max(le, axis=-1, keepdims=True)
    i1 = jnp.min(jnp.where(le == v1, lanef, big), axis=-1, keepdims=True)
    le2 = jnp.where(lanef == i1, -jnp.inf, le)
    v2 = jnp.max(le2, axis=-1, keepdims=True)
    i2 = jnp.min(jnp.where(le2 == v2, lanef, big), axis=-1, keepdims=True)
    t = jnp.exp(v2 - v1)
    tot = 1.0 + t
    return gsel, i1, i2, (1.0 / tot) * p_sel, (t / tot) * p_sel


def _router_routed_kernel(x_ref, g_ref, w_ref, b_ref, xd_ref, meta_ref, cnt_ref, carry_ref):
    tm = x_ref.shape[0]

    @pl.when(pl.program_id(0) == 0)
    def _():
        carry_ref[...] = jnp.zeros_like(carry_ref)

    xn = _rms_rows(x_ref[...], g_ref[...])
    logits = _router_logits(xn, w_ref, b_ref)
    gsel, i1, i2, p1, p2 = _route(logits)
    lo = jnp.minimum(i1, i2) - EXPERTS_PER_GROUP * gsel
    hi = jnp.maximum(i1, i2) - EXPERTS_PER_GROUP * gsel
    bucket = gsel * 6.0 + lo * (7.0 - lo) * 0.5 + (hi - lo - 1.0)
    gate_lo = jnp.where(i1 < i2, p1, p2)
    gate_hi = jnp.where(i1 < i2, p2, p1)

    lane = lax.broadcasted_iota(jnp.int32, (tm, LANES), 1)
    onehot = (lane.astype(F32) == bucket).astype(F32)
    r_i = lax.broadcasted_iota(jnp.int32, (tm, tm), 0)
    c_i = lax.broadcasted_iota(jnp.int32, (tm, tm), 1)
    before = (c_i < r_i).astype(BF16)
    rank_local = jnp.dot(before, onehot.astype(BF16), preferred_element_type=F32)
    rank = jnp.sum(onehot * (rank_local + carry_ref[0:1, :]), axis=-1, keepdims=True)
    carry_ref[0:1, :] = carry_ref[0:1, :] + jnp.sum(onehot, axis=0, keepdims=True)
    cnt_ref[...] = jnp.broadcast_to(carry_ref[0:1, :], cnt_ref.shape)
    meta_ref[...] = jnp.where(lane == 0, bucket, jnp.where(lane == 1, rank, 0.0))

    for j in range(ROW_TILES):
        xd_ref[pl.ds(j, tm, stride=PAYLOAD_ROWS), :] = xn[:, j * LANES:(j + 1) * LANES]
    xd_ref[pl.ds(ROW_TILES, tm, stride=PAYLOAD_ROWS), :] = jnp.where(
        lane == 0, gate_lo, jnp.where(lane == 1, gate_hi, 0.0))
    for j in range(ROW_TILES + 1, PAYLOAD_ROWS):
        xd_ref[pl.ds(j, tm, stride=PAYLOAD_ROWS), :] = jnp.zeros((tm, LANES), F32)


def _router_weights(w_rg, b_rg, w_re, b_re):
    pad = LANES - N_EXPERTS - N_EXPERT_GROUPS
    w = jnp.concatenate([w_re, w_rg, jnp.zeros((D_MODEL, pad), F32)], axis=1)
    b = jnp.concatenate([b_re, b_rg, jnp.zeros((pad,), F32)]).reshape(1, LANES)
    hi = w.astype(BF16)
    lo = (w - hi.astype(F32)).astype(BF16)
    return jnp.stack([hi, lo]), b


def _router_logits(xn, w_ref, b_ref):
    hi = xn.astype(BF16)
    lo = (xn - hi.astype(F32)).astype(BF16)
    acc = jnp.dot(hi, w_ref[0], preferred_element_type=F32)
    acc = acc + jnp.dot(lo, w_ref[0], preferred_element_type=F32)
    acc = acc + jnp.dot(hi, w_ref[1], preferred_element_type=F32)
    return acc + b_ref[...]


def moe_router_routed(x, g, w_rg, b_rg, w_re, b_re, tm=512):
    m = x.shape[0]
    w, b = _router_weights(w_rg, b_rg, w_re, b_re)
    row = lambda i: (i, 0)
    const = lambda i: (0, 0)
    return pl.pallas_call(
        _router_routed_kernel,
        grid=(m // tm,),
        in_specs=[pl.BlockSpec((tm, D_MODEL), row), pl.BlockSpec((1, D_MODEL), const),
                  pl.BlockSpec((2, D_MODEL, LANES), lambda i: (0, 0, 0)), pl.BlockSpec((1, LANES), const)],
        out_specs=[pl.BlockSpec((tm * PAYLOAD_ROWS, LANES), row), pl.BlockSpec((tm, LANES), row),
                   pl.BlockSpec((8, LANES), const)],
        out_shape=[jax.ShapeDtypeStruct((m * PAYLOAD_ROWS, LANES), F32), jax.ShapeDtypeStruct((m, LANES), F32),
                   jax.ShapeDtypeStruct((8, LANES), F32)],
        scratch_shapes=[pltpu.VMEM((8, LANES), F32)],
        compiler_params=_params("arbitrary"),
        name="moe_router_routed",
    )(x, g.reshape(1, D_MODEL), w, b)


INV_CHUNK = 2048


def _slot_maps_kernel(bucket_ref, rank_ref, first_ref, dest_ref, inv_ref):
    step = pl.program_id(0)

    @pl.when(step == 0)
    def _():
        def clear(k, c):
            inv_ref[k] = 0
            return c
        lax.fori_loop(0, inv_ref.shape[0], clear, 0, unroll=8)

    def put(t, c):
        slot = first_ref[bucket_ref[0, 0, t]] + rank_ref[0, 0, t]
        dest_ref[0, 0, t] = slot
        inv_ref[slot] = step * INV_CHUNK + t
        return c
    lax.fori_loop(0, INV_CHUNK, put, 0, unroll=8)


def slot_maps(bucket, rank, first_slot, n_slots):
    m = bucket.shape[0]
    nc = m // INV_CHUNK
    smem = dict(memory_space=pltpu.SMEM)
    chunk = pl.BlockSpec((1, 1, INV_CHUNK), lambda i: (i, 0, 0), **smem)
    dest, inv = pl.pallas_call(
        _slot_maps_kernel,
        grid=(nc,),
        in_specs=[chunk, chunk, pl.BlockSpec(**smem)],
        out_specs=[chunk, pl.BlockSpec(**smem)],
        out_shape=[jax.ShapeDtypeStruct((nc, 1, INV_CHUNK), jnp.int32),
                   jax.ShapeDtypeStruct((n_slots,), jnp.int32)],
        compiler_params=_params("arbitrary"),
        name="slot_maps",
    )(bucket.reshape(nc, 1, INV_CHUNK), rank.reshape(nc, 1, INV_CHUNK), first_slot)
    return dest.reshape(m), inv


def _start_row_gather(idx_ref, src_hbm, dst, sem, n_items, rows):
    def start(r2, c):
        for queue in range(2):
            r = 2 * r2 + queue
            first = pl.multiple_of(idx_ref[0, 0, r] * rows, 8)
            pltpu.make_async_copy(src_hbm.at[pl.ds(first, rows)], dst.at[pl.ds(r * rows, rows)],
                                  sem).start(priority=queue)
        return c
    lax.fori_loop(0, n_items // 2, start, 0, unroll=4)


def _wait_row_gather(src_hbm, dst, sem, n_items, rows):
    pltpu.make_async_copy(src_hbm.at[pl.ds(0, n_items * rows)], dst, sem).wait()


def _moe_routed_kernel(ea_ref, eb_ref, nv_ref, inv_ref, inv_next_ref, xd_hbm, w1a_ref, w1b_ref, w2a_ref, w2b_ref,
                       ys_ref, xbuf, sems, x_scr):
    del ea_ref, eb_ref
    i = pl.program_id(0)
    n_valid = nv_ref[0]
    slot = i % 2
    tm = MOE_TM

    @pl.when(i == 0)
    def _():
        _start_row_gather(inv_ref, xd_hbm, xbuf.at[0], sems.at[0], tm, PAYLOAD_ROWS)

    @pl.when(i + 1 < n_valid)
    def _():
        _start_row_gather(inv_next_ref, xd_hbm, xbuf.at[1 - slot], sems.at[1 - slot], tm, PAYLOAD_ROWS)

    @pl.when(i < n_valid)
    def _():
        buf = xbuf.at[slot]
        _wait_row_gather(xd_hbm, buf, sems.at[slot], tm, PAYLOAD_ROWS)
        for j in range(ROW_TILES):
            x_scr[:, j * LANES:(j + 1) * LANES] = buf[pl.ds(j, tm, stride=PAYLOAD_ROWS), :].astype(BF16)
        gates = buf[pl.ds(ROW_TILES, tm, stride=PAYLOAD_ROWS), :]
        x = x_scr[...]
        hu = jnp.dot(x, w1a_ref[0], preferred_element_type=F32)
        ha = jax.nn.silu(hu[:, :D_EXPERT]) * hu[:, D_EXPERT:] * gates[:, 0:1]
        hu = jnp.dot(x, w1b_ref[0], preferred_element_type=F32)
        hb = jax.nn.silu(hu[:, :D_EXPERT]) * hu[:, D_EXPERT:] * gates[:, 1:2]
        y = (jnp.dot(ha.astype(BF16), w2a_ref[0], preferred_element_type=F32)
             + jnp.dot(hb.astype(BF16), w2b_ref[0], preferred_element_type=F32))
        for j in range(ROW_TILES):
            ys_ref[pl.ds(j, tm, stride=ROW_TILES), :] = y[:, j * LANES:(j + 1) * LANES]

    @pl.when(i >= n_valid)
    def _():
        ys_ref[...] = jnp.zeros_like(ys_ref)


def moe_experts_routed(xd, inv, tile_ea, tile_eb, n_valid, w_ein, w_eout):
    n_tiles = tile_ea.shape[0]
    tm = MOE_TM
    inv3 = inv.reshape(n_tiles, 1, tm)
    smem = dict(memory_space=pltpu.SMEM)
    grid_spec = pltpu.PrefetchScalarGridSpec(
        num_scalar_prefetch=3,
        grid=(n_tiles,),
        in_specs=[
            pl.BlockSpec((1, 1, tm), lambda i, ea, eb, nv: (i, 0, 0), **smem),
            pl.BlockSpec((1, 1, tm), lambda i, ea, eb, nv: (jnp.minimum(i + 1, n_tiles - 1), 0, 0), **smem),
            pl.BlockSpec(memory_space=pl.ANY),
            pl.BlockSpec((1, D_MODEL, 2 * D_EXPERT), lambda i, ea, eb, nv: (ea[i], 0, 0)),
            pl.BlockSpec((1, D_MODEL, 2 * D_EXPERT), lambda i, ea, eb, nv: (eb[i], 0, 0)),
            pl.BlockSpec((1, D_EXPERT, D_MODEL), lambda i, ea, eb, nv: (ea[i], 0, 0)),
            pl.BlockSpec((1, D_EXPERT, D_MODEL), lambda i, ea, eb, nv: (eb[i], 0, 0))],
        out_specs=pl.BlockSpec((tm * ROW_TILES, LANES), lambda i, ea, eb, nv: (i, 0)),
        scratch_shapes=[pltpu.VMEM((2, tm * PAYLOAD_ROWS, LANES), F32), pltpu.SemaphoreType.DMA((2,)),
                        pltpu.VMEM((tm, D_MODEL), BF16)])
    return pl.pallas_call(
        _moe_routed_kernel,
        grid_spec=grid_spec,
        out_shape=jax.ShapeDtypeStruct((n_tiles * tm * ROW_TILES, LANES), F32),
        compiler_params=_params("arbitrary"),
        name="moe_experts_routed",
    )(tile_ea, tile_eb, n_valid, inv3, inv3, xd, w_ein, w_ein, w_eout, w_eout)


def _combine_kernel(dest_ref, dest_next_ref, ys_hbm, res_ref, gf_ref, y_ref, buf, sems):
    i = pl.program_id(0)
    slot = i % 2
    tm = res_ref.shape[0]

    @pl.when(i == 0)
    def _():
        _start_row_gather(dest_ref, ys_hbm, buf.at[0], sems.at[0], tm, ROW_TILES)

    @pl.when(i + 1 < pl.num_programs(0))
    def _():
        _start_row_gather(dest_next_ref, ys_hbm, buf.at[1 - slot], sems.at[1 - slot], tm, ROW_TILES)

    cur = buf.at[slot]
    _wait_row_gather(ys_hbm, cur, sems.at[slot], tm, ROW_TILES)
    for j in range(ROW_TILES):
        js = slice(j * LANES, (j + 1) * LANES)
        y_ref[:, js] = res_ref[:, js] + cur[pl.ds(j, tm, stride=ROW_TILES), :]
    y_ref[...] = _rms_rows(y_ref[...], gf_ref[...])


def moe_combine_final(ys, dest, res, g_final, tm=256):
    m = res.shape[0]
    nt = m // tm
    dest3 = dest.reshape(nt, 1, tm)
    smem = dict(memory_space=pltpu.SMEM)
    return pl.pallas_call(
        _combine_kernel,
        grid=(nt,),
        in_specs=[pl.BlockSpec((1, 1, tm), lambda i: (i, 0, 0), **smem),
                  pl.BlockSpec((1, 1, tm), lambda i: (jnp.minimum(i + 1, nt - 1), 0, 0), **smem),
                  pl.BlockSpec(memory_space=pl.ANY),
                  pl.BlockSpec((tm, D_MODEL), lambda i: (i, 0)),
                  pl.BlockSpec((1, D_MODEL), lambda i: (0, 0))],
        out_specs=pl.BlockSpec((tm, D_MODEL), lambda i: (i, 0)),
        out_shape=jax.ShapeDtypeStruct((m, D_MODEL), F32),
        scratch_shapes=[pltpu.VMEM((2, tm * ROW_TILES, LANES), F32), pltpu.SemaphoreType.DMA((2,))],
        compiler_params=_params("arbitrary"),
        name="moe_combine_final",
    )(dest3, dest3, ys, res, g_final.reshape(1, D_MODEL))


def moe_routed(x, g_ffn, w_rg, b_rg, w_re, b_re, w_ein, w_eout, g_final):
    m = x.shape[0]
    tm = MOE_TM
    n_tiles = m // tm + N_BUCKETS
    xd, meta, cnt = moe_router_routed(x, g_ffn, w_rg, b_rg, w_re, b_re)
    bucket = meta[:, 0].astype(jnp.int32)
    rank = meta[:, 1].astype(jnp.int32)
    counts = cnt[0, :N_BUCKETS].astype(jnp.int32)
    tiles_per_bucket = (counts + tm - 1) // tm
    tile_end = jnp.cumsum(tiles_per_bucket)
    tile_start = tile_end - tiles_per_bucket
    n_valid = tile_end[-1]
    tile_id = jnp.arange(n_tiles, dtype=jnp.int32)
    tile_bucket = jnp.searchsorted(tile_end, jnp.minimum(tile_id, n_valid - 1), side="right").astype(jnp.int32)
    group = tile_bucket // 6
    tile_ea = group * EXPERTS_PER_GROUP + jnp.take(jnp.array(PAIR_LO, jnp.int32), tile_bucket % 6)
    tile_eb = group * EXPERTS_PER_GROUP + jnp.take(jnp.array(PAIR_HI, jnp.int32), tile_bucket % 6)
    dest, inv = slot_maps(bucket, rank, tile_start * tm, n_tiles * tm)
    ys = moe_experts_routed(xd, inv, tile_ea, tile_eb, n_valid.reshape(1), w_ein, w_eout)
    return moe_combine_final(ys, dest, x, g_final)


def _router_kernel(x_ref, g_ref, w_ref, b_ref, xn_ref, gate_ref):
    xn = _rms_rows(x_ref[...], g_ref[...])
    xn_ref[...] = xn.astype(BF16)
    logits = _router_logits(xn, w_ref, b_ref)
    _, i1, i2, p1, p2 = _route(logits)
    lanef = lax.broadcasted_iota(jnp.int32, logits.shape, 1).astype(F32)
    gate_ref[...] = jnp.where(lanef == i1, p1, 0.0) + jnp.where(lanef == i2, p2, 0.0)


def moe_router(x, g, w_rg, b_rg, w_re, b_re, tm):
    m = x.shape[0]
    tm = min(tm, m)
    w, b = _router_weights(w_rg, b_rg, w_re, b_re)
    row = lambda i: (i, 0)
    const = lambda i: (0, 0)
    return pl.pallas_call(
        _router_kernel,
        grid=(m // tm,),
        in_specs=[pl.BlockSpec((tm, D_MODEL), row), pl.BlockSpec((1, D_MODEL), const),
                  pl.BlockSpec((2, D_MODEL, LANES), lambda i: (0, 0, 0)), pl.BlockSpec((1, LANES), const)],
        out_specs=[pl.BlockSpec((tm, D_MODEL), row), pl.BlockSpec((tm, LANES), row)],
        out_shape=[jax.ShapeDtypeStruct((m, D_MODEL), BF16), jax.ShapeDtypeStruct((m, LANES), F32)],
        compiler_params=_params("parallel"),
        name="moe_router",
    )(x, g.reshape(1, D_MODEL), w, b)


def _moe_dense_kernel(xn_ref, gate_ref, w1_ref, w2_ref, res_ref, gf_ref, y_ref):
    e = pl.program_id(1)

    @pl.when(e == 0)
    def _():
        y_ref[...] = res_ref[...]

    hu = jnp.dot(xn_ref[...], w1_ref[0], preferred_element_type=F32)
    h = jax.nn.silu(hu[:, :D_EXPERT]) * hu[:, D_EXPERT:]
    lane = lax.broadcasted_iota(jnp.int32, gate_ref.shape, 1)
    gate = jnp.sum(jnp.where(lane == e, gate_ref[...], 0.0), axis=-1, keepdims=True)
    h = h * gate
    y_ref[...] += jnp.dot(h.astype(BF16), w2_ref[0], preferred_element_type=F32)

    @pl.when(e == N_EXPERTS - 1)
    def _():
        y_ref[...] = _rms_rows(y_ref[...], gf_ref[...])


def moe_experts_final(xn, gate, w_ein, w_eout, res, g_final, tm):
    m = xn.shape[0]
    tm = min(tm, m)
    row = lambda i, e: (i, 0)
    return pl.pallas_call(
        _moe_dense_kernel,
        grid=(m // tm, N_EXPERTS),
        in_specs=[pl.BlockSpec((tm, D_MODEL), row), pl.BlockSpec((tm, LANES), row),
                  pl.BlockSpec((1, D_MODEL, 2 * D_EXPERT), lambda i, e: (e, 0, 0)),
                  pl.BlockSpec((1, D_EXPERT, D_MODEL), lambda i, e: (e, 0, 0)),
                  pl.BlockSpec((tm, D_MODEL), row),
                  pl.BlockSpec((1, D_MODEL), lambda i, e: (0, 0))],
        out_specs=pl.BlockSpec((tm, D_MODEL), row),
        out_shape=jax.ShapeDtypeStruct((m, D_MODEL), F32),
        compiler_params=_params("parallel", "arbitrary"),
        name="moe_experts_final",
    )(xn, gate, w_ein, w_eout, res, g_final.reshape(1, D_MODEL))


def _window_decode_kernel(z_ref, c1_ref, c2_ref, c3_ref, o_ref):
    scale = HEAD_DIM ** -0.5
    os, ms, ds = [], [], []
    for g, c_ref in enumerate((c1_ref, c2_ref, c3_ref)):
        q = z_ref[0, g:g + 1]
        k_new = z_ref[0, 3 + g:4 + g]
        v_new = z_ref[0, 6 + g:7 + g]
        k = c_ref[0, 0, :, 0, 0]
        v = c_ref[0, 0, :, 0, 1]
        s = jnp.sum(k * q, axis=-1, keepdims=True) * scale
        s_new = jnp.sum(k_new * q, axis=-1, keepdims=True) * scale
        m = jnp.maximum(jnp.max(s, axis=0, keepdims=True), s_new)
        p = jnp.exp(s - m)
        p_new = jnp.exp(s_new - m)
        den = jnp.sum(p, axis=0, keepdims=True) + p_new
        os.append(jnp.sum(p * v, axis=0, keepdims=True) + p_new * v_new)
        ms.append(m)
        ds.append(den)
    m_all = functools.reduce(jnp.maximum, ms)
    ws = [jnp.exp(m - m_all) for m in ms]
    num = sum(w * o for w, o in zip(ws, os))
    den = sum(w * d for w, d in zip(ws, ds))
    o_ref[0] = (num / den).astype(o_ref.dtype)


def window_decode(z4, caches):
    b = z4.shape[0]
    views, specs = [], []
    for cache, (win, dil) in zip(caches, ATT_GROUPS):
        n = cache.shape[2]
        assert n == win and n // dil == BAND
        views.append(cache.reshape(1, b, BAND, dil, 2, HEADS, HEAD_DIM))
        specs.append(pl.BlockSpec((1, 1, BAND, 1, 2, HEADS, HEAD_DIM), lambda i: (0, i, 0, 0, 0, 0, 0)))
    return pl.pallas_call(
        _window_decode_kernel,
        grid=(b,),
        in_specs=[pl.BlockSpec((1,) + z4.shape[1:], lambda i: (i, 0, 0, 0))] + specs,
        out_specs=pl.BlockSpec((1, 1, HEADS, HEAD_DIM), lambda i: (i, 0, 0, 0)),
        out_shape=jax.ShapeDtypeStruct((b, 1, HEADS, HEAD_DIM), BF16),
        compiler_params=_params("parallel"),
        name="window_decode",
    )(z4, *views)


SHIFT_ROWS = 64


def _shift_kernel(z_ref, c1_ref, c2_ref, c3_ref, o1_ref, o2_ref, o3_ref):
    for g, (c_ref, o_ref) in enumerate(((c1_ref, o1_ref), (c2_ref, o2_ref), (c3_ref, o3_ref))):
        n = c_ref.shape[2]
        full, rem = divmod(n - 1, SHIFT_ROWS)

        def move(j, carry, c_ref=c_ref, o_ref=o_ref):
            o_ref[0, 0, pl.ds(j * SHIFT_ROWS, SHIFT_ROWS)] = c_ref[0, 0, pl.ds(j * SHIFT_ROWS + 1, SHIFT_ROWS)]
            return carry

        lax.fori_loop(0, full, move, 0)
        if rem:
            o_ref[0, 0, pl.ds(full * SHIFT_ROWS, rem)] = c_ref[0, 0, pl.ds(full * SHIFT_ROWS + 1, rem)]
        o_ref[0, 0, n - 1, 0] = z_ref[0, 3 + g]
        o_ref[0, 0, n - 1, 1] = z_ref[0, 6 + g]


def shift_caches(z4, caches):
    b = z4.shape[0]
    specs = [pl.BlockSpec((1, 1) + c.shape[2:], lambda i: (0, i, 0, 0, 0, 0)) for c in caches]
    return pl.pallas_call(
        _shift_kernel,
        grid=(b,),
        in_specs=[pl.BlockSpec((1,) + z4.shape[1:], lambda i: (i, 0, 0, 0))] + specs,
        out_specs=specs,
        out_shape=[jax.ShapeDtypeStruct(c.shape, c.dtype) for c in caches],
        compiler_params=_params("parallel"),
        name="shift_caches",
    )(z4, *caches)


def _conv_step_kernel(a_ref, b_ref, s_ref, w_ref, cb_ref, lg_ref, lb_ref, c_ref, so_ref):
    hist = CONV_K - 1
    u = a_ref[...] * jax.nn.sigmoid(b_ref[...])
    y = (jnp.sum(s_ref[0] * w_ref[0:hist, :], axis=1, keepdims=True)
         + u * w_ref[hist:hist + 1, :] + cb_ref[...])
    c_ref[...] = _layernorm_silu(y, lg_ref[...], lb_ref[...]).astype(c_ref.dtype)
    so_ref[0, :, pl.ds(0, hist - 1), :] = s_ref[0, :, pl.ds(1, hist - 1), :]
    so_ref[0, :, pl.ds(hist - 1, 1), :] = u


def conv_step(a, b, state, conv_w, conv_b, ln_g, ln_b):
    bsz = a.shape[0]
    return pl.pallas_call(
        _conv_step_kernel,
        out_shape=[jax.ShapeDtypeStruct((bsz, 1, D_CONV), BF16), jax.ShapeDtypeStruct(state.shape, F32)],
        compiler_params=pltpu.CompilerParams(vmem_limit_bytes=VMEM_LIMIT),
        name="conv_step",
    )(a, b, state, conv_w, conv_b.reshape(1, D_CONV), ln_g.reshape(1, D_CONV), ln_b.reshape(1, D_CONV))


def _trunk_tail(x1, xo_in, w_xo, norm_ffn_g, w_rg, b_rg, w_re, b_re, w_ein, w_eout, norm_final_g, bm, routed):
    x2 = matmul_residual(xo_in, w_xo, x1, min(bm, 512), bn=D_MODEL)
    if routed:
        return moe_routed(x2, norm_ffn_g, w_rg, b_rg, w_re, b_re, w_ein, w_eout, norm_final_g)
    xn3, gate = moe_router(x2, norm_ffn_g, w_rg, b_rg, w_re, b_re, bm)
    return moe_experts_final(xn3, gate, w_ein, w_eout, x2, norm_final_g, bm)


def kernel(x_prompt, x_sample, mem_prompt, cache_kv_g1, cache_kv_g2, cache_kv_g3, state_conv, cache_mem_kv,
           norm_mix_g, w_in, conv_w, conv_b, conv_ln_g, conv_ln_b, w_proj_a, w_proj_b, w_out,
           norm_xattn_g, norm_mem_g, w_xq, w_xkv, w_xo, norm_ffn_g,
           w_router_group, b_router_group, w_router_expert, b_router_expert, w_expert_in, w_expert_out,
           norm_final_g):
    depth = norm_mix_g.shape[0]
    assert depth == 1, "single-layer trunk"
    batch, seq, _ = x_prompt.shape
    dec_b, dec_t, _ = x_sample.shape
    assert dec_t == 1
    (g_mix, w_in, conv_w, conv_b, ln_g, ln_b, w_pa, w_pb, w_o, g_x, g_mem, w_xq, w_xkv, w_xo, g_ffn,
     w_rg, b_rg, w_re, b_re, w_ein, w_eout) = [t[0] for t in (
         norm_mix_g, w_in, conv_w, conv_b, conv_ln_g, conv_ln_b, w_proj_a, w_proj_b, w_out, norm_xattn_g,
         norm_mem_g, w_xq, w_xkv, w_xo, norm_ffn_g, w_router_group, b_router_group, w_router_expert,
         b_router_expert, w_expert_in, w_expert_out)]
    w_pa16, w_pb16, w_o16 = w_pa.astype(BF16), w_pb.astype(BF16), w_o.astype(BF16)
    w_ein, w_eout = w_ein.astype(BF16), w_eout.astype(BF16)
    w_za, w_gt = w_in[:, :ZA_W].astype(BF16), w_in[:, ZA_W:].astype(BF16)
    w_xq, w_xkv, w_xo = w_xq.astype(BF16), w_xkv.astype(BF16), w_xo.astype(BF16)
    caches = (cache_kv_g1, cache_kv_g2, cache_kv_g3)

    m_p = batch * seq
    xp = x_prompt.reshape(m_p, D_MODEL)
    za, gates = in_proj(xp, g_mix, w_za, w_gt, bm=1024)
    o_a = band_attention(za, batch, seq)
    x1, conv_prompt = conv_merge_branches(o_a, za, gates, xp, batch, seq, conv_w, conv_b, ln_g, ln_b,
                                          w_pa16, w_pb16, w_o16)
    mkv = norm_matmul(mem_prompt.reshape(batch * N_MEM, D_MODEL), g_mem, w_xkv, F32, bm=1024)
    q = norm_matmul(x1, g_x, w_xq, BF16, bm=512, bn=D_MODEL)
    xo_in = cross_attention_prompt(q, mkv, batch, seq)
    y_prompt = _trunk_tail(x1, xo_in, w_xo, g_ffn, w_rg, b_rg, w_re, b_re, w_ein, w_eout, norm_final_g,
                           bm=1024, routed=True)

    za3 = za.reshape(batch, seq, ZA_W)
    kv_prompt = []
    for g, (win, _) in enumerate(ATT_GROUPS):
        keep = min(win, seq)
        k = za3[:, seq - keep:, ATT_WIDTH + g * GROUP_W:ATT_WIDTH + (g + 1) * GROUP_W]
        v = za3[:, seq - keep:, 2 * ATT_WIDTH + g * GROUP_W:2 * ATT_WIDTH + (g + 1) * GROUP_W]
        kv_prompt.append(jnp.stack([k, v], axis=2).reshape(1, batch, keep, 2, HEADS, HEAD_DIM))
    conv_prompt = conv_prompt[None]
    mem_kv_prompt = mkv.reshape(1, batch, N_MEM, 2, X_HEADS, X_HEAD_DIM)

    xs = x_sample.reshape(dec_b, D_MODEL)
    zs, gates_s = in_proj(xs, g_mix, w_za, w_gt, bm=dec_b)
    z4 = zs.reshape(dec_b, ZA_W // HEAD_DIM // HEADS, HEADS, HEAD_DIM)
    o_as = window_decode(z4, caches).reshape(dec_b, GROUP_W)
    kv_sample = shift_caches(z4, caches)
    a_s = zs[:, 3 * ATT_WIDTH:3 * ATT_WIDTH + D_CONV].reshape(dec_b, 1, D_CONV)
    b_s = zs[:, 3 * ATT_WIDTH + D_CONV:].reshape(dec_b, 1, D_CONV)
    c_s, conv_sample = conv_step(a_s, b_s, state_conv, conv_w, conv_b, ln_g, ln_b)
    x1s = merge_branches(o_as, c_s.reshape(dec_b, D_CONV), gates_s, xs, w_pa16, w_pb16, w_o16, tm=dec_b)
    q_s = norm_matmul(x1s, g_x, w_xq, BF16, bm=dec_b)
    xo_s = cross_attention_decode(q_s, cache_mem_kv)
    y_sample = _trunk_tail(x1s, xo_s, w_xo, g_ffn, w_rg, b_rg, w_re, b_re, w_ein, w_eout, norm_final_g,
                           bm=dec_b, routed=False)

    return (y_prompt.reshape(batch, seq, D_MODEL), y_sample.reshape(dec_b, 1, D_MODEL),
            kv_prompt[0], kv_prompt[1], kv_prompt[2], conv_prompt, mem_kv_prompt,
            kv_sample[0], kv_sample[1], kv_sample[2], conv_sample)
```

```python
import functools

import jax
import jax.numpy as jnp
from jax import lax
from jax.experimental import pallas as pl
from jax.experimental.pallas import tpu as pltpu

F32 = jnp.float32
BF16 = jnp.bfloat16

D_MODEL = 2048
ATT_GROUPS = ((128, 1), (512, 4), (2048, 16))
HEADS = 4
HEAD_DIM = 128
GROUP_W = HEADS * HEAD_DIM
ATT_WIDTH = len(ATT_GROUPS) * GROUP_W
D_CONV = 1536
CONV_K = 31
N_MEM = 256
X_HEADS = 4
X_HEAD_DIM = D_MODEL // X_HEADS
N_EXPERT_GROUPS = 4
EXPERTS_PER_GROUP = 4
N_EXPERTS = 16
D_EXPERT = 512
ZA_W = 3 * ATT_WIDTH + 2 * D_CONV
GATE_W = 2 * D_MODEL
RMS_EPS = 1e-6
LN_EPS = 1e-5
NEG_INF = -1e30
BAND = 128
LANES = 128
ROUTER_GROUP_LANE = N_EXPERTS

VMEM_LIMIT = 56 * 1024 * 1024


def _params(*sem):
    return pltpu.CompilerParams(dimension_semantics=sem, vmem_limit_bytes=VMEM_LIMIT)


def _rms_rows(x, g):
    ms = jnp.mean(x * x, axis=-1, keepdims=True)
    return x * lax.rsqrt(ms + RMS_EPS) * g


def _store_normed(x_ref, g_ref, xn_ref, chunk=256):
    rows = x_ref.shape[0]
    step = min(chunk, rows)
    for r0 in range(0, rows, step):
        xn_ref[r0:r0 + step, :] = _rms_rows(x_ref[r0:r0 + step, :], g_ref[...]).astype(xn_ref.dtype)


def _norm_matmul_kernel(x_ref, g_ref, w_ref, o_ref, xn_ref):
    @pl.when(pl.program_id(1) == 0)
    def _():
        _store_normed(x_ref, g_ref, xn_ref)

    o_ref[...] = jnp.dot(xn_ref[...], w_ref[...], preferred_element_type=F32).astype(o_ref.dtype)


def norm_matmul(x, g, w, out_dtype, bm, bn=512):
    m, k = x.shape
    n = w.shape[1]
    bm = min(bm, m)
    return pl.pallas_call(
        _norm_matmul_kernel,
        grid=(m // bm, n // bn),
        in_specs=[pl.BlockSpec((bm, k), lambda i, j: (i, 0)),
                  pl.BlockSpec((1, k), lambda i, j: (0, 0)),
                  pl.BlockSpec((k, bn), lambda i, j: (0, j))],
        out_specs=pl.BlockSpec((bm, bn), lambda i, j: (i, j)),
        out_shape=jax.ShapeDtypeStruct((m, n), out_dtype),
        scratch_shapes=[pltpu.VMEM((bm, k), BF16)],
        compiler_params=_params("parallel", "arbitrary"),
        name="norm_matmul",
    )(x, g.reshape(1, k), w)


def _norm_matmul_keep_kernel(x_ref, g_ref, w_ref, o_ref, xn_ref):
    @pl.when(pl.program_id(1) == 0)
    def _():
        _store_normed(x_ref, g_ref, xn_ref)

    o_ref[...] = jnp.dot(xn_ref[...], w_ref[...], preferred_element_type=F32)


def _matmul_kernel(x_ref, w_ref, o_ref):
    o_ref[...] = jnp.dot(x_ref[...], w_ref[...], preferred_element_type=F32)


ZA_BN = 1280


def in_proj(x, g, w_za, w_gt, bm):
    m, k = x.shape
    bm = min(bm, m)
    za, xn = pl.pallas_call(
        _norm_matmul_keep_kernel,
        grid=(m // bm, ZA_W // ZA_BN),
        in_specs=[pl.BlockSpec((bm, k), lambda i, j: (i, 0)),
                  pl.BlockSpec((1, k), lambda i, j: (0, 0)),
                  pl.BlockSpec((k, ZA_BN), lambda i, j: (0, j))],
        out_specs=[pl.BlockSpec((bm, ZA_BN), lambda i, j: (i, j)),
                   pl.BlockSpec((bm, k), lambda i, j: (i, 0))],
        out_shape=[jax.ShapeDtypeStruct((m, ZA_W), F32), jax.ShapeDtypeStruct((m, k), BF16)],
        compiler_params=_params("parallel", "arbitrary"),
        name="in_proj_za",
    )(x, g.reshape(1, k), w_za)
    bn = GATE_W // 2
    gates = pl.pallas_call(
        _matmul_kernel,
        grid=(m // bm, GATE_W // bn),
        in_specs=[pl.BlockSpec((bm, k), lambda i, j: (i, 0)),
                  pl.BlockSpec((k, bn), lambda i, j: (0, j))],
        out_specs=pl.BlockSpec((bm, bn), lambda i, j: (i, j)),
        out_shape=jax.ShapeDtypeStruct((m, GATE_W), F32),
        compiler_params=_params("parallel", "arbitrary"),
        name="in_proj_gates",
    )(xn, w_gt)
    return za, gates


def _matmul_res_kernel(x_ref, w_ref, r_ref, o_ref):
    o_ref[...] = r_ref[...] + jnp.dot(x_ref[...], w_ref[...], preferred_element_type=F32)


def matmul_residual(x, w, res, bm, bn=512):
    m, k = x.shape
    n = w.shape[1]
    bm = min(bm, m)
    return pl.pallas_call(
        _matmul_res_kernel,
        grid=(m // bm, n // bn),
        in_specs=[pl.BlockSpec((bm, k), lambda i, j: (i, 0)),
                  pl.BlockSpec((k, bn), lambda i, j: (0, j)),
                  pl.BlockSpec((bm, bn), lambda i, j: (i, j))],
        out_specs=pl.BlockSpec((bm, bn), lambda i, j: (i, j)),
        out_shape=jax.ShapeDtypeStruct((m, n), F32),
        compiler_params=_params("parallel", "arbitrary"),
        name="matmul_residual",
    )(x, w, res)


ATT_R = 2048
ATT_UNROLL = 8


def _band_attn_kernel(*refs):
    n_g = len(ATT_GROUPS)
    in_refs = refs[:5 * n_g]
    o_ref = refs[5 * n_g]
    scr = refs[5 * n_g + 1:]
    kbufs, vbufs = scr[:n_g], scr[n_g:2 * n_g]
    obufs, mbufs, dbufs = scr[2 * n_g:3 * n_g], scr[3 * n_g:4 * n_g], scr[4 * n_g:5 * n_g]
    first_chunk = pl.program_id(1) == 0
    row = lax.broadcasted_iota(jnp.int32, (BAND, 2 * BAND), 0)
    col = lax.broadcasted_iota(jnp.int32, (BAND, 2 * BAND), 1)
    in_band = (col >= row) & (col <= row + BAND)
    in_cur = col >= BAND
    scale = HEAD_DIM ** -0.5
    nt = (((1,), (1,)), ((), ()))

    for g, (win, dil) in enumerate(ATT_GROUPS):
        q_ref, kc_ref, vc_ref, kp_ref, vp_ref = in_refs[5 * g:5 * g + 5]
        kbuf, vbuf, obuf, mbuf, dbuf = kbufs[g], vbufs[g], obufs[g], mbufs[g], dbufs[g]
        kbuf[0:win, :] = kp_ref[...]
        kbuf[win:win + ATT_R, :] = kc_ref[...]
        vbuf[0:win, :] = vp_ref[...]
        vbuf[win:win + ATT_R, :] = vc_ref[...]
        shift = dil.bit_length() - 1

        def sub_blocks(i4, carry, q_ref=q_ref, kbuf=kbuf, vbuf=vbuf, obuf=obuf, mbuf=mbuf, dbuf=dbuf,
                       win=win, dil=dil, shift=shift):
            done = []
            for u in range(ATT_UNROLL):
                i = i4 * ATT_UNROLL + u
                span = lax.shift_right_logical(i, shift)
                base = span * win + (i & (dil - 1))
                if dil == 1:
                    q_rows, kv_rows = pl.ds(base, BAND), pl.ds(base, 2 * BAND)
                else:
                    q_rows, kv_rows = pl.ds(base, BAND, stride=dil), pl.ds(base, 2 * BAND, stride=dil)
                q = q_ref[q_rows, :].astype(BF16)
                k = kbuf[kv_rows, :].astype(BF16)
                v = vbuf[kv_rows, :].astype(BF16)
                s = lax.dot_general(q, k, nt, preferred_element_type=F32) * scale
                has_prev = jnp.logical_not(first_chunk & (span == 0))
                s = jnp.where(in_band & (in_cur | has_prev), s, NEG_INF)
                m = jnp.max(s, axis=-1, keepdims=True)
                p = jnp.exp(s - m)
                den = jnp.sum(p, axis=-1, keepdims=True)
                done.append((q_rows, jnp.dot(p.astype(BF16), v, preferred_element_type=F32), m, den))
            for q_rows, o, m, den in done:
                obuf[q_rows, :] = o
                mbuf[q_rows, :] = jnp.broadcast_to(m, (BAND, HEAD_DIM))
                dbuf[q_rows, :] = jnp.broadcast_to(den, (BAND, HEAD_DIM))
            return carry

        lax.fori_loop(0, ATT_R // BAND // ATT_UNROLL, sub_blocks, 0)

    chunk = 256
    for r0 in range(0, ATT_R, chunk):
        rs = slice(r0, r0 + chunk)
        ms = [mb[rs, :] for mb in mbufs]
        m_all = functools.reduce(jnp.maximum, ms)
        ws = [jnp.exp(m - m_all) for m in ms]
        num = sum(w * ob[rs, :] for w, ob in zip(ws, obufs))
        den = sum(w * db[rs, :] for w, db in zip(ws, dbufs))
        o_ref[rs, :] = (num / den).astype(o_ref.dtype)


def band_attention(za, batch, seq):
    assert seq % ATT_R == 0
    nch = seq // ATT_R
    in_specs, scratch = [], []
    for g, (win, dil) in enumerate(ATT_GROUPS):
        assert win // dil == BAND and ATT_R % win == 0
        cols = [(part * len(ATT_GROUPS) + g) * HEADS for part in range(3)]
        cur = lambda c: pl.BlockSpec((ATT_R, HEAD_DIM), lambda b, ch, h, c=c: (b * nch + ch, c + h))
        prev = lambda c, win=win: pl.BlockSpec(
            (win, HEAD_DIM), lambda b, ch, h, c=c, win=win: (jnp.maximum((b * seq + ch * ATT_R) // win - 1, 0), c + h))
        in_specs += [cur(cols[0]), cur(cols[1]), cur(cols[2]), prev(cols[1]), prev(cols[2])]
    for _ in range(2):
        scratch += [pltpu.VMEM((win + ATT_R, HEAD_DIM), F32) for win, _ in ATT_GROUPS]
    scratch += [pltpu.VMEM((ATT_R, HEAD_DIM), F32)] * (3 * len(ATT_GROUPS))
    return pl.pallas_call(
        _band_attn_kernel,
        grid=(batch, nch, HEADS),
        in_specs=in_specs,
        out_specs=pl.BlockSpec((ATT_R, HEAD_DIM), lambda b, ch, h: (b * nch + ch, h)),
        out_shape=jax.ShapeDtypeStruct((batch * seq, GROUP_W), BF16),
        scratch_shapes=scratch,
        compiler_params=_params("parallel", "parallel", "parallel"),
        name="band_attention",
    )(*([za] * (5 * len(ATT_GROUPS))))


def _window_tails_kernel(*refs):
    n_g = len(ATT_GROUPS)
    slabs = 2 * HEADS
    for g in range(n_g):
        k_ref, v_ref, o_ref = refs[2 * g], refs[2 * g + 1], refs[2 * n_g + g]
        rows = k_ref.shape[0]
        for part, src in enumerate((k_ref, v_ref)):
            for h in range(HEADS):
                o_ref[pl.ds(part * HEADS + h, rows, stride=slabs), :] = src[:, h * HEAD_DIM:(h + 1) * HEAD_DIM]


def window_tails(za, batch, seq):
    in_specs, out_specs, out_shape = [], [], []
    slabs = 2 * HEADS
    for g, (win, _) in enumerate(ATT_GROUPS):
        assert win <= seq and seq % win == 0
        last = seq // win - 1
        for part in (1, 2):
            in_specs.append(pl.BlockSpec(
                (win, GROUP_W), lambda b, g=g, part=part, win=win, last=last:
                (b * (seq // win) + last, part * len(ATT_GROUPS) + g)))
        out_specs.append(pl.BlockSpec((win * slabs, HEAD_DIM), lambda b: (b, 0)))
        out_shape.append(jax.ShapeDtypeStruct((batch * win * slabs, HEAD_DIM), F32))
    outs = pl.pallas_call(
        _window_tails_kernel,
        grid=(batch,),
        in_specs=in_specs,
        out_specs=out_specs,
        out_shape=out_shape,
        compiler_params=_params("parallel"),
        name="window_tails",
    )(*([za] * (2 * len(ATT_GROUPS))))
    return [o.reshape(1, batch, win, 2, HEADS, HEAD_DIM) for o, (win, _) in zip(outs, ATT_GROUPS)]


CONV_T = 256
CONV_HIST = 32
CONV_RC = 128
CONV_PHASES = 4
CONV_SLABS = D_CONV // LANES


def _layernorm_silu(y, g, b):
    mu = jnp.mean(y, axis=-1, keepdims=True)
    yc = y - mu
    var = jnp.mean(yc * yc, axis=-1, keepdims=True)
    yn = yc * lax.rsqrt(var + LN_EPS) * g + b
    return yn * jax.nn.sigmoid(yn)


def _conv_history(t, ubuf):
    @pl.when(t == 0)
    def _():
        ubuf[:, 0:CONV_HIST, :] = jnp.zeros((CONV_SLABS, CONV_HIST, LANES), F32)

    @pl.when(t > 0)
    def _():
        ubuf[:, 0:CONV_HIST, :] = ubuf[:, CONV_T:CONV_T + CONV_HIST, :]


def _conv_slab(j, a_ref, b_ref, w_ref, cb_ref, tail_ref, ubuf, ybuf):
    first = CONV_HIST - (CONV_K - 1)
    n = CONV_RC // CONV_PHASES
    js = slice(j * LANES, (j + 1) * LANES)
    u = a_ref[:, js] * jax.nn.sigmoid(b_ref[:, js])
    ubuf[j, CONV_HIST:CONV_HIST + CONV_T, :] = u
    tail_ref[0, :, js] = u[CONV_T - CONV_HIST:, :]
    for r0 in range(0, CONV_T, CONV_RC):
        accs = [jnp.broadcast_to(cb_ref[:, js], (n, LANES))] * CONV_PHASES
        for s in range(first, first + CONV_K + CONV_PHASES - 1):
            rows = ubuf[j, pl.ds(r0 + s, n, stride=CONV_PHASES), :]
            for p in range(CONV_PHASES):
                k = s - first - p
                if 0 <= k < CONV_K:
                    accs[p] = accs[p] + w_ref[k:k + 1, js] * rows
        for p in range(CONV_PHASES):
            ybuf[j, pl.ds(r0 + p, n, stride=CONV_PHASES), :] = accs[p]
    return jnp.sum(ybuf[j], axis=-1, keepdims=True)


def _conv_centered_sq(j, mu, ybuf):
    yc = ybuf[j] - mu
    return jnp.sum(yc * yc, axis=-1, keepdims=True)


def _conv_norm_slab(j, mu, inv, lg_ref, lb_ref, c_ref, ybuf):
    js = slice(j * LANES, (j + 1) * LANES)
    yn = (ybuf[j] - mu) * inv * lg_ref[:, js] + lb_ref[:, js]
    c_ref[:, js] = (yn * jax.nn.sigmoid(yn)).astype(c_ref.dtype)


def _merge_kernel(oa_ref, c_ref, ga_ref, gb_ref, x_ref, wpa_ref, wpb_ref, wo_ref, y_ref):
    ta = jnp.dot(oa_ref[...], wpa_ref[...], preferred_element_type=F32)
    tb = jnp.dot(c_ref[...], wpb_ref[...], preferred_element_type=F32)
    hmix = jax.nn.sigmoid(ga_ref[...]) * ta + jax.nn.sigmoid(gb_ref[...]) * tb
    y_ref[...] = x_ref[...] + jnp.dot(hmix.astype(BF16), wo_ref[...], preferred_element_type=F32)


def _conv_merge_kernel(oa_ref, a_ref, b_ref, ga_ref, gb_ref, x_ref, cw_ref, cb_ref, lg_ref, lb_ref,
                       wpa_ref, wpb_ref, wo_ref, y_ref, tail_ref, ubuf, ybuf, c_scr):
    _conv_history(pl.program_id(1), ubuf)
    total = jnp.zeros((CONV_T, 1), F32)
    for j in range(CONV_SLABS):
        total = total + _conv_slab(j, a_ref, b_ref, cw_ref, cb_ref, tail_ref, ubuf, ybuf)
    mu = total * (1.0 / D_CONV)
    sq = jnp.zeros((CONV_T, 1), F32)
    for j in range(CONV_SLABS):
        sq = sq + _conv_centered_sq(j, mu, ybuf)
    inv = lax.rsqrt(sq * (1.0 / D_CONV) + LN_EPS)
    for j in range(CONV_SLABS):
        _conv_norm_slab(j, mu, inv, lg_ref, lb_ref, c_scr, ybuf)
    _merge_kernel(oa_ref, c_scr, ga_ref, gb_ref, x_ref, wpa_ref, wpb_ref, wo_ref, y_ref)


def conv_merge_branches(o_a, za, gates, x, batch, seq, conv_w, conv_b, ln_g, ln_b, w_pa, w_pb, w_out):
    nt = seq // CONV_T
    row = lambda b, t: (b * nt + t, 0)
    const = lambda b, t: (0, 0)
    once = dict(pipeline_mode=pl.Buffered(1))
    glu = 3 * ATT_WIDTH // D_CONV
    y, tail = pl.pallas_call(
        _conv_merge_kernel,
        grid=(batch, nt),
        in_specs=[pl.BlockSpec((CONV_T, GROUP_W), row),
                  pl.BlockSpec((CONV_T, D_CONV), lambda b, t: (b * nt + t, glu)),
                  pl.BlockSpec((CONV_T, D_CONV), lambda b, t: (b * nt + t, glu + 1)),
                  pl.BlockSpec((CONV_T, D_MODEL), lambda b, t: (b * nt + t, 0)),
                  pl.BlockSpec((CONV_T, D_MODEL), lambda b, t: (b * nt + t, 1)),
                  pl.BlockSpec((CONV_T, D_MODEL), row),
                  pl.BlockSpec((CONV_K, D_CONV), const),
                  pl.BlockSpec((1, D_CONV), const),
                  pl.BlockSpec((1, D_CONV), const),
                  pl.BlockSpec((1, D_CONV), const),
                  pl.BlockSpec((GROUP_W, D_MODEL), const, **once),
                  pl.BlockSpec((D_CONV, D_MODEL), const, **once),
                  pl.BlockSpec((D_MODEL, D_MODEL), const, **once)],
        out_specs=[pl.BlockSpec((CONV_T, D_MODEL), row),
                   pl.BlockSpec((1, CONV_HIST, D_CONV), lambda b, t: (b, 0, 0))],
        out_shape=[jax.ShapeDtypeStruct((batch * seq, D_MODEL), F32),
                   jax.ShapeDtypeStruct((batch, CONV_HIST, D_CONV), F32)],
        scratch_shapes=[pltpu.VMEM((CONV_SLABS, CONV_HIST + CONV_T, LANES), F32),
                        pltpu.VMEM((CONV_SLABS, CONV_T, LANES), F32),
                        pltpu.VMEM((CONV_T, D_CONV), BF16)],
        compiler_params=_params("parallel", "arbitrary"),
        name="conv_merge_branches",
    )(o_a, za, za, gates, gates, x, conv_w, conv_b.reshape(1, D_CONV), ln_g.reshape(1, D_CONV),
      ln_b.reshape(1, D_CONV), w_pa, w_pb, w_out)
    return y, tail[:, CONV_HIST - (CONV_K - 1):]


def merge_branches(o_a, c, gates, x, w_pa, w_pb, w_out, tm):
    m = x.shape[0]
    tm = min(tm, m)
    row = lambda i: (i, 0)
    const = lambda i: (0, 0)
    once = dict(pipeline_mode=pl.Buffered(1))
    return pl.pallas_call(
        _merge_kernel,
        grid=(m // tm,),
        in_specs=[
            pl.BlockSpec((tm, GROUP_W), row),
            pl.BlockSpec((tm, D_CONV), row),
            pl.BlockSpec((tm, D_MODEL), lambda i: (i, 0)),
            pl.BlockSpec((tm, D_MODEL), lambda i: (i, 1)),
            pl.BlockSpec((tm, D_MODEL), row),
            pl.BlockSpec((GROUP_W, D_MODEL), const, **once),
            pl.BlockSpec((D_CONV, D_MODEL), const, **once),
            pl.BlockSpec((D_MODEL, D_MODEL), const, **once)],
        out_specs=pl.BlockSpec((tm, D_MODEL), row),
        out_shape=jax.ShapeDtypeStruct((m, D_MODEL), F32),
        compiler_params=_params("parallel"),
        name="merge_branches",
    )(o_a, c, gates, gates, x, w_pa, w_pb, w_out)


def _xattn_kernel(q_ref, kv_ref, o_ref):
    scale = X_HEAD_DIM ** -0.5
    nt = (((1,), (1,)), ((), ()))
    for h in range(X_HEADS):
        hs = slice(h * X_HEAD_DIM, (h + 1) * X_HEAD_DIM)
        vs = slice(D_MODEL + h * X_HEAD_DIM, D_MODEL + (h + 1) * X_HEAD_DIM)
        s = lax.dot_general(q_ref[:, hs], kv_ref[:, hs].astype(BF16), nt, preferred_element_type=F32) * scale
        m = jnp.max(s, axis=-1, keepdims=True)
        p = jnp.exp(s - m)
        p = p / jnp.sum(p, axis=-1, keepdims=True)
        o = jnp.dot(p.astype(BF16), kv_ref[:, vs].astype(BF16), preferred_element_type=F32)
        o_ref[:, hs] = o.astype(o_ref.dtype)


def cross_attention_prompt(q, mkv, batch, seq, tm=512):
    nt = seq // tm
    return pl.pallas_call(
        _xattn_kernel,
        grid=(batch, nt),
        in_specs=[pl.BlockSpec((tm, D_MODEL), lambda b, t: (b * nt + t, 0)),
                  pl.BlockSpec((N_MEM, 2 * D_MODEL), lambda b, t: (b, 0))],
        out_specs=pl.BlockSpec((tm, D_MODEL), lambda b, t: (b * nt + t, 0)),
        out_shape=jax.ShapeDtypeStruct((batch * seq, D_MODEL), BF16),
        compiler_params=_params("parallel", "arbitrary"),
        name="cross_attention_prompt",
    )(q, mkv)


def _xattn_decode_kernel(q_ref, kv_ref, o_ref):
    scale = X_HEAD_DIM ** -0.5
    q = q_ref[0].astype(F32)
    k = kv_ref[0, 0, :, 0]
    v = kv_ref[0, 0, :, 1]
    s = jnp.sum(k * q, axis=-1, keepdims=True) * scale
    m = jnp.max(s, axis=0, keepdims=True)
    p = jnp.exp(s - m)
    p = p / jnp.sum(p, axis=0, keepdims=True)
    o_ref[0] = jnp.sum(p * v, axis=0, keepdims=True).astype(o_ref.dtype)


def cross_attention_decode(q, cache_mem_kv):
    b = q.shape[0]
    q4 = q.reshape(b, 1, X_HEADS, X_HEAD_DIM)
    o = pl.pallas_call(
        _xattn_decode_kernel,
        grid=(b,),
        in_specs=[pl.BlockSpec((1, 1, X_HEADS, X_HEAD_DIM), lambda i: (i, 0, 0, 0)),
                  pl.BlockSpec((1, 1, N_MEM, 2, X_HEADS, X_HEAD_DIM), lambda i: (0, i, 0, 0, 0, 0))],
        out_specs=pl.BlockSpec((1, 1, X_HEADS, X_HEAD_DIM), lambda i: (i, 0, 0, 0)),
        out_shape=jax.ShapeDtypeStruct((b, 1, X_HEADS, X_HEAD_DIM), BF16),
        compiler_params=_params("parallel"),
        name="cross_attention_decode",
    )(q4, cache_mem_kv)
    return o.reshape(b, D_MODEL)


N_BUCKETS = N_EXPERT_GROUPS * 6
PAIR_LO = (0, 0, 0, 1, 1, 2)
PAIR_HI = (1, 2, 3, 2, 3, 3)
ROW_TILES = D_MODEL // LANES
PAYLOAD_ROWS = ROW_TILES + 8
MOE_TM = 256


def _route(logits):
    lane = lax.broadcasted_iota(jnp.int32, logits.shape, 1)
    lanef = lane.astype(F32)
    big = float(LANES)
    is_group = (lane >= ROUTER_GROUP_LANE) & (lane < ROUTER_GROUP_LANE + N_EXPERT_GROUPS)
    lg = jnp.where(is_group, logits, -jnp.inf)
    mg = jnp.max(lg, axis=-1, keepdims=True)
    p_sel = 1.0 / jnp.sum(jnp.exp(lg - mg), axis=-1, keepdims=True)
    gsel = jnp.min(jnp.where(lg == mg, lanef, big), axis=-1, keepdims=True) - ROUTER_GROUP_LANE
    group_of_lane = lax.shift_right_logical(lane, 2)
    in_group = (lane < N_EXPERTS) & (group_of_lane == gsel.astype(jnp.int32))
    le = jnp.where(in_group, logits, -jnp.inf)
    v1 = jnp.max(le, axis=-1, keepdims=True)
    i1 = jnp.min(jnp.where(le == v1, lanef, big), axis=-1, keepdims=True)
    le2 = jnp.where(lanef == i1, -jnp.inf, le)
    v2 = jnp.max(le2, axis=-1, keepdims=True)
    i2 = jnp.min(jnp.where(le2 == v2, lanef, big), axis=-1, keepdims=True)
    t = jnp.exp(v2 - v1)
    tot = 1.0 + t
    return gsel, i1, i2, (1.0 / tot) * p_sel, (t / tot) * p_sel


def _router_routed_kernel(x_ref, g_ref, w_ref, b_ref, xd_ref, meta_ref, cnt_ref, carry_ref):
    tm = x_ref.shape[0]

    @pl.when(pl.program_id(0) == 0)
    def _():
        carry_ref[...] = jnp.zeros_like(carry_ref)

    xn = _rms_rows(x_ref[...], g_ref[...])
    logits = _router_logits(xn, w_ref, b_ref)
    gsel, i1, i2, p1, p2 = _route(logits)
    lo = jnp.minimum(i1, i2) - EXPERTS_PER_GROUP * gsel
    hi = jnp.maximum(i1, i2) - EXPERTS_PER_GROUP * gsel
    bucket = gsel * 6.0 + lo * (7.0 - lo) * 0.5 + (hi - lo - 1.0)
    gate_lo = jnp.where(i1 < i2, p1, p2)
    gate_hi = jnp.where(i1 < i2, p2, p1)

    lane = lax.broadcasted_iota(jnp.int32, (tm, LANES), 1)
    onehot = (lane.astype(F32) == bucket).astype(F32)
    r_i = lax.broadcasted_iota(jnp.int32, (tm, tm), 0)
    c_i = lax.broadcasted_iota(jnp.int32, (tm, tm), 1)
    before = (c_i < r_i).astype(BF16)
    rank_local = jnp.dot(before, onehot.astype(BF16), preferred_element_type=F32)
    rank = jnp.sum(onehot * (rank_local + carry_ref[0:1, :]), axis=-1, keepdims=True)
    carry_ref[0:1, :] = carry_ref[0:1, :] + jnp.sum(onehot, axis=0, keepdims=True)
    cnt_ref[...] = jnp.broadcast_to(carry_ref[0:1, :], cnt_ref.shape)
    meta_ref[...] = jnp.where(lane == 0, bucket, jnp.where(lane == 1, rank, 0.0))

    for j in range(ROW_TILES):
        xd_ref[pl.ds(j, tm, stride=PAYLOAD_ROWS), :] = xn[:, j * LANES:(j + 1) * LANES]
    xd_ref[pl.ds(ROW_TILES, tm, stride=PAYLOAD_ROWS), :] = jnp.where(
        lane == 0, gate_lo, jnp.where(lane == 1, gate_hi, 0.0))
    for j in range(ROW_TILES + 1, PAYLOAD_ROWS):
        xd_ref[pl.ds(j, tm, stride=PAYLOAD_ROWS), :] = jnp.zeros((tm, LANES), F32)


def _router_weights(w_rg, b_rg, w_re, b_re):
    pad = LANES - N_EXPERTS - N_EXPERT_GROUPS
    w = jnp.concatenate([w_re, w_rg, jnp.zeros((D_MODEL, pad), F32)], axis=1)
    b = jnp.concatenate([b_re, b_rg, jnp.zeros((pad,), F32)]).reshape(1, LANES)
    hi = w.astype(BF16)
    lo = (w - hi.astype(F32)).astype(BF16)
    return jnp.stack([hi, lo]), b


def _router_logits(xn, w_ref, b_ref):
    hi = xn.astype(BF16)
    lo = (xn - hi.astype(F32)).astype(BF16)
    acc = jnp.dot(hi, w_ref[0], preferred_element_type=F32)
    acc = acc + jnp.dot(lo, w_ref[0], preferred_element_type=F32)
    acc = acc + jnp.dot(hi, w_ref[1], preferred_element_type=F32)
    return acc + b_ref[...]


def moe_router_routed(x, g, w_rg, b_rg, w_re, b_re, tm=512):
    m = x.shape[0]
    w, b = _router_weights(w_rg, b_rg, w_re, b_re)
    row = lambda i: (i, 0)
    const = lambda i: (0, 0)
    return pl.pallas_call(
        _router_routed_kernel,
        grid=(m // tm,),
        in_specs=[pl.BlockSpec((tm, D_MODEL), row), pl.BlockSpec((1, D_MODEL), const),
                  pl.BlockSpec((2, D_MODEL, LANES), lambda i: (0, 0, 0)), pl.BlockSpec((1, LANES), const)],
        out_specs=[pl.BlockSpec((tm * PAYLOAD_ROWS, LANES), row), pl.BlockSpec((tm, LANES), row),
                   pl.BlockSpec((8, LANES), const)],
        out_shape=[jax.ShapeDtypeStruct((m * PAYLOAD_ROWS, LANES), F32), jax.ShapeDtypeStruct((m, LANES), F32),
                   jax.ShapeDtypeStruct((8, LANES), F32)],
        scratch_shapes=[pltpu.VMEM((8, LANES), F32)],
        compiler_params=_params("arbitrary"),
        name="moe_router_routed",
    )(x, g.reshape(1, D_MODEL), w, b)


INV_CHUNK = 2048


def _slot_maps_kernel(bucket_ref, rank_ref, first_ref, dest_ref, inv_ref):
    step = pl.program_id(0)

    @pl.when(step == 0)
    def _():
        def clear(k, c):
            inv_ref[k] = 0
            return c
        lax.fori_loop(0, inv_ref.shape[0], clear, 0, unroll=8)

    def put(t, c):
        slot = first_ref[bucket_ref[0, 0, t]] + rank_ref[0, 0, t]
        dest_ref[0, 0, t] = slot
        inv_ref[slot] = step * INV_CHUNK + t
        return c
    lax.fori_loop(0, INV_CHUNK, put, 0, unroll=8)


def slot_maps(bucket, rank, first_slot, n_slots):
    m = bucket.shape[0]
    nc = m // INV_CHUNK
    smem = dict(memory_space=pltpu.SMEM)
    chunk = pl.BlockSpec((1, 1, INV_CHUNK), lambda i: (i, 0, 0), **smem)
    dest, inv = pl.pallas_call(
        _slot_maps_kernel,
        grid=(nc,),
        in_specs=[chunk, chunk, pl.BlockSpec(**smem)],
        out_specs=[chunk, pl.BlockSpec(**smem)],
        out_shape=[jax.ShapeDtypeStruct((nc, 1, INV_CHUNK), jnp.int32),
                   jax.ShapeDtypeStruct((n_slots,), jnp.int32)],
        compiler_params=_params("arbitrary"),
        name="slot_maps",
    )(bucket.reshape(nc, 1, INV_CHUNK), rank.reshape(nc, 1, INV_CHUNK), first_slot)
    return dest.reshape(m), inv


def _start_row_gather(idx_ref, src_hbm, dst, sem, n_items, rows):
    for r in range(n_items):
        first = pl.multiple_of(idx_ref[0, 0, r] * rows, 8)
        pltpu.make_async_copy(src_hbm.at[pl.ds(first, rows)], dst.at[pl.ds(r * rows, rows)],
                              sem).start(priority=r % 2)


def _wait_row_gather(src_hbm, dst, sem, n_items, rows):
    pltpu.make_async_copy(src_hbm.at[pl.ds(0, n_items * rows)], dst, sem).wait()


def _moe_routed_kernel(ea_ref, eb_ref, nv_ref, inv_ref, inv_next_ref, xd_hbm, w1a_ref, w1b_ref, w2a_ref, w2b_ref,
                       ys_ref, xbuf, sems, x_scr):
    del ea_ref, eb_ref
    i = pl.program_id(0)
    n_valid = nv_ref[0]
    slot = i % 2
    tm = MOE_TM

    @pl.when(i == 0)
    def _():
        _start_row_gather(inv_ref, xd_hbm, xbuf.at[0], sems.at[0], tm, PAYLOAD_ROWS)

    @pl.when(i < n_valid)
    def _():
        buf = xbuf.at[slot]
        _wait_row_gather(xd_hbm, buf, sems.at[slot], tm, PAYLOAD_ROWS)
        _start_row_gather(inv_next_ref, xd_hbm, xbuf.at[1 - slot], sems.at[1 - slot], tm, PAYLOAD_ROWS)
        for j in range(ROW_TILES):
            x_scr[:, j * LANES:(j + 1) * LANES] = buf[pl.ds(j, tm, stride=PAYLOAD_ROWS), :].astype(BF16)
        gates = buf[pl.ds(ROW_TILES, tm, stride=PAYLOAD_ROWS), :]
        x = x_scr[...]
        hu = jnp.dot(x, w1a_ref[0], preferred_element_type=F32)
        ha = jax.nn.silu(hu[:, :D_EXPERT]) * hu[:, D_EXPERT:] * gates[:, 0:1]
        hu = jnp.dot(x, w1b_ref[0], preferred_element_type=F32)
        hb = jax.nn.silu(hu[:, :D_EXPERT]) * hu[:, D_EXPERT:] * gates[:, 1:2]
        y = (jnp.dot(ha.astype(BF16), w2a_ref[0], preferred_element_type=F32)
             + jnp.dot(hb.astype(BF16), w2b_ref[0], preferred_element_type=F32))
        for j in range(ROW_TILES):
            ys_ref[pl.ds(j, tm, stride=ROW_TILES), :] = y[:, j * LANES:(j + 1) * LANES]

    @pl.when(i == n_valid - 1)
    def _():
        _wait_row_gather(xd_hbm, xbuf.at[1 - slot], sems.at[1 - slot], tm, PAYLOAD_ROWS)

    @pl.when(i >= n_valid)
    def _():
        ys_ref[...] = jnp.zeros_like(ys_ref)


def moe_experts_routed(xd, inv, tile_ea, tile_eb, n_valid, w_ein, w_eout):
    n_tiles = tile_ea.shape[0]
    tm = MOE_TM
    inv3 = inv.reshape(n_tiles, 1, tm)
    smem = dict(memory_space=pltpu.SMEM)
    grid_spec = pltpu.PrefetchScalarGridSpec(
        num_scalar_prefetch=3,
        grid=(n_tiles,),
        in_specs=[
            pl.BlockSpec((1, 1, tm), lambda i, ea, eb, nv: (i, 0, 0), **smem),
            pl.BlockSpec((1, 1, tm), lambda i, ea, eb, nv: (jnp.minimum(i + 1, n_tiles - 1), 0, 0), **smem),
            pl.BlockSpec(memory_space=pl.ANY),
            pl.BlockSpec((1, D_MODEL, 2 * D_EXPERT), lambda i, ea, eb, nv: (ea[i], 0, 0)),
            pl.BlockSpec((1, D_MODEL, 2 * D_EXPERT), lambda i, ea, eb, nv: (eb[i], 0, 0)),
            pl.BlockSpec((1, D_EXPERT, D_MODEL), lambda i, ea, eb, nv: (ea[i], 0, 0)),
            pl.BlockSpec((1, D_EXPERT, D_MODEL), lambda i, ea, eb, nv: (eb[i], 0, 0))],
        out_specs=pl.BlockSpec((tm * ROW_TILES, LANES), lambda i, ea, eb, nv: (i, 0)),
        scratch_shapes=[pltpu.VMEM((2, tm * PAYLOAD_ROWS, LANES), F32), pltpu.SemaphoreType.DMA((2,)),
                        pltpu.VMEM((tm, D_MODEL), BF16)])
    return pl.pallas_call(
        _moe_routed_kernel,
        grid_spec=grid_spec,
        out_shape=jax.ShapeDtypeStruct((n_tiles * tm * ROW_TILES, LANES), F32),
        compiler_params=_params("arbitrary"),
        name="moe_experts_routed",
    )(tile_ea, tile_eb, n_valid, inv3, inv3, xd, w_ein, w_ein, w_eout, w_eout)


def _combine_kernel(dest_ref, dest_next_ref, ys_hbm, res_ref, gf_ref, y_ref, buf, sems):
    i = pl.program_id(0)
    slot = i % 2
    tm = res_ref.shape[0]

    @pl.when(i == 0)
    def _():
        _start_row_gather(dest_ref, ys_hbm, buf.at[0], sems.at[0], tm, ROW_TILES)

    cur = buf.at[slot]
    _wait_row_gather(ys_hbm, cur, sems.at[slot], tm, ROW_TILES)
    _start_row_gather(dest_next_ref, ys_hbm, buf.at[1 - slot], sems.at[1 - slot], tm, ROW_TILES)
    for j in range(ROW_TILES):
        js = slice(j * LANES, (j + 1) * LANES)
        y_ref[:, js] = res_ref[:, js] + cur[pl.ds(j, tm, stride=ROW_TILES), :]
    y_ref[...] = _rms_rows(y_ref[...], gf_ref[...])

    @pl.when(i == pl.num_programs(0) - 1)
    def _():
        _wait_row_gather(ys_hbm, buf.at[1 - slot], sems.at[1 - slot], tm, ROW_TILES)


def moe_combine_final(ys, dest, res, g_final, tm=256):
    m = res.shape[0]
    nt = m // tm
    dest3 = dest.reshape(nt, 1, tm)
    smem = dict(memory_space=pltpu.SMEM)
    return pl.pallas_call(
        _combine_kernel,
        grid=(nt,),
        in_specs=[pl.BlockSpec((1, 1, tm), lambda i: (i, 0, 0), **smem),
                  pl.BlockSpec((1, 1, tm), lambda i: (jnp.minimum(i + 1, nt - 1), 0, 0), **smem),
                  pl.BlockSpec(memory_space=pl.ANY),
                  pl.BlockSpec((tm, D_MODEL), lambda i: (i, 0)),
                  pl.BlockSpec((1, D_MODEL), lambda i: (0, 0))],
        out_specs=pl.BlockSpec((tm, D_MODEL), lambda i: (i, 0)),
        out_shape=jax.ShapeDtypeStruct((m, D_MODEL), F32),
        scratch_shapes=[pltpu.VMEM((2, tm * ROW_TILES, LANES), F32), pltpu.SemaphoreType.DMA((2,))],
        compiler_params=_params("arbitrary"),
        name="moe_combine_final",
    )(dest3, dest3, ys, res, g_final.reshape(1, D_MODEL))


def moe_routed(x, g_ffn, w_rg, b_rg, w_re, b_re, w_ein, w_eout, g_final):
    m = x.shape[0]
    tm = MOE_TM
    n_tiles = m // tm + N_BUCKETS
    xd, meta, cnt = moe_router_routed(x, g_ffn, w_rg, b_rg, w_re, b_re)
    bucket = meta[:, 0].astype(jnp.int32)
    rank = meta[:, 1].astype(jnp.int32)
    counts = cnt[0, :N_BUCKETS].astype(jnp.int32)
    tiles_per_bucket = (counts + tm - 1) // tm
    tile_end = jnp.cumsum(tiles_per_bucket)
    tile_start = tile_end - tiles_per_bucket
    n_valid = tile_end[-1]
    tile_id = jnp.arange(n_tiles, dtype=jnp.int32)
    tile_bucket = jnp.searchsorted(tile_end, jnp.minimum(tile_id, n_valid - 1), side="right").astype(jnp.int32)
    group = tile_bucket // 6
    tile_ea = group * EXPERTS_PER_GROUP + jnp.take(jnp.array(PAIR_LO, jnp.int32), tile_bucket % 6)
    tile_eb = group * EXPERTS_PER_GROUP + jnp.take(jnp.array(PAIR_HI, jnp.int32), tile_bucket % 6)
    dest, inv = slot_maps(bucket, rank, tile_start * tm, n_tiles * tm)
    ys = moe_experts_routed(xd, inv, tile_ea, tile_eb, n_valid.reshape(1), w_ein, w_eout)
    return moe_combine_final(ys, dest, x, g_final)


def _router_kernel(x_ref, g_ref, w_ref, b_ref, xn_ref, gate_ref):
    xn = _rms_rows(x_ref[...], g_ref[...])
    xn_ref[...] = xn.astype(BF16)
    logits = _router_logits(xn, w_ref, b_ref)
    _, i1, i2, p1, p2 = _route(logits)
    lanef = lax.broadcasted_iota(jnp.int32, logits.shape, 1).astype(F32)
    gate_ref[...] = jnp.where(lanef == i1, p1, 0.0) + jnp.where(lanef == i2, p2, 0.0)


def moe_router(x, g, w_rg, b_rg, w_re, b_re, tm):
    m = x.shape[0]
    tm = min(tm, m)
    w, b = _router_weights(w_rg, b_rg, w_re, b_re)
    row = lambda i: (i, 0)
    const = lambda i: (0, 0)
    return pl.pallas_call(
        _router_kernel,
        grid=(m // tm,),
        in_specs=[pl.BlockSpec((tm, D_MODEL), row), pl.BlockSpec((1, D_MODEL), const),
                  pl.BlockSpec((2, D_MODEL, LANES), lambda i: (0, 0, 0)), pl.BlockSpec((1, LANES), const)],
        out_specs=[pl.BlockSpec((tm, D_MODEL), row), pl.BlockSpec((tm, LANES), row)],
        out_shape=[jax.ShapeDtypeStruct((m, D_MODEL), BF16), jax.ShapeDtypeStruct((m, LANES), F32)],
        compiler_params=_params("parallel"),
        name="moe_router",
    )(x, g.reshape(1, D_MODEL), w, b)


def _moe_dense_kernel(xn_ref, gate_ref, w1_ref, w2_ref, res_ref, gf_ref, y_ref):
    e = pl.program_id(1)

    @pl.when(e == 0)
    def _():
        y_ref[...] = res_ref[...]

    hu = jnp.dot(xn_ref[...], w1_ref[0], preferred_element_type=F32)
    h = jax.nn.silu(hu[:, :D_EXPERT]) * hu[:, D_EXPERT:]
    lane = lax.broadcasted_iota(jnp.int32, gate_ref.shape, 1)
    gate = jnp.sum(jnp.where(lane == e, gate_ref[...], 0.0), axis=-1, keepdims=True)
    h = h * gate
    y_ref[...] += jnp.dot(h.astype(BF16), w2_ref[0], preferred_element_type=F32)

    @pl.when(e == N_EXPERTS - 1)
    def _():
        y_ref[...] = _rms_rows(y_ref[...], gf_ref[...])


def moe_experts_final(xn, gate, w_ein, w_eout, res, g_final, tm):
    m = xn.shape[0]
    tm = min(tm, m)
    row = lambda i, e: (i, 0)
    return pl.pallas_call(
        _moe_dense_kernel,
        grid=(m // tm, N_EXPERTS),
        in_specs=[pl.BlockSpec((tm, D_MODEL), row), pl.BlockSpec((tm, LANES), row),
                  pl.BlockSpec((1, D_MODEL, 2 * D_EXPERT), lambda i, e: (e, 0, 0)),
                  pl.BlockSpec((1, D_EXPERT, D_MODEL), lambda i, e: (e, 0, 0)),
                  pl.BlockSpec((tm, D_MODEL), row),
                  pl.BlockSpec((1, D_MODEL), lambda i, e: (0, 0))],
        out_specs=pl.BlockSpec((tm, D_MODEL), row),
        out_shape=jax.ShapeDtypeStruct((m, D_MODEL), F32),
        compiler_params=_params("parallel", "arbitrary"),
        name="moe_experts_final",
    )(xn, gate, w_ein, w_eout, res, g_final.reshape(1, D_MODEL))


def _window_decode_kernel(z_ref, c1_ref, c2_ref, c3_ref, o_ref):
    scale = HEAD_DIM ** -0.5
    os, ms, ds = [], [], []
    for g, c_ref in enumerate((c1_ref, c2_ref, c3_ref)):
        q = z_ref[0, g:g + 1]
        k_new = z_ref[0, 3 + g:4 + g]
        v_new = z_ref[0, 6 + g:7 + g]
        k = c_ref[0, 0, :, 0, 0]
        v = c_ref[0, 0, :, 0, 1]
        s = jnp.sum(k * q, axis=-1, keepdims=True) * scale
        s_new = jnp.sum(k_new * q, axis=-1, keepdims=True) * scale
        m = jnp.maximum(jnp.max(s, axis=0, keepdims=True), s_new)
        p = jnp.exp(s - m)
        p_new = jnp.exp(s_new - m)
        den = jnp.sum(p, axis=0, keepdims=True) + p_new
        os.append(jnp.sum(p * v, axis=0, keepdims=True) + p_new * v_new)
        ms.append(m)
        ds.append(den)
    m_all = functools.reduce(jnp.maximum, ms)
    ws = [jnp.exp(m - m_all) for m in ms]
    num = sum(w * o for w, o in zip(ws, os))
    den = sum(w * d for w, d in zip(ws, ds))
    o_ref[0] = (num / den).astype(o_ref.dtype)


def window_decode(z4, caches):
    b = z4.shape[0]
    views, specs = [], []
    for cache, (win, dil) in zip(caches, ATT_GROUPS):
        n = cache.shape[2]
        assert n == win and n // dil == BAND
        views.append(cache.reshape(1, b, BAND, dil, 2, HEADS, HEAD_DIM))
        specs.append(pl.BlockSpec((1, 1, BAND, 1, 2, HEADS, HEAD_DIM), lambda i: (0, i, 0, 0, 0, 0, 0)))
    return pl.pallas_call(
        _window_decode_kernel,
        grid=(b,),
        in_specs=[pl.BlockSpec((1,) + z4.shape[1:], lambda i: (i, 0, 0, 0))] + specs,
        out_specs=pl.BlockSpec((1, 1, HEADS, HEAD_DIM), lambda i: (i, 0, 0, 0)),
        out_shape=jax.ShapeDtypeStruct((b, 1, HEADS, HEAD_DIM), BF16),
        compiler_params=_params("parallel"),
        name="window_decode",
    )(z4, *views)


SHIFT_ROWS = 64


def _shift_kernel(z_ref, c1_ref, c2_ref, c3_ref, o1_ref, o2_ref, o3_ref):
    for g, (c_ref, o_ref) in enumerate(((c1_ref, o1_ref), (c2_ref, o2_ref), (c3_ref, o3_ref))):
        n = c_ref.shape[2]
        full, rem = divmod(n - 1, SHIFT_ROWS)

        def move(j, carry, c_ref=c_ref, o_ref=o_ref):
            o_ref[0, 0, pl.ds(j * SHIFT_ROWS, SHIFT_ROWS)] = c_ref[0, 0, pl.ds(j * SHIFT_ROWS + 1, SHIFT_ROWS)]
            return carry

        lax.fori_loop(0, full, move, 0)
        if rem:
            o_ref[0, 0, pl.ds(full * SHIFT_ROWS, rem)] = c_ref[0, 0, pl.ds(full * SHIFT_ROWS + 1, rem)]
        o_ref[0, 0, n - 1, 0] = z_ref[0, 3 + g]
        o_ref[0, 0, n - 1, 1] = z_ref[0, 6 + g]


def shift_caches(z4, caches):
    b = z4.shape[0]
    specs = [pl.BlockSpec((1, 1) + c.shape[2:], lambda i: (0, i, 0, 0, 0, 0)) for c in caches]
    return pl.pallas_call(
        _shift_kernel,
        grid=(b,),
        in_specs=[pl.BlockSpec((1,) + z4.shape[1:], lambda i: (i, 0, 0, 0))] + specs,
        out_specs=specs,
        out_shape=[jax.ShapeDtypeStruct(c.shape, c.dtype) for c in caches],
        compiler_params=_params("parallel"),
        name="shift_caches",
    )(z4, *caches)


def _conv_step_kernel(a_ref, b_ref, s_ref, w_ref, cb_ref, lg_ref, lb_ref, c_ref, so_ref):
    hist = CONV_K - 1
    u = a_ref[...] * jax.nn.sigmoid(b_ref[...])
    y = (jnp.sum(s_ref[0] * w_ref[0:hist, :], axis=1, keepdims=True)
         + u * w_ref[hist:hist + 1, :] + cb_ref[...])
    c_ref[...] = _layernorm_silu(y, lg_ref[...], lb_ref[...]).astype(c_ref.dtype)
    so_ref[0, :, pl.ds(0, hist - 1), :] = s_ref[0, :, pl.ds(1, hist - 1), :]
    so_ref[0, :, pl.ds(hist - 1, 1), :] = u


def conv_step(a, b, state, conv_w, conv_b, ln_g, ln_b):
    bsz = a.shape[0]
    return pl.pallas_call(
        _conv_step_kernel,
        out_shape=[jax.ShapeDtypeStruct((bsz, 1, D_CONV), BF16), jax.ShapeDtypeStruct(state.shape, F32)],
        compiler_params=pltpu.CompilerParams(vmem_limit_bytes=VMEM_LIMIT),
        name="conv_step",
    )(a, b, state, conv_w, conv_b.reshape(1, D_CONV), ln_g.reshape(1, D_CONV), ln_b.reshape(1, D_CONV))


def _trunk_tail(x1, xo_in, w_xo, norm_ffn_g, w_rg, b_rg, w_re, b_re, w_ein, w_eout, norm_final_g, bm, routed):
    x2 = matmul_residual(xo_in, w_xo, x1, min(bm, 512), bn=D_MODEL)
    if routed:
        return moe_routed(x2, norm_ffn_g, w_rg, b_rg, w_re, b_re, w_ein, w_eout, norm_final_g)
    xn3, gate = moe_router(x2, norm_ffn_g, w_rg, b_rg, w_re, b_re, bm)
    return moe_experts_final(xn3, gate, w_ein, w_eout, x2, norm_final_g, bm)


def kernel(x_prompt, x_sample, mem_prompt, cache_kv_g1, cache_kv_g2, cache_kv_g3, state_conv, cache_mem_kv,
           norm_mix_g, w_in, conv_w, conv_b, conv_ln_g, conv_ln_b, w_proj_a, w_proj_b, w_out,
           norm_xattn_g, norm_mem_g, w_xq, w_xkv, w_xo, norm_ffn_g,
           w_router_group, b_router_group, w_router_expert, b_router_expert, w_expert_in, w_expert_out,
           norm_final_g):
    depth = norm_mix_g.shape[0]
    assert depth == 1, "single-layer trunk"
    batch, seq, _ = x_prompt.shape
    dec_b, dec_t, _ = x_sample.shape
    assert dec_t == 1
    (g_mix, w_in, conv_w, conv_b, ln_g, ln_b, w_pa, w_pb, w_o, g_x, g_mem, w_xq, w_xkv, w_xo, g_ffn,
     w_rg, b_rg, w_re, b_re, w_ein, w_eout) = [t[0] for t in (
         norm_mix_g, w_in, conv_w, conv_b, conv_ln_g, conv_ln_b, w_proj_a, w_proj_b, w_out, norm_xattn_g,
         norm_mem_g, w_xq, w_xkv, w_xo, norm_ffn_g, w_router_group, b_router_group, w_router_expert,
         b_router_expert, w_expert_in, w_expert_out)]
    w_pa16, w_pb16, w_o16 = w_pa.astype(BF16), w_pb.astype(BF16), w_o.astype(BF16)
    w_ein, w_eout = w_ein.astype(BF16), w_eout.astype(BF16)
    w_za, w_gt = w_in[:, :ZA_W].astype(BF16), w_in[:, ZA_W:].astype(BF16)
    w_xq, w_xkv, w_xo = w_xq.astype(BF16), w_xkv.astype(BF16), w_xo.astype(BF16)
    caches = (cache_kv_g1, cache_kv_g2, cache_kv_g3)

    m_p = batch * seq
    xp = x_prompt.reshape(m_p, D_MODEL)
    za, gates = in_proj(xp, g_mix, w_za, w_gt, bm=1024)
    o_a = band_attention(za, batch, seq)
    x1, conv_prompt = conv_merge_branches(o_a, za, gates, xp, batch, seq, conv_w, conv_b, ln_g, ln_b,
                                          w_pa16, w_pb16, w_o16)
    mkv = norm_matmul(mem_prompt.reshape(batch * N_MEM, D_MODEL), g_mem, w_xkv, F32, bm=1024)
    q = norm_matmul(x1, g_x, w_xq, BF16, bm=512, bn=D_MODEL)
    xo_in = cross_attention_prompt(q, mkv, batch, seq)
    y_prompt = _trunk_tail(x1, xo_in, w_xo, g_ffn, w_rg, b_rg, w_re, b_re, w_ein, w_eout, norm_final_g,
                           bm=1024, routed=True)

    kv_prompt = window_tails(za, batch, seq)
    conv_prompt = conv_prompt[None]
    mem_kv_prompt = mkv.reshape(1, batch, N_MEM, 2, X_HEADS, X_HEAD_DIM)

    xs = x_sample.reshape(dec_b, D_MODEL)
    zs, gates_s = in_proj(xs, g_mix, w_za, w_gt, bm=dec_b)
    z4 = zs.reshape(dec_b, ZA_W // HEAD_DIM // HEADS, HEADS, HEAD_DIM)
    o_as = window_decode(z4, caches).reshape(dec_b, GROUP_W)
    kv_sample = shift_caches(z4, caches)
    a_s = zs[:, 3 * ATT_WIDTH:3 * ATT_WIDTH + D_CONV].reshape(dec_b, 1, D_CONV)
    b_s = zs[:, 3 * ATT_WIDTH + D_CONV:].reshape(dec_b, 1, D_CONV)
    c_s, conv_sample = conv_step(a_s, b_s, state_conv, conv_w, conv_b, ln_g, ln_b)
    x1s = merge_branches(o_as, c_s.reshape(dec_b, D_CONV), gates_s, xs, w_pa16, w_pb16, w_o16, tm=dec_b)
    q_s = norm_matmul(x1s, g_x, w_xq, BF16, bm=dec_b)
    xo_s = cross_attention_decode(q_s, cache_mem_kv)
    y_sample = _trunk_tail(x1s, xo_s, w_xo, g_ffn, w_rg, b_rg, w_re, b_re, w_ein, w_eout, norm_final_g,
                           bm=dec_b, routed=False)

    return (y_prompt.reshape(batch, seq, D_MODEL), y_sample.reshape(dec_b, 1, D_MODEL),
            kv_prompt[0], kv_prompt[1], kv_prompt[2], conv_prompt, mem_kv_prompt,
            kv_sample[0], kv_sample[1], kv_sample[2], conv_sample)
```

```python
import functools

import jax
import jax.numpy as jnp
from jax import lax
from jax.experimental import pallas as pl
from jax.experimental.pallas import tpu as pltpu

F32 = jnp.float32
BF16 = jnp.bfloat16

D_MODEL = 2048
ATT_GROUPS = ((128, 1), (512, 4), (2048, 16))
HEADS = 4
HEAD_DIM = 128
GROUP_W = HEADS * HEAD_DIM
ATT_WIDTH = len(ATT_GROUPS) * GROUP_W
D_CONV = 1536
CONV_K = 31
N_MEM = 256
X_HEADS = 4
X_HEAD_DIM = D_MODEL // X_HEADS
N_EXPERT_GROUPS = 4
EXPERTS_PER_GROUP = 4
N_EXPERTS = 16
D_EXPERT = 512
ZA_W = 3 * ATT_WIDTH + 2 * D_CONV
GATE_W = 2 * D_MODEL
RMS_EPS = 1e-6
LN_EPS = 1e-5
NEG_INF = -1e30
BAND = 128
LANES = 128
ROUTER_GROUP_LANE = N_EXPERTS

V7X_VMEM_BYTES = 64 * 1024 * 1024
VMEM_LIMIT = V7X_VMEM_BYTES * 7 // 8


def _params(*sem):
    return pltpu.CompilerParams(dimension_semantics=sem, vmem_limit_bytes=VMEM_LIMIT)


def _rms_rows(x, g):
    ms = jnp.mean(x * x, axis=-1, keepdims=True)
    return x * lax.rsqrt(ms + RMS_EPS) * g


def _store_normed(x_ref, g_ref, xn_ref, chunk=256):
    rows = x_ref.shape[0]
    step = min(chunk, rows)
    for r0 in range(0, rows, step):
        xn_ref[r0:r0 + step, :] = _rms_rows(x_ref[r0:r0 + step, :], g_ref[...]).astype(xn_ref.dtype)


def _norm_matmul_kernel(x_ref, g_ref, w_ref, o_ref, xn_ref):
    @pl.when(pl.program_id(1) == 0)
    def _():
        _store_normed(x_ref, g_ref, xn_ref)

    o_ref[...] = jnp.dot(xn_ref[...], w_ref[...], preferred_element_type=F32).astype(o_ref.dtype)


def norm_matmul(x, g, w, out_dtype, bm, bn=512):
    m, k = x.shape
    n = w.shape[1]
    bm = min(bm, m)
    return pl.pallas_call(
        _norm_matmul_kernel,
        grid=(m // bm, n // bn),
        in_specs=[pl.BlockSpec((bm, k), lambda i, j: (i, 0)),
                  pl.BlockSpec((1, k), lambda i, j: (0, 0)),
                  pl.BlockSpec((k, bn), lambda i, j: (0, j))],
        out_specs=pl.BlockSpec((bm, bn), lambda i, j: (i, j)),
        out_shape=jax.ShapeDtypeStruct((m, n), out_dtype),
        scratch_shapes=[pltpu.VMEM((bm, k), BF16)],
        compiler_params=_params("parallel", "arbitrary"),
        name="norm_matmul",
    )(x, g.reshape(1, k), w)


def _norm_matmul_keep_kernel(x_ref, g_ref, w_ref, o_ref, xn_ref):
    @pl.when(pl.program_id(1) == 0)
    def _():
        _store_normed(x_ref, g_ref, xn_ref)

    o_ref[...] = jnp.dot(xn_ref[...], w_ref[...], preferred_element_type=F32)


def _matmul_kernel(x_ref, w_ref, o_ref):
    o_ref[...] = jnp.dot(x_ref[...], w_ref[...], preferred_element_type=F32)


ZA_BN = 1536


def in_proj(x, g, w_za, w_gt, bm):
    m, k = x.shape
    bm = min(bm, m)
    za, xn = pl.pallas_call(
        _norm_matmul_keep_kernel,
        grid=(m // bm, ZA_W // ZA_BN),
        in_specs=[pl.BlockSpec((bm, k), lambda i, j: (i, 0)),
                  pl.BlockSpec((1, k), lambda i, j: (0, 0)),
                  pl.BlockSpec((k, ZA_BN), lambda i, j: (0, j))],
        out_specs=[pl.BlockSpec((bm, ZA_BN), lambda i, j: (i, j)),
                   pl.BlockSpec((bm, k), lambda i, j: (i, 0))],
        out_shape=[jax.ShapeDtypeStruct((m, ZA_W), F32), jax.ShapeDtypeStruct((m, k), BF16)],
        compiler_params=_params("parallel", "arbitrary"),
        name="in_proj_za",
    )(x, g.reshape(1, k), w_za)
    bn = GATE_W // 2
    gates = pl.pallas_call(
        _matmul_kernel,
        grid=(m // bm, GATE_W // bn),
        in_specs=[pl.BlockSpec((bm, k), lambda i, j: (i, 0)),
                  pl.BlockSpec((k, bn), lambda i, j: (0, j))],
        out_specs=pl.BlockSpec((bm, bn), lambda i, j: (i, j)),
        out_shape=jax.ShapeDtypeStruct((m, GATE_W), F32),
        compiler_params=_params("parallel", "arbitrary"),
        name="in_proj_gates",
    )(xn, w_gt)
    return za, gates


def _matmul_res_kernel(x_ref, w_ref, r_ref, o_ref):
    o_ref[...] = r_ref[...] + jnp.dot(x_ref[...], w_ref[...], preferred_element_type=F32)


def matmul_residual(x, w, res, bm, bn=512):
    m, k = x.shape
    n = w.shape[1]
    bm = min(bm, m)
    return pl.pallas_call(
        _matmul_res_kernel,
        grid=(m // bm, n // bn),
        in_specs=[pl.BlockSpec((bm, k), lambda i, j: (i, 0)),
                  pl.BlockSpec((k, bn), lambda i, j: (0, j)),
                  pl.BlockSpec((bm, bn), lambda i, j: (i, j))],
        out_specs=pl.BlockSpec((bm, bn), lambda i, j: (i, j)),
        out_shape=jax.ShapeDtypeStruct((m, n), F32),
        compiler_params=_params("parallel", "arbitrary"),
        name="matmul_residual",
    )(x, w, res)


ATT_R = 2048
ATT_UNROLL = 16


def _band_attn_kernel(*refs):
    n_g = len(ATT_GROUPS)
    in_refs = refs[:5 * n_g]
    o_ref = refs[5 * n_g]
    scr = refs[5 * n_g + 1:]
    kbufs, vbufs = scr[:n_g], scr[n_g:2 * n_g]
    obufs, mbufs, dbufs = scr[2 * n_g:3 * n_g], scr[3 * n_g:4 * n_g], scr[4 * n_g:5 * n_g]
    first_chunk = pl.program_id(1) == 0
    row = lax.broadcasted_iota(jnp.int32, (BAND, 2 * BAND), 0)
    col = lax.broadcasted_iota(jnp.int32, (BAND, 2 * BAND), 1)
    in_band = (col >= row) & (col <= row + BAND)
    in_cur = col >= BAND
    scale = HEAD_DIM ** -0.5
    nt = (((1,), (1,)), ((), ()))

    for g, (win, dil) in enumerate(ATT_GROUPS):
        q_ref, kc_ref, vc_ref, kp_ref, vp_ref = in_refs[5 * g:5 * g + 5]
        kbuf, vbuf, obuf, mbuf, dbuf = kbufs[g], vbufs[g], obufs[g], mbufs[g], dbufs[g]
        kbuf[0:win, :] = kp_ref[...]
        kbuf[win:win + ATT_R, :] = kc_ref[...]
        vbuf[0:win, :] = vp_ref[...]
        vbuf[win:win + ATT_R, :] = vc_ref[...]
        shift = dil.bit_length() - 1

        def sub_blocks(i4, carry, q_ref=q_ref, kbuf=kbuf, vbuf=vbuf, obuf=obuf, mbuf=mbuf, dbuf=dbuf,
                       win=win, dil=dil, shift=shift):
            done = []
            for u in range(ATT_UNROLL):
                i = i4 * ATT_UNROLL + u
                span = lax.shift_right_logical(i, shift)
                base = span * win + (i & (dil - 1))
                if dil == 1:
                    q_rows, kv_rows = pl.ds(base, BAND), pl.ds(base, 2 * BAND)
                else:
                    q_rows, kv_rows = pl.ds(base, BAND, stride=dil), pl.ds(base, 2 * BAND, stride=dil)
                q = q_ref[q_rows, :].astype(BF16)
                k = kbuf[kv_rows, :].astype(BF16)
                v = vbuf[kv_rows, :].astype(BF16)
                s = lax.dot_general(q, k, nt, preferred_element_type=F32) * scale
                has_prev = jnp.logical_not(first_chunk & (span == 0))
                s = jnp.where(in_band & (in_cur | has_prev), s, NEG_INF)
                m = jnp.max(s, axis=-1, keepdims=True)
                p = jnp.exp(s - m)
                den = jnp.sum(p, axis=-1, keepdims=True)
                done.append((q_rows, jnp.dot(p.astype(BF16), v, preferred_element_type=F32), m, den))
            for q_rows, o, m, den in done:
                obuf[q_rows, :] = o
                mbuf[q_rows, :] = jnp.broadcast_to(m, (BAND, HEAD_DIM))
                dbuf[q_rows, :] = jnp.broadcast_to(den, (BAND, HEAD_DIM))
            return carry

        lax.fori_loop(0, ATT_R // BAND // ATT_UNROLL, sub_blocks, 0)

    chunk = 256
    for r0 in range(0, ATT_R, chunk):
        rs = slice(r0, r0 + chunk)
        ms = [mb[rs, :] for mb in mbufs]
        m_all = functools.reduce(jnp.maximum, ms)
        ws = [jnp.exp(m - m_all) for m in ms]
        num = sum(w * ob[rs, :] for w, ob in zip(ws, obufs))
        den = sum(w * db[rs, :] for w, db in zip(ws, dbufs))
        o_ref[rs, :] = (num / den).astype(o_ref.dtype)


def band_attention(za, batch, seq):
    assert seq % ATT_R == 0
    nch = seq // ATT_R
    in_specs, scratch = [], []
    for g, (win, dil) in enumerate(ATT_GROUPS):
        assert win // dil == BAND and ATT_R % win == 0
        cols = [(part * len(ATT_GROUPS) + g) * HEADS for part in range(3)]
        cur = lambda c: pl.BlockSpec((ATT_R, HEAD_DIM), lambda b, ch, h, c=c: (b * nch + ch, c + h))
        prev = lambda c, win=win: pl.BlockSpec(
            (win, HEAD_DIM), lambda b, ch, h, c=c, win=win: (jnp.maximum((b * seq + ch * ATT_R) // win - 1, 0), c + h))
        in_specs += [cur(cols[0]), cur(cols[1]), cur(cols[2]), prev(cols[1]), prev(cols[2])]
    for _ in range(2):
        scratch += [pltpu.VMEM((win + ATT_R, HEAD_DIM), F32) for win, _ in ATT_GROUPS]
    scratch += [pltpu.VMEM((ATT_R, HEAD_DIM), F32)] * (3 * len(ATT_GROUPS))
    return pl.pallas_call(
        _band_attn_kernel,
        grid=(batch, nch, HEADS),
        in_specs=in_specs,
        out_specs=pl.BlockSpec((ATT_R, HEAD_DIM), lambda b, ch, h: (b * nch + ch, h)),
        out_shape=jax.ShapeDtypeStruct((batch * seq, GROUP_W), BF16),
        scratch_shapes=scratch,
        compiler_params=_params("parallel", "parallel", "parallel"),
        name="band_attention",
    )(*([za] * (5 * len(ATT_GROUPS))))


def _window_tails_kernel(*refs):
    n_g = len(ATT_GROUPS)
    slabs = 2 * HEADS
    for g in range(n_g):
        k_ref, v_ref, o_ref = refs[2 * g], refs[2 * g + 1], refs[2 * n_g + g]
        rows = k_ref.shape[0]
        for part, src in enumerate((k_ref, v_ref)):
            for h in range(HEADS):
                o_ref[pl.ds(part * HEADS + h, rows, stride=slabs), :] = src[:, h * HEAD_DIM:(h + 1) * HEAD_DIM]


def window_tails(za, batch, seq):
    in_specs, out_specs, out_shape = [], [], []
    slabs = 2 * HEADS
    for g, (win, _) in enumerate(ATT_GROUPS):
        assert win <= seq and seq % win == 0
        last = seq // win - 1
        for part in (1, 2):
            in_specs.append(pl.BlockSpec(
                (win, GROUP_W), lambda b, g=g, part=part, win=win, last=last:
                (b * (seq // win) + last, part * len(ATT_GROUPS) + g)))
        out_specs.append(pl.BlockSpec((win * slabs, HEAD_DIM), lambda b: (b, 0)))
        out_shape.append(jax.ShapeDtypeStruct((batch * win * slabs, HEAD_DIM), F32))
    outs = pl.pallas_call(
        _window_tails_kernel,
        grid=(batch,),
        in_specs=in_specs,
        out_specs=out_specs,
        out_shape=out_shape,
        compiler_params=_params("parallel"),
        name="window_tails",
    )(*([za] * (2 * len(ATT_GROUPS))))
    return [o.reshape(1, batch, win, 2, HEADS, HEAD_DIM) for o, (win, _) in zip(outs, ATT_GROUPS)]


CONV_T = 256
CONV_HIST = 32
CONV_RC = 128
CONV_PHASES = 4
CONV_SLABS = D_CONV // LANES


def _layernorm_silu(y, g, b):
    mu = jnp.mean(y, axis=-1, keepdims=True)
    yc = y - mu
    var = jnp.mean(yc * yc, axis=-1, keepdims=True)
    yn = yc * lax.rsqrt(var + LN_EPS) * g + b
    return yn * jax.nn.sigmoid(yn)


def _conv_history(t, ubuf):
    @pl.when(t == 0)
    def _():
        ubuf[:, 0:CONV_HIST, :] = jnp.zeros((CONV_SLABS, CONV_HIST, LANES), F32)

    @pl.when(t > 0)
    def _():
        ubuf[:, 0:CONV_HIST, :] = ubuf[:, CONV_T:CONV_T + CONV_HIST, :]


def _conv_slab(j, a_ref, b_ref, w_ref, cb_ref, tail_ref, ubuf, ybuf):
    first = CONV_HIST - (CONV_K - 1)
    n = CONV_RC // CONV_PHASES
    js = slice(j * LANES, (j + 1) * LANES)
    u = a_ref[:, js] * jax.nn.sigmoid(b_ref[:, js])
    ubuf[j, CONV_HIST:CONV_HIST + CONV_T, :] = u
    tail_ref[0, :, js] = u[CONV_T - CONV_HIST:, :]
    for r0 in range(0, CONV_T, CONV_RC):
        accs = [jnp.broadcast_to(cb_ref[:, js], (n, LANES))] * CONV_PHASES
        for s in range(first, first + CONV_K + CONV_PHASES - 1):
            rows = ubuf[j, pl.ds(r0 + s, n, stride=CONV_PHASES), :]
            for p in range(CONV_PHASES):
                k = s - first - p
                if 0 <= k < CONV_K:
                    accs[p] = accs[p] + w_ref[k:k + 1, js] * rows
        for p in range(CONV_PHASES):
            ybuf[j, pl.ds(r0 + p, n, stride=CONV_PHASES), :] = accs[p]
    return jnp.sum(ybuf[j], axis=-1, keepdims=True)


def _conv_centered_sq(j, mu, ybuf):
    yc = ybuf[j] - mu
    return jnp.sum(yc * yc, axis=-1, keepdims=True)


def _conv_norm_slab(j, mu, inv, lg_ref, lb_ref, c_ref, ybuf):
    js = slice(j * LANES, (j + 1) * LANES)
    yn = (ybuf[j] - mu) * inv * lg_ref[:, js] + lb_ref[:, js]
    c_ref[:, js] = (yn * jax.nn.sigmoid(yn)).astype(c_ref.dtype)


def _merge_kernel(oa_ref, c_ref, ga_ref, gb_ref, x_ref, wpa_ref, wpb_ref, wo_ref, y_ref):
    ta = jnp.dot(oa_ref[...], wpa_ref[...], preferred_element_type=F32)
    tb = jnp.dot(c_ref[...], wpb_ref[...], preferred_element_type=F32)
    hmix = jax.nn.sigmoid(ga_ref[...]) * ta + jax.nn.sigmoid(gb_ref[...]) * tb
    y_ref[...] = x_ref[...] + jnp.dot(hmix.astype(BF16), wo_ref[...], preferred_element_type=F32)


def _conv_merge_kernel(oa_ref, a_ref, b_ref, ga_ref, gb_ref, x_ref, cw_ref, cb_ref, lg_ref, lb_ref,
                       wpa_ref, wpb_ref, wo_ref, y_ref, tail_ref, ubuf, ybuf, c_scr):
    _conv_history(pl.program_id(1), ubuf)
    total = jnp.zeros((CONV_T, 1), F32)
    for j in range(CONV_SLABS):
        total = total + _conv_slab(j, a_ref, b_ref, cw_ref, cb_ref, tail_ref, ubuf, ybuf)
    mu = total * (1.0 / D_CONV)
    sq = jnp.zeros((CONV_T, 1), F32)
    for j in range(CONV_SLABS):
        sq = sq + _conv_centered_sq(j, mu, ybuf)
    inv = lax.rsqrt(sq * (1.0 / D_CONV) + LN_EPS)
    for j in range(CONV_SLABS):
        _conv_norm_slab(j, mu, inv, lg_ref, lb_ref, c_scr, ybuf)
    _merge_kernel(oa_ref, c_scr, ga_ref, gb_ref, x_ref, wpa_ref, wpb_ref, wo_ref, y_ref)


def conv_merge_branches(o_a, za, gates, x, batch, seq, conv_w, conv_b, ln_g, ln_b, w_pa, w_pb, w_out):
    nt = seq // CONV_T
    row = lambda b, t: (b * nt + t, 0)
    const = lambda b, t: (0, 0)
    once = dict(pipeline_mode=pl.Buffered(1))
    glu = 3 * ATT_WIDTH // D_CONV
    y, tail = pl.pallas_call(
        _conv_merge_kernel,
        grid=(batch, nt),
        in_specs=[pl.BlockSpec((CONV_T, GROUP_W), row),
                  pl.BlockSpec((CONV_T, D_CONV), lambda b, t: (b * nt + t, glu)),
                  pl.BlockSpec((CONV_T, D_CONV), lambda b, t: (b * nt + t, glu + 1)),
                  pl.BlockSpec((CONV_T, D_MODEL), lambda b, t: (b * nt + t, 0)),
                  pl.BlockSpec((CONV_T, D_MODEL), lambda b, t: (b * nt + t, 1)),
                  pl.BlockSpec((CONV_T, D_MODEL), row),
                  pl.BlockSpec((CONV_K, D_CONV), const),
                  pl.BlockSpec((1, D_CONV), const),
                  pl.BlockSpec((1, D_CONV), const),
                  pl.BlockSpec((1, D_CONV), const),
                  pl.BlockSpec((GROUP_W, D_MODEL), const, **once),
                  pl.BlockSpec((D_CONV, D_MODEL), const, **once),
                  pl.BlockSpec((D_MODEL, D_MODEL), const, **once)],
        out_specs=[pl.BlockSpec((CONV_T, D_MODEL), row),
                   pl.BlockSpec((1, CONV_HIST, D_CONV), lambda b, t: (b, 0, 0))],
        out_shape=[jax.ShapeDtypeStruct((batch * seq, D_MODEL), F32),
                   jax.ShapeDtypeStruct((batch, CONV_HIST, D_CONV), F32)],
        scratch_shapes=[pltpu.VMEM((CONV_SLABS, CONV_HIST + CONV_T, LANES), F32),
                        pltpu.VMEM((CONV_SLABS, CONV_T, LANES), F32),
                        pltpu.VMEM((CONV_T, D_CONV), BF16)],
        compiler_params=_params("parallel", "arbitrary"),
        name="conv_merge_branches",
    )(o_a, za, za, gates, gates, x, conv_w, conv_b.reshape(1, D_CONV), ln_g.reshape(1, D_CONV),
      ln_b.reshape(1, D_CONV), w_pa, w_pb, w_out)
    return y, tail[:, CONV_HIST - (CONV_K - 1):]


def merge_branches(o_a, c, gates, x, w_pa, w_pb, w_out, tm):
    m = x.shape[0]
    tm = min(tm, m)
    row = lambda i: (i, 0)
    const = lambda i: (0, 0)
    once = dict(pipeline_mode=pl.Buffered(1))
    return pl.pallas_call(
        _merge_kernel,
        grid=(m // tm,),
        in_specs=[
            pl.BlockSpec((tm, GROUP_W), row),
            pl.BlockSpec((tm, D_CONV), row),
            pl.BlockSpec((tm, D_MODEL), lambda i: (i, 0)),
            pl.BlockSpec((tm, D_MODEL), lambda i: (i, 1)),
            pl.BlockSpec((tm, D_MODEL), row),
            pl.BlockSpec((GROUP_W, D_MODEL), const, **once),
            pl.BlockSpec((D_CONV, D_MODEL), const, **once),
            pl.BlockSpec((D_MODEL, D_MODEL), const, **once)],
        out_specs=pl.BlockSpec((tm, D_MODEL), row),
        out_shape=jax.ShapeDtypeStruct((m, D_MODEL), F32),
        compiler_params=_params("parallel"),
        name="merge_branches",
    )(o_a, c, gates, gates, x, w_pa, w_pb, w_out)


def _xattn_kernel(q_ref, kv_ref, o_ref):
    scale = X_HEAD_DIM ** -0.5
    nt = (((1,), (1,)), ((), ()))
    for h in range(X_HEADS):
        hs = slice(h * X_HEAD_DIM, (h + 1) * X_HEAD_DIM)
        vs = slice(D_MODEL + h * X_HEAD_DIM, D_MODEL + (h + 1) * X_HEAD_DIM)
        s = lax.dot_general(q_ref[:, hs], kv_ref[:, hs].astype(BF16), nt, preferred_element_type=F32) * scale
        m = jnp.max(s, axis=-1, keepdims=True)
        p = jnp.exp(s - m)
        p = p / jnp.sum(p, axis=-1, keepdims=True)
        o = jnp.dot(p.astype(BF16), kv_ref[:, vs].astype(BF16), preferred_element_type=F32)
        o_ref[:, hs] = o.astype(o_ref.dtype)


def cross_attention_prompt(q, mkv, batch, seq, tm=512):
    nt = seq // tm
    return pl.pallas_call(
        _xattn_kernel,
        grid=(batch, nt),
        in_specs=[pl.BlockSpec((tm, D_MODEL), lambda b, t: (b * nt + t, 0)),
                  pl.BlockSpec((N_MEM, 2 * D_MODEL), lambda b, t: (b, 0))],
        out_specs=pl.BlockSpec((tm, D_MODEL), lambda b, t: (b * nt + t, 0)),
        out_shape=jax.ShapeDtypeStruct((batch * seq, D_MODEL), BF16),
        compiler_params=_params("parallel", "arbitrary"),
        name="cross_attention_prompt",
    )(q, mkv)


def _xattn_decode_kernel(q_ref, kv_ref, o_ref):
    scale = X_HEAD_DIM ** -0.5
    q = q_ref[0].astype(F32)
    k = kv_ref[0, 0, :, 0]
    v = kv_ref[0, 0, :, 1]
    s = jnp.sum(k * q, axis=-1, keepdims=True) * scale
    m = jnp.max(s, axis=0, keepdims=True)
    p = jnp.exp(s - m)
    p = p / jnp.sum(p, axis=0, keepdims=True)
    o_ref[0] = jnp.sum(p * v, axis=0, keepdims=True).astype(o_ref.dtype)


def cross_attention_decode(q, cache_mem_kv):
    b = q.shape[0]
    q4 = q.reshape(b, 1, X_HEADS, X_HEAD_DIM)
    o = pl.pallas_call(
        _xattn_decode_kernel,
        grid=(b,),
        in_specs=[pl.BlockSpec((1, 1, X_HEADS, X_HEAD_DIM), lambda i: (i, 0, 0, 0)),
                  pl.BlockSpec((1, 1, N_MEM, 2, X_HEADS, X_HEAD_DIM), lambda i: (0, i, 0, 0, 0, 0))],
        out_specs=pl.BlockSpec((1, 1, X_HEADS, X_HEAD_DIM), lambda i: (i, 0, 0, 0)),
        out_shape=jax.ShapeDtypeStruct((b, 1, X_HEADS, X_HEAD_DIM), BF16),
        compiler_params=_params("parallel"),
        name="cross_attention_decode",
    )(q4, cache_mem_kv)
    return o.reshape(b, D_MODEL)


_PAIRS = [(lo, hi) for lo in range(EXPERTS_PER_GROUP) for hi in range(lo + 1, EXPERTS_PER_GROUP)]
PAIRS_PER_GROUP = len(_PAIRS)
N_BUCKETS = N_EXPERT_GROUPS * PAIRS_PER_GROUP
PAIR_LO = tuple(lo for lo, _ in _PAIRS)
PAIR_HI = tuple(hi for _, hi in _PAIRS)
ROW_TILES = D_MODEL // LANES
PAYLOAD_ROWS = ROW_TILES + 8
MOE_TM = 256


def _route(logits):
    lane = lax.broadcasted_iota(jnp.int32, logits.shape, 1)
    lanef = lane.astype(F32)
    big = float(LANES)
    is_group = (lane >= ROUTER_GROUP_LANE) & (lane < ROUTER_GROUP_LANE + N_EXPERT_GROUPS)
    lg = jnp.where(is_group, logits, -jnp.inf)
    mg = jnp.max(lg, axis=-1, keepdims=True)
    p_sel = 1.0 / jnp.sum(jnp.exp(lg - mg), axis=-1, keepdims=True)
    gsel = jnp.min(jnp.where(lg == mg, lanef, big), axis=-1, keepdims=True) - ROUTER_GROUP_LANE
    group_of_lane = lax.shift_right_logical(lane, EXPERTS_PER_GROUP.bit_length() - 1)
    in_group = (lane < N_EXPERTS) & (group_of_lane == gsel.astype(jnp.int32))
    le = jnp.where(in_group, logits, -jnp.inf)
    v1 = jnp.max(le, axis=-1, keepdims=True)
    i1 = jnp.min(jnp.where(le == v1, lanef, big), axis=-1, keepdims=True)
    le2 = jnp.where(lanef == i1, -jnp.inf, le)
    v2 = jnp.max(le2, axis=-1, keepdims=True)
    i2 = jnp.min(jnp.where(le2 == v2, lanef, big), axis=-1, keepdims=True)
    t = jnp.exp(v2 - v1)
    tot = 1.0 + t
    return gsel, i1, i2, (1.0 / tot) * p_sel, (t / tot) * p_sel


def _router_routed_kernel(x_ref, g_ref, w_ref, b_ref, xd_ref, meta_ref, cnt_ref, carry_ref):
    tm = x_ref.shape[0]

    @pl.when(pl.program_id(0) == 0)
    def _():
        carry_ref[...] = jnp.zeros_like(carry_ref)

    xn = _rms_rows(x_ref[...], g_ref[...])
    logits = _router_logits(xn, w_ref, b_ref)
    gsel, i1, i2, p1, p2 = _route(logits)
    lo = jnp.minimum(i1, i2) - EXPERTS_PER_GROUP * gsel
    hi = jnp.maximum(i1, i2) - EXPERTS_PER_GROUP * gsel
    pair = lo * (2.0 * EXPERTS_PER_GROUP - 1.0 - lo) * 0.5 + (hi - lo - 1.0)
    bucket = gsel * float(PAIRS_PER_GROUP) + pair
    gate_lo = jnp.where(i1 < i2, p1, p2)
    gate_hi = jnp.where(i1 < i2, p2, p1)

    lane = lax.broadcasted_iota(jnp.int32, (tm, LANES), 1)
    onehot = (lane.astype(F32) == bucket).astype(F32)
    r_i = lax.broadcasted_iota(jnp.int32, (tm, tm), 0)
    c_i = lax.broadcasted_iota(jnp.int32, (tm, tm), 1)
    before = (c_i < r_i).astype(BF16)
    rank_local = jnp.dot(before, onehot.astype(BF16), preferred_element_type=F32)
    rank = jnp.sum(onehot * (rank_local + carry_ref[0:1, :]), axis=-1, keepdims=True)
    carry_ref[0:1, :] = carry_ref[0:1, :] + jnp.sum(onehot, axis=0, keepdims=True)
    cnt_ref[...] = jnp.broadcast_to(carry_ref[0:1, :], cnt_ref.shape)
    meta_ref[...] = jnp.where(lane == 0, bucket, jnp.where(lane == 1, rank, 0.0))

    for j in range(ROW_TILES):
        xd_ref[pl.ds(j, tm, stride=PAYLOAD_ROWS), :] = xn[:, j * LANES:(j + 1) * LANES]
    xd_ref[pl.ds(ROW_TILES, tm, stride=PAYLOAD_ROWS), :] = jnp.where(
        lane == 0, gate_lo, jnp.where(lane == 1, gate_hi, 0.0))
    for j in range(ROW_TILES + 1, PAYLOAD_ROWS):
        xd_ref[pl.ds(j, tm, stride=PAYLOAD_ROWS), :] = jnp.zeros((tm, LANES), F32)


def _router_weights(w_rg, b_rg, w_re, b_re):
    pad = LANES - N_EXPERTS - N_EXPERT_GROUPS
    w = jnp.concatenate([w_re, w_rg, jnp.zeros((D_MODEL, pad), F32)], axis=1)
    b = jnp.concatenate([b_re, b_rg, jnp.zeros((pad,), F32)]).reshape(1, LANES)
    hi = w.astype(BF16)
    lo = (w - hi.astype(F32)).astype(BF16)
    return jnp.stack([hi, lo]), b


def _router_logits(xn, w_ref, b_ref):
    hi = xn.astype(BF16)
    lo = (xn - hi.astype(F32)).astype(BF16)
    acc = jnp.dot(hi, w_ref[0], preferred_element_type=F32)
    acc = acc + jnp.dot(lo, w_ref[0], preferred_element_type=F32)
    acc = acc + jnp.dot(hi, w_ref[1], preferred_element_type=F32)
    return acc + b_ref[...]


def moe_router_routed(x, g, w_rg, b_rg, w_re, b_re, tm=512):
    m = x.shape[0]
    w, b = _router_weights(w_rg, b_rg, w_re, b_re)
    row = lambda i: (i, 0)
    const = lambda i: (0, 0)
    return pl.pallas_call(
        _router_routed_kernel,
        grid=(m // tm,),
        in_specs=[pl.BlockSpec((tm, D_MODEL), row), pl.BlockSpec((1, D_MODEL), const),
                  pl.BlockSpec((2, D_MODEL, LANES), lambda i: (0, 0, 0)), pl.BlockSpec((1, LANES), const)],
        out_specs=[pl.BlockSpec((tm * PAYLOAD_ROWS, LANES), row), pl.BlockSpec((tm, LANES), row),
                   pl.BlockSpec((8, LANES), const)],
        out_shape=[jax.ShapeDtypeStruct((m * PAYLOAD_ROWS, LANES), F32), jax.ShapeDtypeStruct((m, LANES), F32),
                   jax.ShapeDtypeStruct((8, LANES), F32)],
        scratch_shapes=[pltpu.VMEM((8, LANES), F32)],
        compiler_params=_params("arbitrary"),
        name="moe_router_routed",
    )(x, g.reshape(1, D_MODEL), w, b)


INV_CHUNK = 2048


def _slot_maps_kernel(bucket_ref, rank_ref, first_ref, dest_ref, inv_ref):
    step = pl.program_id(0)

    @pl.when(step == 0)
    def _():
        def clear(k, c):
            inv_ref[k] = 0
            return c
        lax.fori_loop(0, inv_ref.shape[0], clear, 0, unroll=8)

    def put(t, c):
        slot = first_ref[bucket_ref[0, 0, t]] + rank_ref[0, 0, t]
        dest_ref[0, 0, t] = slot
        inv_ref[slot] = step * INV_CHUNK + t
        return c
    lax.fori_loop(0, INV_CHUNK, put, 0, unroll=8)


def slot_maps(bucket, rank, first_slot, n_slots):
    m = bucket.shape[0]
    nc = m // INV_CHUNK
    smem = dict(memory_space=pltpu.SMEM)
    chunk = pl.BlockSpec((1, 1, INV_CHUNK), lambda i: (i, 0, 0), **smem)
    dest, inv = pl.pallas_call(
        _slot_maps_kernel,
        grid=(nc,),
        in_specs=[chunk, chunk, pl.BlockSpec(**smem)],
        out_specs=[chunk, pl.BlockSpec(**smem)],
        out_shape=[jax.ShapeDtypeStruct((nc, 1, INV_CHUNK), jnp.int32),
                   jax.ShapeDtypeStruct((n_slots,), jnp.int32)],
        compiler_params=_params("arbitrary"),
        name="slot_maps",
    )(bucket.reshape(nc, 1, INV_CHUNK), rank.reshape(nc, 1, INV_CHUNK), first_slot)
    return dest.reshape(m), inv


def _start_row_gather(idx_ref, src_hbm, dst, sem, n_items, rows):
    def start(r2, c):
        for queue in range(2):
            r = 2 * r2 + queue
            first = pl.multiple_of(idx_ref[0, 0, r] * rows, 8)
            pltpu.make_async_copy(src_hbm.at[pl.ds(first, rows)], dst.at[pl.ds(r * rows, rows)],
                                  sem).start(priority=queue)
        return c
    lax.fori_loop(0, n_items // 2, start, 0, unroll=4)


def _wait_row_gather(src_hbm, dst, sem, n_items, rows):
    pltpu.make_async_copy(src_hbm.at[pl.ds(0, n_items * rows)], dst, sem).wait()


def _moe_routed_kernel(ea_ref, eb_ref, nv_ref, inv_ref, inv_next_ref, xd_hbm, w1a_ref, w1b_ref, w2a_ref, w2b_ref,
                       ys_ref, xbuf, sems, x_scr):
    del ea_ref, eb_ref
    i = pl.program_id(0)
    n_valid = nv_ref[0]
    slot = i % 2
    tm = MOE_TM

    @pl.when(i == 0)
    def _():
        _start_row_gather(inv_ref, xd_hbm, xbuf.at[0], sems.at[0], tm, PAYLOAD_ROWS)

    @pl.when(i + 1 < n_valid)
    def _():
        _start_row_gather(inv_next_ref, xd_hbm, xbuf.at[1 - slot], sems.at[1 - slot], tm, PAYLOAD_ROWS)

    @pl.when(i < n_valid)
    def _():
        buf = xbuf.at[slot]
        _wait_row_gather(xd_hbm, buf, sems.at[slot], tm, PAYLOAD_ROWS)
        for j in range(ROW_TILES):
            x_scr[:, j * LANES:(j + 1) * LANES] = buf[pl.ds(j, tm, stride=PAYLOAD_ROWS), :].astype(BF16)
        gates = buf[pl.ds(ROW_TILES, tm, stride=PAYLOAD_ROWS), :]
        x = x_scr[...]
        hu = jnp.dot(x, w1a_ref[0], preferred_element_type=F32)
        ha = jax.nn.silu(hu[:, :D_EXPERT]) * hu[:, D_EXPERT:] * gates[:, 0:1]
        hu = jnp.dot(x, w1b_ref[0], preferred_element_type=F32)
        hb = jax.nn.silu(hu[:, :D_EXPERT]) * hu[:, D_EXPERT:] * gates[:, 1:2]
        y = (jnp.dot(ha.astype(BF16), w2a_ref[0], preferred_element_type=F32)
             + jnp.dot(hb.astype(BF16), w2b_ref[0], preferred_element_type=F32))
        for j in range(ROW_TILES):
            ys_ref[pl.ds(j, tm, stride=ROW_TILES), :] = y[:, j * LANES:(j + 1) * LANES]

    @pl.when(i >= n_valid)
    def _():
        ys_ref[...] = jnp.zeros_like(ys_ref)


def moe_experts_routed(xd, inv, tile_ea, tile_eb, n_valid, w_ein, w_eout):
    n_tiles = tile_ea.shape[0]
    tm = MOE_TM
    inv3 = inv.reshape(n_tiles, 1, tm)
    smem = dict(memory_space=pltpu.SMEM)
    grid_spec = pltpu.PrefetchScalarGridSpec(
        num_scalar_prefetch=3,
        grid=(n_tiles,),
        in_specs=[
            pl.BlockSpec((1, 1, tm), lambda i, ea, eb, nv: (i, 0, 0), **smem),
            pl.BlockSpec((1, 1, tm), lambda i, ea, eb, nv: (jnp.minimum(i + 1, n_tiles - 1), 0, 0), **smem),
            pl.BlockSpec(memory_space=pl.ANY),
            pl.BlockSpec((1, D_MODEL, 2 * D_EXPERT), lambda i, ea, eb, nv: (ea[i], 0, 0)),
            pl.BlockSpec((1, D_MODEL, 2 * D_EXPERT), lambda i, ea, eb, nv: (eb[i], 0, 0)),
            pl.BlockSpec((1, D_EXPERT, D_MODEL), lambda i, ea, eb, nv: (ea[i], 0, 0)),
            pl.BlockSpec((1, D_EXPERT, D_MODEL), lambda i, ea, eb, nv: (eb[i], 0, 0))],
        out_specs=pl.BlockSpec((tm * ROW_TILES, LANES), lambda i, ea, eb, nv: (i, 0)),
        scratch_shapes=[pltpu.VMEM((2, tm * PAYLOAD_ROWS, LANES), F32), pltpu.SemaphoreType.DMA((2,)),
                        pltpu.VMEM((tm, D_MODEL), BF16)])
    return pl.pallas_call(
        _moe_routed_kernel,
        grid_spec=grid_spec,
        out_shape=jax.ShapeDtypeStruct((n_tiles * tm * ROW_TILES, LANES), F32),
        compiler_params=_params("arbitrary"),
        name="moe_experts_routed",
    )(tile_ea, tile_eb, n_valid, inv3, inv3, xd, w_ein, w_ein, w_eout, w_eout)


def _combine_kernel(dest_ref, dest_next_ref, ys_hbm, res_ref, gf_ref, y_ref, buf, sems):
    i = pl.program_id(0)
    slot = i % 2
    tm = res_ref.shape[0]

    @pl.when(i == 0)
    def _():
        _start_row_gather(dest_ref, ys_hbm, buf.at[0], sems.at[0], tm, ROW_TILES)

    @pl.when(i + 1 < pl.num_programs(0))
    def _():
        _start_row_gather(dest_next_ref, ys_hbm, buf.at[1 - slot], sems.at[1 - slot], tm, ROW_TILES)

    cur = buf.at[slot]
    _wait_row_gather(ys_hbm, cur, sems.at[slot], tm, ROW_TILES)
    for j in range(ROW_TILES):
        js = slice(j * LANES, (j + 1) * LANES)
        y_ref[:, js] = res_ref[:, js] + cur[pl.ds(j, tm, stride=ROW_TILES), :]
    y_ref[...] = _rms_rows(y_ref[...], gf_ref[...])


def moe_combine_final(ys, dest, res, g_final, tm=256):
    m = res.shape[0]
    nt = m // tm
    dest3 = dest.reshape(nt, 1, tm)
    smem = dict(memory_space=pltpu.SMEM)
    return pl.pallas_call(
        _combine_kernel,
        grid=(nt,),
        in_specs=[pl.BlockSpec((1, 1, tm), lambda i: (i, 0, 0), **smem),
                  pl.BlockSpec((1, 1, tm), lambda i: (jnp.minimum(i + 1, nt - 1), 0, 0), **smem),
                  pl.BlockSpec(memory_space=pl.ANY),
                  pl.BlockSpec((tm, D_MODEL), lambda i: (i, 0)),
                  pl.BlockSpec((1, D_MODEL), lambda i: (0, 0))],
        out_specs=pl.BlockSpec((tm, D_MODEL), lambda i: (i, 0)),
        out_shape=jax.ShapeDtypeStruct((m, D_MODEL), F32),
        scratch_shapes=[pltpu.VMEM((2, tm * ROW_TILES, LANES), F32), pltpu.SemaphoreType.DMA((2,))],
        compiler_params=_params("arbitrary"),
        name="moe_combine_final",
    )(dest3, dest3, ys, res, g_final.reshape(1, D_MODEL))


def moe_routed(x, g_ffn, w_rg, b_rg, w_re, b_re, w_ein, w_eout, g_final):
    m = x.shape[0]
    tm = MOE_TM
    n_tiles = m // tm + N_BUCKETS
    xd, meta, cnt = moe_router_routed(x, g_ffn, w_rg, b_rg, w_re, b_re)
    bucket = meta[:, 0].astype(jnp.int32)
    rank = meta[:, 1].astype(jnp.int32)
    counts = cnt[0, :N_BUCKETS].astype(jnp.int32)
    tiles_per_bucket = (counts + tm - 1) // tm
    tile_end = jnp.cumsum(tiles_per_bucket)
    tile_start = tile_end - tiles_per_bucket
    n_valid = tile_end[-1]
    tile_id = jnp.arange(n_tiles, dtype=jnp.int32)
    tile_bucket = jnp.searchsorted(tile_end, jnp.minimum(tile_id, n_valid - 1), side="right").astype(jnp.int32)
    group, pair = tile_bucket // PAIRS_PER_GROUP, tile_bucket % PAIRS_PER_GROUP
    tile_ea = group * EXPERTS_PER_GROUP + jnp.take(jnp.array(PAIR_LO, jnp.int32), pair)
    tile_eb = group * EXPERTS_PER_GROUP + jnp.take(jnp.array(PAIR_HI, jnp.int32), pair)
    dest, inv = slot_maps(bucket, rank, tile_start * tm, n_tiles * tm)
    ys = moe_experts_routed(xd, inv, tile_ea, tile_eb, n_valid.reshape(1), w_ein, w_eout)
    return moe_combine_final(ys, dest, x, g_final)


def _router_kernel(x_ref, g_ref, w_ref, b_ref, xn_ref, gate_ref):
    xn = _rms_rows(x_ref[...], g_ref[...])
    xn_ref[...] = xn.astype(BF16)
    logits = _router_logits(xn, w_ref, b_ref)
    _, i1, i2, p1, p2 = _route(logits)
    lanef = lax.broadcasted_iota(jnp.int32, logits.shape, 1).astype(F32)
    gate_ref[...] = jnp.where(lanef == i1, p1, 0.0) + jnp.where(lanef == i2, p2, 0.0)


def moe_router(x, g, w_rg, b_rg, w_re, b_re, tm):
    m = x.shape[0]
    tm = min(tm, m)
    w, b = _router_weights(w_rg, b_rg, w_re, b_re)
    row = lambda i: (i, 0)
    const = lambda i: (0, 0)
    return pl.pallas_call(
        _router_kernel,
        grid=(m // tm,),
        in_specs=[pl.BlockSpec((tm, D_MODEL), row), pl.BlockSpec((1, D_MODEL), const),
                  pl.BlockSpec((2, D_MODEL, LANES), lambda i: (0, 0, 0)), pl.BlockSpec((1, LANES), const)],
        out_specs=[pl.BlockSpec((tm, D_MODEL), row), pl.BlockSpec((tm, LANES), row)],
        out_shape=[jax.ShapeDtypeStruct((m, D_MODEL), BF16), jax.ShapeDtypeStruct((m, LANES), F32)],
        compiler_params=_params("parallel"),
        name="moe_router",
    )(x, g.reshape(1, D_MODEL), w, b)


def _moe_dense_kernel(xn_ref, gate_ref, w1_ref, w2_ref, res_ref, gf_ref, y_ref):
    e = pl.program_id(1)

    @pl.when(e == 0)
    def _():
        y_ref[...] = res_ref[...]

    hu = jnp.dot(xn_ref[...], w1_ref[0], preferred_element_type=F32)
    h = jax.nn.silu(hu[:, :D_EXPERT]) * hu[:, D_EXPERT:]
    lane = lax.broadcasted_iota(jnp.int32, gate_ref.shape, 1)
    gate = jnp.sum(jnp.where(lane == e, gate_ref[...], 0.0), axis=-1, keepdims=True)
    h = h * gate
    y_ref[...] += jnp.dot(h.astype(BF16), w2_ref[0], preferred_element_type=F32)

    @pl.when(e == N_EXPERTS - 1)
    def _():
        y_ref[...] = _rms_rows(y_ref[...], gf_ref[...])


def moe_experts_final(xn, gate, w_ein, w_eout, res, g_final, tm):
    m = xn.shape[0]
    tm = min(tm, m)
    row = lambda i, e: (i, 0)
    return pl.pallas_call(
        _moe_dense_kernel,
        grid=(m // tm, N_EXPERTS),
        in_specs=[pl.BlockSpec((tm, D_MODEL), row), pl.BlockSpec((tm, LANES), row),
                  pl.BlockSpec((1, D_MODEL, 2 * D_EXPERT), lambda i, e: (e, 0, 0)),
                  pl.BlockSpec((1, D_EXPERT, D_MODEL), lambda i, e: (e, 0, 0)),
                  pl.BlockSpec((tm, D_MODEL), row),
                  pl.BlockSpec((1, D_MODEL), lambda i, e: (0, 0))],
        out_specs=pl.BlockSpec((tm, D_MODEL), row),
        out_shape=jax.ShapeDtypeStruct((m, D_MODEL), F32),
        compiler_params=_params("parallel", "arbitrary"),
        name="moe_experts_final",
    )(xn, gate, w_ein, w_eout, res, g_final.reshape(1, D_MODEL))


def _window_decode_kernel(z_ref, c1_ref, c2_ref, c3_ref, o_ref):
    scale = HEAD_DIM ** -0.5
    os, ms, ds = [], [], []
    for g, c_ref in enumerate((c1_ref, c2_ref, c3_ref)):
        q = z_ref[0, g:g + 1]
        k_new = z_ref[0, 3 + g:4 + g]
        v_new = z_ref[0, 6 + g:7 + g]
        k = c_ref[0, 0, :, 0, 0]
        v = c_ref[0, 0, :, 0, 1]
        s = jnp.sum(k * q, axis=-1, keepdims=True) * scale
        s_new = jnp.sum(k_new * q, axis=-1, keepdims=True) * scale
        m = jnp.maximum(jnp.max(s, axis=0, keepdims=True), s_new)
        p = jnp.exp(s - m)
        p_new = jnp.exp(s_new - m)
        den = jnp.sum(p, axis=0, keepdims=True) + p_new
        os.append(jnp.sum(p * v, axis=0, keepdims=True) + p_new * v_new)
        ms.append(m)
        ds.append(den)
    m_all = functools.reduce(jnp.maximum, ms)
    ws = [jnp.exp(m - m_all) for m in ms]
    num = sum(w * o for w, o in zip(ws, os))
    den = sum(w * d for w, d in zip(ws, ds))
    o_ref[0] = (num / den).astype(o_ref.dtype)


def window_decode(z4, caches):
    b = z4.shape[0]
    views, specs = [], []
    for cache, (win, dil) in zip(caches, ATT_GROUPS):
        n = cache.shape[2]
        assert n == win and n // dil == BAND
        views.append(cache.reshape(1, b, BAND, dil, 2, HEADS, HEAD_DIM))
        specs.append(pl.BlockSpec((1, 1, BAND, 1, 2, HEADS, HEAD_DIM), lambda i: (0, i, 0, 0, 0, 0, 0)))
    return pl.pallas_call(
        _window_decode_kernel,
        grid=(b,),
        in_specs=[pl.BlockSpec((1,) + z4.shape[1:], lambda i: (i, 0, 0, 0))] + specs,
        out_specs=pl.BlockSpec((1, 1, HEADS, HEAD_DIM), lambda i: (i, 0, 0, 0)),
        out_shape=jax.ShapeDtypeStruct((b, 1, HEADS, HEAD_DIM), BF16),
        compiler_params=_params("parallel"),
        name="window_decode",
    )(z4, *views)


SHIFT_ROWS = 64


def _shift_kernel(z_ref, c1_ref, c2_ref, c3_ref, o1_ref, o2_ref, o3_ref):
    for g, (c_ref, o_ref) in enumerate(((c1_ref, o1_ref), (c2_ref, o2_ref), (c3_ref, o3_ref))):
        n = c_ref.shape[2]
        full, rem = divmod(n - 1, SHIFT_ROWS)

        def move(j, carry, c_ref=c_ref, o_ref=o_ref):
            o_ref[0, 0, pl.ds(j * SHIFT_ROWS, SHIFT_ROWS)] = c_ref[0, 0, pl.ds(j * SHIFT_ROWS + 1, SHIFT_ROWS)]
            return carry

        lax.fori_loop(0, full, move, 0)
        if rem:
            o_ref[0, 0, pl.ds(full * SHIFT_ROWS, rem)] = c_ref[0, 0, pl.ds(full * SHIFT_ROWS + 1, rem)]
        o_ref[0, 0, n - 1, 0] = z_ref[0, 3 + g]
        o_ref[0, 0, n - 1, 1] = z_ref[0, 6 + g]


def shift_caches(z4, caches):
    b = z4.shape[0]
    specs = [pl.BlockSpec((1, 1) + c.shape[2:], lambda i: (0, i, 0, 0, 0, 0)) for c in caches]
    return pl.pallas_call(
        _shift_kernel,
        grid=(b,),
        in_specs=[pl.BlockSpec((1,) + z4.shape[1:], lambda i: (i, 0, 0, 0))] + specs,
        out_specs=specs,
        out_shape=[jax.ShapeDtypeStruct(c.shape, c.dtype) for c in caches],
        compiler_params=_params("parallel"),
        name="shift_caches",
    )(z4, *caches)


def _conv_step_kernel(a_ref, b_ref, s_ref, w_ref, cb_ref, lg_ref, lb_ref, c_ref, so_ref):
    hist = CONV_K - 1
    u = a_ref[...] * jax.nn.sigmoid(b_ref[...])
    y = (jnp.sum(s_ref[0] * w_ref[0:hist, :], axis=1, keepdims=True)
         + u * w_ref[hist:hist + 1, :] + cb_ref[...])
    c_ref[...] = _layernorm_silu(y, lg_ref[...], lb_ref[...]).astype(c_ref.dtype)
    so_ref[0, :, pl.ds(0, hist - 1), :] = s_ref[0, :, pl.ds(1, hist - 1), :]
    so_ref[0, :, pl.ds(hist - 1, 1), :] = u


def conv_step(a, b, state, conv_w, conv_b, ln_g, ln_b):
    bsz = a.shape[0]
    return pl.pallas_call(
        _conv_step_kernel,
        out_shape=[jax.ShapeDtypeStruct((bsz, 1, D_CONV), BF16), jax.ShapeDtypeStruct(state.shape, F32)],
        compiler_params=pltpu.CompilerParams(vmem_limit_bytes=VMEM_LIMIT),
        name="conv_step",
    )(a, b, state, conv_w, conv_b.reshape(1, D_CONV), ln_g.reshape(1, D_CONV), ln_b.reshape(1, D_CONV))


def _trunk_tail(x1, xo_in, w_xo, norm_ffn_g, w_rg, b_rg, w_re, b_re, w_ein, w_eout, norm_final_g, bm, routed):
    x2 = matmul_residual(xo_in, w_xo, x1, min(bm, 512), bn=D_MODEL)
    if routed:
        return moe_routed(x2, norm_ffn_g, w_rg, b_rg, w_re, b_re, w_ein, w_eout, norm_final_g)
    xn3, gate = moe_router(x2, norm_ffn_g, w_rg, b_rg, w_re, b_re, bm)
    return moe_experts_final(xn3, gate, w_ein, w_eout, x2, norm_final_g, bm)


def kernel(x_prompt, x_sample, mem_prompt, cache_kv_g1, cache_kv_g2, cache_kv_g3, state_conv, cache_mem_kv,
           norm_mix_g, w_in, conv_w, conv_b, conv_ln_g, conv_ln_b, w_proj_a, w_proj_b, w_out,
           norm_xattn_g, norm_mem_g, w_xq, w_xkv, w_xo, norm_ffn_g,
           w_router_group, b_router_group, w_router_expert, b_router_expert, w_expert_in, w_expert_out,
           norm_final_g):
    depth = norm_mix_g.shape[0]
    assert depth == 1, "single-layer trunk"
    batch, seq, _ = x_prompt.shape
    dec_b, dec_t, _ = x_sample.shape
    assert dec_t == 1
    (g_mix, w_in, conv_w, conv_b, ln_g, ln_b, w_pa, w_pb, w_o, g_x, g_mem, w_xq, w_xkv, w_xo, g_ffn,
     w_rg, b_rg, w_re, b_re, w_ein, w_eout) = [t[0] for t in (
         norm_mix_g, w_in, conv_w, conv_b, conv_ln_g, conv_ln_b, w_proj_a, w_proj_b, w_out, norm_xattn_g,
         norm_mem_g, w_xq, w_xkv, w_xo, norm_ffn_g, w_router_group, b_router_group, w_router_expert,
         b_router_expert, w_expert_in, w_expert_out)]
    w_pa16, w_pb16, w_o16 = w_pa.astype(BF16), w_pb.astype(BF16), w_o.astype(BF16)
    w_ein, w_eout = w_ein.astype(BF16), w_eout.astype(BF16)
    w_za, w_gt = w_in[:, :ZA_W].astype(BF16), w_in[:, ZA_W:].astype(BF16)
    w_xq, w_xkv, w_xo = w_xq.astype(BF16), w_xkv.astype(BF16), w_xo.astype(BF16)
    caches = (cache_kv_g1, cache_kv_g2, cache_kv_g3)

    m_p = batch * seq
    xp = x_prompt.reshape(m_p, D_MODEL)
    za, gates = in_proj(xp, g_mix, w_za, w_gt, bm=1024)
    o_a = band_attention(za, batch, seq)
    x1, conv_prompt = conv_merge_branches(o_a, za, gates, xp, batch, seq, conv_w, conv_b, ln_g, ln_b,
                                          w_pa16, w_pb16, w_o16)
    mkv = norm_matmul(mem_prompt.reshape(batch * N_MEM, D_MODEL), g_mem, w_xkv, F32, bm=1024)
    q = norm_matmul(x1, g_x, w_xq, BF16, bm=512, bn=D_MODEL)
    xo_in = cross_attention_prompt(q, mkv, batch, seq)
    y_prompt = _trunk_tail(x1, xo_in, w_xo, g_ffn, w_rg, b_rg, w_re, b_re, w_ein, w_eout, norm_final_g,
                           bm=1024, routed=True)

    kv_prompt = window_tails(za, batch, seq)
    conv_prompt = conv_prompt[None]
    mem_kv_prompt = mkv.reshape(1, batch, N_MEM, 2, X_HEADS, X_HEAD_DIM)

    xs = x_sample.reshape(dec_b, D_MODEL)
    zs, gates_s = in_proj(xs, g_mix, w_za, w_gt, bm=dec_b)
    z4 = zs.reshape(dec_b, ZA_W // HEAD_DIM // HEADS, HEADS, HEAD_DIM)
    o_as = window_decode(z4, caches).reshape(dec_b, GROUP_W)
    kv_sample = shift_caches(z4, caches)
    a_s = zs[:, 3 * ATT_WIDTH:3 * ATT_WIDTH + D_CONV].reshape(dec_b, 1, D_CONV)
    b_s = zs[:, 3 * ATT_WIDTH + D_CONV:].reshape(dec_b, 1, D_CONV)
    c_s, conv_sample = conv_step(a_s, b_s, state_conv, conv_w, conv_b, ln_g, ln_b)
    x1s = merge_branches(o_as, c_s.reshape(dec_b, D_CONV), gates_s, xs, w_pa16, w_pb16, w_o16, tm=dec_b)
    q_s = norm_matmul(x1s, g_x, w_xq, BF16, bm=dec_b)
    xo_s = cross_attention_decode(q_s, cache_mem_kv)
    y_sample = _trunk_tail(x1s, xo_s, w_xo, g_ffn, w_rg, b_rg, w_re, b_re, w_ein, w_eout, norm_final_g,
                           bm=dec_b, routed=False)

    return (y_prompt.reshape(batch, seq, D_MODEL), y_sample.reshape(dec_b, 1, D_MODEL),
            kv_prompt[0], kv_prompt[1], kv_prompt[2], conv_prompt, mem_kv_prompt,
            kv_sample[0], kv_sample[1], kv_sample[2], conv_sample)
```

```python
import functools

import jax
import jax.numpy as jnp
from jax import lax
from jax.experimental import pallas as pl
from jax.experimental.pallas import tpu as pltpu

F32 = jnp.float32
BF16 = jnp.bfloat16

D_MODEL = 2048
ATT_GROUPS = ((128, 1), (512, 4), (2048, 16))
HEADS = 4
HEAD_DIM = 128
GROUP_W = HEADS * HEAD_DIM
ATT_WIDTH = len(ATT_GROUPS) * GROUP_W
D_CONV = 1536
CONV_K = 31
N_MEM = 256
X_HEADS = 4
X_HEAD_DIM = D_MODEL // X_HEADS
N_EXPERT_GROUPS = 4
EXPERTS_PER_GROUP = 4
N_EXPERTS = 16
D_EXPERT = 512
ZA_W = 3 * ATT_WIDTH + 2 * D_CONV
GATE_W = 2 * D_MODEL
RMS_EPS = 1e-6
LN_EPS = 1e-5
NEG_INF = -1e30
BAND = 128
LANES = 128
ROUTER_GROUP_LANE = N_EXPERTS

V7X_VMEM_BYTES = 64 * 1024 * 1024
VMEM_LIMIT = V7X_VMEM_BYTES * 7 // 8


def _params(*sem):
    return pltpu.CompilerParams(dimension_semantics=sem, vmem_limit_bytes=VMEM_LIMIT)


def _rms_rows(x, g):
    ms = jnp.mean(x * x, axis=-1, keepdims=True)
    return x * lax.rsqrt(ms + RMS_EPS) * g


def _store_normed(x_ref, g_ref, xn_ref, chunk=256):
    rows = x_ref.shape[0]
    step = min(chunk, rows)
    for r0 in range(0, rows, step):
        xn_ref[r0:r0 + step, :] = _rms_rows(x_ref[r0:r0 + step, :], g_ref[...]).astype(xn_ref.dtype)


def _norm_matmul_kernel(x_ref, g_ref, w_ref, o_ref, xn_ref):
    @pl.when(pl.program_id(1) == 0)
    def _():
        _store_normed(x_ref, g_ref, xn_ref)

    o_ref[...] = jnp.dot(xn_ref[...], w_ref[...], preferred_element_type=F32).astype(o_ref.dtype)


def norm_matmul(x, g, w, out_dtype, bm, bn=512):
    m, k = x.shape
    n = w.shape[1]
    bm = min(bm, m)
    return pl.pallas_call(
        _norm_matmul_kernel,
        grid=(m // bm, n // bn),
        in_specs=[pl.BlockSpec((bm, k), lambda i, j: (i, 0)),
                  pl.BlockSpec((1, k), lambda i, j: (0, 0)),
                  pl.BlockSpec((k, bn), lambda i, j: (0, j))],
        out_specs=pl.BlockSpec((bm, bn), lambda i, j: (i, j)),
        out_shape=jax.ShapeDtypeStruct((m, n), out_dtype),
        scratch_shapes=[pltpu.VMEM((bm, k), BF16)],
        compiler_params=_params("parallel", "arbitrary"),
        name="norm_matmul",
    )(x, g.reshape(1, k), w)


def _norm_matmul_keep_kernel(x_ref, g_ref, w_ref, o_ref, xn_ref):
    @pl.when(pl.program_id(1) == 0)
    def _():
        _store_normed(x_ref, g_ref, xn_ref)

    o_ref[...] = jnp.dot(xn_ref[...], w_ref[...], preferred_element_type=F32)


def _matmul_kernel(x_ref, w_ref, o_ref):
    o_ref[...] = jnp.dot(x_ref[...], w_ref[...], preferred_element_type=F32)


ZA_BN = 1536


def in_proj(x, g, w_za, w_gt, bm):
    m, k = x.shape
    bm = min(bm, m)
    za, xn = pl.pallas_call(
        _norm_matmul_keep_kernel,
        grid=(m // bm, ZA_W // ZA_BN),
        in_specs=[pl.BlockSpec((bm, k), lambda i, j: (i, 0)),
                  pl.BlockSpec((1, k), lambda i, j: (0, 0)),
                  pl.BlockSpec((k, ZA_BN), lambda i, j: (0, j))],
        out_specs=[pl.BlockSpec((bm, ZA_BN), lambda i, j: (i, j)),
                   pl.BlockSpec((bm, k), lambda i, j: (i, 0))],
        out_shape=[jax.ShapeDtypeStruct((m, ZA_W), F32), jax.ShapeDtypeStruct((m, k), BF16)],
        compiler_params=_params("parallel", "arbitrary"),
        name="in_proj_za",
    )(x, g.reshape(1, k), w_za)
    bn = GATE_W // 2
    gates = pl.pallas_call(
        _matmul_kernel,
        grid=(m // bm, GATE_W // bn),
        in_specs=[pl.BlockSpec((bm, k), lambda i, j: (i, 0)),
                  pl.BlockSpec((k, bn), lambda i, j: (0, j))],
        out_specs=pl.BlockSpec((bm, bn), lambda i, j: (i, j)),
        out_shape=jax.ShapeDtypeStruct((m, GATE_W), F32),
        compiler_params=_params("parallel", "arbitrary"),
        name="in_proj_gates",
    )(xn, w_gt)
    return za, gates


def _matmul_res_kernel(x_ref, w_ref, r_ref, o_ref):
    o_ref[...] = r_ref[...] + jnp.dot(x_ref[...], w_ref[...], preferred_element_type=F32)


def matmul_residual(x, w, res, bm, bn=512):
    m, k = x.shape
    n = w.shape[1]
    bm = min(bm, m)
    return pl.pallas_call(
        _matmul_res_kernel,
        grid=(m // bm, n // bn),
        in_specs=[pl.BlockSpec((bm, k), lambda i, j: (i, 0)),
                  pl.BlockSpec((k, bn), lambda i, j: (0, j)),
                  pl.BlockSpec((bm, bn), lambda i, j: (i, j))],
        out_specs=pl.BlockSpec((bm, bn), lambda i, j: (i, j)),
        out_shape=jax.ShapeDtypeStruct((m, n), F32),
        compiler_params=_params("parallel", "arbitrary"),
        name="matmul_residual",
    )(x, w, res)


ATT_R = 2048
ATT_UNROLL = 16


def _band_attn_kernel(*refs):
    n_g = len(ATT_GROUPS)
    in_refs = refs[:5 * n_g]
    o_ref = refs[5 * n_g]
    scr = refs[5 * n_g + 1:]
    kbufs, vbufs = scr[:n_g], scr[n_g:2 * n_g]
    obufs, mbufs, dbufs = scr[2 * n_g:3 * n_g], scr[3 * n_g:4 * n_g], scr[4 * n_g:5 * n_g]
    first_chunk = pl.program_id(1) == 0
    row = lax.broadcasted_iota(jnp.int32, (BAND, 2 * BAND), 0)
    col = lax.broadcasted_iota(jnp.int32, (BAND, 2 * BAND), 1)
    in_band = (col >= row) & (col <= row + BAND)
    in_cur = col >= BAND
    scale = HEAD_DIM ** -0.5
    nt = (((1,), (1,)), ((), ()))

    for g, (win, dil) in enumerate(ATT_GROUPS):
        q_ref, kc_ref, vc_ref, kp_ref, vp_ref = in_refs[5 * g:5 * g + 5]
        kbuf, vbuf, obuf, mbuf, dbuf = kbufs[g], vbufs[g], obufs[g], mbufs[g], dbufs[g]
        kbuf[0:win, :] = kp_ref[...]
        kbuf[win:win + ATT_R, :] = kc_ref[...]
        vbuf[0:win, :] = vp_ref[...]
        vbuf[win:win + ATT_R, :] = vc_ref[...]
        shift = dil.bit_length() - 1

        def sub_blocks(i4, carry, q_ref=q_ref, kbuf=kbuf, vbuf=vbuf, obuf=obuf, mbuf=mbuf, dbuf=dbuf,
                       win=win, dil=dil, shift=shift):
            done = []
            for u in range(ATT_UNROLL):
                i = i4 * ATT_UNROLL + u
                span = lax.shift_right_logical(i, shift)
                base = span * win + (i & (dil - 1))
                if dil == 1:
                    q_rows, kv_rows = pl.ds(base, BAND), pl.ds(base, 2 * BAND)
                else:
                    q_rows, kv_rows = pl.ds(base, BAND, stride=dil), pl.ds(base, 2 * BAND, stride=dil)
                q = q_ref[q_rows, :].astype(BF16)
                k = kbuf[kv_rows, :].astype(BF16)
                v = vbuf[kv_rows, :].astype(BF16)
                s = lax.dot_general(q, k, nt, preferred_element_type=F32) * scale
                has_prev = jnp.logical_not(first_chunk & (span == 0))
                s = jnp.where(in_band & (in_cur | has_prev), s, NEG_INF)
                m = jnp.max(s, axis=-1, keepdims=True)
                p = jnp.exp(s - m)
                den = jnp.sum(p, axis=-1, keepdims=True)
                done.append((q_rows, jnp.dot(p.astype(BF16), v, preferred_element_type=F32), m, den))
            for q_rows, o, m, den in done:
                obuf[q_rows, :] = o
                mbuf[q_rows, :] = jnp.broadcast_to(m, (BAND, HEAD_DIM))
                dbuf[q_rows, :] = jnp.broadcast_to(den, (BAND, HEAD_DIM))
            return carry

        lax.fori_loop(0, ATT_R // BAND // ATT_UNROLL, sub_blocks, 0)

    chunk = 256
    for r0 in range(0, ATT_R, chunk):
        rs = slice(r0, r0 + chunk)
        ms = [mb[rs, :] for mb in mbufs]
        m_all = functools.reduce(jnp.maximum, ms)
        ws = [jnp.exp(m - m_all) for m in ms]
        num = sum(w * ob[rs, :] for w, ob in zip(ws, obufs))
        den = sum(w * db[rs, :] for w, db in zip(ws, dbufs))
        o_ref[rs, :] = (num / den).astype(o_ref.dtype)


def band_attention(za, batch, seq):
    assert seq % ATT_R == 0
    nch = seq // ATT_R
    in_specs, scratch = [], []
    for g, (win, dil) in enumerate(ATT_GROUPS):
        assert win // dil == BAND and ATT_R % win == 0
        cols = [(part * len(ATT_GROUPS) + g) * HEADS for part in range(3)]
        cur = lambda c: pl.BlockSpec((ATT_R, HEAD_DIM), lambda b, ch, h, c=c: (b * nch + ch, c + h))
        prev = lambda c, win=win: pl.BlockSpec(
            (win, HEAD_DIM), lambda b, ch, h, c=c, win=win: (jnp.maximum((b * seq + ch * ATT_R) // win - 1, 0), c + h))
        in_specs += [cur(cols[0]), cur(cols[1]), cur(cols[2]), prev(cols[1]), prev(cols[2])]
    for _ in range(2):
        scratch += [pltpu.VMEM((win + ATT_R, HEAD_DIM), F32) for win, _ in ATT_GROUPS]
    scratch += [pltpu.VMEM((ATT_R, HEAD_DIM), F32)] * (3 * len(ATT_GROUPS))
    return pl.pallas_call(
        _band_attn_kernel,
        grid=(batch, nch, HEADS),
        in_specs=in_specs,
        out_specs=pl.BlockSpec((ATT_R, HEAD_DIM), lambda b, ch, h: (b * nch + ch, h)),
        out_shape=jax.ShapeDtypeStruct((batch * seq, GROUP_W), BF16),
        scratch_shapes=scratch,
        compiler_params=_params("parallel", "parallel", "parallel"),
        name="band_attention",
    )(*([za] * (5 * len(ATT_GROUPS))))


def _window_tails_kernel(*refs):
    n_g = len(ATT_GROUPS)
    slabs = 2 * HEADS
    for g in range(n_g):
        k_ref, v_ref, o_ref = refs[2 * g], refs[2 * g + 1], refs[2 * n_g + g]
        rows = k_ref.shape[0]
        for part, src in enumerate((k_ref, v_ref)):
            for h in range(HEADS):
                o_ref[pl.ds(part * HEADS + h, rows, stride=slabs), :] = src[:, h * HEAD_DIM:(h + 1) * HEAD_DIM]


def window_tails(za, batch, seq):
    in_specs, out_specs, out_shape = [], [], []
    slabs = 2 * HEADS
    for g, (win, _) in enumerate(ATT_GROUPS):
        assert win <= seq and seq % win == 0
        last = seq // win - 1
        for part in (1, 2):
            in_specs.append(pl.BlockSpec(
                (win, GROUP_W), lambda b, g=g, part=part, win=win, last=last:
                (b * (seq // win) + last, part * len(ATT_GROUPS) + g)))
        out_specs.append(pl.BlockSpec((win * slabs, HEAD_DIM), lambda b: (b, 0)))
        out_shape.append(jax.ShapeDtypeStruct((batch * win * slabs, HEAD_DIM), F32))
    outs = pl.pallas_call(
        _window_tails_kernel,
        grid=(batch,),
        in_specs=in_specs,
        out_specs=out_specs,
        out_shape=out_shape,
        compiler_params=_params("parallel"),
        name="window_tails",
    )(*([za] * (2 * len(ATT_GROUPS))))
    return [o.reshape(1, batch, win, 2, HEADS, HEAD_DIM) for o, (win, _) in zip(outs, ATT_GROUPS)]


CONV_T = 256
CONV_HIST = 32
CONV_RC = 128
CONV_PHASES = 4
CONV_SLABS = D_CONV // LANES


def _layernorm_silu(y, g, b):
    mu = jnp.mean(y, axis=-1, keepdims=True)
    yc = y - mu
    var = jnp.mean(yc * yc, axis=-1, keepdims=True)
    yn = yc * lax.rsqrt(var + LN_EPS) * g + b
    return yn * jax.nn.sigmoid(yn)


def _conv_history(t, ubuf):
    @pl.when(t == 0)
    def _():
        ubuf[:, 0:CONV_HIST, :] = jnp.zeros((CONV_SLABS, CONV_HIST, LANES), F32)

    @pl.when(t > 0)
    def _():
        ubuf[:, 0:CONV_HIST, :] = ubuf[:, CONV_T:CONV_T + CONV_HIST, :]


def _conv_slab(j, a_ref, b_ref, w_ref, cb_ref, tail_ref, ubuf, ybuf):
    first = CONV_HIST - (CONV_K - 1)
    n = CONV_RC // CONV_PHASES
    js = slice(j * LANES, (j + 1) * LANES)
    u = a_ref[:, js] * jax.nn.sigmoid(b_ref[:, js])
    ubuf[j, CONV_HIST:CONV_HIST + CONV_T, :] = u
    tail_ref[0, :, js] = u[CONV_T - CONV_HIST:, :]
    for r0 in range(0, CONV_T, CONV_RC):
        accs = [jnp.broadcast_to(cb_ref[:, js], (n, LANES))] * CONV_PHASES
        for s in range(first, first + CONV_K + CONV_PHASES - 1):
            rows = ubuf[j, pl.ds(r0 + s, n, stride=CONV_PHASES), :]
            for p in range(CONV_PHASES):
                k = s - first - p
                if 0 <= k < CONV_K:
                    accs[p] = accs[p] + w_ref[k:k + 1, js] * rows
        for p in range(CONV_PHASES):
            ybuf[j, pl.ds(r0 + p, n, stride=CONV_PHASES), :] = accs[p]
    return jnp.sum(ybuf[j], axis=-1, keepdims=True)


def _conv_centered_sq(j, mu, ybuf):
    yc = ybuf[j] - mu
    return jnp.sum(yc * yc, axis=-1, keepdims=True)


def _conv_norm_slab(j, mu, inv, lg_ref, lb_ref, c_ref, ybuf):
    js = slice(j * LANES, (j + 1) * LANES)
    yn = (ybuf[j] - mu) * inv * lg_ref[:, js] + lb_ref[:, js]
    c_ref[:, js] = (yn * jax.nn.sigmoid(yn)).astype(c_ref.dtype)


def _merge_kernel(oa_ref, c_ref, ga_ref, gb_ref, x_ref, wpa_ref, wpb_ref, wo_ref, y_ref):
    ta = jnp.dot(oa_ref[...], wpa_ref[...], preferred_element_type=F32)
    tb = jnp.dot(c_ref[...], wpb_ref[...], preferred_element_type=F32)
    hmix = jax.nn.sigmoid(ga_ref[...]) * ta + jax.nn.sigmoid(gb_ref[...]) * tb
    y_ref[...] = x_ref[...] + jnp.dot(hmix.astype(BF16), wo_ref[...], preferred_element_type=F32)


def _conv_merge_kernel(oa_ref, a_ref, b_ref, ga_ref, gb_ref, x_ref, cw_ref, cb_ref, lg_ref, lb_ref,
                       wpa_ref, wpb_ref, wo_ref, y_ref, tail_ref, ubuf, ybuf, c_scr):
    _conv_history(pl.program_id(1), ubuf)
    total = jnp.zeros((CONV_T, 1), F32)
    for j in range(CONV_SLABS):
        total = total + _conv_slab(j, a_ref, b_ref, cw_ref, cb_ref, tail_ref, ubuf, ybuf)
    mu = total * (1.0 / D_CONV)
    sq = jnp.zeros((CONV_T, 1), F32)
    for j in range(CONV_SLABS):
        sq = sq + _conv_centered_sq(j, mu, ybuf)
    inv = lax.rsqrt(sq * (1.0 / D_CONV) + LN_EPS)
    for j in range(CONV_SLABS):
        _conv_norm_slab(j, mu, inv, lg_ref, lb_ref, c_scr, ybuf)
    _merge_kernel(oa_ref, c_scr, ga_ref, gb_ref, x_ref, wpa_ref, wpb_ref, wo_ref, y_ref)


def conv_merge_branches(o_a, za, gates, x, batch, seq, conv_w, conv_b, ln_g, ln_b, w_pa, w_pb, w_out):
    nt = seq // CONV_T
    row = lambda b, t: (b * nt + t, 0)
    const = lambda b, t: (0, 0)
    once = dict(pipeline_mode=pl.Buffered(1))
    glu = 3 * ATT_WIDTH // D_CONV
    y, tail = pl.pallas_call(
        _conv_merge_kernel,
        grid=(batch, nt),
        in_specs=[pl.BlockSpec((CONV_T, GROUP_W), row),
                  pl.BlockSpec((CONV_T, D_CONV), lambda b, t: (b * nt + t, glu)),
                  pl.BlockSpec((CONV_T, D_CONV), lambda b, t: (b * nt + t, glu + 1)),
                  pl.BlockSpec((CONV_T, D_MODEL), lambda b, t: (b * nt + t, 0)),
                  pl.BlockSpec((CONV_T, D_MODEL), lambda b, t: (b * nt + t, 1)),
                  pl.BlockSpec((CONV_T, D_MODEL), row),
                  pl.BlockSpec((CONV_K, D_CONV), const),
                  pl.BlockSpec((1, D_CONV), const),
                  pl.BlockSpec((1, D_CONV), const),
                  pl.BlockSpec((1, D_CONV), const),
                  pl.BlockSpec((GROUP_W, D_MODEL), const, **once),
                  pl.BlockSpec((D_CONV, D_MODEL), const, **once),
                  pl.BlockSpec((D_MODEL, D_MODEL), const, **once)],
        out_specs=[pl.BlockSpec((CONV_T, D_MODEL), row),
                   pl.BlockSpec((1, CONV_HIST, D_CONV), lambda b, t: (b, 0, 0))],
        out_shape=[jax.ShapeDtypeStruct((batch * seq, D_MODEL), F32),
                   jax.ShapeDtypeStruct((batch, CONV_HIST, D_CONV), F32)],
        scratch_shapes=[pltpu.VMEM((CONV_SLABS, CONV_HIST + CONV_T, LANES), F32),
                        pltpu.VMEM((CONV_SLABS, CONV_T, LANES), F32),
                        pltpu.VMEM((CONV_T, D_CONV), BF16)],
        compiler_params=_params("parallel", "arbitrary"),
        name="conv_merge_branches",
    )(o_a, za, za, gates, gates, x, conv_w, conv_b.reshape(1, D_CONV), ln_g.reshape(1, D_CONV),
      ln_b.reshape(1, D_CONV), w_pa, w_pb, w_out)
    return y, tail[:, CONV_HIST - (CONV_K - 1):]


def merge_branches(o_a, c, gates, x, w_pa, w_pb, w_out, tm):
    m = x.shape[0]
    tm = min(tm, m)
    row = lambda i: (i, 0)
    const = lambda i: (0, 0)
    once = dict(pipeline_mode=pl.Buffered(1))
    return pl.pallas_call(
        _merge_kernel,
        grid=(m // tm,),
        in_specs=[
            pl.BlockSpec((tm, GROUP_W), row),
            pl.BlockSpec((tm, D_CONV), row),
            pl.BlockSpec((tm, D_MODEL), lambda i: (i, 0)),
            pl.BlockSpec((tm, D_MODEL), lambda i: (i, 1)),
            pl.BlockSpec((tm, D_MODEL), row),
            pl.BlockSpec((GROUP_W, D_MODEL), const, **once),
            pl.BlockSpec((D_CONV, D_MODEL), const, **once),
            pl.BlockSpec((D_MODEL, D_MODEL), const, **once)],
        out_specs=pl.BlockSpec((tm, D_MODEL), row),
        out_shape=jax.ShapeDtypeStruct((m, D_MODEL), F32),
        compiler_params=_params("parallel"),
        name="merge_branches",
    )(o_a, c, gates, gates, x, w_pa, w_pb, w_out)


def _xattn_kernel(q_ref, kv_ref, o_ref):
    scale = X_HEAD_DIM ** -0.5
    nt = (((1,), (1,)), ((), ()))
    for h in range(X_HEADS):
        hs = slice(h * X_HEAD_DIM, (h + 1) * X_HEAD_DIM)
        vs = slice(D_MODEL + h * X_HEAD_DIM, D_MODEL + (h + 1) * X_HEAD_DIM)
        s = lax.dot_general(q_ref[:, hs], kv_ref[:, hs].astype(BF16), nt, preferred_element_type=F32) * scale
        m = jnp.max(s, axis=-1, keepdims=True)
        p = jnp.exp(s - m)
        p = p / jnp.sum(p, axis=-1, keepdims=True)
        o = jnp.dot(p.astype(BF16), kv_ref[:, vs].astype(BF16), preferred_element_type=F32)
        o_ref[:, hs] = o.astype(o_ref.dtype)


def cross_attention_prompt(q, mkv, batch, seq, tm=512):
    nt = seq // tm
    return pl.pallas_call(
        _xattn_kernel,
        grid=(batch, nt),
        in_specs=[pl.BlockSpec((tm, D_MODEL), lambda b, t: (b * nt + t, 0)),
                  pl.BlockSpec((N_MEM, 2 * D_MODEL), lambda b, t: (b, 0))],
        out_specs=pl.BlockSpec((tm, D_MODEL), lambda b, t: (b * nt + t, 0)),
        out_shape=jax.ShapeDtypeStruct((batch * seq, D_MODEL), BF16),
        compiler_params=_params("parallel", "arbitrary"),
        name="cross_attention_prompt",
    )(q, mkv)


def _xattn_decode_kernel(q_ref, kv_ref, o_ref):
    scale = X_HEAD_DIM ** -0.5
    q = q_ref[0].astype(F32)
    k = kv_ref[0, 0, :, 0]
    v = kv_ref[0, 0, :, 1]
    s = jnp.sum(k * q, axis=-1, keepdims=True) * scale
    m = jnp.max(s, axis=0, keepdims=True)
    p = jnp.exp(s - m)
    p = p / jnp.sum(p, axis=0, keepdims=True)
    o_ref[0] = jnp.sum(p * v, axis=0, keepdims=True).astype(o_ref.dtype)


def cross_attention_decode(q, cache_mem_kv):
    b = q.shape[0]
    q4 = q.reshape(b, 1, X_HEADS, X_HEAD_DIM)
    o = pl.pallas_call(
        _xattn_decode_kernel,
        grid=(b,),
        in_specs=[pl.BlockSpec((1, 1, X_HEADS, X_HEAD_DIM), lambda i: (i, 0, 0, 0)),
                  pl.BlockSpec((1, 1, N_MEM, 2, X_HEADS, X_HEAD_DIM), lambda i: (0, i, 0, 0, 0, 0))],
        out_specs=pl.BlockSpec((1, 1, X_HEADS, X_HEAD_DIM), lambda i: (i, 0, 0, 0)),
        out_shape=jax.ShapeDtypeStruct((b, 1, X_HEADS, X_HEAD_DIM), BF16),
        compiler_params=_params("parallel"),
        name="cross_attention_decode",
    )(q4, cache_mem_kv)
    return o.reshape(b, D_MODEL)


_PAIRS = [(lo, hi) for lo in range(EXPERTS_PER_GROUP) for hi in range(lo + 1, EXPERTS_PER_GROUP)]
PAIRS_PER_GROUP = len(_PAIRS)
N_BUCKETS = N_EXPERT_GROUPS * PAIRS_PER_GROUP
PAIR_LO = tuple(lo for lo, _ in _PAIRS)
PAIR_HI = tuple(hi for _, hi in _PAIRS)
ROW_TILES = D_MODEL // LANES
PAYLOAD_ROWS = ROW_TILES + 8
MOE_TM = 256


def _route(logits):
    lane = lax.broadcasted_iota(jnp.int32, logits.shape, 1)
    lanef = lane.astype(F32)
    big = float(LANES)
    is_group = (lane >= ROUTER_GROUP_LANE) & (lane < ROUTER_GROUP_LANE + N_EXPERT_GROUPS)
    lg = jnp.where(is_group, logits, -jnp.inf)
    mg = jnp.max(lg, axis=-1, keepdims=True)
    p_sel = 1.0 / jnp.sum(jnp.exp(lg - mg), axis=-1, keepdims=True)
    gsel = jnp.min(jnp.where(lg == mg, lanef, big), axis=-1, keepdims=True) - ROUTER_GROUP_LANE
    group_of_lane = lax.shift_right_logical(lane, EXPERTS_PER_GROUP.bit_length() - 1)
    in_group = (lane < N_EXPERTS) & (group_of_lane == gsel.astype(jnp.int32))
    le = jnp.where(in_group, logits, -jnp.inf)
    v1 = jnp.max(le, axis=-1, keepdims=True)
    i1 = jnp.min(jnp.where(le == v1, lanef, big), axis=-1, keepdims=True)
    le2 = jnp.where(lanef == i1, -jnp.inf, le)
    v2 = jnp.max(le2, axis=-1, keepdims=True)
    i2 = jnp.min(jnp.where(le2 == v2, lanef, big), axis=-1, keepdims=True)
    t = jnp.exp(v2 - v1)
    tot = 1.0 + t
    return gsel, i1, i2, (1.0 / tot) * p_sel, (t / tot) * p_sel


def _router_routed_kernel(x_ref, g_ref, w_ref, b_ref, xd_ref, meta_ref, cnt_ref, carry_ref):
    tm = x_ref.shape[0]

    @pl.when(pl.program_id(0) == 0)
    def _():
        carry_ref[...] = jnp.zeros_like(carry_ref)

    xn = _rms_rows(x_ref[...], g_ref[...])
    logits = _router_logits(xn, w_ref, b_ref)
    gsel, i1, i2, p1, p2 = _route(logits)
    lo = jnp.minimum(i1, i2) - EXPERTS_PER_GROUP * gsel
    hi = jnp.maximum(i1, i2) - EXPERTS_PER_GROUP * gsel
    pair = lo * (2.0 * EXPERTS_PER_GROUP - 1.0 - lo) * 0.5 + (hi - lo - 1.0)
    bucket = gsel * float(PAIRS_PER_GROUP) + pair
    gate_lo = jnp.where(i1 < i2, p1, p2)
    gate_hi = jnp.where(i1 < i2, p2, p1)

    lane = lax.broadcasted_iota(jnp.int32, (tm, LANES), 1)
    onehot = (lane.astype(F32) == bucket).astype(F32)
    r_i = lax.broadcasted_iota(jnp.int32, (tm, tm), 0)
    c_i = lax.broadcasted_iota(jnp.int32, (tm, tm), 1)
    before = (c_i < r_i).astype(BF16)
    rank_local = jnp.dot(before, onehot.astype(BF16), preferred_element_type=F32)
    rank = jnp.sum(onehot * (rank_local + carry_ref[0:1, :]), axis=-1, keepdims=True)
    carry_ref[0:1, :] = carry_ref[0:1, :] + jnp.sum(onehot, axis=0, keepdims=True)
    cnt_ref[...] = jnp.broadcast_to(carry_ref[0:1, :], cnt_ref.shape)
    meta_ref[...] = jnp.where(lane == 0, bucket, jnp.where(lane == 1, rank, 0.0))

    for j in range(ROW_TILES):
        xd_ref[pl.ds(j, tm, stride=PAYLOAD_ROWS), :] = xn[:, j * LANES:(j + 1) * LANES]
    xd_ref[pl.ds(ROW_TILES, tm, stride=PAYLOAD_ROWS), :] = jnp.where(
        lane == 0, gate_lo, jnp.where(lane == 1, gate_hi, 0.0))
    for j in range(ROW_TILES + 1, PAYLOAD_ROWS):
        xd_ref[pl.ds(j, tm, stride=PAYLOAD_ROWS), :] = jnp.zeros((tm, LANES), F32)


def _router_weights(w_rg, b_rg, w_re, b_re):
    pad = LANES - N_EXPERTS - N_EXPERT_GROUPS
    w = jnp.concatenate([w_re, w_rg, jnp.zeros((D_MODEL, pad), F32)], axis=1)
    b = jnp.concatenate([b_re, b_rg, jnp.zeros((pad,), F32)]).reshape(1, LANES)
    hi = w.astype(BF16)
    lo = (w - hi.astype(F32)).astype(BF16)
    return jnp.stack([hi, lo]), b


def _router_logits(xn, w_ref, b_ref):
    hi = xn.astype(BF16)
    lo = (xn - hi.astype(F32)).astype(BF16)
    acc = jnp.dot(hi, w_ref[0], preferred_element_type=F32)
    acc = acc + jnp.dot(lo, w_ref[0], preferred_element_type=F32)
    acc = acc + jnp.dot(hi, w_ref[1], preferred_element_type=F32)
    return acc + b_ref[...]


def moe_router_routed(x, g, w_rg, b_rg, w_re, b_re, tm=512):
    m = x.shape[0]
    w, b = _router_weights(w_rg, b_rg, w_re, b_re)
    row = lambda i: (i, 0)
    const = lambda i: (0, 0)
    return pl.pallas_call(
        _router_routed_kernel,
        grid=(m // tm,),
        in_specs=[pl.BlockSpec((tm, D_MODEL), row), pl.BlockSpec((1, D_MODEL), const),
                  pl.BlockSpec((2, D_MODEL, LANES), lambda i: (0, 0, 0)), pl.BlockSpec((1, LANES), const)],
        out_specs=[pl.BlockSpec((tm * PAYLOAD_ROWS, LANES), row), pl.BlockSpec((tm, LANES), row),
                   pl.BlockSpec((8, LANES), const)],
        out_shape=[jax.ShapeDtypeStruct((m * PAYLOAD_ROWS, LANES), F32), jax.ShapeDtypeStruct((m, LANES), F32),
                   jax.ShapeDtypeStruct((8, LANES), F32)],
        scratch_shapes=[pltpu.VMEM((8, LANES), F32)],
        compiler_params=_params("arbitrary"),
        name="moe_router_routed",
    )(x, g.reshape(1, D_MODEL), w, b)


INV_CHUNK = 2048


def _slot_maps_kernel(bucket_ref, rank_ref, first_ref, dest_ref, inv_ref):
    step = pl.program_id(0)

    @pl.when(step == 0)
    def _():
        def clear(k, c):
            inv_ref[k] = 0
            return c
        lax.fori_loop(0, inv_ref.shape[0], clear, 0, unroll=8)

    def put(t, c):
        slot = first_ref[bucket_ref[0, 0, t]] + rank_ref[0, 0, t]
        dest_ref[0, 0, t] = slot
        inv_ref[slot] = step * INV_CHUNK + t
        return c
    lax.fori_loop(0, INV_CHUNK, put, 0, unroll=8)


def slot_maps(bucket, rank, first_slot, n_slots):
    m = bucket.shape[0]
    nc = m // INV_CHUNK
    smem = dict(memory_space=pltpu.SMEM)
    chunk = pl.BlockSpec((1, 1, INV_CHUNK), lambda i: (i, 0, 0), **smem)
    dest, inv = pl.pallas_call(
        _slot_maps_kernel,
        grid=(nc,),
        in_specs=[chunk, chunk, pl.BlockSpec(**smem)],
        out_specs=[chunk, pl.BlockSpec(**smem)],
        out_shape=[jax.ShapeDtypeStruct((nc, 1, INV_CHUNK), jnp.int32),
                   jax.ShapeDtypeStruct((n_slots,), jnp.int32)],
        compiler_params=_params("arbitrary"),
        name="slot_maps",
    )(bucket.reshape(nc, 1, INV_CHUNK), rank.reshape(nc, 1, INV_CHUNK), first_slot)
    return dest.reshape(m), inv


def _start_row_gather(idx_ref, src_hbm, dst, sem, n_items, rows):
    def start(r2, c):
        for queue in range(2):
            r = 2 * r2 + queue
            first = pl.multiple_of(idx_ref[0, 0, r] * rows, 8)
            pltpu.make_async_copy(src_hbm.at[pl.ds(first, rows)], dst.at[pl.ds(r * rows, rows)],
                                  sem).start(priority=queue)
        return c
    lax.fori_loop(0, n_items // 2, start, 0, unroll=4)


def _wait_row_gather(src_hbm, dst, sem, n_items, rows):
    pltpu.make_async_copy(src_hbm.at[pl.ds(0, n_items * rows)], dst, sem).wait()


GATHER_AHEAD = 2


def _moe_routed_kernel(ea_ref, eb_ref, nv_ref, *refs):
    del ea_ref, eb_ref
    inv_refs = refs[:GATHER_AHEAD + 1]
    xd_hbm, w1a_ref, w1b_ref, w2a_ref, w2b_ref, ys_ref, xbuf, sems, x_scr = refs[GATHER_AHEAD + 1:]
    n_buf = GATHER_AHEAD + 1
    i = pl.program_id(0)
    n_valid = nv_ref[0]
    slot = i % n_buf
    tm = MOE_TM

    @pl.when(i == 0)
    def _():
        for a in range(GATHER_AHEAD):
            @pl.when(a < n_valid)
            def _():
                _start_row_gather(inv_refs[a], xd_hbm, xbuf.at[a], sems.at[a], tm, PAYLOAD_ROWS)

    @pl.when(i + GATHER_AHEAD < n_valid)
    def _():
        ahead = (i + GATHER_AHEAD) % n_buf
        _start_row_gather(inv_refs[GATHER_AHEAD], xd_hbm, xbuf.at[ahead], sems.at[ahead], tm, PAYLOAD_ROWS)

    @pl.when(i < n_valid)
    def _():
        buf = xbuf.at[slot]
        _wait_row_gather(xd_hbm, buf, sems.at[slot], tm, PAYLOAD_ROWS)
        for j in range(ROW_TILES):
            x_scr[:, j * LANES:(j + 1) * LANES] = buf[pl.ds(j, tm, stride=PAYLOAD_ROWS), :].astype(BF16)
        gates = buf[pl.ds(ROW_TILES, tm, stride=PAYLOAD_ROWS), :]
        x = x_scr[...]
        hu = jnp.dot(x, w1a_ref[0], preferred_element_type=F32)
        ha = jax.nn.silu(hu[:, :D_EXPERT]) * hu[:, D_EXPERT:] * gates[:, 0:1]
        hu = jnp.dot(x, w1b_ref[0], preferred_element_type=F32)
        hb = jax.nn.silu(hu[:, :D_EXPERT]) * hu[:, D_EXPERT:] * gates[:, 1:2]
        y = (jnp.dot(ha.astype(BF16), w2a_ref[0], preferred_element_type=F32)
             + jnp.dot(hb.astype(BF16), w2b_ref[0], preferred_element_type=F32))
        for j in range(ROW_TILES):
            ys_ref[pl.ds(j, tm, stride=ROW_TILES), :] = y[:, j * LANES:(j + 1) * LANES]

    @pl.when(i >= n_valid)
    def _():
        ys_ref[...] = jnp.zeros_like(ys_ref)


def moe_experts_routed(xd, inv, tile_ea, tile_eb, n_valid, w_ein, w_eout):
    n_tiles = tile_ea.shape[0]
    tm = MOE_TM
    inv3 = inv.reshape(n_tiles, 1, tm)
    smem = dict(memory_space=pltpu.SMEM)
    grid_spec = pltpu.PrefetchScalarGridSpec(
        num_scalar_prefetch=3,
        grid=(n_tiles,),
        in_specs=[
            pl.BlockSpec((1, 1, tm), lambda i, ea, eb, nv, a=a: (jnp.minimum(i + a, n_tiles - 1), 0, 0), **smem)
            for a in range(GATHER_AHEAD + 1)] + [
            pl.BlockSpec(memory_space=pl.ANY),
            pl.BlockSpec((1, D_MODEL, 2 * D_EXPERT), lambda i, ea, eb, nv: (ea[i], 0, 0)),
            pl.BlockSpec((1, D_MODEL, 2 * D_EXPERT), lambda i, ea, eb, nv: (eb[i], 0, 0)),
            pl.BlockSpec((1, D_EXPERT, D_MODEL), lambda i, ea, eb, nv: (ea[i], 0, 0)),
            pl.BlockSpec((1, D_EXPERT, D_MODEL), lambda i, ea, eb, nv: (eb[i], 0, 0))],
        out_specs=pl.BlockSpec((tm * ROW_TILES, LANES), lambda i, ea, eb, nv: (i, 0)),
        scratch_shapes=[pltpu.VMEM((GATHER_AHEAD + 1, tm * PAYLOAD_ROWS, LANES), F32),
                        pltpu.SemaphoreType.DMA((GATHER_AHEAD + 1,)),
                        pltpu.VMEM((tm, D_MODEL), BF16)])
    return pl.pallas_call(
        _moe_routed_kernel,
        grid_spec=grid_spec,
        out_shape=jax.ShapeDtypeStruct((n_tiles * tm * ROW_TILES, LANES), F32),
        compiler_params=_params("arbitrary"),
        name="moe_experts_routed",
    )(tile_ea, tile_eb, n_valid, *([inv3] * (GATHER_AHEAD + 1)), xd, w_ein, w_ein, w_eout, w_eout)


def _combine_kernel(dest_ref, dest_next_ref, ys_hbm, res_ref, gf_ref, y_ref, buf, sems):
    i = pl.program_id(0)
    slot = i % 2
    tm = res_ref.shape[0]

    @pl.when(i == 0)
    def _():
        _start_row_gather(dest_ref, ys_hbm, buf.at[0], sems.at[0], tm, ROW_TILES)

    @pl.when(i + 1 < pl.num_programs(0))
    def _():
        _start_row_gather(dest_next_ref, ys_hbm, buf.at[1 - slot], sems.at[1 - slot], tm, ROW_TILES)

    cur = buf.at[slot]
    _wait_row_gather(ys_hbm, cur, sems.at[slot], tm, ROW_TILES)
    for j in range(ROW_TILES):
        js = slice(j * LANES, (j + 1) * LANES)
        y_ref[:, js] = res_ref[:, js] + cur[pl.ds(j, tm, stride=ROW_TILES), :]
    y_ref[...] = _rms_rows(y_ref[...], gf_ref[...])


def moe_combine_final(ys, dest, res, g_final, tm=256):
    m = res.shape[0]
    nt = m // tm
    dest3 = dest.reshape(nt, 1, tm)
    smem = dict(memory_space=pltpu.SMEM)
    return pl.pallas_call(
        _combine_kernel,
        grid=(nt,),
        in_specs=[pl.BlockSpec((1, 1, tm), lambda i: (i, 0, 0), **smem),
                  pl.BlockSpec((1, 1, tm), lambda i: (jnp.minimum(i + 1, nt - 1), 0, 0), **smem),
                  pl.BlockSpec(memory_space=pl.ANY),
                  pl.BlockSpec((tm, D_MODEL), lambda i: (i, 0)),
                  pl.BlockSpec((1, D_MODEL), lambda i: (0, 0))],
        out_specs=pl.BlockSpec((tm, D_MODEL), lambda i: (i, 0)),
        out_shape=jax.ShapeDtypeStruct((m, D_MODEL), F32),
        scratch_shapes=[pltpu.VMEM((2, tm * ROW_TILES, LANES), F32), pltpu.SemaphoreType.DMA((2,))],
        compiler_params=_params("arbitrary"),
        name="moe_combine_final",
    )(dest3, dest3, ys, res, g_final.reshape(1, D_MODEL))


def moe_routed(x, g_ffn, w_rg, b_rg, w_re, b_re, w_ein, w_eout, g_final):
    m = x.shape[0]
    tm = MOE_TM
    n_tiles = m // tm + N_BUCKETS
    xd, meta, cnt = moe_router_routed(x, g_ffn, w_rg, b_rg, w_re, b_re)
    bucket = meta[:, 0].astype(jnp.int32)
    rank = meta[:, 1].astype(jnp.int32)
    counts = cnt[0, :N_BUCKETS].astype(jnp.int32)
    tiles_per_bucket = (counts + tm - 1) // tm
    tile_end = jnp.cumsum(tiles_per_bucket)
    tile_start = tile_end - tiles_per_bucket
    n_valid = tile_end[-1]
    tile_id = jnp.arange(n_tiles, dtype=jnp.int32)
    tile_bucket = jnp.searchsorted(tile_end, jnp.minimum(tile_id, n_valid - 1), side="right").astype(jnp.int32)
    group, pair = tile_bucket // PAIRS_PER_GROUP, tile_bucket % PAIRS_PER_GROUP
    tile_ea = group * EXPERTS_PER_GROUP + jnp.take(jnp.array(PAIR_LO, jnp.int32), pair)
    tile_eb = group * EXPERTS_PER_GROUP + jnp.take(jnp.array(PAIR_HI, jnp.int32), pair)
    dest, inv = slot_maps(bucket, rank, tile_start * tm, n_tiles * tm)
    ys = moe_experts_routed(xd, inv, tile_ea, tile_eb, n_valid.reshape(1), w_ein, w_eout)
    return moe_combine_final(ys, dest, x, g_final)


def _router_kernel(x_ref, g_ref, w_ref, b_ref, xn_ref, gate_ref):
    xn = _rms_rows(x_ref[...], g_ref[...])
    xn_ref[...] = xn.astype(BF16)
    logits = _router_logits(xn, w_ref, b_ref)
    _, i1, i2, p1, p2 = _route(logits)
    lanef = lax.broadcasted_iota(jnp.int32, logits.shape, 1).astype(F32)
    gate_ref[...] = jnp.where(lanef == i1, p1, 0.0) + jnp.where(lanef == i2, p2, 0.0)


def moe_router(x, g, w_rg, b_rg, w_re, b_re, tm):
    m = x.shape[0]
    tm = min(tm, m)
    w, b = _router_weights(w_rg, b_rg, w_re, b_re)
    row = lambda i: (i, 0)
    const = lambda i: (0, 0)
    return pl.pallas_call(
        _router_kernel,
        grid=(m // tm,),
        in_specs=[pl.BlockSpec((tm, D_MODEL), row), pl.BlockSpec((1, D_MODEL), const),
                  pl.BlockSpec((2, D_MODEL, LANES), lambda i: (0, 0, 0)), pl.BlockSpec((1, LANES), const)],
        out_specs=[pl.BlockSpec((tm, D_MODEL), row), pl.BlockSpec((tm, LANES), row)],
        out_shape=[jax.ShapeDtypeStruct((m, D_MODEL), BF16), jax.ShapeDtypeStruct((m, LANES), F32)],
        compiler_params=_params("parallel"),
        name="moe_router",
    )(x, g.reshape(1, D_MODEL), w, b)


def _moe_dense_kernel(xn_ref, gate_ref, w1_ref, w2_ref, res_ref, gf_ref, y_ref):
    e = pl.program_id(1)

    @pl.when(e == 0)
    def _():
        y_ref[...] = res_ref[...]

    hu = jnp.dot(xn_ref[...], w1_ref[0], preferred_element_type=F32)
    h = jax.nn.silu(hu[:, :D_EXPERT]) * hu[:, D_EXPERT:]
    lane = lax.broadcasted_iota(jnp.int32, gate_ref.shape, 1)
    gate = jnp.sum(jnp.where(lane == e, gate_ref[...], 0.0), axis=-1, keepdims=True)
    h = h * gate
    y_ref[...] += jnp.dot(h.astype(BF16), w2_ref[0], preferred_element_type=F32)

    @pl.when(e == N_EXPERTS - 1)
    def _():
        y_ref[...] = _rms_rows(y_ref[...], gf_ref[...])


def moe_experts_final(xn, gate, w_ein, w_eout, res, g_final, tm):
    m = xn.shape[0]
    tm = min(tm, m)
    row = lambda i, e: (i, 0)
    return pl.pallas_call(
        _moe_dense_kernel,
        grid=(m // tm, N_EXPERTS),
        in_specs=[pl.BlockSpec((tm, D_MODEL), row), pl.BlockSpec((tm, LANES), row),
                  pl.BlockSpec((1, D_MODEL, 2 * D_EXPERT), lambda i, e: (e, 0, 0)),
                  pl.BlockSpec((1, D_EXPERT, D_MODEL), lambda i, e: (e, 0, 0)),
                  pl.BlockSpec((tm, D_MODEL), row),
                  pl.BlockSpec((1, D_MODEL), lambda i, e: (0, 0))],
        out_specs=pl.BlockSpec((tm, D_MODEL), row),
        out_shape=jax.ShapeDtypeStruct((m, D_MODEL), F32),
        compiler_params=_params("parallel", "arbitrary"),
        name="moe_experts_final",
    )(xn, gate, w_ein, w_eout, res, g_final.reshape(1, D_MODEL))


def _window_decode_kernel(z_ref, c1_ref, c2_ref, c3_ref, o_ref):
    scale = HEAD_DIM ** -0.5
    os, ms, ds = [], [], []
    for g, c_ref in enumerate((c1_ref, c2_ref, c3_ref)):
        q = z_ref[0, g:g + 1]
        k_new = z_ref[0, 3 + g:4 + g]
        v_new = z_ref[0, 6 + g:7 + g]
        k = c_ref[0, 0, :, 0, 0]
        v = c_ref[0, 0, :, 0, 1]
        s = jnp.sum(k * q, axis=-1, keepdims=True) * scale
        s_new = jnp.sum(k_new * q, axis=-1, keepdims=True) * scale
        m = jnp.maximum(jnp.max(s, axis=0, keepdims=True), s_new)
        p = jnp.exp(s - m)
        p_new = jnp.exp(s_new - m)
        den = jnp.sum(p, axis=0, keepdims=True) + p_new
        os.append(jnp.sum(p * v, axis=0, keepdims=True) + p_new * v_new)
        ms.append(m)
        ds.append(den)
    m_all = functools.reduce(jnp.maximum, ms)
    ws = [jnp.exp(m - m_all) for m in ms]
    num = sum(w * o for w, o in zip(ws, os))
    den = sum(w * d for w, d in zip(ws, ds))
    o_ref[0] = (num / den).astype(o_ref.dtype)


def window_decode(z4, caches):
    b = z4.shape[0]
    views, specs = [], []
    for cache, (win, dil) in zip(caches, ATT_GROUPS):
        n = cache.shape[2]
        assert n == win and n // dil == BAND
        views.append(cache.reshape(1, b, BAND, dil, 2, HEADS, HEAD_DIM))
        specs.append(pl.BlockSpec((1, 1, BAND, 1, 2, HEADS, HEAD_DIM), lambda i: (0, i, 0, 0, 0, 0, 0)))
    return pl.pallas_call(
        _window_decode_kernel,
        grid=(b,),
        in_specs=[pl.BlockSpec((1,) + z4.shape[1:], lambda i: (i, 0, 0, 0))] + specs,
        out_specs=pl.BlockSpec((1, 1, HEADS, HEAD_DIM), lambda i: (i, 0, 0, 0)),
        out_shape=jax.ShapeDtypeStruct((b, 1, HEADS, HEAD_DIM), BF16),
        compiler_params=_params("parallel"),
        name="window_decode",
    )(z4, *views)


SHIFT_ROWS = 64


def _shift_kernel(z_ref, c1_ref, c2_ref, c3_ref, o1_ref, o2_ref, o3_ref):
    for g, (c_ref, o_ref) in enumerate(((c1_ref, o1_ref), (c2_ref, o2_ref), (c3_ref, o3_ref))):
        n = c_ref.shape[2]
        full, rem = divmod(n - 1, SHIFT_ROWS)

        def move(j, carry, c_ref=c_ref, o_ref=o_ref):
            o_ref[0, 0, pl.ds(j * SHIFT_ROWS, SHIFT_ROWS)] = c_ref[0, 0, pl.ds(j * SHIFT_ROWS + 1, SHIFT_ROWS)]
            return carry

        lax.fori_loop(0, full, move, 0)
        if rem:
            o_ref[0, 0, pl.ds(full * SHIFT_ROWS, rem)] = c_ref[0, 0, pl.ds(full * SHIFT_ROWS + 1, rem)]
        o_ref[0, 0, n - 1, 0] = z_ref[0, 3 + g]
        o_ref[0, 0, n - 1, 1] = z_ref[0, 6 + g]


def shift_caches(z4, caches):
    b = z4.shape[0]
    specs = [pl.BlockSpec((1, 1) + c.shape[2:], lambda i: (0, i, 0, 0, 0, 0)) for c in caches]
    return pl.pallas_call(
        _shift_kernel,
        grid=(b,),
        in_specs=[pl.BlockSpec((1,) + z4.shape[1:], lambda i: (i, 0, 0, 0))] + specs,
        out_specs=specs,
        out_shape=[jax.ShapeDtypeStruct(c.shape, c.dtype) for c in caches],
        compiler_params=_params("parallel"),
        name="shift_caches",
    )(z4, *caches)


def _conv_step_kernel(a_ref, b_ref, s_ref, w_ref, cb_ref, lg_ref, lb_ref, c_ref, so_ref):
    hist = CONV_K - 1
    u = a_ref[...] * jax.nn.sigmoid(b_ref[...])
    y = (jnp.sum(s_ref[0] * w_ref[0:hist, :], axis=1, keepdims=True)
         + u * w_ref[hist:hist + 1, :] + cb_ref[...])
    c_ref[...] = _layernorm_silu(y, lg_ref[...], lb_ref[...]).astype(c_ref.dtype)
    so_ref[0, :, pl.ds(0, hist - 1), :] = s_ref[0, :, pl.ds(1, hist - 1), :]
    so_ref[0, :, pl.ds(hist - 1, 1), :] = u


def conv_step(a, b, state, conv_w, conv_b, ln_g, ln_b):
    bsz = a.shape[0]
    return pl.pallas_call(
        _conv_step_kernel,
        out_shape=[jax.ShapeDtypeStruct((bsz, 1, D_CONV), BF16), jax.ShapeDtypeStruct(state.shape, F32)],
        compiler_params=pltpu.CompilerParams(vmem_limit_bytes=VMEM_LIMIT),
        name="conv_step",
    )(a, b, state, conv_w, conv_b.reshape(1, D_CONV), ln_g.reshape(1, D_CONV), ln_b.reshape(1, D_CONV))


def _trunk_tail(x1, xo_in, w_xo, norm_ffn_g, w_rg, b_rg, w_re, b_re, w_ein, w_eout, norm_final_g, bm, routed):
    x2 = matmul_residual(xo_in, w_xo, x1, min(bm, 512), bn=D_MODEL)
    if routed:
        return moe_routed(x2, norm_ffn_g, w_rg, b_rg, w_re, b_re, w_ein, w_eout, norm_final_g)
    xn3, gate = moe_router(x2, norm_ffn_g, w_rg, b_rg, w_re, b_re, bm)
    return moe_experts_final(xn3, gate, w_ein, w_eout, x2, norm_final_g, bm)


def kernel(x_prompt, x_sample, mem_prompt, cache_kv_g1, cache_kv_g2, cache_kv_g3, state_conv, cache_mem_kv,
           norm_mix_g, w_in, conv_w, conv_b, conv_ln_g, conv_ln_b, w_proj_a, w_proj_b, w_out,
           norm_xattn_g, norm_mem_g, w_xq, w_xkv, w_xo, norm_ffn_g,
           w_router_group, b_router_group, w_router_expert, b_router_expert, w_expert_in, w_expert_out,
           norm_final_g):
    depth = norm_mix_g.shape[0]
    assert depth == 1, "single-layer trunk"
    batch, seq, _ = x_prompt.shape
    dec_b, dec_t, _ = x_sample.shape
    assert dec_t == 1
    (g_mix, w_in, conv_w, conv_b, ln_g, ln_b, w_pa, w_pb, w_o, g_x, g_mem, w_xq, w_xkv, w_xo, g_ffn,
     w_rg, b_rg, w_re, b_re, w_ein, w_eout) = [t[0] for t in (
         norm_mix_g, w_in, conv_w, conv_b, conv_ln_g, conv_ln_b, w_proj_a, w_proj_b, w_out, norm_xattn_g,
         norm_mem_g, w_xq, w_xkv, w_xo, norm_ffn_g, w_router_group, b_router_group, w_router_expert,
         b_router_expert, w_expert_in, w_expert_out)]
    w_pa16, w_pb16, w_o16 = w_pa.astype(BF16), w_pb.astype(BF16), w_o.astype(BF16)
    w_ein, w_eout = w_ein.astype(BF16), w_eout.astype(BF16)
    w_za, w_gt = w_in[:, :ZA_W].astype(BF16), w_in[:, ZA_W:].astype(BF16)
    w_xq, w_xkv, w_xo = w_xq.astype(BF16), w_xkv.astype(BF16), w_xo.astype(BF16)
    caches = (cache_kv_g1, cache_kv_g2, cache_kv_g3)

    m_p = batch * seq
    xp = x_prompt.reshape(m_p, D_MODEL)
    za, gates = in_proj(xp, g_mix, w_za, w_gt, bm=1024)
    o_a = band_attention(za, batch, seq)
    x1, conv_prompt = conv_merge_branches(o_a, za, gates, xp, batch, seq, conv_w, conv_b, ln_g, ln_b,
                                          w_pa16, w_pb16, w_o16)
    mkv = norm_matmul(mem_prompt.reshape(batch * N_MEM, D_MODEL), g_mem, w_xkv, F32, bm=1024)
    q = norm_matmul(x1, g_x, w_xq, BF16, bm=512, bn=D_MODEL)
    xo_in = cross_attention_prompt(q, mkv, batch, seq)
    y_prompt = _trunk_tail(x1, xo_in, w_xo, g_ffn, w_rg, b_rg, w_re, b_re, w_ein, w_eout, norm_final_g,
                           bm=1024, routed=True)

    kv_prompt = window_tails(za, batch, seq)
    conv_prompt = conv_prompt[None]
    mem_kv_prompt = mkv.reshape(1, batch, N_MEM, 2, X_HEADS, X_HEAD_DIM)

    xs = x_sample.reshape(dec_b, D_MODEL)
    zs, gates_s = in_proj(xs, g_mix, w_za, w_gt, bm=dec_b)
    z4 = zs.reshape(dec_b, ZA_W // HEAD_DIM // HEADS, HEADS, HEAD_DIM)
    o_as = window_decode(z4, caches).reshape(dec_b, GROUP_W)
    kv_sample = shift_caches(z4, caches)
    a_s = zs[:, 3 * ATT_WIDTH:3 * ATT_WIDTH + D_CONV].reshape(dec_b, 1, D_CONV)
    b_s = zs[:, 3 * ATT_WIDTH + D_CONV:].reshape(dec_b, 1, D_CONV)
    c_s, conv_sample = conv_step(a_s, b_s, state_conv, conv_w, conv_b, ln_g, ln_b)
    x1s = merge_branches(o_as, c_s.reshape(dec_b, D_CONV), gates_s, xs, w_pa16, w_pb16, w_o16, tm=dec_b)
    q_s = norm_matmul(x1s, g_x, w_xq, BF16, bm=dec_b)
    xo_s = cross_attention_decode(q_s, cache_mem_kv)
    y_sample = _trunk_tail(x1s, xo_s, w_xo, g_ffn, w_rg, b_rg, w_re, b_re, w_ein, w_eout, norm_final_g,
                           bm=dec_b, routed=False)

    return (y_prompt.reshape(batch, seq, D_MODEL), y_sample.reshape(dec_b, 1, D_MODEL),
            kv_prompt[0], kv_prompt[1], kv_prompt[2], conv_prompt, mem_kv_prompt,
            kv_sample[0], kv_sample[1], kv_sample[2], conv_sample)
```

```python
import functools

import jax
import jax.numpy as jnp
from jax import lax
from jax.experimental import pallas as pl
from jax.experimental.pallas import tpu as pltpu

F32 = jnp.float32
BF16 = jnp.bfloat16

D_MODEL = 2048
ATT_GROUPS = ((128, 1), (512, 4), (2048, 16))
HEADS = 4
HEAD_DIM = 128
GROUP_W = HEADS * HEAD_DIM
ATT_WIDTH = len(ATT_GROUPS) * GROUP_W
D_CONV = 1536
CONV_K = 31
N_MEM = 256
X_HEADS = 4
X_HEAD_DIM = D_MODEL // X_HEADS
N_EXPERT_GROUPS = 4
EXPERTS_PER_GROUP = 4
N_EXPERTS = 16
D_EXPERT = 512
ZA_W = 3 * ATT_WIDTH + 2 * D_CONV
GATE_W = 2 * D_MODEL
RMS_EPS = 1e-6
LN_EPS = 1e-5
NEG_INF = -1e30
BAND = 128
LANES = 128
ROUTER_GROUP_LANE = N_EXPERTS

V7X_VMEM_BYTES = 64 * 1024 * 1024
VMEM_LIMIT = V7X_VMEM_BYTES * 7 // 8


def _params(*sem):
    return pltpu.CompilerParams(dimension_semantics=sem, vmem_limit_bytes=VMEM_LIMIT)


def _rms_rows(x, g):
    ms = jnp.mean(x * x, axis=-1, keepdims=True)
    return x * lax.rsqrt(ms + RMS_EPS) * g


def _store_normed(x_ref, g_ref, xn_ref, chunk=256):
    rows = x_ref.shape[0]
    step = min(chunk, rows)
    for r0 in range(0, rows, step):
        xn_ref[r0:r0 + step, :] = _rms_rows(x_ref[r0:r0 + step, :], g_ref[...]).astype(xn_ref.dtype)


def _norm_matmul_kernel(x_ref, g_ref, w_ref, o_ref, xn_ref):
    @pl.when(pl.program_id(1) == 0)
    def _():
        _store_normed(x_ref, g_ref, xn_ref)

    o_ref[...] = jnp.dot(xn_ref[...], w_ref[...], preferred_element_type=F32).astype(o_ref.dtype)


def norm_matmul(x, g, w, out_dtype, bm, bn=512):
    m, k = x.shape
    n = w.shape[1]
    bm = min(bm, m)
    return pl.pallas_call(
        _norm_matmul_kernel,
        grid=(m // bm, n // bn),
        in_specs=[pl.BlockSpec((bm, k), lambda i, j: (i, 0)),
                  pl.BlockSpec((1, k), lambda i, j: (0, 0)),
                  pl.BlockSpec((k, bn), lambda i, j: (0, j))],
        out_specs=pl.BlockSpec((bm, bn), lambda i, j: (i, j)),
        out_shape=jax.ShapeDtypeStruct((m, n), out_dtype),
        scratch_shapes=[pltpu.VMEM((bm, k), BF16)],
        compiler_params=_params("parallel", "arbitrary"),
        name="norm_matmul",
    )(x, g.reshape(1, k), w)


def _norm_matmul_keep_kernel(x_ref, g_ref, w_ref, o_ref, xn_ref):
    @pl.when(pl.program_id(1) == 0)
    def _():
        _store_normed(x_ref, g_ref, xn_ref)

    o_ref[...] = jnp.dot(xn_ref[...], w_ref[...], preferred_element_type=F32)


def _matmul_kernel(x_ref, w_ref, o_ref):
    o_ref[...] = jnp.dot(x_ref[...], w_ref[...], preferred_element_type=F32)


ZA_BN = 1536


def in_proj(x, g, w_za, w_gt, bm):
    m, k = x.shape
    bm = min(bm, m)
    za, xn = pl.pallas_call(
        _norm_matmul_keep_kernel,
        grid=(m // bm, ZA_W // ZA_BN),
        in_specs=[pl.BlockSpec((bm, k), lambda i, j: (i, 0)),
                  pl.BlockSpec((1, k), lambda i, j: (0, 0)),
                  pl.BlockSpec((k, ZA_BN), lambda i, j: (0, j))],
        out_specs=[pl.BlockSpec((bm, ZA_BN), lambda i, j: (i, j)),
                   pl.BlockSpec((bm, k), lambda i, j: (i, 0))],
        out_shape=[jax.ShapeDtypeStruct((m, ZA_W), F32), jax.ShapeDtypeStruct((m, k), BF16)],
        compiler_params=_params("parallel", "arbitrary"),
        name="in_proj_za",
    )(x, g.reshape(1, k), w_za)
    bn = GATE_W // 2
    gates = pl.pallas_call(
        _matmul_kernel,
        grid=(m // bm, GATE_W // bn),
        in_specs=[pl.BlockSpec((bm, k), lambda i, j: (i, 0)),
                  pl.BlockSpec((k, bn), lambda i, j: (0, j))],
        out_specs=pl.BlockSpec((bm, bn), lambda i, j: (i, j)),
        out_shape=jax.ShapeDtypeStruct((m, GATE_W), F32),
        compiler_params=_params("parallel", "arbitrary"),
        name="in_proj_gates",
    )(xn, w_gt)
    return za, gates


def _matmul_res_kernel(x_ref, w_ref, r_ref, o_ref):
    o_ref[...] = r_ref[...] + jnp.dot(x_ref[...], w_ref[...], preferred_element_type=F32)


def matmul_residual(x, w, res, bm, bn=512):
    m, k = x.shape
    n = w.shape[1]
    bm = min(bm, m)
    return pl.pallas_call(
        _matmul_res_kernel,
        grid=(m // bm, n // bn),
        in_specs=[pl.BlockSpec((bm, k), lambda i, j: (i, 0)),
                  pl.BlockSpec((k, bn), lambda i, j: (0, j)),
                  pl.BlockSpec((bm, bn), lambda i, j: (i, j))],
        out_specs=pl.BlockSpec((bm, bn), lambda i, j: (i, j)),
        out_shape=jax.ShapeDtypeStruct((m, n), F32),
        compiler_params=_params("parallel", "arbitrary"),
        name="matmul_residual",
    )(x, w, res)


ATT_R = 2048
ATT_UNROLL = 16


def _band_attn_kernel(*refs):
    n_g = len(ATT_GROUPS)
    in_refs = refs[:5 * n_g]
    o_ref = refs[5 * n_g]
    scr = refs[5 * n_g + 1:]
    kbufs, vbufs = scr[:n_g], scr[n_g:2 * n_g]
    obufs, mbufs, dbufs = scr[2 * n_g:3 * n_g], scr[3 * n_g:4 * n_g], scr[4 * n_g:5 * n_g]
    first_chunk = pl.program_id(1) == 0
    row = lax.broadcasted_iota(jnp.int32, (BAND, 2 * BAND), 0)
    col = lax.broadcasted_iota(jnp.int32, (BAND, 2 * BAND), 1)
    in_band = (col >= row) & (col <= row + BAND)
    in_cur = col >= BAND
    scale = HEAD_DIM ** -0.5
    nt = (((1,), (1,)), ((), ()))

    for g, (win, dil) in enumerate(ATT_GROUPS):
        q_ref, kc_ref, vc_ref, kp_ref, vp_ref = in_refs[5 * g:5 * g + 5]
        kbuf, vbuf, obuf, mbuf, dbuf = kbufs[g], vbufs[g], obufs[g], mbufs[g], dbufs[g]
        kbuf[0:win, :] = kp_ref[...]
        kbuf[win:win + ATT_R, :] = kc_ref[...]
        vbuf[0:win, :] = vp_ref[...]
        vbuf[win:win + ATT_R, :] = vc_ref[...]
        shift = dil.bit_length() - 1

        def sub_blocks(i4, carry, q_ref=q_ref, kbuf=kbuf, vbuf=vbuf, obuf=obuf, mbuf=mbuf, dbuf=dbuf,
                       win=win, dil=dil, shift=shift):
            done = []
            for u in range(ATT_UNROLL):
                i = i4 * ATT_UNROLL + u
                span = lax.shift_right_logical(i, shift)
                base = span * win + (i & (dil - 1))
                if dil == 1:
                    q_rows, kv_rows = pl.ds(base, BAND), pl.ds(base, 2 * BAND)
                else:
                    q_rows, kv_rows = pl.ds(base, BAND, stride=dil), pl.ds(base, 2 * BAND, stride=dil)
                q = q_ref[q_rows, :].astype(BF16)
                k = kbuf[kv_rows, :].astype(BF16)
                v = vbuf[kv_rows, :].astype(BF16)
                s = lax.dot_general(q, k, nt, preferred_element_type=F32) * scale
                has_prev = jnp.logical_not(first_chunk & (span == 0))
                s = jnp.where(in_band & (in_cur | has_prev), s, NEG_INF)
                m = jnp.max(s, axis=-1, keepdims=True)
                p = jnp.exp(s - m)
                den = jnp.sum(p, axis=-1, keepdims=True)
                done.append((q_rows, jnp.dot(p.astype(BF16), v, preferred_element_type=F32), m, den))
            for q_rows, o, m, den in done:
                obuf[q_rows, :] = o
                mbuf[q_rows, :] = jnp.broadcast_to(m, (BAND, HEAD_DIM))
                dbuf[q_rows, :] = jnp.broadcast_to(den, (BAND, HEAD_DIM))
            return carry

        lax.fori_loop(0, ATT_R // BAND // ATT_UNROLL, sub_blocks, 0)

    chunk = 256
    for r0 in range(0, ATT_R, chunk):
        rs = slice(r0, r0 + chunk)
        ms = [mb[rs, :] for mb in mbufs]
        m_all = functools.reduce(jnp.maximum, ms)
        ws = [jnp.exp(m - m_all) for m in ms]
        num = sum(w * ob[rs, :] for w, ob in zip(ws, obufs))
        den = sum(w * db[rs, :] for w, db in zip(ws, dbufs))
        o_ref[rs, :] = (num / den).astype(o_ref.dtype)


def band_attention(za, batch, seq):
    assert seq % ATT_R == 0
    nch = seq // ATT_R
    in_specs, scratch = [], []
    for g, (win, dil) in enumerate(ATT_GROUPS):
        assert win // dil == BAND and ATT_R % win == 0
        cols = [(part * len(ATT_GROUPS) + g) * HEADS for part in range(3)]
        cur = lambda c: pl.BlockSpec((ATT_R, HEAD_DIM), lambda b, ch, h, c=c: (b * nch + ch, c + h))
        prev = lambda c, win=win: pl.BlockSpec(
            (win, HEAD_DIM), lambda b, ch, h, c=c, win=win: (jnp.maximum((b * seq + ch * ATT_R) // win - 1, 0), c + h))
        in_specs += [cur(cols[0]), cur(cols[1]), cur(cols[2]), prev(cols[1]), prev(cols[2])]
    for _ in range(2):
        scratch += [pltpu.VMEM((win + ATT_R, HEAD_DIM), F32) for win, _ in ATT_GROUPS]
    scratch += [pltpu.VMEM((ATT_R, HEAD_DIM), F32)] * (3 * len(ATT_GROUPS))
    return pl.pallas_call(
        _band_attn_kernel,
        grid=(batch, nch, HEADS),
        in_specs=in_specs,
        out_specs=pl.BlockSpec((ATT_R, HEAD_DIM), lambda b, ch, h: (b * nch + ch, h)),
        out_shape=jax.ShapeDtypeStruct((batch * seq, GROUP_W), BF16),
        scratch_shapes=scratch,
        compiler_params=_params("parallel", "parallel", "parallel"),
        name="band_attention",
    )(*([za] * (5 * len(ATT_GROUPS))))


def _window_tails_kernel(*refs):
    n_g = len(ATT_GROUPS)
    slabs = 2 * HEADS
    for g in range(n_g):
        k_ref, v_ref, o_ref = refs[2 * g], refs[2 * g + 1], refs[2 * n_g + g]
        rows = k_ref.shape[0]
        for part, src in enumerate((k_ref, v_ref)):
            for h in range(HEADS):
                o_ref[pl.ds(part * HEADS + h, rows, stride=slabs), :] = src[:, h * HEAD_DIM:(h + 1) * HEAD_DIM]


def window_tails(za, batch, seq):
    in_specs, out_specs, out_shape = [], [], []
    slabs = 2 * HEADS
    for g, (win, _) in enumerate(ATT_GROUPS):
        assert win <= seq and seq % win == 0
        last = seq // win - 1
        for part in (1, 2):
            in_specs.append(pl.BlockSpec(
                (win, GROUP_W), lambda b, g=g, part=part, win=win, last=last:
                (b * (seq // win) + last, part * len(ATT_GROUPS) + g)))
        out_specs.append(pl.BlockSpec((win * slabs, HEAD_DIM), lambda b: (b, 0)))
        out_shape.append(jax.ShapeDtypeStruct((batch * win * slabs, HEAD_DIM), F32))
    outs = pl.pallas_call(
        _window_tails_kernel,
        grid=(batch,),
        in_specs=in_specs,
        out_specs=out_specs,
        out_shape=out_shape,
        compiler_params=_params("parallel"),
        name="window_tails",
    )(*([za] * (2 * len(ATT_GROUPS))))
    return [o.reshape(1, batch, win, 2, HEADS, HEAD_DIM) for o, (win, _) in zip(outs, ATT_GROUPS)]


CONV_T = 256
CONV_HIST = 32
CONV_RC = 128
CONV_PHASES = 4
CONV_SLABS = D_CONV // LANES


def _layernorm_silu(y, g, b):
    mu = jnp.mean(y, axis=-1, keepdims=True)
    yc = y - mu
    var = jnp.mean(yc * yc, axis=-1, keepdims=True)
    yn = yc * lax.rsqrt(var + LN_EPS) * g + b
    return yn * jax.nn.sigmoid(yn)


def _conv_history(t, ubuf):
    @pl.when(t == 0)
    def _():
        ubuf[:, 0:CONV_HIST, :] = jnp.zeros((CONV_SLABS, CONV_HIST, LANES), F32)

    @pl.when(t > 0)
    def _():
        ubuf[:, 0:CONV_HIST, :] = ubuf[:, CONV_T:CONV_T + CONV_HIST, :]


def _conv_slab(j, a_ref, b_ref, w_ref, cb_ref, tail_ref, ubuf, ybuf):
    first = CONV_HIST - (CONV_K - 1)
    n = CONV_RC // CONV_PHASES
    js = slice(j * LANES, (j + 1) * LANES)
    u = a_ref[:, js] * jax.nn.sigmoid(b_ref[:, js])
    ubuf[j, CONV_HIST:CONV_HIST + CONV_T, :] = u
    tail_ref[0, :, js] = u[CONV_T - CONV_HIST:, :]
    for r0 in range(0, CONV_T, CONV_RC):
        accs = [jnp.broadcast_to(cb_ref[:, js], (n, LANES))] * CONV_PHASES
        for s in range(first, first + CONV_K + CONV_PHASES - 1):
            rows = ubuf[j, pl.ds(r0 + s, n, stride=CONV_PHASES), :]
            for p in range(CONV_PHASES):
                k = s - first - p
                if 0 <= k < CONV_K:
                    accs[p] = accs[p] + w_ref[k:k + 1, js] * rows
        for p in range(CONV_PHASES):
            ybuf[j, pl.ds(r0 + p, n, stride=CONV_PHASES), :] = accs[p]
    return jnp.sum(ybuf[j], axis=-1, keepdims=True)


def _conv_centered_sq(j, mu, ybuf):
    yc = ybuf[j] - mu
    return jnp.sum(yc * yc, axis=-1, keepdims=True)


def _conv_norm_slab(j, mu, inv, lg_ref, lb_ref, c_ref, ybuf):
    js = slice(j * LANES, (j + 1) * LANES)
    yn = (ybuf[j] - mu) * inv * lg_ref[:, js] + lb_ref[:, js]
    c_ref[:, js] = (yn * jax.nn.sigmoid(yn)).astype(c_ref.dtype)


def _merge_kernel(oa_ref, c_ref, ga_ref, gb_ref, x_ref, wpa_ref, wpb_ref, wo_ref, y_ref):
    ta = jnp.dot(oa_ref[...], wpa_ref[...], preferred_element_type=F32)
    tb = jnp.dot(c_ref[...], wpb_ref[...], preferred_element_type=F32)
    hmix = jax.nn.sigmoid(ga_ref[...]) * ta + jax.nn.sigmoid(gb_ref[...]) * tb
    y_ref[...] = x_ref[...] + jnp.dot(hmix.astype(BF16), wo_ref[...], preferred_element_type=F32)


def _conv_merge_kernel(oa_ref, a_ref, b_ref, ga_ref, gb_ref, x_ref, cw_ref, cb_ref, lg_ref, lb_ref,
                       wpa_ref, wpb_ref, wo_ref, y_ref, tail_ref, ubuf, ybuf, c_scr):
    _conv_history(pl.program_id(1), ubuf)
    total = jnp.zeros((CONV_T, 1), F32)
    for j in range(CONV_SLABS):
        total = total + _conv_slab(j, a_ref, b_ref, cw_ref, cb_ref, tail_ref, ubuf, ybuf)
    mu = total * (1.0 / D_CONV)
    sq = jnp.zeros((CONV_T, 1), F32)
    for j in range(CONV_SLABS):
        sq = sq + _conv_centered_sq(j, mu, ybuf)
    inv = lax.rsqrt(sq * (1.0 / D_CONV) + LN_EPS)
    for j in range(CONV_SLABS):
        _conv_norm_slab(j, mu, inv, lg_ref, lb_ref, c_scr, ybuf)
    _merge_kernel(oa_ref, c_scr, ga_ref, gb_ref, x_ref, wpa_ref, wpb_ref, wo_ref, y_ref)


def conv_merge_branches(o_a, za, gates, x, batch, seq, conv_w, conv_b, ln_g, ln_b, w_pa, w_pb, w_out):
    nt = seq // CONV_T
    row = lambda b, t: (b * nt + t, 0)
    const = lambda b, t: (0, 0)
    once = dict(pipeline_mode=pl.Buffered(1))
    glu = 3 * ATT_WIDTH // D_CONV
    y, tail = pl.pallas_call(
        _conv_merge_kernel,
        grid=(batch, nt),
        in_specs=[pl.BlockSpec((CONV_T, GROUP_W), row),
                  pl.BlockSpec((CONV_T, D_CONV), lambda b, t: (b * nt + t, glu)),
                  pl.BlockSpec((CONV_T, D_CONV), lambda b, t: (b * nt + t, glu + 1)),
                  pl.BlockSpec((CONV_T, D_MODEL), lambda b, t: (b * nt + t, 0)),
                  pl.BlockSpec((CONV_T, D_MODEL), lambda b, t: (b * nt + t, 1)),
                  pl.BlockSpec((CONV_T, D_MODEL), row),
                  pl.BlockSpec((CONV_K, D_CONV), const),
                  pl.BlockSpec((1, D_CONV), const),
                  pl.BlockSpec((1, D_CONV), const),
                  pl.BlockSpec((1, D_CONV), const),
                  pl.BlockSpec((GROUP_W, D_MODEL), const, **once),
                  pl.BlockSpec((D_CONV, D_MODEL), const, **once),
                  pl.BlockSpec((D_MODEL, D_MODEL), const, **once)],
        out_specs=[pl.BlockSpec((CONV_T, D_MODEL), row),
                   pl.BlockSpec((1, CONV_HIST, D_CONV), lambda b, t: (b, 0, 0))],
        out_shape=[jax.ShapeDtypeStruct((batch * seq, D_MODEL), F32),
                   jax.ShapeDtypeStruct((batch, CONV_HIST, D_CONV), F32)],
        scratch_shapes=[pltpu.VMEM((CONV_SLABS, CONV_HIST + CONV_T, LANES), F32),
                        pltpu.VMEM((CONV_SLABS, CONV_T, LANES), F32),
                        pltpu.VMEM((CONV_T, D_CONV), BF16)],
        compiler_params=_params("parallel", "arbitrary"),
        name="conv_merge_branches",
    )(o_a, za, za, gates, gates, x, conv_w, conv_b.reshape(1, D_CONV), ln_g.reshape(1, D_CONV),
      ln_b.reshape(1, D_CONV), w_pa, w_pb, w_out)
    return y, tail[:, CONV_HIST - (CONV_K - 1):]


def merge_branches(o_a, c, gates, x, w_pa, w_pb, w_out, tm):
    m = x.shape[0]
    tm = min(tm, m)
    row = lambda i: (i, 0)
    const = lambda i: (0, 0)
    once = dict(pipeline_mode=pl.Buffered(1))
    return pl.pallas_call(
        _merge_kernel,
        grid=(m // tm,),
        in_specs=[
            pl.BlockSpec((tm, GROUP_W), row),
            pl.BlockSpec((tm, D_CONV), row),
            pl.BlockSpec((tm, D_MODEL), lambda i: (i, 0)),
            pl.BlockSpec((tm, D_MODEL), lambda i: (i, 1)),
            pl.BlockSpec((tm, D_MODEL), row),
            pl.BlockSpec((GROUP_W, D_MODEL), const, **once),
            pl.BlockSpec((D_CONV, D_MODEL), const, **once),
            pl.BlockSpec((D_MODEL, D_MODEL), const, **once)],
        out_specs=pl.BlockSpec((tm, D_MODEL), row),
        out_shape=jax.ShapeDtypeStruct((m, D_MODEL), F32),
        compiler_params=_params("parallel"),
        name="merge_branches",
    )(o_a, c, gates, gates, x, w_pa, w_pb, w_out)


def _xattn_kernel(q_ref, kv_ref, o_ref):
    scale = X_HEAD_DIM ** -0.5
    nt = (((1,), (1,)), ((), ()))
    for h in range(X_HEADS):
        hs = slice(h * X_HEAD_DIM, (h + 1) * X_HEAD_DIM)
        vs = slice(D_MODEL + h * X_HEAD_DIM, D_MODEL + (h + 1) * X_HEAD_DIM)
        s = lax.dot_general(q_ref[:, hs], kv_ref[:, hs].astype(BF16), nt, preferred_element_type=F32) * scale
        m = jnp.max(s, axis=-1, keepdims=True)
        p = jnp.exp(s - m)
        p = p / jnp.sum(p, axis=-1, keepdims=True)
        o = jnp.dot(p.astype(BF16), kv_ref[:, vs].astype(BF16), preferred_element_type=F32)
        o_ref[:, hs] = o.astype(o_ref.dtype)


def cross_attention_prompt(q, mkv, batch, seq, tm=512):
    nt = seq // tm
    return pl.pallas_call(
        _xattn_kernel,
        grid=(batch, nt),
        in_specs=[pl.BlockSpec((tm, D_MODEL), lambda b, t: (b * nt + t, 0)),
                  pl.BlockSpec((N_MEM, 2 * D_MODEL), lambda b, t: (b, 0))],
        out_specs=pl.BlockSpec((tm, D_MODEL), lambda b, t: (b * nt + t, 0)),
        out_shape=jax.ShapeDtypeStruct((batch * seq, D_MODEL), BF16),
        compiler_params=_params("parallel", "arbitrary"),
        name="cross_attention_prompt",
    )(q, mkv)


def _xattn_decode_kernel(q_ref, kv_ref, o_ref):
    scale = X_HEAD_DIM ** -0.5
    q = q_ref[0].astype(F32)
    k = kv_ref[0, 0, :, 0]
    v = kv_ref[0, 0, :, 1]
    s = jnp.sum(k * q, axis=-1, keepdims=True) * scale
    m = jnp.max(s, axis=0, keepdims=True)
    p = jnp.exp(s - m)
    p = p / jnp.sum(p, axis=0, keepdims=True)
    o_ref[0] = jnp.sum(p * v, axis=0, keepdims=True).astype(o_ref.dtype)


def cross_attention_decode(q, cache_mem_kv):
    b = q.shape[0]
    q4 = q.reshape(b, 1, X_HEADS, X_HEAD_DIM)
    o = pl.pallas_call(
        _xattn_decode_kernel,
        grid=(b,),
        in_specs=[pl.BlockSpec((1, 1, X_HEADS, X_HEAD_DIM), lambda i: (i, 0, 0, 0)),
                  pl.BlockSpec((1, 1, N_MEM, 2, X_HEADS, X_HEAD_DIM), lambda i: (0, i, 0, 0, 0, 0))],
        out_specs=pl.BlockSpec((1, 1, X_HEADS, X_HEAD_DIM), lambda i: (i, 0, 0, 0)),
        out_shape=jax.ShapeDtypeStruct((b, 1, X_HEADS, X_HEAD_DIM), BF16),
        compiler_params=_params("parallel"),
        name="cross_attention_decode",
    )(q4, cache_mem_kv)
    return o.reshape(b, D_MODEL)


_PAIRS = [(lo, hi) for lo in range(EXPERTS_PER_GROUP) for hi in range(lo + 1, EXPERTS_PER_GROUP)]
PAIRS_PER_GROUP = len(_PAIRS)
N_BUCKETS = N_EXPERT_GROUPS * PAIRS_PER_GROUP
PAIR_LO = tuple(lo for lo, _ in _PAIRS)
PAIR_HI = tuple(hi for _, hi in _PAIRS)
ROW_TILES = D_MODEL // LANES
PAYLOAD_ROWS = ROW_TILES + 8
MOE_TM = 256


def _route(logits):
    lane = lax.broadcasted_iota(jnp.int32, logits.shape, 1)
    lanef = lane.astype(F32)
    big = float(LANES)
    is_group = (lane >= ROUTER_GROUP_LANE) & (lane < ROUTER_GROUP_LANE + N_EXPERT_GROUPS)
    lg = jnp.where(is_group, logits, -jnp.inf)
    mg = jnp.max(lg, axis=-1, keepdims=True)
    p_sel = 1.0 / jnp.sum(jnp.exp(lg - mg), axis=-1, keepdims=True)
    gsel = jnp.min(jnp.where(lg == mg, lanef, big), axis=-1, keepdims=True) - ROUTER_GROUP_LANE
    group_of_lane = lax.shift_right_logical(lane, EXPERTS_PER_GROUP.bit_length() - 1)
    in_group = (lane < N_EXPERTS) & (group_of_lane == gsel.astype(jnp.int32))
    le = jnp.where(in_group, logits, -jnp.inf)
    v1 = jnp.max(le, axis=-1, keepdims=True)
    i1 = jnp.min(jnp.where(le == v1, lanef, big), axis=-1, keepdims=True)
    le2 = jnp.where(lanef == i1, -jnp.inf, le)
    v2 = jnp.max(le2, axis=-1, keepdims=True)
    i2 = jnp.min(jnp.where(le2 == v2, lanef, big), axis=-1, keepdims=True)
    t = jnp.exp(v2 - v1)
    tot = 1.0 + t
    return gsel, i1, i2, (1.0 / tot) * p_sel, (t / tot) * p_sel


def _router_routed_kernel(x_ref, g_ref, w_ref, b_ref, xd_ref, meta_ref, cnt_ref, carry_ref):
    tm = x_ref.shape[0]

    @pl.when(pl.program_id(0) == 0)
    def _():
        carry_ref[...] = jnp.zeros_like(carry_ref)

    xn = _rms_rows(x_ref[...], g_ref[...])
    logits = _router_logits(xn, w_ref, b_ref)
    gsel, i1, i2, p1, p2 = _route(logits)
    lo = jnp.minimum(i1, i2) - EXPERTS_PER_GROUP * gsel
    hi = jnp.maximum(i1, i2) - EXPERTS_PER_GROUP * gsel
    pair = lo * (2.0 * EXPERTS_PER_GROUP - 1.0 - lo) * 0.5 + (hi - lo - 1.0)
    bucket = gsel * float(PAIRS_PER_GROUP) + pair
    gate_lo = jnp.where(i1 < i2, p1, p2)
    gate_hi = jnp.where(i1 < i2, p2, p1)

    lane = lax.broadcasted_iota(jnp.int32, (tm, LANES), 1)
    onehot = (lane.astype(F32) == bucket).astype(F32)
    r_i = lax.broadcasted_iota(jnp.int32, (tm, tm), 0)
    c_i = lax.broadcasted_iota(jnp.int32, (tm, tm), 1)
    before = (c_i < r_i).astype(BF16)
    rank_local = jnp.dot(before, onehot.astype(BF16), preferred_element_type=F32)
    rank = jnp.sum(onehot * (rank_local + carry_ref[0:1, :]), axis=-1, keepdims=True)
    carry_ref[0:1, :] = carry_ref[0:1, :] + jnp.sum(onehot, axis=0, keepdims=True)
    cnt_ref[...] = jnp.broadcast_to(carry_ref[0:1, :], cnt_ref.shape)
    meta_ref[...] = jnp.where(lane == 0, bucket, jnp.where(lane == 1, rank, 0.0))

    for j in range(ROW_TILES):
        xd_ref[pl.ds(j, tm, stride=PAYLOAD_ROWS), :] = xn[:, j * LANES:(j + 1) * LANES]
    xd_ref[pl.ds(ROW_TILES, tm, stride=PAYLOAD_ROWS), :] = jnp.where(
        lane == 0, gate_lo, jnp.where(lane == 1, gate_hi, 0.0))
    for j in range(ROW_TILES + 1, PAYLOAD_ROWS):
        xd_ref[pl.ds(j, tm, stride=PAYLOAD_ROWS), :] = jnp.zeros((tm, LANES), F32)


def _router_weights(w_rg, b_rg, w_re, b_re):
    pad = LANES - N_EXPERTS - N_EXPERT_GROUPS
    w = jnp.concatenate([w_re, w_rg, jnp.zeros((D_MODEL, pad), F32)], axis=1)
    b = jnp.concatenate([b_re, b_rg, jnp.zeros((pad,), F32)]).reshape(1, LANES)
    hi = w.astype(BF16)
    lo = (w - hi.astype(F32)).astype(BF16)
    return jnp.stack([hi, lo]), b


def _router_logits(xn, w_ref, b_ref):
    hi = xn.astype(BF16)
    lo = (xn - hi.astype(F32)).astype(BF16)
    acc = jnp.dot(hi, w_ref[0], preferred_element_type=F32)
    acc = acc + jnp.dot(lo, w_ref[0], preferred_element_type=F32)
    acc = acc + jnp.dot(hi, w_ref[1], preferred_element_type=F32)
    return acc + b_ref[...]


def moe_router_routed(x, g, w_rg, b_rg, w_re, b_re, tm=512):
    m = x.shape[0]
    w, b = _router_weights(w_rg, b_rg, w_re, b_re)
    row = lambda i: (i, 0)
    const = lambda i: (0, 0)
    return pl.pallas_call(
        _router_routed_kernel,
        grid=(m // tm,),
        in_specs=[pl.BlockSpec((tm, D_MODEL), row), pl.BlockSpec((1, D_MODEL), const),
                  pl.BlockSpec((2, D_MODEL, LANES), lambda i: (0, 0, 0)), pl.BlockSpec((1, LANES), const)],
        out_specs=[pl.BlockSpec((tm * PAYLOAD_ROWS, LANES), row), pl.BlockSpec((tm, LANES), row),
                   pl.BlockSpec((8, LANES), const)],
        out_shape=[jax.ShapeDtypeStruct((m * PAYLOAD_ROWS, LANES), F32), jax.ShapeDtypeStruct((m, LANES), F32),
                   jax.ShapeDtypeStruct((8, LANES), F32)],
        scratch_shapes=[pltpu.VMEM((8, LANES), F32)],
        compiler_params=_params("arbitrary"),
        name="moe_router_routed",
    )(x, g.reshape(1, D_MODEL), w, b)


INV_CHUNK = 2048


def _slot_maps_kernel(bucket_ref, rank_ref, first_ref, dest_ref, inv_ref):
    step = pl.program_id(0)

    @pl.when(step == 0)
    def _():
        def clear(k, c):
            inv_ref[k] = 0
            return c
        lax.fori_loop(0, inv_ref.shape[0], clear, 0, unroll=8)

    def put(t, c):
        slot = first_ref[bucket_ref[0, 0, t]] + rank_ref[0, 0, t]
        dest_ref[0, 0, t] = slot
        inv_ref[slot] = step * INV_CHUNK + t
        return c
    lax.fori_loop(0, INV_CHUNK, put, 0, unroll=8)


def slot_maps(bucket, rank, first_slot, n_slots):
    m = bucket.shape[0]
    nc = m // INV_CHUNK
    smem = dict(memory_space=pltpu.SMEM)
    chunk = pl.BlockSpec((1, 1, INV_CHUNK), lambda i: (i, 0, 0), **smem)
    dest, inv = pl.pallas_call(
        _slot_maps_kernel,
        grid=(nc,),
        in_specs=[chunk, chunk, pl.BlockSpec(**smem)],
        out_specs=[chunk, pl.BlockSpec(**smem)],
        out_shape=[jax.ShapeDtypeStruct((nc, 1, INV_CHUNK), jnp.int32),
                   jax.ShapeDtypeStruct((n_slots,), jnp.int32)],
        compiler_params=_params("arbitrary"),
        name="slot_maps",
    )(bucket.reshape(nc, 1, INV_CHUNK), rank.reshape(nc, 1, INV_CHUNK), first_slot)
    return dest.reshape(m), inv


def _start_row_gather(idx_ref, src_hbm, dst, sem, n_items, rows):
    def start(r2, c):
        for queue in range(2):
            r = 2 * r2 + queue
            first = pl.multiple_of(idx_ref[0, 0, r] * rows, 8)
            pltpu.make_async_copy(src_hbm.at[pl.ds(first, rows)], dst.at[pl.ds(r * rows, rows)],
                                  sem).start(priority=queue)
        return c
    lax.fori_loop(0, n_items // 2, start, 0, unroll=4)


def _wait_row_gather(src_hbm, dst, sem, n_items, rows):
    pltpu.make_async_copy(src_hbm.at[pl.ds(0, n_items * rows)], dst, sem).wait()


GATHER_AHEAD = 3


def _moe_routed_kernel(ea_ref, eb_ref, nv_ref, *refs):
    del ea_ref, eb_ref
    inv_refs = refs[:GATHER_AHEAD + 1]
    xd_hbm, w1a_ref, w1b_ref, w2a_ref, w2b_ref, ys_ref, xbuf, sems, x_scr = refs[GATHER_AHEAD + 1:]
    n_buf = GATHER_AHEAD + 1
    i = pl.program_id(0)
    n_valid = nv_ref[0]
    slot = i % n_buf
    tm = MOE_TM

    @pl.when(i == 0)
    def _():
        for a in range(GATHER_AHEAD):
            @pl.when(a < n_valid)
            def _():
                _start_row_gather(inv_refs[a], xd_hbm, xbuf.at[a], sems.at[a], tm, PAYLOAD_ROWS)

    @pl.when(i + GATHER_AHEAD < n_valid)
    def _():
        ahead = (i + GATHER_AHEAD) % n_buf
        _start_row_gather(inv_refs[GATHER_AHEAD], xd_hbm, xbuf.at[ahead], sems.at[ahead], tm, PAYLOAD_ROWS)

    @pl.when(i < n_valid)
    def _():
        buf = xbuf.at[slot]
        _wait_row_gather(xd_hbm, buf, sems.at[slot], tm, PAYLOAD_ROWS)
        for j in range(ROW_TILES):
            x_scr[:, j * LANES:(j + 1) * LANES] = buf[pl.ds(j, tm, stride=PAYLOAD_ROWS), :].astype(BF16)
        gates = buf[pl.ds(ROW_TILES, tm, stride=PAYLOAD_ROWS), :]
        x = x_scr[...]
        hu = jnp.dot(x, w1a_ref[0], preferred_element_type=F32)
        ha = jax.nn.silu(hu[:, :D_EXPERT]) * hu[:, D_EXPERT:] * gates[:, 0:1]
        hu = jnp.dot(x, w1b_ref[0], preferred_element_type=F32)
        hb = jax.nn.silu(hu[:, :D_EXPERT]) * hu[:, D_EXPERT:] * gates[:, 1:2]
        y = (jnp.dot(ha.astype(BF16), w2a_ref[0], preferred_element_type=F32)
             + jnp.dot(hb.astype(BF16), w2b_ref[0], preferred_element_type=F32))
        for j in range(ROW_TILES):
            ys_ref[pl.ds(j, tm, stride=ROW_TILES), :] = y[:, j * LANES:(j + 1) * LANES]

    @pl.when(i >= n_valid)
    def _():
        ys_ref[...] = jnp.zeros_like(ys_ref)


def moe_experts_routed(xd, inv, tile_ea, tile_eb, n_valid, w_ein, w_eout):
    n_tiles = tile_ea.shape[0]
    tm = MOE_TM
    inv3 = inv.reshape(n_tiles, 1, tm)
    smem = dict(memory_space=pltpu.SMEM)
    grid_spec = pltpu.PrefetchScalarGridSpec(
        num_scalar_prefetch=3,
        grid=(n_tiles,),
        in_specs=[
            pl.BlockSpec((1, 1, tm), lambda i, ea, eb, nv, a=a: (jnp.minimum(i + a, n_tiles - 1), 0, 0), **smem)
            for a in range(GATHER_AHEAD + 1)] + [
            pl.BlockSpec(memory_space=pl.ANY),
            pl.BlockSpec((1, D_MODEL, 2 * D_EXPERT), lambda i, ea, eb, nv: (ea[i], 0, 0)),
            pl.BlockSpec((1, D_MODEL, 2 * D_EXPERT), lambda i, ea, eb, nv: (eb[i], 0, 0)),
            pl.BlockSpec((1, D_EXPERT, D_MODEL), lambda i, ea, eb, nv: (ea[i], 0, 0)),
            pl.BlockSpec((1, D_EXPERT, D_MODEL), lambda i, ea, eb, nv: (eb[i], 0, 0))],
        out_specs=pl.BlockSpec((tm * ROW_TILES, LANES), lambda i, ea, eb, nv: (i, 0)),
        scratch_shapes=[pltpu.VMEM((GATHER_AHEAD + 1, tm * PAYLOAD_ROWS, LANES), F32),
                        pltpu.SemaphoreType.DMA((GATHER_AHEAD + 1,)),
                        pltpu.VMEM((tm, D_MODEL), BF16)])
    return pl.pallas_call(
        _moe_routed_kernel,
        grid_spec=grid_spec,
        out_shape=jax.ShapeDtypeStruct((n_tiles * tm * ROW_TILES, LANES), F32),
        compiler_params=_params("arbitrary"),
        name="moe_experts_routed",
    )(tile_ea, tile_eb, n_valid, *([inv3] * (GATHER_AHEAD + 1)), xd, w_ein, w_ein, w_eout, w_eout)


def _combine_kernel(*refs):
    dest_refs = refs[:GATHER_AHEAD + 1]
    ys_hbm, res_ref, gf_ref, y_ref, buf, sems = refs[GATHER_AHEAD + 1:]
    n_buf = GATHER_AHEAD + 1
    i = pl.program_id(0)
    n_steps = pl.num_programs(0)
    slot = i % n_buf
    tm = res_ref.shape[0]

    @pl.when(i == 0)
    def _():
        for a in range(GATHER_AHEAD):
            @pl.when(a < n_steps)
            def _():
                _start_row_gather(dest_refs[a], ys_hbm, buf.at[a], sems.at[a], tm, ROW_TILES)

    @pl.when(i + GATHER_AHEAD < n_steps)
    def _():
        ahead = (i + GATHER_AHEAD) % n_buf
        _start_row_gather(dest_refs[GATHER_AHEAD], ys_hbm, buf.at[ahead], sems.at[ahead], tm, ROW_TILES)

    cur = buf.at[slot]
    _wait_row_gather(ys_hbm, cur, sems.at[slot], tm, ROW_TILES)
    for j in range(ROW_TILES):
        js = slice(j * LANES, (j + 1) * LANES)
        y_ref[:, js] = res_ref[:, js] + cur[pl.ds(j, tm, stride=ROW_TILES), :]
    y_ref[...] = _rms_rows(y_ref[...], gf_ref[...])


def moe_combine_final(ys, dest, res, g_final, tm=256):
    m = res.shape[0]
    nt = m // tm
    dest3 = dest.reshape(nt, 1, tm)
    smem = dict(memory_space=pltpu.SMEM)
    return pl.pallas_call(
        _combine_kernel,
        grid=(nt,),
        in_specs=[pl.BlockSpec((1, 1, tm), lambda i, a=a: (jnp.minimum(i + a, nt - 1), 0, 0), **smem)
                  for a in range(GATHER_AHEAD + 1)] + [
                  pl.BlockSpec(memory_space=pl.ANY),
                  pl.BlockSpec((tm, D_MODEL), lambda i: (i, 0)),
                  pl.BlockSpec((1, D_MODEL), lambda i: (0, 0))],
        out_specs=pl.BlockSpec((tm, D_MODEL), lambda i: (i, 0)),
        out_shape=jax.ShapeDtypeStruct((m, D_MODEL), F32),
        scratch_shapes=[pltpu.VMEM((GATHER_AHEAD + 1, tm * ROW_TILES, LANES), F32),
                        pltpu.SemaphoreType.DMA((GATHER_AHEAD + 1,))],
        compiler_params=_params("arbitrary"),
        name="moe_combine_final",
    )(*([dest3] * (GATHER_AHEAD + 1)), ys, res, g_final.reshape(1, D_MODEL))


def moe_routed(x, g_ffn, w_rg, b_rg, w_re, b_re, w_ein, w_eout, g_final):
    m = x.shape[0]
    tm = MOE_TM
    n_tiles = m // tm + N_BUCKETS
    xd, meta, cnt = moe_router_routed(x, g_ffn, w_rg, b_rg, w_re, b_re)
    bucket = meta[:, 0].astype(jnp.int32)
    rank = meta[:, 1].astype(jnp.int32)
    counts = cnt[0, :N_BUCKETS].astype(jnp.int32)
    tiles_per_bucket = (counts + tm - 1) // tm
    tile_end = jnp.cumsum(tiles_per_bucket)
    tile_start = tile_end - tiles_per_bucket
    n_valid = tile_end[-1]
    tile_id = jnp.arange(n_tiles, dtype=jnp.int32)
    tile_bucket = jnp.searchsorted(tile_end, jnp.minimum(tile_id, n_valid - 1), side="right").astype(jnp.int32)
    group, pair = tile_bucket // PAIRS_PER_GROUP, tile_bucket % PAIRS_PER_GROUP
    tile_ea = group * EXPERTS_PER_GROUP + jnp.take(jnp.array(PAIR_LO, jnp.int32), pair)
    tile_eb = group * EXPERTS_PER_GROUP + jnp.take(jnp.array(PAIR_HI, jnp.int32), pair)
    dest, inv = slot_maps(bucket, rank, tile_start * tm, n_tiles * tm)
    ys = moe_experts_routed(xd, inv, tile_ea, tile_eb, n_valid.reshape(1), w_ein, w_eout)
    return moe_combine_final(ys, dest, x, g_final)


def _router_kernel(x_ref, g_ref, w_ref, b_ref, xn_ref, gate_ref):
    xn = _rms_rows(x_ref[...], g_ref[...])
    xn_ref[...] = xn.astype(BF16)
    logits = _router_logits(xn, w_ref, b_ref)
    _, i1, i2, p1, p2 = _route(logits)
    lanef = lax.broadcasted_iota(jnp.int32, logits.shape, 1).astype(F32)
    gate_ref[...] = jnp.where(lanef == i1, p1, 0.0) + jnp.where(lanef == i2, p2, 0.0)


def moe_router(x, g, w_rg, b_rg, w_re, b_re, tm):
    m = x.shape[0]
    tm = min(tm, m)
    w, b = _router_weights(w_rg, b_rg, w_re, b_re)
    row = lambda i: (i, 0)
    const = lambda i: (0, 0)
    return pl.pallas_call(
        _router_kernel,
        grid=(m // tm,),
        in_specs=[pl.BlockSpec((tm, D_MODEL), row), pl.BlockSpec((1, D_MODEL), const),
                  pl.BlockSpec((2, D_MODEL, LANES), lambda i: (0, 0, 0)), pl.BlockSpec((1, LANES), const)],
        out_specs=[pl.BlockSpec((tm, D_MODEL), row), pl.BlockSpec((tm, LANES), row)],
        out_shape=[jax.ShapeDtypeStruct((m, D_MODEL), BF16), jax.ShapeDtypeStruct((m, LANES), F32)],
        compiler_params=_params("parallel"),
        name="moe_router",
    )(x, g.reshape(1, D_MODEL), w, b)


def _moe_dense_kernel(xn_ref, gate_ref, w1_ref, w2_ref, res_ref, gf_ref, y_ref):
    e = pl.program_id(1)

    @pl.when(e == 0)
    def _():
        y_ref[...] = res_ref[...]

    hu = jnp.dot(xn_ref[...], w1_ref[0], preferred_element_type=F32)
    h = jax.nn.silu(hu[:, :D_EXPERT]) * hu[:, D_EXPERT:]
    lane = lax.broadcasted_iota(jnp.int32, gate_ref.shape, 1)
    gate = jnp.sum(jnp.where(lane == e, gate_ref[...], 0.0), axis=-1, keepdims=True)
    h = h * gate
    y_ref[...] += jnp.dot(h.astype(BF16), w2_ref[0], preferred_element_type=F32)

    @pl.when(e == N_EXPERTS - 1)
    def _():
        y_ref[...] = _rms_rows(y_ref[...], gf_ref[...])


def moe_experts_final(xn, gate, w_ein, w_eout, res, g_final, tm):
    m = xn.shape[0]
    tm = min(tm, m)
    row = lambda i, e: (i, 0)
    return pl.pallas_call(
        _moe_dense_kernel,
        grid=(m // tm, N_EXPERTS),
        in_specs=[pl.BlockSpec((tm, D_MODEL), row), pl.BlockSpec((tm, LANES), row),
                  pl.BlockSpec((1, D_MODEL, 2 * D_EXPERT), lambda i, e: (e, 0, 0)),
                  pl.BlockSpec((1, D_EXPERT, D_MODEL), lambda i, e: (e, 0, 0)),
                  pl.BlockSpec((tm, D_MODEL), row),
                  pl.BlockSpec((1, D_MODEL), lambda i, e: (0, 0))],
        out_specs=pl.BlockSpec((tm, D_MODEL), row),
        out_shape=jax.ShapeDtypeStruct((m, D_MODEL), F32),
        compiler_params=_params("parallel", "arbitrary"),
        name="moe_experts_final",
    )(xn, gate, w_ein, w_eout, res, g_final.reshape(1, D_MODEL))


def _window_decode_kernel(z_ref, c1_ref, c2_ref, c3_ref, o_ref):
    scale = HEAD_DIM ** -0.5
    os, ms, ds = [], [], []
    for g, c_ref in enumerate((c1_ref, c2_ref, c3_ref)):
        q = z_ref[0, g:g + 1]
        k_new = z_ref[0, 3 + g:4 + g]
        v_new = z_ref[0, 6 + g:7 + g]
        k = c_ref[0, 0, :, 0, 0]
        v = c_ref[0, 0, :, 0, 1]
        s = jnp.sum(k * q, axis=-1, keepdims=True) * scale
        s_new = jnp.sum(k_new * q, axis=-1, keepdims=True) * scale
        m = jnp.maximum(jnp.max(s, axis=0, keepdims=True), s_new)
        p = jnp.exp(s - m)
        p_new = jnp.exp(s_new - m)
        den = jnp.sum(p, axis=0, keepdims=True) + p_new
        os.append(jnp.sum(p * v, axis=0, keepdims=True) + p_new * v_new)
        ms.append(m)
        ds.append(den)
    m_all = functools.reduce(jnp.maximum, ms)
    ws = [jnp.exp(m - m_all) for m in ms]
    num = sum(w * o for w, o in zip(ws, os))
    den = sum(w * d for w, d in zip(ws, ds))
    o_ref[0] = (num / den).astype(o_ref.dtype)


def window_decode(z4, caches):
    b = z4.shape[0]
    views, specs = [], []
    for cache, (win, dil) in zip(caches, ATT_GROUPS):
        n = cache.shape[2]
        assert n == win and n // dil == BAND
        views.append(cache.reshape(1, b, BAND, dil, 2, HEADS, HEAD_DIM))
        specs.append(pl.BlockSpec((1, 1, BAND, 1, 2, HEADS, HEAD_DIM), lambda i: (0, i, 0, 0, 0, 0, 0)))
    return pl.pallas_call(
        _window_decode_kernel,
        grid=(b,),
        in_specs=[pl.BlockSpec((1,) + z4.shape[1:], lambda i: (i, 0, 0, 0))] + specs,
        out_specs=pl.BlockSpec((1, 1, HEADS, HEAD_DIM), lambda i: (i, 0, 0, 0)),
        out_shape=jax.ShapeDtypeStruct((b, 1, HEADS, HEAD_DIM), BF16),
        compiler_params=_params("parallel"),
        name="window_decode",
    )(z4, *views)


SHIFT_ROWS = 64


def _shift_kernel(z_ref, c1_ref, c2_ref, c3_ref, o1_ref, o2_ref, o3_ref):
    for g, (c_ref, o_ref) in enumerate(((c1_ref, o1_ref), (c2_ref, o2_ref), (c3_ref, o3_ref))):
        n = c_ref.shape[2]
        full, rem = divmod(n - 1, SHIFT_ROWS)

        def move(j, carry, c_ref=c_ref, o_ref=o_ref):
            o_ref[0, 0, pl.ds(j * SHIFT_ROWS, SHIFT_ROWS)] = c_ref[0, 0, pl.ds(j * SHIFT_ROWS + 1, SHIFT_ROWS)]
            return carry

        lax.fori_loop(0, full, move, 0)
        if rem:
            o_ref[0, 0, pl.ds(full * SHIFT_ROWS, rem)] = c_ref[0, 0, pl.ds(full * SHIFT_ROWS + 1, rem)]
        o_ref[0, 0, n - 1, 0] = z_ref[0, 3 + g]
        o_ref[0, 0, n - 1, 1] = z_ref[0, 6 + g]


def shift_caches(z4, caches):
    b = z4.shape[0]
    specs = [pl.BlockSpec((1, 1) + c.shape[2:], lambda i: (0, i, 0, 0, 0, 0)) for c in caches]
    return pl.pallas_call(
        _shift_kernel,
        grid=(b,),
        in_specs=[pl.BlockSpec((1,) + z4.shape[1:], lambda i: (i, 0, 0, 0))] + specs,
        out_specs=specs,
        out_shape=[jax.ShapeDtypeStruct(c.shape, c.dtype) for c in caches],
        compiler_params=_params("parallel"),
        name="shift_caches",
    )(z4, *caches)


def _conv_step_kernel(a_ref, b_ref, s_ref, w_ref, cb_ref, lg_ref, lb_ref, c_ref, so_ref):
    hist = CONV_K - 1
    u = a_ref[...] * jax.nn.sigmoid(b_ref[...])
    y = (jnp.sum(s_ref[0] * w_ref[0:hist, :], axis=1, keepdims=True)
         + u * w_ref[hist:hist + 1, :] + cb_ref[...])
    c_ref[...] = _layernorm_silu(y, lg_ref[...], lb_ref[...]).astype(c_ref.dtype)
    so_ref[0, :, pl.ds(0, hist - 1), :] = s_ref[0, :, pl.ds(1, hist - 1), :]
    so_ref[0, :, pl.ds(hist - 1, 1), :] = u


def conv_step(a, b, state, conv_w, conv_b, ln_g, ln_b):
    bsz = a.shape[0]
    return pl.pallas_call(
        _conv_step_kernel,
        out_shape=[jax.ShapeDtypeStruct((bsz, 1, D_CONV), BF16), jax.ShapeDtypeStruct(state.shape, F32)],
        compiler_params=pltpu.CompilerParams(vmem_limit_bytes=VMEM_LIMIT),
        name="conv_step",
    )(a, b, state, conv_w, conv_b.reshape(1, D_CONV), ln_g.reshape(1, D_CONV), ln_b.reshape(1, D_CONV))


def _trunk_tail(x1, xo_in, w_xo, norm_ffn_g, w_rg, b_rg, w_re, b_re, w_ein, w_eout, norm_final_g, bm, routed):
    x2 = matmul_residual(xo_in, w_xo, x1, min(bm, 512), bn=D_MODEL)
    if routed:
        return moe_routed(x2, norm_ffn_g, w_rg, b_rg, w_re, b_re, w_ein, w_eout, norm_final_g)
    xn3, gate = moe_router(x2, norm_ffn_g, w_rg, b_rg, w_re, b_re, bm)
    return moe_experts_final(xn3, gate, w_ein, w_eout, x2, norm_final_g, bm)


def kernel(x_prompt, x_sample, mem_prompt, cache_kv_g1, cache_kv_g2, cache_kv_g3, state_conv, cache_mem_kv,
           norm_mix_g, w_in, conv_w, conv_b, conv_ln_g, conv_ln_b, w_proj_a, w_proj_b, w_out,
           norm_xattn_g, norm_mem_g, w_xq, w_xkv, w_xo, norm_ffn_g,
           w_router_group, b_router_group, w_router_expert, b_router_expert, w_expert_in, w_expert_out,
           norm_final_g):
    depth = norm_mix_g.shape[0]
    assert depth == 1, "single-layer trunk"
    batch, seq, _ = x_prompt.shape
    dec_b, dec_t, _ = x_sample.shape
    assert dec_t == 1
    (g_mix, w_in, conv_w, conv_b, ln_g, ln_b, w_pa, w_pb, w_o, g_x, g_mem, w_xq, w_xkv, w_xo, g_ffn,
     w_rg, b_rg, w_re, b_re, w_ein, w_eout) = [t[0] for t in (
         norm_mix_g, w_in, conv_w, conv_b, conv_ln_g, conv_ln_b, w_proj_a, w_proj_b, w_out, norm_xattn_g,
         norm_mem_g, w_xq, w_xkv, w_xo, norm_ffn_g, w_router_group, b_router_group, w_router_expert,
         b_router_expert, w_expert_in, w_expert_out)]
    w_pa16, w_pb16, w_o16 = w_pa.astype(BF16), w_pb.astype(BF16), w_o.astype(BF16)
    w_ein, w_eout = w_ein.astype(BF16), w_eout.astype(BF16)
    w_za, w_gt = w_in[:, :ZA_W].astype(BF16), w_in[:, ZA_W:].astype(BF16)
    w_xq, w_xkv, w_xo = w_xq.astype(BF16), w_xkv.astype(BF16), w_xo.astype(BF16)
    caches = (cache_kv_g1, cache_kv_g2, cache_kv_g3)

    m_p = batch * seq
    xp = x_prompt.reshape(m_p, D_MODEL)
    za, gates = in_proj(xp, g_mix, w_za, w_gt, bm=1024)
    o_a = band_attention(za, batch, seq)
    x1, conv_prompt = conv_merge_branches(o_a, za, gates, xp, batch, seq, conv_w, conv_b, ln_g, ln_b,
                                          w_pa16, w_pb16, w_o16)
    mkv = norm_matmul(mem_prompt.reshape(batch * N_MEM, D_MODEL), g_mem, w_xkv, F32, bm=1024)
    q = norm_matmul(x1, g_x, w_xq, BF16, bm=512, bn=D_MODEL)
    xo_in = cross_attention_prompt(q, mkv, batch, seq)
    y_prompt = _trunk_tail(x1, xo_in, w_xo, g_ffn, w_rg, b_rg, w_re, b_re, w_ein, w_eout, norm_final_g,
                           bm=1024, routed=True)

    kv_prompt = window_tails(za, batch, seq)
    conv_prompt = conv_prompt[None]
    mem_kv_prompt = mkv.reshape(1, batch, N_MEM, 2, X_HEADS, X_HEAD_DIM)

    xs = x_sample.reshape(dec_b, D_MODEL)
    zs, gates_s = in_proj(xs, g_mix, w_za, w_gt, bm=dec_b)
    z4 = zs.reshape(dec_b, ZA_W // HEAD_DIM // HEADS, HEADS, HEAD_DIM)
    o_as = window_decode(z4, caches).reshape(dec_b, GROUP_W)
    kv_sample = shift_caches(z4, caches)
    a_s = zs[:, 3 * ATT_WIDTH:3 * ATT_WIDTH + D_CONV].reshape(dec_b, 1, D_CONV)
    b_s = zs[:, 3 * ATT_WIDTH + D_CONV:].reshape(dec_b, 1, D_CONV)
    c_s, conv_sample = conv_step(a_s, b_s, state_conv, conv_w, conv_b, ln_g, ln_b)
    x1s = merge_branches(o_as, c_s.reshape(dec_b, D_CONV), gates_s, xs, w_pa16, w_pb16, w_o16, tm=dec_b)
    q_s = norm_matmul(x1s, g_x, w_xq, BF16, bm=dec_b)
    xo_s = cross_attention_decode(q_s, cache_mem_kv)
    y_sample = _trunk_tail(x1s, xo_s, w_xo, g_ffn, w_rg, b_rg, w_re, b_re, w_ein, w_eout, norm_final_g,
                           bm=dec_b, routed=False)

    return (y_prompt.reshape(batch, seq, D_MODEL), y_sample.reshape(dec_b, 1, D_MODEL),
            kv_prompt[0], kv_prompt[1], kv_prompt[2], conv_prompt, mem_kv_prompt,
            kv_sample[0], kv_sample[1], kv_sample[2], conv_sample)
```

```python
import functools

import jax
import jax.numpy as jnp
from jax import lax
from jax.experimental import pallas as pl
from jax.experimental.pallas import tpu as pltpu

F32 = jnp.float32
BF16 = jnp.bfloat16

D_MODEL = 2048
ATT_GROUPS = ((128, 1), (512, 4), (2048, 16))
HEADS = 4
HEAD_DIM = 128
GROUP_W = HEADS * HEAD_DIM
ATT_WIDTH = len(ATT_GROUPS) * GROUP_W
D_CONV = 1536
CONV_K = 31
N_MEM = 256
X_HEADS = 4
X_HEAD_DIM = D_MODEL // X_HEADS
N_EXPERT_GROUPS = 4
EXPERTS_PER_GROUP = 4
N_EXPERTS = 16
D_EXPERT = 512
ZA_W = 3 * ATT_WIDTH + 2 * D_CONV
GATE_W = 2 * D_MODEL
RMS_EPS = 1e-6
LN_EPS = 1e-5
NEG_INF = -1e30
BAND = 128
LANES = 128
ROUTER_GROUP_LANE = N_EXPERTS

V7X_VMEM_BYTES = 64 * 1024 * 1024
VMEM_LIMIT = V7X_VMEM_BYTES * 7 // 8


def _params(*sem):
    return pltpu.CompilerParams(dimension_semantics=sem, vmem_limit_bytes=VMEM_LIMIT)


def _rms_rows(x, g):
    ms = jnp.mean(x * x, axis=-1, keepdims=True)
    return x * lax.rsqrt(ms + RMS_EPS) * g


def _store_normed(x_ref, g_ref, xn_ref, chunk=256):
    rows = x_ref.shape[0]
    step = min(chunk, rows)
    for r0 in range(0, rows, step):
        xn_ref[r0:r0 + step, :] = _rms_rows(x_ref[r0:r0 + step, :], g_ref[...]).astype(xn_ref.dtype)


def _norm_matmul_kernel(x_ref, g_ref, w_ref, o_ref, xn_ref):
    @pl.when(pl.program_id(1) == 0)
    def _():
        _store_normed(x_ref, g_ref, xn_ref)

    o_ref[...] = jnp.dot(xn_ref[...], w_ref[...], preferred_element_type=F32).astype(o_ref.dtype)


def norm_matmul(x, g, w, out_dtype, bm, bn=512):
    m, k = x.shape
    n = w.shape[1]
    bm = min(bm, m)
    return pl.pallas_call(
        _norm_matmul_kernel,
        grid=(m // bm, n // bn),
        in_specs=[pl.BlockSpec((bm, k), lambda i, j: (i, 0)),
                  pl.BlockSpec((1, k), lambda i, j: (0, 0)),
                  pl.BlockSpec((k, bn), lambda i, j: (0, j))],
        out_specs=pl.BlockSpec((bm, bn), lambda i, j: (i, j)),
        out_shape=jax.ShapeDtypeStruct((m, n), out_dtype),
        scratch_shapes=[pltpu.VMEM((bm, k), BF16)],
        compiler_params=_params("parallel", "arbitrary"),
        name="norm_matmul",
    )(x, g.reshape(1, k), w)


def _norm_matmul_keep_kernel(x_hbm, g_ref, w_ref, o_ref, xn_ref, xbuf, sems):
    i = pl.program_id(0)
    bm = xn_ref.shape[0]

    def tile_copy(t, slot):
        rows = pl.ds(pl.multiple_of(t * bm, 8), bm)
        return pltpu.make_async_copy(x_hbm.at[rows], xbuf.at[slot], sems.at[slot])

    @pl.when(pl.program_id(1) == 0)
    def _():
        slot = i % 2

        @pl.when(i == 0)
        def _():
            tile_copy(0, 0).start()

        @pl.when(i + 1 < pl.num_programs(0))
        def _():
            tile_copy(i + 1, 1 - slot).start()

        tile_copy(i, slot).wait()
        _store_normed(xbuf.at[slot], g_ref, xn_ref)

    o_ref[...] = jnp.dot(xn_ref[...], w_ref[...], preferred_element_type=F32)


def _matmul_kernel(x_ref, w_ref, o_ref):
    o_ref[...] = jnp.dot(x_ref[...], w_ref[...], preferred_element_type=F32)


ZA_BN = 1536


def in_proj(x, g, w_za, w_gt, bm):
    m, k = x.shape
    bm = min(bm, m)
    za, xn = pl.pallas_call(
        _norm_matmul_keep_kernel,
        grid=(m // bm, ZA_W // ZA_BN),
        in_specs=[pl.BlockSpec(memory_space=pl.ANY),
                  pl.BlockSpec((1, k), lambda i, j: (0, 0)),
                  pl.BlockSpec((k, ZA_BN), lambda i, j: (0, j))],
        out_specs=[pl.BlockSpec((bm, ZA_BN), lambda i, j: (i, j)),
                   pl.BlockSpec((bm, k), lambda i, j: (i, 0))],
        out_shape=[jax.ShapeDtypeStruct((m, ZA_W), F32), jax.ShapeDtypeStruct((m, k), BF16)],
        scratch_shapes=[pltpu.VMEM((2, bm, k), F32), pltpu.SemaphoreType.DMA((2,))],
        compiler_params=_params("arbitrary", "arbitrary"),
        name="in_proj_za",
    )(x, g.reshape(1, k), w_za)
    bn = GATE_W // 2
    gates = pl.pallas_call(
        _matmul_kernel,
        grid=(m // bm, GATE_W // bn),
        in_specs=[pl.BlockSpec((bm, k), lambda i, j: (i, 0)),
                  pl.BlockSpec((k, bn), lambda i, j: (0, j))],
        out_specs=pl.BlockSpec((bm, bn), lambda i, j: (i, j)),
        out_shape=jax.ShapeDtypeStruct((m, GATE_W), F32),
        compiler_params=_params("parallel", "arbitrary"),
        name="in_proj_gates",
    )(xn, w_gt)
    return za, gates


def _matmul_res_kernel(x_ref, w_ref, r_ref, o_ref):
    o_ref[...] = r_ref[...] + jnp.dot(x_ref[...], w_ref[...], preferred_element_type=F32)


def matmul_residual(x, w, res, bm, bn=512):
    m, k = x.shape
    n = w.shape[1]
    bm = min(bm, m)
    return pl.pallas_call(
        _matmul_res_kernel,
        grid=(m // bm, n // bn),
        in_specs=[pl.BlockSpec((bm, k), lambda i, j: (i, 0)),
                  pl.BlockSpec((k, bn), lambda i, j: (0, j)),
                  pl.BlockSpec((bm, bn), lambda i, j: (i, j))],
        out_specs=pl.BlockSpec((bm, bn), lambda i, j: (i, j)),
        out_shape=jax.ShapeDtypeStruct((m, n), F32),
        compiler_params=_params("parallel", "arbitrary"),
        name="matmul_residual",
    )(x, w, res)


ATT_R = 2048
ATT_UNROLL = 16


def _band_attn_kernel(*refs):
    n_g = len(ATT_GROUPS)
    in_refs = refs[:5 * n_g]
    o_ref = refs[5 * n_g]
    scr = refs[5 * n_g + 1:]
    kbufs, vbufs = scr[:n_g], scr[n_g:2 * n_g]
    obufs, mbufs, dbufs = scr[2 * n_g:3 * n_g], scr[3 * n_g:4 * n_g], scr[4 * n_g:5 * n_g]
    first_chunk = pl.program_id(1) == 0
    row = lax.broadcasted_iota(jnp.int32, (BAND, 2 * BAND), 0)
    col = lax.broadcasted_iota(jnp.int32, (BAND, 2 * BAND), 1)
    in_band = (col >= row) & (col <= row + BAND)
    in_cur = col >= BAND
    scale = HEAD_DIM ** -0.5
    nt = (((1,), (1,)), ((), ()))

    for g, (win, dil) in enumerate(ATT_GROUPS):
        q_ref, kc_ref, vc_ref, kp_ref, vp_ref = in_refs[5 * g:5 * g + 5]
        kbuf, vbuf, obuf, mbuf, dbuf = kbufs[g], vbufs[g], obufs[g], mbufs[g], dbufs[g]
        kbuf[0:win, :] = kp_ref[...]
        kbuf[win:win + ATT_R, :] = kc_ref[...]
        vbuf[0:win, :] = vp_ref[...]
        vbuf[win:win + ATT_R, :] = vc_ref[...]
        shift = dil.bit_length() - 1

        def sub_blocks(i4, carry, q_ref=q_ref, kbuf=kbuf, vbuf=vbuf, obuf=obuf, mbuf=mbuf, dbuf=dbuf,
                       win=win, dil=dil, shift=shift):
            done = []
            for u in range(ATT_UNROLL):
                i = i4 * ATT_UNROLL + u
                span = lax.shift_right_logical(i, shift)
                base = span * win + (i & (dil - 1))
                if dil == 1:
                    q_rows, kv_rows = pl.ds(base, BAND), pl.ds(base, 2 * BAND)
                else:
                    q_rows, kv_rows = pl.ds(base, BAND, stride=dil), pl.ds(base, 2 * BAND, stride=dil)
                q = q_ref[q_rows, :].astype(BF16)
                k = kbuf[kv_rows, :].astype(BF16)
                v = vbuf[kv_rows, :].astype(BF16)
                s = lax.dot_general(q, k, nt, preferred_element_type=F32) * scale
                has_prev = jnp.logical_not(first_chunk & (span == 0))
                s = jnp.where(in_band & (in_cur | has_prev), s, NEG_INF)
                m = jnp.max(s, axis=-1, keepdims=True)
                p = jnp.exp(s - m)
                den = jnp.sum(p, axis=-1, keepdims=True)
                done.append((q_rows, jnp.dot(p.astype(BF16), v, preferred_element_type=F32), m, den))
            for q_rows, o, m, den in done:
                obuf[q_rows, :] = o
                mbuf[q_rows, :] = jnp.broadcast_to(m, (BAND, HEAD_DIM))
                dbuf[q_rows, :] = jnp.broadcast_to(den, (BAND, HEAD_DIM))
            return carry

        lax.fori_loop(0, ATT_R // BAND // ATT_UNROLL, sub_blocks, 0)

    chunk = 256
    for r0 in range(0, ATT_R, chunk):
        rs = slice(r0, r0 + chunk)
        ms = [mb[rs, :] for mb in mbufs]
        m_all = functools.reduce(jnp.maximum, ms)
        ws = [jnp.exp(m - m_all) for m in ms]
        num = sum(w * ob[rs, :] for w, ob in zip(ws, obufs))
        den = sum(w * db[rs, :] for w, db in zip(ws, dbufs))
        o_ref[rs, :] = (num / den).astype(o_ref.dtype)


def band_attention(za, batch, seq):
    assert seq % ATT_R == 0
    nch = seq // ATT_R
    in_specs, scratch = [], []
    for g, (win, dil) in enumerate(ATT_GROUPS):
        assert win // dil == BAND and ATT_R % win == 0
        cols = [(part * len(ATT_GROUPS) + g) * HEADS for part in range(3)]
        cur = lambda c: pl.BlockSpec((ATT_R, HEAD_DIM), lambda b, ch, h, c=c: (b * nch + ch, c + h))
        prev = lambda c, win=win: pl.BlockSpec(
            (win, HEAD_DIM), lambda b, ch, h, c=c, win=win: (jnp.maximum((b * seq + ch * ATT_R) // win - 1, 0), c + h))
        in_specs += [cur(cols[0]), cur(cols[1]), cur(cols[2]), prev(cols[1]), prev(cols[2])]
    for _ in range(2):
        scratch += [pltpu.VMEM((win + ATT_R, HEAD_DIM), F32) for win, _ in ATT_GROUPS]
    scratch += [pltpu.VMEM((ATT_R, HEAD_DIM), F32)] * (3 * len(ATT_GROUPS))
    return pl.pallas_call(
        _band_attn_kernel,
        grid=(batch, nch, HEADS),
        in_specs=in_specs,
        out_specs=pl.BlockSpec((ATT_R, HEAD_DIM), lambda b, ch, h: (b * nch + ch, h)),
        out_shape=jax.ShapeDtypeStruct((batch * seq, GROUP_W), BF16),
        scratch_shapes=scratch,
        compiler_params=_params("parallel", "parallel", "parallel"),
        name="band_attention",
    )(*([za] * (5 * len(ATT_GROUPS))))


def _window_tails_kernel(*refs):
    n_g = len(ATT_GROUPS)
    slabs = 2 * HEADS
    for g in range(n_g):
        k_ref, v_ref, o_ref = refs[2 * g], refs[2 * g + 1], refs[2 * n_g + g]
        rows = k_ref.shape[0]
        for part, src in enumerate((k_ref, v_ref)):
            for h in range(HEADS):
                o_ref[pl.ds(part * HEADS + h, rows, stride=slabs), :] = src[:, h * HEAD_DIM:(h + 1) * HEAD_DIM]


def window_tails(za, batch, seq):
    in_specs, out_specs, out_shape = [], [], []
    slabs = 2 * HEADS
    for g, (win, _) in enumerate(ATT_GROUPS):
        assert win <= seq and seq % win == 0
        last = seq // win - 1
        for part in (1, 2):
            in_specs.append(pl.BlockSpec(
                (win, GROUP_W), lambda b, g=g, part=part, win=win, last=last:
                (b * (seq // win) + last, part * len(ATT_GROUPS) + g)))
        out_specs.append(pl.BlockSpec((win * slabs, HEAD_DIM), lambda b: (b, 0)))
        out_shape.append(jax.ShapeDtypeStruct((batch * win * slabs, HEAD_DIM), F32))
    outs = pl.pallas_call(
        _window_tails_kernel,
        grid=(batch,),
        in_specs=in_specs,
        out_specs=out_specs,
        out_shape=out_shape,
        compiler_params=_params("parallel"),
        name="window_tails",
    )(*([za] * (2 * len(ATT_GROUPS))))
    return [o.reshape(1, batch, win, 2, HEADS, HEAD_DIM) for o, (win, _) in zip(outs, ATT_GROUPS)]


CONV_T = 256
CONV_HIST = 32
CONV_RC = 128
CONV_PHASES = 4
CONV_SLABS = D_CONV // LANES


def _layernorm_silu(y, g, b):
    mu = jnp.mean(y, axis=-1, keepdims=True)
    yc = y - mu
    var = jnp.mean(yc * yc, axis=-1, keepdims=True)
    yn = yc * lax.rsqrt(var + LN_EPS) * g + b
    return yn * jax.nn.sigmoid(yn)


def _conv_history(t, ubuf):
    @pl.when(t == 0)
    def _():
        ubuf[:, 0:CONV_HIST, :] = jnp.zeros((CONV_SLABS, CONV_HIST, LANES), F32)

    @pl.when(t > 0)
    def _():
        ubuf[:, 0:CONV_HIST, :] = ubuf[:, CONV_T:CONV_T + CONV_HIST, :]


def _conv_slab(j, a_ref, b_ref, w_ref, cb_ref, tail_ref, ubuf, ybuf):
    first = CONV_HIST - (CONV_K - 1)
    n = CONV_RC // CONV_PHASES
    js = slice(j * LANES, (j + 1) * LANES)
    u = a_ref[:, js] * jax.nn.sigmoid(b_ref[:, js])
    ubuf[j, CONV_HIST:CONV_HIST + CONV_T, :] = u
    tail_ref[0, :, js] = u[CONV_T - CONV_HIST:, :]
    for r0 in range(0, CONV_T, CONV_RC):
        accs = [jnp.broadcast_to(cb_ref[:, js], (n, LANES))] * CONV_PHASES
        for s in range(first, first + CONV_K + CONV_PHASES - 1):
            rows = ubuf[j, pl.ds(r0 + s, n, stride=CONV_PHASES), :]
            for p in range(CONV_PHASES):
                k = s - first - p
                if 0 <= k < CONV_K:
                    accs[p] = accs[p] + w_ref[k:k + 1, js] * rows
        for p in range(CONV_PHASES):
            ybuf[j, pl.ds(r0 + p, n, stride=CONV_PHASES), :] = accs[p]
    return jnp.sum(ybuf[j], axis=-1, keepdims=True)


def _conv_centered_sq(j, mu, ybuf):
    yc = ybuf[j] - mu
    return jnp.sum(yc * yc, axis=-1, keepdims=True)


def _conv_norm_slab(j, mu, inv, lg_ref, lb_ref, c_ref, ybuf):
    js = slice(j * LANES, (j + 1) * LANES)
    yn = (ybuf[j] - mu) * inv * lg_ref[:, js] + lb_ref[:, js]
    c_ref[:, js] = (yn * jax.nn.sigmoid(yn)).astype(c_ref.dtype)


def _merge_kernel(oa_ref, c_ref, ga_ref, gb_ref, x_ref, wpa_ref, wpb_ref, wo_ref, y_ref):
    ta = jnp.dot(oa_ref[...], wpa_ref[...], preferred_element_type=F32)
    tb = jnp.dot(c_ref[...], wpb_ref[...], preferred_element_type=F32)
    hmix = jax.nn.sigmoid(ga_ref[...]) * ta + jax.nn.sigmoid(gb_ref[...]) * tb
    y_ref[...] = x_ref[...] + jnp.dot(hmix.astype(BF16), wo_ref[...], preferred_element_type=F32)


def _conv_merge_kernel(oa_ref, a_ref, b_ref, ga_ref, gb_ref, x_ref, cw_ref, cb_ref, lg_ref, lb_ref,
                       wpa_ref, wpb_ref, wo_ref, y_ref, tail_ref, ubuf, ybuf, c_scr):
    _conv_history(pl.program_id(1), ubuf)
    total = jnp.zeros((CONV_T, 1), F32)
    for j in range(CONV_SLABS):
        total = total + _conv_slab(j, a_ref, b_ref, cw_ref, cb_ref, tail_ref, ubuf, ybuf)
    mu = total * (1.0 / D_CONV)
    sq = jnp.zeros((CONV_T, 1), F32)
    for j in range(CONV_SLABS):
        sq = sq + _conv_centered_sq(j, mu, ybuf)
    inv = lax.rsqrt(sq * (1.0 / D_CONV) + LN_EPS)
    for j in range(CONV_SLABS):
        _conv_norm_slab(j, mu, inv, lg_ref, lb_ref, c_scr, ybuf)
    _merge_kernel(oa_ref, c_scr, ga_ref, gb_ref, x_ref, wpa_ref, wpb_ref, wo_ref, y_ref)


def conv_merge_branches(o_a, za, gates, x, batch, seq, conv_w, conv_b, ln_g, ln_b, w_pa, w_pb, w_out):
    nt = seq // CONV_T
    row = lambda b, t: (b * nt + t, 0)
    const = lambda b, t: (0, 0)
    once = dict(pipeline_mode=pl.Buffered(1))
    glu = 3 * ATT_WIDTH // D_CONV
    y, tail = pl.pallas_call(
        _conv_merge_kernel,
        grid=(batch, nt),
        in_specs=[pl.BlockSpec((CONV_T, GROUP_W), row),
                  pl.BlockSpec((CONV_T, D_CONV), lambda b, t: (b * nt + t, glu)),
                  pl.BlockSpec((CONV_T, D_CONV), lambda b, t: (b * nt + t, glu + 1)),
                  pl.BlockSpec((CONV_T, D_MODEL), lambda b, t: (b * nt + t, 0)),
                  pl.BlockSpec((CONV_T, D_MODEL), lambda b, t: (b * nt + t, 1)),
                  pl.BlockSpec((CONV_T, D_MODEL), row),
                  pl.BlockSpec((CONV_K, D_CONV), const),
                  pl.BlockSpec((1, D_CONV), const),
                  pl.BlockSpec((1, D_CONV), const),
                  pl.BlockSpec((1, D_CONV), const),
                  pl.BlockSpec((GROUP_W, D_MODEL), const, **once),
                  pl.BlockSpec((D_CONV, D_MODEL), const, **once),
                  pl.BlockSpec((D_MODEL, D_MODEL), const, **once)],
        out_specs=[pl.BlockSpec((CONV_T, D_MODEL), row),
                   pl.BlockSpec((1, CONV_HIST, D_CONV), lambda b, t: (b, 0, 0))],
        out_shape=[jax.ShapeDtypeStruct((batch * seq, D_MODEL), F32),
                   jax.ShapeDtypeStruct((batch, CONV_HIST, D_CONV), F32)],
        scratch_shapes=[pltpu.VMEM((CONV_SLABS, CONV_HIST + CONV_T, LANES), F32),
                        pltpu.VMEM((CONV_SLABS, CONV_T, LANES), F32),
                        pltpu.VMEM((CONV_T, D_CONV), BF16)],
        compiler_params=_params("parallel", "arbitrary"),
        name="conv_merge_branches",
    )(o_a, za, za, gates, gates, x, conv_w, conv_b.reshape(1, D_CONV), ln_g.reshape(1, D_CONV),
      ln_b.reshape(1, D_CONV), w_pa, w_pb, w_out)
    return y, tail[:, CONV_HIST - (CONV_K - 1):]


def merge_branches(o_a, c, gates, x, w_pa, w_pb, w_out, tm):
    m = x.shape[0]
    tm = min(tm, m)
    row = lambda i: (i, 0)
    const = lambda i: (0, 0)
    once = dict(pipeline_mode=pl.Buffered(1))
    return pl.pallas_call(
        _merge_kernel,
        grid=(m // tm,),
        in_specs=[
            pl.BlockSpec((tm, GROUP_W), row),
            pl.BlockSpec((tm, D_CONV), row),
            pl.BlockSpec((tm, D_MODEL), lambda i: (i, 0)),
            pl.BlockSpec((tm, D_MODEL), lambda i: (i, 1)),
            pl.BlockSpec((tm, D_MODEL), row),
            pl.BlockSpec((GROUP_W, D_MODEL), const, **once),
            pl.BlockSpec((D_CONV, D_MODEL), const, **once),
            pl.BlockSpec((D_MODEL, D_MODEL), const, **once)],
        out_specs=pl.BlockSpec((tm, D_MODEL), row),
        out_shape=jax.ShapeDtypeStruct((m, D_MODEL), F32),
        compiler_params=_params("parallel"),
        name="merge_branches",
    )(o_a, c, gates, gates, x, w_pa, w_pb, w_out)


def _xattn_kernel(q_ref, kv_ref, o_ref):
    scale = X_HEAD_DIM ** -0.5
    nt = (((1,), (1,)), ((), ()))
    for h in range(X_HEADS):
        hs = slice(h * X_HEAD_DIM, (h + 1) * X_HEAD_DIM)
        vs = slice(D_MODEL + h * X_HEAD_DIM, D_MODEL + (h + 1) * X_HEAD_DIM)
        s = lax.dot_general(q_ref[:, hs], kv_ref[:, hs].astype(BF16), nt, preferred_element_type=F32) * scale
        m = jnp.max(s, axis=-1, keepdims=True)
        p = jnp.exp(s - m)
        p = p / jnp.sum(p, axis=-1, keepdims=True)
        o = jnp.dot(p.astype(BF16), kv_ref[:, vs].astype(BF16), preferred_element_type=F32)
        o_ref[:, hs] = o.astype(o_ref.dtype)


def cross_attention_prompt(q, mkv, batch, seq, tm=512):
    nt = seq // tm
    return pl.pallas_call(
        _xattn_kernel,
        grid=(batch, nt),
        in_specs=[pl.BlockSpec((tm, D_MODEL), lambda b, t: (b * nt + t, 0)),
                  pl.BlockSpec((N_MEM, 2 * D_MODEL), lambda b, t: (b, 0))],
        out_specs=pl.BlockSpec((tm, D_MODEL), lambda b, t: (b * nt + t, 0)),
        out_shape=jax.ShapeDtypeStruct((batch * seq, D_MODEL), BF16),
        compiler_params=_params("parallel", "arbitrary"),
        name="cross_attention_prompt",
    )(q, mkv)


def _xattn_decode_kernel(q_ref, kv_ref, o_ref):
    scale = X_HEAD_DIM ** -0.5
    q = q_ref[0].astype(F32)
    k = kv_ref[0, 0, :, 0]
    v = kv_ref[0, 0, :, 1]
    s = jnp.sum(k * q, axis=-1, keepdims=True) * scale
    m = jnp.max(s, axis=0, keepdims=True)
    p = jnp.exp(s - m)
    p = p / jnp.sum(p, axis=0, keepdims=True)
    o_ref[0] = jnp.sum(p * v, axis=0, keepdims=True).astype(o_ref.dtype)


def cross_attention_decode(q, cache_mem_kv):
    b = q.shape[0]
    q4 = q.reshape(b, 1, X_HEADS, X_HEAD_DIM)
    o = pl.pallas_call(
        _xattn_decode_kernel,
        grid=(b,),
        in_specs=[pl.BlockSpec((1, 1, X_HEADS, X_HEAD_DIM), lambda i: (i, 0, 0, 0)),
                  pl.BlockSpec((1, 1, N_MEM, 2, X_HEADS, X_HEAD_DIM), lambda i: (0, i, 0, 0, 0, 0))],
        out_specs=pl.BlockSpec((1, 1, X_HEADS, X_HEAD_DIM), lambda i: (i, 0, 0, 0)),
        out_shape=jax.ShapeDtypeStruct((b, 1, X_HEADS, X_HEAD_DIM), BF16),
        compiler_params=_params("parallel"),
        name="cross_attention_decode",
    )(q4, cache_mem_kv)
    return o.reshape(b, D_MODEL)


_PAIRS = [(lo, hi) for lo in range(EXPERTS_PER_GROUP) for hi in range(lo + 1, EXPERTS_PER_GROUP)]
PAIRS_PER_GROUP = len(_PAIRS)
N_BUCKETS = N_EXPERT_GROUPS * PAIRS_PER_GROUP
PAIR_LO = tuple(lo for lo, _ in _PAIRS)
PAIR_HI = tuple(hi for _, hi in _PAIRS)
ROW_TILES = D_MODEL // LANES
PAYLOAD_ROWS = ROW_TILES + 8
MOE_TM = 256


def _route(logits):
    lane = lax.broadcasted_iota(jnp.int32, logits.shape, 1)
    lanef = lane.astype(F32)
    big = float(LANES)
    is_group = (lane >= ROUTER_GROUP_LANE) & (lane < ROUTER_GROUP_LANE + N_EXPERT_GROUPS)
    lg = jnp.where(is_group, logits, -jnp.inf)
    mg = jnp.max(lg, axis=-1, keepdims=True)
    p_sel = 1.0 / jnp.sum(jnp.exp(lg - mg), axis=-1, keepdims=True)
    gsel = jnp.min(jnp.where(lg == mg, lanef, big), axis=-1, keepdims=True) - ROUTER_GROUP_LANE
    group_of_lane = lax.shift_right_logical(lane, EXPERTS_PER_GROUP.bit_length() - 1)
    in_group = (lane < N_EXPERTS) & (group_of_lane == gsel.astype(jnp.int32))
    le = jnp.where(in_group, logits, -jnp.inf)
    v1 = jnp.max(le, axis=-1, keepdims=True)
    i1 = jnp.min(jnp.where(le == v1, lanef, big), axis=-1, keepdims=True)
    le2 = jnp.where(lanef == i1, -jnp.inf, le)
    v2 = jnp.max(le2, axis=-1, keepdims=True)
    i2 = jnp.min(jnp.where(le2 == v2, lanef, big), axis=-1, keepdims=True)
    t = jnp.exp(v2 - v1)
    tot = 1.0 + t
    return gsel, i1, i2, (1.0 / tot) * p_sel, (t / tot) * p_sel


def _router_routed_kernel(x_ref, g_ref, w_ref, b_ref, xd_ref, meta_ref, cnt_ref, carry_ref):
    tm = x_ref.shape[0]

    @pl.when(pl.program_id(0) == 0)
    def _():
        carry_ref[...] = jnp.zeros_like(carry_ref)

    xn = _rms_rows(x_ref[...], g_ref[...])
    logits = _router_logits(xn, w_ref, b_ref)
    gsel, i1, i2, p1, p2 = _route(logits)
    lo = jnp.minimum(i1, i2) - EXPERTS_PER_GROUP * gsel
    hi = jnp.maximum(i1, i2) - EXPERTS_PER_GROUP * gsel
    pair = lo * (2.0 * EXPERTS_PER_GROUP - 1.0 - lo) * 0.5 + (hi - lo - 1.0)
    bucket = gsel * float(PAIRS_PER_GROUP) + pair
    gate_lo = jnp.where(i1 < i2, p1, p2)
    gate_hi = jnp.where(i1 < i2, p2, p1)

    lane = lax.broadcasted_iota(jnp.int32, (tm, LANES), 1)
    onehot = (lane.astype(F32) == bucket).astype(F32)
    r_i = lax.broadcasted_iota(jnp.int32, (tm, tm), 0)
    c_i = lax.broadcasted_iota(jnp.int32, (tm, tm), 1)
    before = (c_i < r_i).astype(BF16)
    rank_local = jnp.dot(before, onehot.astype(BF16), preferred_element_type=F32)
    rank = jnp.sum(onehot * (rank_local + carry_ref[0:1, :]), axis=-1, keepdims=True)
    carry_ref[0:1, :] = carry_ref[0:1, :] + jnp.sum(onehot, axis=0, keepdims=True)
    cnt_ref[...] = jnp.broadcast_to(carry_ref[0:1, :], cnt_ref.shape)
    meta_ref[...] = jnp.where(lane == 0, bucket, jnp.where(lane == 1, rank, 0.0))

    for j in range(ROW_TILES):
        xd_ref[pl.ds(j, tm, stride=PAYLOAD_ROWS), :] = xn[:, j * LANES:(j + 1) * LANES]
    xd_ref[pl.ds(ROW_TILES, tm, stride=PAYLOAD_ROWS), :] = jnp.where(
        lane == 0, gate_lo, jnp.where(lane == 1, gate_hi, 0.0))
    for j in range(ROW_TILES + 1, PAYLOAD_ROWS):
        xd_ref[pl.ds(j, tm, stride=PAYLOAD_ROWS), :] = jnp.zeros((tm, LANES), F32)


def _router_weights(w_rg, b_rg, w_re, b_re):
    pad = LANES - N_EXPERTS - N_EXPERT_GROUPS
    w = jnp.concatenate([w_re, w_rg, jnp.zeros((D_MODEL, pad), F32)], axis=1)
    b = jnp.concatenate([b_re, b_rg, jnp.zeros((pad,), F32)]).reshape(1, LANES)
    hi = w.astype(BF16)
    lo = (w - hi.astype(F32)).astype(BF16)
    return jnp.stack([hi, lo]), b


def _router_logits(xn, w_ref, b_ref):
    hi = xn.astype(BF16)
    lo = (xn - hi.astype(F32)).astype(BF16)
    acc = jnp.dot(hi, w_ref[0], preferred_element_type=F32)
    acc = acc + jnp.dot(lo, w_ref[0], preferred_element_type=F32)
    acc = acc + jnp.dot(hi, w_ref[1], preferred_element_type=F32)
    return acc + b_ref[...]


def moe_router_routed(x, g, w_rg, b_rg, w_re, b_re, tm=512):
    m = x.shape[0]
    w, b = _router_weights(w_rg, b_rg, w_re, b_re)
    row = lambda i: (i, 0)
    const = lambda i: (0, 0)
    return pl.pallas_call(
        _router_routed_kernel,
        grid=(m // tm,),
        in_specs=[pl.BlockSpec((tm, D_MODEL), row), pl.BlockSpec((1, D_MODEL), const),
                  pl.BlockSpec((2, D_MODEL, LANES), lambda i: (0, 0, 0)), pl.BlockSpec((1, LANES), const)],
        out_specs=[pl.BlockSpec((tm * PAYLOAD_ROWS, LANES), row), pl.BlockSpec((tm, LANES), row),
                   pl.BlockSpec((8, LANES), const)],
        out_shape=[jax.ShapeDtypeStruct((m * PAYLOAD_ROWS, LANES), F32), jax.ShapeDtypeStruct((m, LANES), F32),
                   jax.ShapeDtypeStruct((8, LANES), F32)],
        scratch_shapes=[pltpu.VMEM((8, LANES), F32)],
        compiler_params=_params("arbitrary"),
        name="moe_router_routed",
    )(x, g.reshape(1, D_MODEL), w, b)


INV_CHUNK = 2048


def _slot_maps_kernel(bucket_ref, rank_ref, first_ref, dest_ref, inv_ref):
    step = pl.program_id(0)

    @pl.when(step == 0)
    def _():
        def clear(k, c):
            inv_ref[k] = 0
            return c
        lax.fori_loop(0, inv_ref.shape[0], clear, 0, unroll=8)

    def put(t, c):
        slot = first_ref[bucket_ref[0, 0, t]] + rank_ref[0, 0, t]
        dest_ref[0, 0, t] = slot
        inv_ref[slot] = step * INV_CHUNK + t
        return c
    lax.fori_loop(0, INV_CHUNK, put, 0, unroll=8)


def slot_maps(bucket, rank, first_slot, n_slots):
    m = bucket.shape[0]
    nc = m // INV_CHUNK
    smem = dict(memory_space=pltpu.SMEM)
    chunk = pl.BlockSpec((1, 1, INV_CHUNK), lambda i: (i, 0, 0), **smem)
    dest, inv = pl.pallas_call(
        _slot_maps_kernel,
        grid=(nc,),
        in_specs=[chunk, chunk, pl.BlockSpec(**smem)],
        out_specs=[chunk, pl.BlockSpec(**smem)],
        out_shape=[jax.ShapeDtypeStruct((nc, 1, INV_CHUNK), jnp.int32),
                   jax.ShapeDtypeStruct((n_slots,), jnp.int32)],
        compiler_params=_params("arbitrary"),
        name="slot_maps",
    )(bucket.reshape(nc, 1, INV_CHUNK), rank.reshape(nc, 1, INV_CHUNK), first_slot)
    return dest.reshape(m), inv


def _start_row_gather(idx_ref, src_hbm, dst, sem, n_items, rows):
    def start(r2, c):
        for queue in range(2):
            r = 2 * r2 + queue
            first = pl.multiple_of(idx_ref[0, 0, r] * rows, 8)
            pltpu.make_async_copy(src_hbm.at[pl.ds(first, rows)], dst.at[pl.ds(r * rows, rows)],
                                  sem).start(priority=queue)
        return c
    lax.fori_loop(0, n_items // 2, start, 0, unroll=4)


def _wait_row_gather(src_hbm, dst, sem, n_items, rows):
    pltpu.make_async_copy(src_hbm.at[pl.ds(0, n_items * rows)], dst, sem).wait()


GATHER_AHEAD = 3


def _moe_routed_kernel(ea_ref, eb_ref, nv_ref, *refs):
    del ea_ref, eb_ref
    inv_refs = refs[:GATHER_AHEAD + 1]
    xd_hbm, w1a_ref, w1b_ref, w2a_ref, w2b_ref, ys_ref, xbuf, sems, x_scr = refs[GATHER_AHEAD + 1:]
    n_buf = GATHER_AHEAD + 1
    i = pl.program_id(0)
    n_valid = nv_ref[0]
    slot = i % n_buf
    tm = MOE_TM

    @pl.when(i == 0)
    def _():
        for a in range(GATHER_AHEAD):
            @pl.when(a < n_valid)
            def _():
                _start_row_gather(inv_refs[a], xd_hbm, xbuf.at[a], sems.at[a], tm, PAYLOAD_ROWS)

    @pl.when(i + GATHER_AHEAD < n_valid)
    def _():
        ahead = (i + GATHER_AHEAD) % n_buf
        _start_row_gather(inv_refs[GATHER_AHEAD], xd_hbm, xbuf.at[ahead], sems.at[ahead], tm, PAYLOAD_ROWS)

    @pl.when(i < n_valid)
    def _():
        buf = xbuf.at[slot]
        _wait_row_gather(xd_hbm, buf, sems.at[slot], tm, PAYLOAD_ROWS)
        for j in range(ROW_TILES):
            x_scr[:, j * LANES:(j + 1) * LANES] = buf[pl.ds(j, tm, stride=PAYLOAD_ROWS), :].astype(BF16)
        gates = buf[pl.ds(ROW_TILES, tm, stride=PAYLOAD_ROWS), :]
        x = x_scr[...]
        hu = jnp.dot(x, w1a_ref[0], preferred_element_type=F32)
        ha = jax.nn.silu(hu[:, :D_EXPERT]) * hu[:, D_EXPERT:] * gates[:, 0:1]
        hu = jnp.dot(x, w1b_ref[0], preferred_element_type=F32)
        hb = jax.nn.silu(hu[:, :D_EXPERT]) * hu[:, D_EXPERT:] * gates[:, 1:2]
        y = (jnp.dot(ha.astype(BF16), w2a_ref[0], preferred_element_type=F32)
             + jnp.dot(hb.astype(BF16), w2b_ref[0], preferred_element_type=F32))
        for j in range(ROW_TILES):
            ys_ref[pl.ds(j, tm, stride=ROW_TILES), :] = y[:, j * LANES:(j + 1) * LANES]

    @pl.when(i >= n_valid)
    def _():
        ys_ref[...] = jnp.zeros_like(ys_ref)


def moe_experts_routed(xd, inv, tile_ea, tile_eb, n_valid, w_ein, w_eout):
    n_tiles = tile_ea.shape[0]
    tm = MOE_TM
    inv3 = inv.reshape(n_tiles, 1, tm)
    smem = dict(memory_space=pltpu.SMEM)
    grid_spec = pltpu.PrefetchScalarGridSpec(
        num_scalar_prefetch=3,
        grid=(n_tiles,),
        in_specs=[
            pl.BlockSpec((1, 1, tm), lambda i, ea, eb, nv, a=a: (jnp.minimum(i + a, n_tiles - 1), 0, 0), **smem)
            for a in range(GATHER_AHEAD + 1)] + [
            pl.BlockSpec(memory_space=pl.ANY),
            pl.BlockSpec((1, D_MODEL, 2 * D_EXPERT), lambda i, ea, eb, nv: (ea[i], 0, 0)),
            pl.BlockSpec((1, D_MODEL, 2 * D_EXPERT), lambda i, ea, eb, nv: (eb[i], 0, 0)),
            pl.BlockSpec((1, D_EXPERT, D_MODEL), lambda i, ea, eb, nv: (ea[i], 0, 0)),
            pl.BlockSpec((1, D_EXPERT, D_MODEL), lambda i, ea, eb, nv: (eb[i], 0, 0))],
        out_specs=pl.BlockSpec((tm * ROW_TILES, LANES), lambda i, ea, eb, nv: (i, 0)),
        scratch_shapes=[pltpu.VMEM((GATHER_AHEAD + 1, tm * PAYLOAD_ROWS, LANES), F32),
                        pltpu.SemaphoreType.DMA((GATHER_AHEAD + 1,)),
                        pltpu.VMEM((tm, D_MODEL), BF16)])
    return pl.pallas_call(
        _moe_routed_kernel,
        grid_spec=grid_spec,
        out_shape=jax.ShapeDtypeStruct((n_tiles * tm * ROW_TILES, LANES), F32),
        compiler_params=_params("arbitrary"),
        name="moe_experts_routed",
    )(tile_ea, tile_eb, n_valid, *([inv3] * (GATHER_AHEAD + 1)), xd, w_ein, w_ein, w_eout, w_eout)


def _combine_kernel(*refs):
    dest_refs = refs[:GATHER_AHEAD + 1]
    ys_hbm, res_ref, gf_ref, y_ref, buf, sems = refs[GATHER_AHEAD + 1:]
    n_buf = GATHER_AHEAD + 1
    i = pl.program_id(0)
    n_steps = pl.num_programs(0)
    slot = i % n_buf
    tm = res_ref.shape[0]

    @pl.when(i == 0)
    def _():
        for a in range(GATHER_AHEAD):
            @pl.when(a < n_steps)
            def _():
                _start_row_gather(dest_refs[a], ys_hbm, buf.at[a], sems.at[a], tm, ROW_TILES)

    @pl.when(i + GATHER_AHEAD < n_steps)
    def _():
        ahead = (i + GATHER_AHEAD) % n_buf
        _start_row_gather(dest_refs[GATHER_AHEAD], ys_hbm, buf.at[ahead], sems.at[ahead], tm, ROW_TILES)

    cur = buf.at[slot]
    _wait_row_gather(ys_hbm, cur, sems.at[slot], tm, ROW_TILES)
    for j in range(ROW_TILES):
        js = slice(j * LANES, (j + 1) * LANES)
        y_ref[:, js] = res_ref[:, js] + cur[pl.ds(j, tm, stride=ROW_TILES), :]
    y_ref[...] = _rms_rows(y_ref[...], gf_ref[...])


def moe_combine_final(ys, dest, res, g_final, tm=256):
    m = res.shape[0]
    nt = m // tm
    dest3 = dest.reshape(nt, 1, tm)
    smem = dict(memory_space=pltpu.SMEM)
    return pl.pallas_call(
        _combine_kernel,
        grid=(nt,),
        in_specs=[pl.BlockSpec((1, 1, tm), lambda i, a=a: (jnp.minimum(i + a, nt - 1), 0, 0), **smem)
                  for a in range(GATHER_AHEAD + 1)] + [
                  pl.BlockSpec(memory_space=pl.ANY),
                  pl.BlockSpec((tm, D_MODEL), lambda i: (i, 0)),
                  pl.BlockSpec((1, D_MODEL), lambda i: (0, 0))],
        out_specs=pl.BlockSpec((tm, D_MODEL), lambda i: (i, 0)),
        out_shape=jax.ShapeDtypeStruct((m, D_MODEL), F32),
        scratch_shapes=[pltpu.VMEM((GATHER_AHEAD + 1, tm * ROW_TILES, LANES), F32),
                        pltpu.SemaphoreType.DMA((GATHER_AHEAD + 1,))],
        compiler_params=_params("arbitrary"),
        name="moe_combine_final",
    )(*([dest3] * (GATHER_AHEAD + 1)), ys, res, g_final.reshape(1, D_MODEL))


def moe_routed(x, g_ffn, w_rg, b_rg, w_re, b_re, w_ein, w_eout, g_final):
    m = x.shape[0]
    tm = MOE_TM
    n_tiles = m // tm + N_BUCKETS
    xd, meta, cnt = moe_router_routed(x, g_ffn, w_rg, b_rg, w_re, b_re)
    bucket = meta[:, 0].astype(jnp.int32)
    rank = meta[:, 1].astype(jnp.int32)
    counts = cnt[0, :N_BUCKETS].astype(jnp.int32)
    tiles_per_bucket = (counts + tm - 1) // tm
    tile_end = jnp.cumsum(tiles_per_bucket)
    tile_start = tile_end - tiles_per_bucket
    n_valid = tile_end[-1]
    tile_id = jnp.arange(n_tiles, dtype=jnp.int32)
    tile_bucket = jnp.searchsorted(tile_end, jnp.minimum(tile_id, n_valid - 1), side="right").astype(jnp.int32)
    group, pair = tile_bucket // PAIRS_PER_GROUP, tile_bucket % PAIRS_PER_GROUP
    tile_ea = group * EXPERTS_PER_GROUP + jnp.take(jnp.array(PAIR_LO, jnp.int32), pair)
    tile_eb = group * EXPERTS_PER_GROUP + jnp.take(jnp.array(PAIR_HI, jnp.int32), pair)
    dest, inv = slot_maps(bucket, rank, tile_start * tm, n_tiles * tm)
    ys = moe_experts_routed(xd, inv, tile_ea, tile_eb, n_valid.reshape(1), w_ein, w_eout)
    return moe_combine_final(ys, dest, x, g_final)


def _router_kernel(x_ref, g_ref, w_ref, b_ref, xn_ref, gate_ref):
    xn = _rms_rows(x_ref[...], g_ref[...])
    xn_ref[...] = xn.astype(BF16)
    logits = _router_logits(xn, w_ref, b_ref)
    _, i1, i2, p1, p2 = _route(logits)
    lanef = lax.broadcasted_iota(jnp.int32, logits.shape, 1).astype(F32)
    gate_ref[...] = jnp.where(lanef == i1, p1, 0.0) + jnp.where(lanef == i2, p2, 0.0)


def moe_router(x, g, w_rg, b_rg, w_re, b_re, tm):
    m = x.shape[0]
    tm = min(tm, m)
    w, b = _router_weights(w_rg, b_rg, w_re, b_re)
    row = lambda i: (i, 0)
    const = lambda i: (0, 0)
    return pl.pallas_call(
        _router_kernel,
        grid=(m // tm,),
        in_specs=[pl.BlockSpec((tm, D_MODEL), row), pl.BlockSpec((1, D_MODEL), const),
                  pl.BlockSpec((2, D_MODEL, LANES), lambda i: (0, 0, 0)), pl.BlockSpec((1, LANES), const)],
        out_specs=[pl.BlockSpec((tm, D_MODEL), row), pl.BlockSpec((tm, LANES), row)],
        out_shape=[jax.ShapeDtypeStruct((m, D_MODEL), BF16), jax.ShapeDtypeStruct((m, LANES), F32)],
        compiler_params=_params("parallel"),
        name="moe_router",
    )(x, g.reshape(1, D_MODEL), w, b)


def _moe_dense_kernel(xn_ref, gate_ref, w1_ref, w2_ref, res_ref, gf_ref, y_ref):
    e = pl.program_id(1)

    @pl.when(e == 0)
    def _():
        y_ref[...] = res_ref[...]

    hu = jnp.dot(xn_ref[...], w1_ref[0], preferred_element_type=F32)
    h = jax.nn.silu(hu[:, :D_EXPERT]) * hu[:, D_EXPERT:]
    lane = lax.broadcasted_iota(jnp.int32, gate_ref.shape, 1)
    gate = jnp.sum(jnp.where(lane == e, gate_ref[...], 0.0), axis=-1, keepdims=True)
    h = h * gate
    y_ref[...] += jnp.dot(h.astype(BF16), w2_ref[0], preferred_element_type=F32)

    @pl.when(e == N_EXPERTS - 1)
    def _():
        y_ref[...] = _rms_rows(y_ref[...], gf_ref[...])


def moe_experts_final(xn, gate, w_ein, w_eout, res, g_final, tm):
    m = xn.shape[0]
    tm = min(tm, m)
    row = lambda i, e: (i, 0)
    return pl.pallas_call(
        _moe_dense_kernel,
        grid=(m // tm, N_EXPERTS),
        in_specs=[pl.BlockSpec((tm, D_MODEL), row), pl.BlockSpec((tm, LANES), row),
                  pl.BlockSpec((1, D_MODEL, 2 * D_EXPERT), lambda i, e: (e, 0, 0)),
                  pl.BlockSpec((1, D_EXPERT, D_MODEL), lambda i, e: (e, 0, 0)),
                  pl.BlockSpec((tm, D_MODEL), row),
                  pl.BlockSpec((1, D_MODEL), lambda i, e: (0, 0))],
        out_specs=pl.BlockSpec((tm, D_MODEL), row),
        out_shape=jax.ShapeDtypeStruct((m, D_MODEL), F32),
        compiler_params=_params("parallel", "arbitrary"),
        name="moe_experts_final",
    )(xn, gate, w_ein, w_eout, res, g_final.reshape(1, D_MODEL))


def _window_decode_kernel(z_ref, c1_ref, c2_ref, c3_ref, o_ref):
    scale = HEAD_DIM ** -0.5
    os, ms, ds = [], [], []
    for g, c_ref in enumerate((c1_ref, c2_ref, c3_ref)):
        q = z_ref[0, g:g + 1]
        k_new = z_ref[0, 3 + g:4 + g]
        v_new = z_ref[0, 6 + g:7 + g]
        k = c_ref[0, 0, :, 0, 0]
        v = c_ref[0, 0, :, 0, 1]
        s = jnp.sum(k * q, axis=-1, keepdims=True) * scale
        s_new = jnp.sum(k_new * q, axis=-1, keepdims=True) * scale
        m = jnp.maximum(jnp.max(s, axis=0, keepdims=True), s_new)
        p = jnp.exp(s - m)
        p_new = jnp.exp(s_new - m)
        den = jnp.sum(p, axis=0, keepdims=True) + p_new
        os.append(jnp.sum(p * v, axis=0, keepdims=True) + p_new * v_new)
        ms.append(m)
        ds.append(den)
    m_all = functools.reduce(jnp.maximum, ms)
    ws = [jnp.exp(m - m_all) for m in ms]
    num = sum(w * o for w, o in zip(ws, os))
    den = sum(w * d for w, d in zip(ws, ds))
    o_ref[0] = (num / den).astype(o_ref.dtype)


def window_decode(z4, caches):
    b = z4.shape[0]
    views, specs = [], []
    for cache, (win, dil) in zip(caches, ATT_GROUPS):
        n = cache.shape[2]
        assert n == win and n // dil == BAND
        views.append(cache.reshape(1, b, BAND, dil, 2, HEADS, HEAD_DIM))
        specs.append(pl.BlockSpec((1, 1, BAND, 1, 2, HEADS, HEAD_DIM), lambda i: (0, i, 0, 0, 0, 0, 0)))
    return pl.pallas_call(
        _window_decode_kernel,
        grid=(b,),
        in_specs=[pl.BlockSpec((1,) + z4.shape[1:], lambda i: (i, 0, 0, 0))] + specs,
        out_specs=pl.BlockSpec((1, 1, HEADS, HEAD_DIM), lambda i: (i, 0, 0, 0)),
        out_shape=jax.ShapeDtypeStruct((b, 1, HEADS, HEAD_DIM), BF16),
        compiler_params=_params("parallel"),
        name="window_decode",
    )(z4, *views)


SHIFT_ROWS = 64


def _shift_kernel(z_ref, c1_ref, c2_ref, c3_ref, o1_ref, o2_ref, o3_ref):
    for g, (c_ref, o_ref) in enumerate(((c1_ref, o1_ref), (c2_ref, o2_ref), (c3_ref, o3_ref))):
        n = c_ref.shape[2]
        full, rem = divmod(n - 1, SHIFT_ROWS)

        def move(j, carry, c_ref=c_ref, o_ref=o_ref):
            o_ref[0, 0, pl.ds(j * SHIFT_ROWS, SHIFT_ROWS)] = c_ref[0, 0, pl.ds(j * SHIFT_ROWS + 1, SHIFT_ROWS)]
            return carry

        lax.fori_loop(0, full, move, 0)
        if rem:
            o_ref[0, 0, pl.ds(full * SHIFT_ROWS, rem)] = c_ref[0, 0, pl.ds(full * SHIFT_ROWS + 1, rem)]
        o_ref[0, 0, n - 1, 0] = z_ref[0, 3 + g]
        o_ref[0, 0, n - 1, 1] = z_ref[0, 6 + g]


def shift_caches(z4, caches):
    b = z4.shape[0]
    specs = [pl.BlockSpec((1, 1) + c.shape[2:], lambda i: (0, i, 0, 0, 0, 0)) for c in caches]
    return pl.pallas_call(
        _shift_kernel,
        grid=(b,),
        in_specs=[pl.BlockSpec((1,) + z4.shape[1:], lambda i: (i, 0, 0, 0))] + specs,
        out_specs=specs,
        out_shape=[jax.ShapeDtypeStruct(c.shape, c.dtype) for c in caches],
        compiler_params=_params("parallel"),
        name="shift_caches",
    )(z4, *caches)


def _conv_step_kernel(a_ref, b_ref, s_ref, w_ref, cb_ref, lg_ref, lb_ref, c_ref, so_ref):
    hist = CONV_K - 1
    u = a_ref[...] * jax.nn.sigmoid(b_ref[...])
    y = (jnp.sum(s_ref[0] * w_ref[0:hist, :], axis=1, keepdims=True)
         + u * w_ref[hist:hist + 1, :] + cb_ref[...])
    c_ref[...] = _layernorm_silu(y, lg_ref[...], lb_ref[...]).astype(c_ref.dtype)
    so_ref[0, :, pl.ds(0, hist - 1), :] = s_ref[0, :, pl.ds(1, hist - 1), :]
    so_ref[0, :, pl.ds(hist - 1, 1), :] = u


def conv_step(a, b, state, conv_w, conv_b, ln_g, ln_b):
    bsz = a.shape[0]
    return pl.pallas_call(
        _conv_step_kernel,
        out_shape=[jax.ShapeDtypeStruct((bsz, 1, D_CONV), BF16), jax.ShapeDtypeStruct(state.shape, F32)],
        compiler_params=pltpu.CompilerParams(vmem_limit_bytes=VMEM_LIMIT),
        name="conv_step",
    )(a, b, state, conv_w, conv_b.reshape(1, D_CONV), ln_g.reshape(1, D_CONV), ln_b.reshape(1, D_CONV))


def _trunk_tail(x1, xo_in, w_xo, norm_ffn_g, w_rg, b_rg, w_re, b_re, w_ein, w_eout, norm_final_g, bm, routed):
    x2 = matmul_residual(xo_in, w_xo, x1, min(bm, 512), bn=D_MODEL)
    if routed:
        return moe_routed(x2, norm_ffn_g, w_rg, b_rg, w_re, b_re, w_ein, w_eout, norm_final_g)
    xn3, gate = moe_router(x2, norm_ffn_g, w_rg, b_rg, w_re, b_re, bm)
    return moe_experts_final(xn3, gate, w_ein, w_eout, x2, norm_final_g, bm)


def kernel(x_prompt, x_sample, mem_prompt, cache_kv_g1, cache_kv_g2, cache_kv_g3, state_conv, cache_mem_kv,
           norm_mix_g, w_in, conv_w, conv_b, conv_ln_g, conv_ln_b, w_proj_a, w_proj_b, w_out,
           norm_xattn_g, norm_mem_g, w_xq, w_xkv, w_xo, norm_ffn_g,
           w_router_group, b_router_group, w_router_expert, b_router_expert, w_expert_in, w_expert_out,
           norm_final_g):
    depth = norm_mix_g.shape[0]
    assert depth == 1, "single-layer trunk"
    batch, seq, _ = x_prompt.shape
    dec_b, dec_t, _ = x_sample.shape
    assert dec_t == 1
    (g_mix, w_in, conv_w, conv_b, ln_g, ln_b, w_pa, w_pb, w_o, g_x, g_mem, w_xq, w_xkv, w_xo, g_ffn,
     w_rg, b_rg, w_re, b_re, w_ein, w_eout) = [t[0] for t in (
         norm_mix_g, w_in, conv_w, conv_b, conv_ln_g, conv_ln_b, w_proj_a, w_proj_b, w_out, norm_xattn_g,
         norm_mem_g, w_xq, w_xkv, w_xo, norm_ffn_g, w_router_group, b_router_group, w_router_expert,
         b_router_expert, w_expert_in, w_expert_out)]
    w_pa16, w_pb16, w_o16 = w_pa.astype(BF16), w_pb.astype(BF16), w_o.astype(BF16)
    w_ein, w_eout = w_ein.astype(BF16), w_eout.astype(BF16)
    w_za, w_gt = w_in[:, :ZA_W].astype(BF16), w_in[:, ZA_W:].astype(BF16)
    w_xq, w_xkv, w_xo = w_xq.astype(BF16), w_xkv.astype(BF16), w_xo.astype(BF16)
    caches = (cache_kv_g1, cache_kv_g2, cache_kv_g3)

    m_p = batch * seq
    xp = x_prompt.reshape(m_p, D_MODEL)
    za, gates = in_proj(xp, g_mix, w_za, w_gt, bm=1024)
    o_a = band_attention(za, batch, seq)
    x1, conv_prompt = conv_merge_branches(o_a, za, gates, xp, batch, seq, conv_w, conv_b, ln_g, ln_b,
                                          w_pa16, w_pb16, w_o16)
    mkv = norm_matmul(mem_prompt.reshape(batch * N_MEM, D_MODEL), g_mem, w_xkv, F32, bm=1024)
    q = norm_matmul(x1, g_x, w_xq, BF16, bm=512, bn=D_MODEL)
    xo_in = cross_attention_prompt(q, mkv, batch, seq)
    y_prompt = _trunk_tail(x1, xo_in, w_xo, g_ffn, w_rg, b_rg, w_re, b_re, w_ein, w_eout, norm_final_g,
                           bm=1024, routed=True)

    kv_prompt = window_tails(za, batch, seq)
    conv_prompt = conv_prompt[None]
    mem_kv_prompt = mkv.reshape(1, batch, N_MEM, 2, X_HEADS, X_HEAD_DIM)

    xs = x_sample.reshape(dec_b, D_MODEL)
    zs, gates_s = in_proj(xs, g_mix, w_za, w_gt, bm=dec_b)
    z4 = zs.reshape(dec_b, ZA_W // HEAD_DIM // HEADS, HEADS, HEAD_DIM)
    o_as = window_decode(z4, caches).reshape(dec_b, GROUP_W)
    kv_sample = shift_caches(z4, caches)
    a_s = zs[:, 3 * ATT_WIDTH:3 * ATT_WIDTH + D_CONV].reshape(dec_b, 1, D_CONV)
    b_s = zs[:, 3 * ATT_WIDTH + D_CONV:].reshape(dec_b, 1, D_CONV)
    c_s, conv_sample = conv_step(a_s, b_s, state_conv, conv_w, conv_b, ln_g, ln_b)
    x1s = merge_branches(o_as, c_s.reshape(dec_b, D_CONV), gates_s, xs, w_pa16, w_pb16, w_o16, tm=dec_b)
    q_s = norm_matmul(x1s, g_x, w_xq, BF16, bm=dec_b)
    xo_s = cross_attention_decode(q_s, cache_mem_kv)
    y_sample = _trunk_tail(x1s, xo_s, w_xo, g_ffn, w_rg, b_rg, w_re, b_re, w_ein, w_eout, norm_final_g,
                           bm=dec_b, routed=False)

    return (y_prompt.reshape(batch, seq, D_MODEL), y_sample.reshape(dec_b, 1, D_MODEL),
            kv_prompt[0], kv_prompt[1], kv_prompt[2], conv_prompt, mem_kv_prompt,
            kv_sample[0], kv_sample[1], kv_sample[2], conv_sample)
```

```python
import functools

import jax
import jax.numpy as jnp
from jax import lax
from jax.experimental import pallas as pl
from jax.experimental.pallas import tpu as pltpu

F32 = jnp.float32
BF16 = jnp.bfloat16

D_MODEL = 2048
ATT_GROUPS = ((128, 1), (512, 4), (2048, 16))
HEADS = 4
HEAD_DIM = 128
GROUP_W = HEADS * HEAD_DIM
ATT_WIDTH = len(ATT_GROUPS) * GROUP_W
D_CONV = 1536
CONV_K = 31
N_MEM = 256
X_HEADS = 4
X_HEAD_DIM = D_MODEL // X_HEADS
N_EXPERT_GROUPS = 4
EXPERTS_PER_GROUP = 4
N_EXPERTS = 16
D_EXPERT = 512
ZA_W = 3 * ATT_WIDTH + 2 * D_CONV
GATE_W = 2 * D_MODEL
RMS_EPS = 1e-6
LN_EPS = 1e-5
NEG_INF = -1e30
BAND = 128
LANES = 128
ROUTER_GROUP_LANE = N_EXPERTS

V7X_VMEM_BYTES = 64 * 1024 * 1024
VMEM_LIMIT = V7X_VMEM_BYTES * 7 // 8


def _params(*sem):
    return pltpu.CompilerParams(dimension_semantics=sem, vmem_limit_bytes=VMEM_LIMIT)


def _rms_rows(x, g):
    ms = jnp.mean(x * x, axis=-1, keepdims=True)
    return x * lax.rsqrt(ms + RMS_EPS) * g


def _store_normed(x_ref, g_ref, xn_ref, chunk=256):
    rows = x_ref.shape[0]
    step = min(chunk, rows)
    for r0 in range(0, rows, step):
        xn_ref[r0:r0 + step, :] = _rms_rows(x_ref[r0:r0 + step, :], g_ref[...]).astype(xn_ref.dtype)


def _norm_matmul_kernel(x_ref, g_ref, w_ref, o_ref, xn_ref):
    @pl.when(pl.program_id(1) == 0)
    def _():
        _store_normed(x_ref, g_ref, xn_ref)

    o_ref[...] = jnp.dot(xn_ref[...], w_ref[...], preferred_element_type=F32).astype(o_ref.dtype)


def norm_matmul(x, g, w, out_dtype, bm, bn=512):
    m, k = x.shape
    n = w.shape[1]
    bm = min(bm, m)
    return pl.pallas_call(
        _norm_matmul_kernel,
        grid=(m // bm, n // bn),
        in_specs=[pl.BlockSpec((bm, k), lambda i, j: (i, 0)),
                  pl.BlockSpec((1, k), lambda i, j: (0, 0)),
                  pl.BlockSpec((k, bn), lambda i, j: (0, j))],
        out_specs=pl.BlockSpec((bm, bn), lambda i, j: (i, j)),
        out_shape=jax.ShapeDtypeStruct((m, n), out_dtype),
        scratch_shapes=[pltpu.VMEM((bm, k), BF16)],
        compiler_params=_params("parallel", "arbitrary"),
        name="norm_matmul",
    )(x, g.reshape(1, k), w)


def _norm_matmul_keep_kernel(x_hbm, g_ref, w_ref, o_ref, xn_ref, xbuf, sems):
    i = pl.program_id(0)
    bm = xn_ref.shape[0]

    def tile_copy(t, slot):
        rows = pl.ds(pl.multiple_of(t * bm, 8), bm)
        return pltpu.make_async_copy(x_hbm.at[rows], xbuf.at[slot], sems.at[slot])

    @pl.when(pl.program_id(1) == 0)
    def _():
        slot = i % 2

        @pl.when(i == 0)
        def _():
            tile_copy(0, 0).start()

        @pl.when(i + 1 < pl.num_programs(0))
        def _():
            tile_copy(i + 1, 1 - slot).start()

        tile_copy(i, slot).wait()
        _store_normed(xbuf.at[slot], g_ref, xn_ref)

    o_ref[...] = jnp.dot(xn_ref[...], w_ref[...], preferred_element_type=F32)


def _matmul_kernel(x_ref, w_ref, o_ref):
    o_ref[...] = jnp.dot(x_ref[...], w_ref[...], preferred_element_type=F32)


ZA_BN = 1536


def in_proj(x, g, w_za, w_gt, bm):
    m, k = x.shape
    bm = min(bm, m)
    za, xn = pl.pallas_call(
        _norm_matmul_keep_kernel,
        grid=(m // bm, ZA_W // ZA_BN),
        in_specs=[pl.BlockSpec(memory_space=pl.ANY),
                  pl.BlockSpec((1, k), lambda i, j: (0, 0)),
                  pl.BlockSpec((k, ZA_BN), lambda i, j: (0, j))],
        out_specs=[pl.BlockSpec((bm, ZA_BN), lambda i, j: (i, j)),
                   pl.BlockSpec((bm, k), lambda i, j: (i, 0))],
        out_shape=[jax.ShapeDtypeStruct((m, ZA_W), F32), jax.ShapeDtypeStruct((m, k), BF16)],
        scratch_shapes=[pltpu.VMEM((2, bm, k), F32), pltpu.SemaphoreType.DMA((2,))],
        compiler_params=_params("arbitrary", "arbitrary"),
        name="in_proj_za",
    )(x, g.reshape(1, k), w_za)
    bn = GATE_W // 2
    gates = pl.pallas_call(
        _matmul_kernel,
        grid=(m // bm, GATE_W // bn),
        in_specs=[pl.BlockSpec((bm, k), lambda i, j: (i, 0)),
                  pl.BlockSpec((k, bn), lambda i, j: (0, j))],
        out_specs=pl.BlockSpec((bm, bn), lambda i, j: (i, j)),
        out_shape=jax.ShapeDtypeStruct((m, GATE_W), F32),
        compiler_params=_params("parallel", "arbitrary"),
        name="in_proj_gates",
    )(xn, w_gt)
    return za, gates


def _matmul_res_kernel(x_ref, w_ref, r_ref, o_ref):
    o_ref[...] = r_ref[...] + jnp.dot(x_ref[...], w_ref[...], preferred_element_type=F32)


def matmul_residual(x, w, res, bm, bn=512):
    m, k = x.shape
    n = w.shape[1]
    bm = min(bm, m)
    return pl.pallas_call(
        _matmul_res_kernel,
        grid=(m // bm, n // bn),
        in_specs=[pl.BlockSpec((bm, k), lambda i, j: (i, 0)),
                  pl.BlockSpec((k, bn), lambda i, j: (0, j)),
                  pl.BlockSpec((bm, bn), lambda i, j: (i, j))],
        out_specs=pl.BlockSpec((bm, bn), lambda i, j: (i, j)),
        out_shape=jax.ShapeDtypeStruct((m, n), F32),
        compiler_params=_params("parallel", "arbitrary"),
        name="matmul_residual",
    )(x, w, res)


ATT_R = 2048
ATT_UNROLL = 16


def _band_attn_kernel(*refs):
    n_g = len(ATT_GROUPS)
    in_refs = refs[:5 * n_g]
    o_ref = refs[5 * n_g]
    scr = refs[5 * n_g + 1:]
    kbufs, vbufs = scr[:n_g], scr[n_g:2 * n_g]
    obufs, mbufs, dbufs = scr[2 * n_g:3 * n_g], scr[3 * n_g:4 * n_g], scr[4 * n_g:5 * n_g]
    first_chunk = pl.program_id(1) == 0
    row = lax.broadcasted_iota(jnp.int32, (BAND, 2 * BAND), 0)
    col = lax.broadcasted_iota(jnp.int32, (BAND, 2 * BAND), 1)
    in_band = (col >= row) & (col <= row + BAND)
    in_cur = col >= BAND
    scale = HEAD_DIM ** -0.5
    nt = (((1,), (1,)), ((), ()))

    for g, (win, dil) in enumerate(ATT_GROUPS):
        q_ref, kc_ref, vc_ref, kp_ref, vp_ref = in_refs[5 * g:5 * g + 5]
        kbuf, vbuf, obuf, mbuf, dbuf = kbufs[g], vbufs[g], obufs[g], mbufs[g], dbufs[g]
        kbuf[0:win, :] = kp_ref[...]
        kbuf[win:win + ATT_R, :] = kc_ref[...]
        vbuf[0:win, :] = vp_ref[...]
        vbuf[win:win + ATT_R, :] = vc_ref[...]
        shift = dil.bit_length() - 1

        def sub_blocks(i4, carry, q_ref=q_ref, kbuf=kbuf, vbuf=vbuf, obuf=obuf, mbuf=mbuf, dbuf=dbuf,
                       win=win, dil=dil, shift=shift):
            done = []
            for u in range(ATT_UNROLL):
                i = i4 * ATT_UNROLL + u
                span = lax.shift_right_logical(i, shift)
                base = span * win + (i & (dil - 1))
                if dil == 1:
                    q_rows, kv_rows = pl.ds(base, BAND), pl.ds(base, 2 * BAND)
                else:
                    q_rows, kv_rows = pl.ds(base, BAND, stride=dil), pl.ds(base, 2 * BAND, stride=dil)
                q = q_ref[q_rows, :].astype(BF16)
                k = kbuf[kv_rows, :].astype(BF16)
                v = vbuf[kv_rows, :].astype(BF16)
                s = lax.dot_general(q, k, nt, preferred_element_type=F32) * scale
                has_prev = jnp.logical_not(first_chunk & (span == 0))
                s = jnp.where(in_band & (in_cur | has_prev), s, NEG_INF)
                m = jnp.max(s, axis=-1, keepdims=True)
                p = jnp.exp(s - m)
                den = jnp.sum(p, axis=-1, keepdims=True)
                done.append((q_rows, jnp.dot(p.astype(BF16), v, preferred_element_type=F32), m, den))
            for q_rows, o, m, den in done:
                obuf[q_rows, :] = o
                mbuf[q_rows, :] = jnp.broadcast_to(m, (BAND, HEAD_DIM))
                dbuf[q_rows, :] = jnp.broadcast_to(den, (BAND, HEAD_DIM))
            return carry

        lax.fori_loop(0, ATT_R // BAND // ATT_UNROLL, sub_blocks, 0)

    chunk = 256
    for r0 in range(0, ATT_R, chunk):
        rs = slice(r0, r0 + chunk)
        ms = [mb[rs, :] for mb in mbufs]
        m_all = functools.reduce(jnp.maximum, ms)
        ws = [jnp.exp(m - m_all) for m in ms]
        num = sum(w * ob[rs, :] for w, ob in zip(ws, obufs))
        den = sum(w * db[rs, :] for w, db in zip(ws, dbufs))
        o_ref[rs, :] = (num / den).astype(o_ref.dtype)


def band_attention(za, batch, seq):
    assert seq % ATT_R == 0
    nch = seq // ATT_R
    in_specs, scratch = [], []
    for g, (win, dil) in enumerate(ATT_GROUPS):
        assert win // dil == BAND and ATT_R % win == 0
        cols = [(part * len(ATT_GROUPS) + g) * HEADS for part in range(3)]
        cur = lambda c: pl.BlockSpec((ATT_R, HEAD_DIM), lambda b, ch, h, c=c: (b * nch + ch, c + h))
        prev = lambda c, win=win: pl.BlockSpec(
            (win, HEAD_DIM), lambda b, ch, h, c=c, win=win: (jnp.maximum((b * seq + ch * ATT_R) // win - 1, 0), c + h))
        in_specs += [cur(cols[0]), cur(cols[1]), cur(cols[2]), prev(cols[1]), prev(cols[2])]
    for _ in range(2):
        scratch += [pltpu.VMEM((win + ATT_R, HEAD_DIM), F32) for win, _ in ATT_GROUPS]
    scratch += [pltpu.VMEM((ATT_R, HEAD_DIM), F32)] * (3 * len(ATT_GROUPS))
    return pl.pallas_call(
        _band_attn_kernel,
        grid=(batch, nch, HEADS),
        in_specs=in_specs,
        out_specs=pl.BlockSpec((ATT_R, HEAD_DIM), lambda b, ch, h: (b * nch + ch, h)),
        out_shape=jax.ShapeDtypeStruct((batch * seq, GROUP_W), BF16),
        scratch_shapes=scratch,
        compiler_params=_params("parallel", "parallel", "parallel"),
        name="band_attention",
    )(*([za] * (5 * len(ATT_GROUPS))))


def _window_tails_kernel(*refs):
    n_g = len(ATT_GROUPS)
    slabs = 2 * HEADS
    for g in range(n_g):
        k_ref, v_ref, o_ref = refs[2 * g], refs[2 * g + 1], refs[2 * n_g + g]
        rows = k_ref.shape[0]
        for part, src in enumerate((k_ref, v_ref)):
            for h in range(HEADS):
                o_ref[pl.ds(part * HEADS + h, rows, stride=slabs), :] = src[:, h * HEAD_DIM:(h + 1) * HEAD_DIM]


def window_tails(za, batch, seq):
    in_specs, out_specs, out_shape = [], [], []
    slabs = 2 * HEADS
    for g, (win, _) in enumerate(ATT_GROUPS):
        assert win <= seq and seq % win == 0
        last = seq // win - 1
        for part in (1, 2):
            in_specs.append(pl.BlockSpec(
                (win, GROUP_W), lambda b, g=g, part=part, win=win, last=last:
                (b * (seq // win) + last, part * len(ATT_GROUPS) + g)))
        out_specs.append(pl.BlockSpec((win * slabs, HEAD_DIM), lambda b: (b, 0)))
        out_shape.append(jax.ShapeDtypeStruct((batch * win * slabs, HEAD_DIM), F32))
    outs = pl.pallas_call(
        _window_tails_kernel,
        grid=(batch,),
        in_specs=in_specs,
        out_specs=out_specs,
        out_shape=out_shape,
        compiler_params=_params("parallel"),
        name="window_tails",
    )(*([za] * (2 * len(ATT_GROUPS))))
    return [o.reshape(1, batch, win, 2, HEADS, HEAD_DIM) for o, (win, _) in zip(outs, ATT_GROUPS)]


CONV_T = 256
CONV_HIST = 32
CONV_RC = 128
CONV_PHASES = 4
CONV_SLABS = D_CONV // LANES


def _layernorm_silu(y, g, b):
    mu = jnp.mean(y, axis=-1, keepdims=True)
    yc = y - mu
    var = jnp.mean(yc * yc, axis=-1, keepdims=True)
    yn = yc * lax.rsqrt(var + LN_EPS) * g + b
    return yn * jax.nn.sigmoid(yn)


def _conv_history(t, ubuf):
    @pl.when(t == 0)
    def _():
        ubuf[:, 0:CONV_HIST, :] = jnp.zeros((CONV_SLABS, CONV_HIST, LANES), F32)

    @pl.when(t > 0)
    def _():
        ubuf[:, 0:CONV_HIST, :] = ubuf[:, CONV_T:CONV_T + CONV_HIST, :]


def _conv_slab(j, a_ref, b_ref, w_ref, cb_ref, tail_ref, ubuf, ybuf):
    first = CONV_HIST - (CONV_K - 1)
    n = CONV_RC // CONV_PHASES
    js = slice(j * LANES, (j + 1) * LANES)
    u = a_ref[:, js] * jax.nn.sigmoid(b_ref[:, js])
    ubuf[j, CONV_HIST:CONV_HIST + CONV_T, :] = u
    tail_ref[0, :, js] = u[CONV_T - CONV_HIST:, :]
    for r0 in range(0, CONV_T, CONV_RC):
        accs = [jnp.broadcast_to(cb_ref[:, js], (n, LANES))] * CONV_PHASES
        for s in range(first, first + CONV_K + CONV_PHASES - 1):
            rows = ubuf[j, pl.ds(r0 + s, n, stride=CONV_PHASES), :]
            for p in range(CONV_PHASES):
                k = s - first - p
                if 0 <= k < CONV_K:
                    accs[p] = accs[p] + w_ref[k:k + 1, js] * rows
        for p in range(CONV_PHASES):
            ybuf[j, pl.ds(r0 + p, n, stride=CONV_PHASES), :] = accs[p]
    return jnp.sum(ybuf[j], axis=-1, keepdims=True)


def _conv_centered_sq(j, mu, ybuf):
    yc = ybuf[j] - mu
    return jnp.sum(yc * yc, axis=-1, keepdims=True)


def _conv_norm_slab(j, mu, inv, lg_ref, lb_ref, c_ref, ybuf):
    js = slice(j * LANES, (j + 1) * LANES)
    yn = (ybuf[j] - mu) * inv * lg_ref[:, js] + lb_ref[:, js]
    c_ref[:, js] = (yn * jax.nn.sigmoid(yn)).astype(c_ref.dtype)


def _merge_kernel(oa_ref, c_ref, ga_ref, gb_ref, x_ref, wpa_ref, wpb_ref, wo_ref, y_ref):
    ta = jnp.dot(oa_ref[...], wpa_ref[...], preferred_element_type=F32)
    tb = jnp.dot(c_ref[...], wpb_ref[...], preferred_element_type=F32)
    hmix = jax.nn.sigmoid(ga_ref[...]) * ta + jax.nn.sigmoid(gb_ref[...]) * tb
    y_ref[...] = x_ref[...] + jnp.dot(hmix.astype(BF16), wo_ref[...], preferred_element_type=F32)


MERGE_AHEAD = 2


def _conv_merge_kernel(oa_ref, a_ref, b_ref, gates_hbm, x_hbm, cw_ref, cb_ref, lg_ref, lb_ref,
                       wpa_ref, wpb_ref, wo_ref, y_ref, tail_ref, ubuf, ybuf, c_scr, gbuf, xbuf, sems):
    n_buf = MERGE_AHEAD + 1
    step = pl.program_id(0) * pl.num_programs(1) + pl.program_id(1)
    n_steps = pl.num_programs(0) * pl.num_programs(1)

    def tile_copies(tile, slot):
        rows = pl.ds(pl.multiple_of(tile * CONV_T, 8), CONV_T)
        return (pltpu.make_async_copy(gates_hbm.at[rows], gbuf.at[slot], sems.at[0, slot]),
                pltpu.make_async_copy(x_hbm.at[rows], xbuf.at[slot], sems.at[1, slot]))

    @pl.when(step == 0)
    def _():
        for a in range(MERGE_AHEAD):
            @pl.when(a < n_steps)
            def _():
                for cp in tile_copies(a, a):
                    cp.start()

    @pl.when(step + MERGE_AHEAD < n_steps)
    def _():
        for cp in tile_copies(step + MERGE_AHEAD, (step + MERGE_AHEAD) % n_buf):
            cp.start()

    slot = step % n_buf
    _conv_history(pl.program_id(1), ubuf)
    total = jnp.zeros((CONV_T, 1), F32)
    for j in range(CONV_SLABS):
        total = total + _conv_slab(j, a_ref, b_ref, cw_ref, cb_ref, tail_ref, ubuf, ybuf)
    mu = total * (1.0 / D_CONV)
    sq = jnp.zeros((CONV_T, 1), F32)
    for j in range(CONV_SLABS):
        sq = sq + _conv_centered_sq(j, mu, ybuf)
    inv = lax.rsqrt(sq * (1.0 / D_CONV) + LN_EPS)
    for j in range(CONV_SLABS):
        _conv_norm_slab(j, mu, inv, lg_ref, lb_ref, c_scr, ybuf)
    for cp in tile_copies(step, slot):
        cp.wait()
    gates = gbuf.at[slot]
    ta = jnp.dot(oa_ref[...], wpa_ref[...], preferred_element_type=F32)
    tb = jnp.dot(c_scr[...], wpb_ref[...], preferred_element_type=F32)
    hmix = jax.nn.sigmoid(gates[:, 0:D_MODEL]) * ta + jax.nn.sigmoid(gates[:, D_MODEL:GATE_W]) * tb
    y_ref[...] = xbuf[slot] + jnp.dot(hmix.astype(BF16), wo_ref[...], preferred_element_type=F32)


def conv_merge_branches(o_a, za, gates, x, batch, seq, conv_w, conv_b, ln_g, ln_b, w_pa, w_pb, w_out):
    nt = seq // CONV_T
    row = lambda b, t: (b * nt + t, 0)
    const = lambda b, t: (0, 0)
    once = dict(pipeline_mode=pl.Buffered(1))
    glu = 3 * ATT_WIDTH // D_CONV
    y, tail = pl.pallas_call(
        _conv_merge_kernel,
        grid=(batch, nt),
        in_specs=[pl.BlockSpec((CONV_T, GROUP_W), row),
                  pl.BlockSpec((CONV_T, D_CONV), lambda b, t: (b * nt + t, glu)),
                  pl.BlockSpec((CONV_T, D_CONV), lambda b, t: (b * nt + t, glu + 1)),
                  pl.BlockSpec(memory_space=pl.ANY),
                  pl.BlockSpec(memory_space=pl.ANY),
                  pl.BlockSpec((CONV_K, D_CONV), const),
                  pl.BlockSpec((1, D_CONV), const),
                  pl.BlockSpec((1, D_CONV), const),
                  pl.BlockSpec((1, D_CONV), const),
                  pl.BlockSpec((GROUP_W, D_MODEL), const, **once),
                  pl.BlockSpec((D_CONV, D_MODEL), const, **once),
                  pl.BlockSpec((D_MODEL, D_MODEL), const, **once)],
        out_specs=[pl.BlockSpec((CONV_T, D_MODEL), row),
                   pl.BlockSpec((1, CONV_HIST, D_CONV), lambda b, t: (b, 0, 0))],
        out_shape=[jax.ShapeDtypeStruct((batch * seq, D_MODEL), F32),
                   jax.ShapeDtypeStruct((batch, CONV_HIST, D_CONV), F32)],
        scratch_shapes=[pltpu.VMEM((CONV_SLABS, CONV_HIST + CONV_T, LANES), F32),
                        pltpu.VMEM((CONV_SLABS, CONV_T, LANES), F32),
                        pltpu.VMEM((CONV_T, D_CONV), BF16),
                        pltpu.VMEM((MERGE_AHEAD + 1, CONV_T, GATE_W), F32),
                        pltpu.VMEM((MERGE_AHEAD + 1, CONV_T, D_MODEL), F32),
                        pltpu.SemaphoreType.DMA((2, MERGE_AHEAD + 1))],
        compiler_params=_params("arbitrary", "arbitrary"),
        name="conv_merge_branches",
    )(o_a, za, za, gates, x, conv_w, conv_b.reshape(1, D_CONV), ln_g.reshape(1, D_CONV),
      ln_b.reshape(1, D_CONV), w_pa, w_pb, w_out)
    return y, tail[:, CONV_HIST - (CONV_K - 1):]


def merge_branches(o_a, c, gates, x, w_pa, w_pb, w_out, tm):
    m = x.shape[0]
    tm = min(tm, m)
    row = lambda i: (i, 0)
    const = lambda i: (0, 0)
    once = dict(pipeline_mode=pl.Buffered(1))
    return pl.pallas_call(
        _merge_kernel,
        grid=(m // tm,),
        in_specs=[
            pl.BlockSpec((tm, GROUP_W), row),
            pl.BlockSpec((tm, D_CONV), row),
            pl.BlockSpec((tm, D_MODEL), lambda i: (i, 0)),
            pl.BlockSpec((tm, D_MODEL), lambda i: (i, 1)),
            pl.BlockSpec((tm, D_MODEL), row),
            pl.BlockSpec((GROUP_W, D_MODEL), const, **once),
            pl.BlockSpec((D_CONV, D_MODEL), const, **once),
            pl.BlockSpec((D_MODEL, D_MODEL), const, **once)],
        out_specs=pl.BlockSpec((tm, D_MODEL), row),
        out_shape=jax.ShapeDtypeStruct((m, D_MODEL), F32),
        compiler_params=_params("parallel"),
        name="merge_branches",
    )(o_a, c, gates, gates, x, w_pa, w_pb, w_out)


def _xattn_kernel(q_ref, kv_ref, o_ref):
    scale = X_HEAD_DIM ** -0.5
    nt = (((1,), (1,)), ((), ()))
    for h in range(X_HEADS):
        hs = slice(h * X_HEAD_DIM, (h + 1) * X_HEAD_DIM)
        vs = slice(D_MODEL + h * X_HEAD_DIM, D_MODEL + (h + 1) * X_HEAD_DIM)
        s = lax.dot_general(q_ref[:, hs], kv_ref[:, hs].astype(BF16), nt, preferred_element_type=F32) * scale
        m = jnp.max(s, axis=-1, keepdims=True)
        p = jnp.exp(s - m)
        p = p / jnp.sum(p, axis=-1, keepdims=True)
        o = jnp.dot(p.astype(BF16), kv_ref[:, vs].astype(BF16), preferred_element_type=F32)
        o_ref[:, hs] = o.astype(o_ref.dtype)


def cross_attention_prompt(q, mkv, batch, seq, tm=512):
    nt = seq // tm
    return pl.pallas_call(
        _xattn_kernel,
        grid=(batch, nt),
        in_specs=[pl.BlockSpec((tm, D_MODEL), lambda b, t: (b * nt + t, 0)),
                  pl.BlockSpec((N_MEM, 2 * D_MODEL), lambda b, t: (b, 0))],
        out_specs=pl.BlockSpec((tm, D_MODEL), lambda b, t: (b * nt + t, 0)),
        out_shape=jax.ShapeDtypeStruct((batch * seq, D_MODEL), BF16),
        compiler_params=_params("parallel", "arbitrary"),
        name="cross_attention_prompt",
    )(q, mkv)


def _xattn_decode_kernel(q_ref, kv_ref, o_ref):
    scale = X_HEAD_DIM ** -0.5
    q = q_ref[0].astype(F32)
    k = kv_ref[0, 0, :, 0]
    v = kv_ref[0, 0, :, 1]
    s = jnp.sum(k * q, axis=-1, keepdims=True) * scale
    m = jnp.max(s, axis=0, keepdims=True)
    p = jnp.exp(s - m)
    p = p / jnp.sum(p, axis=0, keepdims=True)
    o_ref[0] = jnp.sum(p * v, axis=0, keepdims=True).astype(o_ref.dtype)


def cross_attention_decode(q, cache_mem_kv):
    b = q.shape[0]
    q4 = q.reshape(b, 1, X_HEADS, X_HEAD_DIM)
    o = pl.pallas_call(
        _xattn_decode_kernel,
        grid=(b,),
        in_specs=[pl.BlockSpec((1, 1, X_HEADS, X_HEAD_DIM), lambda i: (i, 0, 0, 0)),
                  pl.BlockSpec((1, 1, N_MEM, 2, X_HEADS, X_HEAD_DIM), lambda i: (0, i, 0, 0, 0, 0))],
        out_specs=pl.BlockSpec((1, 1, X_HEADS, X_HEAD_DIM), lambda i: (i, 0, 0, 0)),
        out_shape=jax.ShapeDtypeStruct((b, 1, X_HEADS, X_HEAD_DIM), BF16),
        compiler_params=_params("parallel"),
        name="cross_attention_decode",
    )(q4, cache_mem_kv)
    return o.reshape(b, D_MODEL)


_PAIRS = [(lo, hi) for lo in range(EXPERTS_PER_GROUP) for hi in range(lo + 1, EXPERTS_PER_GROUP)]
PAIRS_PER_GROUP = len(_PAIRS)
N_BUCKETS = N_EXPERT_GROUPS * PAIRS_PER_GROUP
PAIR_LO = tuple(lo for lo, _ in _PAIRS)
PAIR_HI = tuple(hi for _, hi in _PAIRS)
ROW_TILES = D_MODEL // LANES
PAYLOAD_ROWS = ROW_TILES + 8
MOE_TM = 256


def _route(logits):
    lane = lax.broadcasted_iota(jnp.int32, logits.shape, 1)
    lanef = lane.astype(F32)
    big = float(LANES)
    is_group = (lane >= ROUTER_GROUP_LANE) & (lane < ROUTER_GROUP_LANE + N_EXPERT_GROUPS)
    lg = jnp.where(is_group, logits, -jnp.inf)
    mg = jnp.max(lg, axis=-1, keepdims=True)
    p_sel = 1.0 / jnp.sum(jnp.exp(lg - mg), axis=-1, keepdims=True)
    gsel = jnp.min(jnp.where(lg == mg, lanef, big), axis=-1, keepdims=True) - ROUTER_GROUP_LANE
    group_of_lane = lax.shift_right_logical(lane, EXPERTS_PER_GROUP.bit_length() - 1)
    in_group = (lane < N_EXPERTS) & (group_of_lane == gsel.astype(jnp.int32))
    le = jnp.where(in_group, logits, -jnp.inf)
    v1 = jnp.max(le, axis=-1, keepdims=True)
    i1 = jnp.min(jnp.where(le == v1, lanef, big), axis=-1, keepdims=True)
    le2 = jnp.where(lanef == i1, -jnp.inf, le)
    v2 = jnp.max(le2, axis=-1, keepdims=True)
    i2 = jnp.min(jnp.where(le2 == v2, lanef, big), axis=-1, keepdims=True)
    t = jnp.exp(v2 - v1)
    tot = 1.0 + t
    return gsel, i1, i2, (1.0 / tot) * p_sel, (t / tot) * p_sel


def _router_routed_kernel(x_ref, g_ref, w_ref, b_ref, xd_ref, meta_ref, cnt_ref, carry_ref):
    tm = x_ref.shape[0]

    @pl.when(pl.program_id(0) == 0)
    def _():
        carry_ref[...] = jnp.zeros_like(carry_ref)

    xn = _rms_rows(x_ref[...], g_ref[...])
    logits = _router_logits(xn, w_ref, b_ref)
    gsel, i1, i2, p1, p2 = _route(logits)
    lo = jnp.minimum(i1, i2) - EXPERTS_PER_GROUP * gsel
    hi = jnp.maximum(i1, i2) - EXPERTS_PER_GROUP * gsel
    pair = lo * (2.0 * EXPERTS_PER_GROUP - 1.0 - lo) * 0.5 + (hi - lo - 1.0)
    bucket = gsel * float(PAIRS_PER_GROUP) + pair
    gate_lo = jnp.where(i1 < i2, p1, p2)
    gate_hi = jnp.where(i1 < i2, p2, p1)

    lane = lax.broadcasted_iota(jnp.int32, (tm, LANES), 1)
    onehot = (lane.astype(F32) == bucket).astype(F32)
    r_i = lax.broadcasted_iota(jnp.int32, (tm, tm), 0)
    c_i = lax.broadcasted_iota(jnp.int32, (tm, tm), 1)
    before = (c_i < r_i).astype(BF16)
    rank_local = jnp.dot(before, onehot.astype(BF16), preferred_element_type=F32)
    rank = jnp.sum(onehot * (rank_local + carry_ref[0:1, :]), axis=-1, keepdims=True)
    carry_ref[0:1, :] = carry_ref[0:1, :] + jnp.sum(onehot, axis=0, keepdims=True)
    cnt_ref[...] = jnp.broadcast_to(carry_ref[0:1, :], cnt_ref.shape)
    meta_ref[...] = jnp.where(lane == 0, bucket, jnp.where(lane == 1, rank, 0.0))

    for j in range(ROW_TILES):
        xd_ref[pl.ds(j, tm, stride=PAYLOAD_ROWS), :] = xn[:, j * LANES:(j + 1) * LANES]
    xd_ref[pl.ds(ROW_TILES, tm, stride=PAYLOAD_ROWS), :] = jnp.where(
        lane == 0, gate_lo, jnp.where(lane == 1, gate_hi, 0.0))
    for j in range(ROW_TILES + 1, PAYLOAD_ROWS):
        xd_ref[pl.ds(j, tm, stride=PAYLOAD_ROWS), :] = jnp.zeros((tm, LANES), F32)


def _router_weights(w_rg, b_rg, w_re, b_re):
    pad = LANES - N_EXPERTS - N_EXPERT_GROUPS
    w = jnp.concatenate([w_re, w_rg, jnp.zeros((D_MODEL, pad), F32)], axis=1)
    b = jnp.concatenate([b_re, b_rg, jnp.zeros((pad,), F32)]).reshape(1, LANES)
    hi = w.astype(BF16)
    lo = (w - hi.astype(F32)).astype(BF16)
    return jnp.stack([hi, lo]), b


def _router_logits(xn, w_ref, b_ref):
    hi = xn.astype(BF16)
    lo = (xn - hi.astype(F32)).astype(BF16)
    acc = jnp.dot(hi, w_ref[0], preferred_element_type=F32)
    acc = acc + jnp.dot(lo, w_ref[0], preferred_element_type=F32)
    acc = acc + jnp.dot(hi, w_ref[1], preferred_element_type=F32)
    return acc + b_ref[...]


def moe_router_routed(x, g, w_rg, b_rg, w_re, b_re, tm=512):
    m = x.shape[0]
    w, b = _router_weights(w_rg, b_rg, w_re, b_re)
    row = lambda i: (i, 0)
    const = lambda i: (0, 0)
    return pl.pallas_call(
        _router_routed_kernel,
        grid=(m // tm,),
        in_specs=[pl.BlockSpec((tm, D_MODEL), row), pl.BlockSpec((1, D_MODEL), const),
                  pl.BlockSpec((2, D_MODEL, LANES), lambda i: (0, 0, 0)), pl.BlockSpec((1, LANES), const)],
        out_specs=[pl.BlockSpec((tm * PAYLOAD_ROWS, LANES), row), pl.BlockSpec((tm, LANES), row),
                   pl.BlockSpec((8, LANES), const)],
        out_shape=[jax.ShapeDtypeStruct((m * PAYLOAD_ROWS, LANES), F32), jax.ShapeDtypeStruct((m, LANES), F32),
                   jax.ShapeDtypeStruct((8, LANES), F32)],
        scratch_shapes=[pltpu.VMEM((8, LANES), F32)],
        compiler_params=_params("arbitrary"),
        name="moe_router_routed",
    )(x, g.reshape(1, D_MODEL), w, b)


INV_CHUNK = 2048


def _slot_maps_kernel(bucket_ref, rank_ref, first_ref, dest_ref, inv_ref):
    step = pl.program_id(0)

    @pl.when(step == 0)
    def _():
        def clear(k, c):
            inv_ref[k] = 0
            return c
        lax.fori_loop(0, inv_ref.shape[0], clear, 0, unroll=8)

    def put(t, c):
        slot = first_ref[bucket_ref[0, 0, t]] + rank_ref[0, 0, t]
        dest_ref[0, 0, t] = slot
        inv_ref[slot] = step * INV_CHUNK + t
        return c
    lax.fori_loop(0, INV_CHUNK, put, 0, unroll=8)


def slot_maps(bucket, rank, first_slot, n_slots):
    m = bucket.shape[0]
    nc = m // INV_CHUNK
    smem = dict(memory_space=pltpu.SMEM)
    chunk = pl.BlockSpec((1, 1, INV_CHUNK), lambda i: (i, 0, 0), **smem)
    dest, inv = pl.pallas_call(
        _slot_maps_kernel,
        grid=(nc,),
        in_specs=[chunk, chunk, pl.BlockSpec(**smem)],
        out_specs=[chunk, pl.BlockSpec(**smem)],
        out_shape=[jax.ShapeDtypeStruct((nc, 1, INV_CHUNK), jnp.int32),
                   jax.ShapeDtypeStruct((n_slots,), jnp.int32)],
        compiler_params=_params("arbitrary"),
        name="slot_maps",
    )(bucket.reshape(nc, 1, INV_CHUNK), rank.reshape(nc, 1, INV_CHUNK), first_slot)
    return dest.reshape(m), inv


def _start_row_gather(idx_ref, src_hbm, dst, sem, n_items, rows):
    def start(r2, c):
        for queue in range(2):
            r = 2 * r2 + queue
            first = pl.multiple_of(idx_ref[0, 0, r] * rows, 8)
            pltpu.make_async_copy(src_hbm.at[pl.ds(first, rows)], dst.at[pl.ds(r * rows, rows)],
                                  sem).start(priority=queue)
        return c
    lax.fori_loop(0, n_items // 2, start, 0, unroll=4)


def _wait_row_gather(src_hbm, dst, sem, n_items, rows):
    pltpu.make_async_copy(src_hbm.at[pl.ds(0, n_items * rows)], dst, sem).wait()


GATHER_AHEAD = 3


def _moe_routed_kernel(ea_ref, eb_ref, nv_ref, *refs):
    del ea_ref, eb_ref
    inv_refs = refs[:GATHER_AHEAD + 1]
    xd_hbm, w1a_ref, w1b_ref, w2a_ref, w2b_ref, ys_ref, xbuf, sems, x_scr = refs[GATHER_AHEAD + 1:]
    n_buf = GATHER_AHEAD + 1
    i = pl.program_id(0)
    n_valid = nv_ref[0]
    slot = i % n_buf
    tm = MOE_TM

    @pl.when(i == 0)
    def _():
        for a in range(GATHER_AHEAD):
            @pl.when(a < n_valid)
            def _():
                _start_row_gather(inv_refs[a], xd_hbm, xbuf.at[a], sems.at[a], tm, PAYLOAD_ROWS)

    @pl.when(i + GATHER_AHEAD < n_valid)
    def _():
        ahead = (i + GATHER_AHEAD) % n_buf
        _start_row_gather(inv_refs[GATHER_AHEAD], xd_hbm, xbuf.at[ahead], sems.at[ahead], tm, PAYLOAD_ROWS)

    @pl.when(i < n_valid)
    def _():
        buf = xbuf.at[slot]
        _wait_row_gather(xd_hbm, buf, sems.at[slot], tm, PAYLOAD_ROWS)
        for j in range(ROW_TILES):
            x_scr[:, j * LANES:(j + 1) * LANES] = buf[pl.ds(j, tm, stride=PAYLOAD_ROWS), :].astype(BF16)
        gates = buf[pl.ds(ROW_TILES, tm, stride=PAYLOAD_ROWS), :]
        x = x_scr[...]
        hu = jnp.dot(x, w1a_ref[0], preferred_element_type=F32)
        ha = jax.nn.silu(hu[:, :D_EXPERT]) * hu[:, D_EXPERT:] * gates[:, 0:1]
        hu = jnp.dot(x, w1b_ref[0], preferred_element_type=F32)
        hb = jax.nn.silu(hu[:, :D_EXPERT]) * hu[:, D_EXPERT:] * gates[:, 1:2]
        y = (jnp.dot(ha.astype(BF16), w2a_ref[0], preferred_element_type=F32)
             + jnp.dot(hb.astype(BF16), w2b_ref[0], preferred_element_type=F32))
        for j in range(ROW_TILES):
            ys_ref[pl.ds(j, tm, stride=ROW_TILES), :] = y[:, j * LANES:(j + 1) * LANES]

    @pl.when(i >= n_valid)
    def _():
        ys_ref[...] = jnp.zeros_like(ys_ref)


def moe_experts_routed(xd, inv, tile_ea, tile_eb, n_valid, w_ein, w_eout):
    n_tiles = tile_ea.shape[0]
    tm = MOE_TM
    inv3 = inv.reshape(n_tiles, 1, tm)
    smem = dict(memory_space=pltpu.SMEM)
    grid_spec = pltpu.PrefetchScalarGridSpec(
        num_scalar_prefetch=3,
        grid=(n_tiles,),
        in_specs=[
            pl.BlockSpec((1, 1, tm), lambda i, ea, eb, nv, a=a: (jnp.minimum(i + a, n_tiles - 1), 0, 0), **smem)
            for a in range(GATHER_AHEAD + 1)] + [
            pl.BlockSpec(memory_space=pl.ANY),
            pl.BlockSpec((1, D_MODEL, 2 * D_EXPERT), lambda i, ea, eb, nv: (ea[i], 0, 0)),
            pl.BlockSpec((1, D_MODEL, 2 * D_EXPERT), lambda i, ea, eb, nv: (eb[i], 0, 0)),
            pl.BlockSpec((1, D_EXPERT, D_MODEL), lambda i, ea, eb, nv: (ea[i], 0, 0)),
            pl.BlockSpec((1, D_EXPERT, D_MODEL), lambda i, ea, eb, nv: (eb[i], 0, 0))],
        out_specs=pl.BlockSpec((tm * ROW_TILES, LANES), lambda i, ea, eb, nv: (i, 0)),
        scratch_shapes=[pltpu.VMEM((GATHER_AHEAD + 1, tm * PAYLOAD_ROWS, LANES), F32),
                        pltpu.SemaphoreType.DMA((GATHER_AHEAD + 1,)),
                        pltpu.VMEM((tm, D_MODEL), BF16)])
    return pl.pallas_call(
        _moe_routed_kernel,
        grid_spec=grid_spec,
        out_shape=jax.ShapeDtypeStruct((n_tiles * tm * ROW_TILES, LANES), F32),
        compiler_params=_params("arbitrary"),
        name="moe_experts_routed",
    )(tile_ea, tile_eb, n_valid, *([inv3] * (GATHER_AHEAD + 1)), xd, w_ein, w_ein, w_eout, w_eout)


def _combine_kernel(*refs):
    dest_refs = refs[:GATHER_AHEAD + 1]
    ys_hbm, res_ref, gf_ref, y_ref, buf, sems = refs[GATHER_AHEAD + 1:]
    n_buf = GATHER_AHEAD + 1
    i = pl.program_id(0)
    n_steps = pl.num_programs(0)
    slot = i % n_buf
    tm = res_ref.shape[0]

    @pl.when(i == 0)
    def _():
        for a in range(GATHER_AHEAD):
            @pl.when(a < n_steps)
            def _():
                _start_row_gather(dest_refs[a], ys_hbm, buf.at[a], sems.at[a], tm, ROW_TILES)

    @pl.when(i + GATHER_AHEAD < n_steps)
    def _():
        ahead = (i + GATHER_AHEAD) % n_buf
        _start_row_gather(dest_refs[GATHER_AHEAD], ys_hbm, buf.at[ahead], sems.at[ahead], tm, ROW_TILES)

    cur = buf.at[slot]
    _wait_row_gather(ys_hbm, cur, sems.at[slot], tm, ROW_TILES)
    for j in range(ROW_TILES):
        js = slice(j * LANES, (j + 1) * LANES)
        y_ref[:, js] = res_ref[:, js] + cur[pl.ds(j, tm, stride=ROW_TILES), :]
    y_ref[...] = _rms_rows(y_ref[...], gf_ref[...])


def moe_combine_final(ys, dest, res, g_final, tm=256):
    m = res.shape[0]
    nt = m // tm
    dest3 = dest.reshape(nt, 1, tm)
    smem = dict(memory_space=pltpu.SMEM)
    return pl.pallas_call(
        _combine_kernel,
        grid=(nt,),
        in_specs=[pl.BlockSpec((1, 1, tm), lambda i, a=a: (jnp.minimum(i + a, nt - 1), 0, 0), **smem)
                  for a in range(GATHER_AHEAD + 1)] + [
                  pl.BlockSpec(memory_space=pl.ANY),
                  pl.BlockSpec((tm, D_MODEL), lambda i: (i, 0)),
                  pl.BlockSpec((1, D_MODEL), lambda i: (0, 0))],
        out_specs=pl.BlockSpec((tm, D_MODEL), lambda i: (i, 0)),
        out_shape=jax.ShapeDtypeStruct((m, D_MODEL), F32),
        scratch_shapes=[pltpu.VMEM((GATHER_AHEAD + 1, tm * ROW_TILES, LANES), F32),
                        pltpu.SemaphoreType.DMA((GATHER_AHEAD + 1,))],
        compiler_params=_params("arbitrary"),
        name="moe_combine_final",
    )(*([dest3] * (GATHER_AHEAD + 1)), ys, res, g_final.reshape(1, D_MODEL))


def moe_routed(x, g_ffn, w_rg, b_rg, w_re, b_re, w_ein, w_eout, g_final):
    m = x.shape[0]
    tm = MOE_TM
    n_tiles = m // tm + N_BUCKETS
    xd, meta, cnt = moe_router_routed(x, g_ffn, w_rg, b_rg, w_re, b_re)
    bucket = meta[:, 0].astype(jnp.int32)
    rank = meta[:, 1].astype(jnp.int32)
    counts = cnt[0, :N_BUCKETS].astype(jnp.int32)
    tiles_per_bucket = (counts + tm - 1) // tm
    tile_end = jnp.cumsum(tiles_per_bucket)
    tile_start = tile_end - tiles_per_bucket
    n_valid = tile_end[-1]
    tile_id = jnp.arange(n_tiles, dtype=jnp.int32)
    tile_bucket = jnp.searchsorted(tile_end, jnp.minimum(tile_id, n_valid - 1), side="right").astype(jnp.int32)
    group, pair = tile_bucket // PAIRS_PER_GROUP, tile_bucket % PAIRS_PER_GROUP
    tile_ea = group * EXPERTS_PER_GROUP + jnp.take(jnp.array(PAIR_LO, jnp.int32), pair)
    tile_eb = group * EXPERTS_PER_GROUP + jnp.take(jnp.array(PAIR_HI, jnp.int32), pair)
    dest, inv = slot_maps(bucket, rank, tile_start * tm, n_tiles * tm)
    ys = moe_experts_routed(xd, inv, tile_ea, tile_eb, n_valid.reshape(1), w_ein, w_eout)
    return moe_combine_final(ys, dest, x, g_final)


def _router_kernel(x_ref, g_ref, w_ref, b_ref, xn_ref, gate_ref):
    xn = _rms_rows(x_ref[...], g_ref[...])
    xn_ref[...] = xn.astype(BF16)
    logits = _router_logits(xn, w_ref, b_ref)
    _, i1, i2, p1, p2 = _route(logits)
    lanef = lax.broadcasted_iota(jnp.int32, logits.shape, 1).astype(F32)
    gate_ref[...] = jnp.where(lanef == i1, p1, 0.0) + jnp.where(lanef == i2, p2, 0.0)


def moe_router(x, g, w_rg, b_rg, w_re, b_re, tm):
    m = x.shape[0]
    tm = min(tm, m)
    w, b = _router_weights(w_rg, b_rg, w_re, b_re)
    row = lambda i: (i, 0)
    const = lambda i: (0, 0)
    return pl.pallas_call(
        _router_kernel,
        grid=(m // tm,),
        in_specs=[pl.BlockSpec((tm, D_MODEL), row), pl.BlockSpec((1, D_MODEL), const),
                  pl.BlockSpec((2, D_MODEL, LANES), lambda i: (0, 0, 0)), pl.BlockSpec((1, LANES), const)],
        out_specs=[pl.BlockSpec((tm, D_MODEL), row), pl.BlockSpec((tm, LANES), row)],
        out_shape=[jax.ShapeDtypeStruct((m, D_MODEL), BF16), jax.ShapeDtypeStruct((m, LANES), F32)],
        compiler_params=_params("parallel"),
        name="moe_router",
    )(x, g.reshape(1, D_MODEL), w, b)


def _moe_dense_kernel(xn_ref, gate_ref, w1_ref, w2_ref, res_ref, gf_ref, y_ref):
    e = pl.program_id(1)

    @pl.when(e == 0)
    def _():
        y_ref[...] = res_ref[...]

    hu = jnp.dot(xn_ref[...], w1_ref[0], preferred_element_type=F32)
    h = jax.nn.silu(hu[:, :D_EXPERT]) * hu[:, D_EXPERT:]
    lane = lax.broadcasted_iota(jnp.int32, gate_ref.shape, 1)
    gate = jnp.sum(jnp.where(lane == e, gate_ref[...], 0.0), axis=-1, keepdims=True)
    h = h * gate
    y_ref[...] += jnp.dot(h.astype(BF16), w2_ref[0], preferred_element_type=F32)

    @pl.when(e == N_EXPERTS - 1)
    def _():
        y_ref[...] = _rms_rows(y_ref[...], gf_ref[...])


def moe_experts_final(xn, gate, w_ein, w_eout, res, g_final, tm):
    m = xn.shape[0]
    tm = min(tm, m)
    row = lambda i, e: (i, 0)
    return pl.pallas_call(
        _moe_dense_kernel,
        grid=(m // tm, N_EXPERTS),
        in_specs=[pl.BlockSpec((tm, D_MODEL), row), pl.BlockSpec((tm, LANES), row),
                  pl.BlockSpec((1, D_MODEL, 2 * D_EXPERT), lambda i, e: (e, 0, 0)),
                  pl.BlockSpec((1, D_EXPERT, D_MODEL), lambda i, e: (e, 0, 0)),
                  pl.BlockSpec((tm, D_MODEL), row),
                  pl.BlockSpec((1, D_MODEL), lambda i, e: (0, 0))],
        out_specs=pl.BlockSpec((tm, D_MODEL), row),
        out_shape=jax.ShapeDtypeStruct((m, D_MODEL), F32),
        compiler_params=_params("parallel", "arbitrary"),
        name="moe_experts_final",
    )(xn, gate, w_ein, w_eout, res, g_final.reshape(1, D_MODEL))


def _window_decode_kernel(z_ref, c1_ref, c2_ref, c3_ref, o_ref):
    scale = HEAD_DIM ** -0.5
    os, ms, ds = [], [], []
    for g, c_ref in enumerate((c1_ref, c2_ref, c3_ref)):
        q = z_ref[0, g:g + 1]
        k_new = z_ref[0, 3 + g:4 + g]
        v_new = z_ref[0, 6 + g:7 + g]
        k = c_ref[0, 0, :, 0, 0]
        v = c_ref[0, 0, :, 0, 1]
        s = jnp.sum(k * q, axis=-1, keepdims=True) * scale
        s_new = jnp.sum(k_new * q, axis=-1, keepdims=True) * scale
        m = jnp.maximum(jnp.max(s, axis=0, keepdims=True), s_new)
        p = jnp.exp(s - m)
        p_new = jnp.exp(s_new - m)
        den = jnp.sum(p, axis=0, keepdims=True) + p_new
        os.append(jnp.sum(p * v, axis=0, keepdims=True) + p_new * v_new)
        ms.append(m)
        ds.append(den)
    m_all = functools.reduce(jnp.maximum, ms)
    ws = [jnp.exp(m - m_all) for m in ms]
    num = sum(w * o for w, o in zip(ws, os))
    den = sum(w * d for w, d in zip(ws, ds))
    o_ref[0] = (num / den).astype(o_ref.dtype)


def window_decode(z4, caches):
    b = z4.shape[0]
    views, specs = [], []
    for cache, (win, dil) in zip(caches, ATT_GROUPS):
        n = cache.shape[2]
        assert n == win and n // dil == BAND
        views.append(cache.reshape(1, b, BAND, dil, 2, HEADS, HEAD_DIM))
        specs.append(pl.BlockSpec((1, 1, BAND, 1, 2, HEADS, HEAD_DIM), lambda i: (0, i, 0, 0, 0, 0, 0)))
    return pl.pallas_call(
        _window_decode_kernel,
        grid=(b,),
        in_specs=[pl.BlockSpec((1,) + z4.shape[1:], lambda i: (i, 0, 0, 0))] + specs,
        out_specs=pl.BlockSpec((1, 1, HEADS, HEAD_DIM), lambda i: (i, 0, 0, 0)),
        out_shape=jax.ShapeDtypeStruct((b, 1, HEADS, HEAD_DIM), BF16),
        compiler_params=_params("parallel"),
        name="window_decode",
    )(z4, *views)


SHIFT_ROWS = 64


def _shift_kernel(z_ref, c1_ref, c2_ref, c3_ref, o1_ref, o2_ref, o3_ref):
    for g, (c_ref, o_ref) in enumerate(((c1_ref, o1_ref), (c2_ref, o2_ref), (c3_ref, o3_ref))):
        n = c_ref.shape[2]
        full, rem = divmod(n - 1, SHIFT_ROWS)

        def move(j, carry, c_ref=c_ref, o_ref=o_ref):
            o_ref[0, 0, pl.ds(j * SHIFT_ROWS, SHIFT_ROWS)] = c_ref[0, 0, pl.ds(j * SHIFT_ROWS + 1, SHIFT_ROWS)]
            return carry

        lax.fori_loop(0, full, move, 0)
        if rem:
            o_ref[0, 0, pl.ds(full * SHIFT_ROWS, rem)] = c_ref[0, 0, pl.ds(full * SHIFT_ROWS + 1, rem)]
        o_ref[0, 0, n - 1, 0] = z_ref[0, 3 + g]
        o_ref[0, 0, n - 1, 1] = z_ref[0, 6 + g]


def shift_caches(z4, caches):
    b = z4.shape[0]
    specs = [pl.BlockSpec((1, 1) + c.shape[2:], lambda i: (0, i, 0, 0, 0, 0)) for c in caches]
    return pl.pallas_call(
        _shift_kernel,
        grid=(b,),
        in_specs=[pl.BlockSpec((1,) + z4.shape[1:], lambda i: (i, 0, 0, 0))] + specs,
        out_specs=specs,
        out_shape=[jax.ShapeDtypeStruct(c.shape, c.dtype) for c in caches],
        compiler_params=_params("parallel"),
        name="shift_caches",
    )(z4, *caches)


def _conv_step_kernel(a_ref, b_ref, s_ref, w_ref, cb_ref, lg_ref, lb_ref, c_ref, so_ref):
    hist = CONV_K - 1
    u = a_ref[...] * jax.nn.sigmoid(b_ref[...])
    y = (jnp.sum(s_ref[0] * w_ref[0:hist, :], axis=1, keepdims=True)
         + u * w_ref[hist:hist + 1, :] + cb_ref[...])
    c_ref[...] = _layernorm_silu(y, lg_ref[...], lb_ref[...]).astype(c_ref.dtype)
    so_ref[0, :, pl.ds(0, hist - 1), :] = s_ref[0, :, pl.ds(1, hist - 1), :]
    so_ref[0, :, pl.ds(hist - 1, 1), :] = u


def conv_step(a, b, state, conv_w, conv_b, ln_g, ln_b):
    bsz = a.shape[0]
    return pl.pallas_call(
        _conv_step_kernel,
        out_shape=[jax.ShapeDtypeStruct((bsz, 1, D_CONV), BF16), jax.ShapeDtypeStruct(state.shape, F32)],
        compiler_params=pltpu.CompilerParams(vmem_limit_bytes=VMEM_LIMIT),
        name="conv_step",
    )(a, b, state, conv_w, conv_b.reshape(1, D_CONV), ln_g.reshape(1, D_CONV), ln_b.reshape(1, D_CONV))


def _trunk_tail(x1, xo_in, w_xo, norm_ffn_g, w_rg, b_rg, w_re, b_re, w_ein, w_eout, norm_final_g, bm, routed):
    x2 = matmul_residual(xo_in, w_xo, x1, min(bm, 512), bn=D_MODEL)
    if routed:
        return moe_routed(x2, norm_ffn_g, w_rg, b_rg, w_re, b_re, w_ein, w_eout, norm_final_g)
    xn3, gate = moe_router(x2, norm_ffn_g, w_rg, b_rg, w_re, b_re, bm)
    return moe_experts_final(xn3, gate, w_ein, w_eout, x2, norm_final_g, bm)


def kernel(x_prompt, x_sample, mem_prompt, cache_kv_g1, cache_kv_g2, cache_kv_g3, state_conv, cache_mem_kv,
           norm_mix_g, w_in, conv_w, conv_b, conv_ln_g, conv_ln_b, w_proj_a, w_proj_b, w_out,
           norm_xattn_g, norm_mem_g, w_xq, w_xkv, w_xo, norm_ffn_g,
           w_router_group, b_router_group, w_router_expert, b_router_expert, w_expert_in, w_expert_out,
           norm_final_g):
    depth = norm_mix_g.shape[0]
    assert depth == 1, "single-layer trunk"
    batch, seq, _ = x_prompt.shape
    dec_b, dec_t, _ = x_sample.shape
    assert dec_t == 1
    (g_mix, w_in, conv_w, conv_b, ln_g, ln_b, w_pa, w_pb, w_o, g_x, g_mem, w_xq, w_xkv, w_xo, g_ffn,
     w_rg, b_rg, w_re, b_re, w_ein, w_eout) = [t[0] for t in (
         norm_mix_g, w_in, conv_w, conv_b, conv_ln_g, conv_ln_b, w_proj_a, w_proj_b, w_out, norm_xattn_g,
         norm_mem_g, w_xq, w_xkv, w_xo, norm_ffn_g, w_router_group, b_router_group, w_router_expert,
         b_router_expert, w_expert_in, w_expert_out)]
    w_pa16, w_pb16, w_o16 = w_pa.astype(BF16), w_pb.astype(BF16), w_o.astype(BF16)
    w_ein, w_eout = w_ein.astype(BF16), w_eout.astype(BF16)
    w_za, w_gt = w_in[:, :ZA_W].astype(BF16), w_in[:, ZA_W:].astype(BF16)
    w_xq, w_xkv, w_xo = w_xq.astype(BF16), w_xkv.astype(BF16), w_xo.astype(BF16)
    caches = (cache_kv_g1, cache_kv_g2, cache_kv_g3)

    m_p = batch * seq
    xp = x_prompt.reshape(m_p, D_MODEL)
    za, gates = in_proj(xp, g_mix, w_za, w_gt, bm=1024)
    o_a = band_attention(za, batch, seq)
    x1, conv_prompt = conv_merge_branches(o_a, za, gates, xp, batch, seq, conv_w, conv_b, ln_g, ln_b,
                                          w_pa16, w_pb16, w_o16)
    mkv = norm_matmul(mem_prompt.reshape(batch * N_MEM, D_MODEL), g_mem, w_xkv, F32, bm=1024)
    q = norm_matmul(x1, g_x, w_xq, BF16, bm=512, bn=D_MODEL)
    xo_in = cross_attention_prompt(q, mkv, batch, seq)
    y_prompt = _trunk_tail(x1, xo_in, w_xo, g_ffn, w_rg, b_rg, w_re, b_re, w_ein, w_eout, norm_final_g,
                           bm=1024, routed=True)

    kv_prompt = window_tails(za, batch, seq)
    conv_prompt = conv_prompt[None]
    mem_kv_prompt = mkv.reshape(1, batch, N_MEM, 2, X_HEADS, X_HEAD_DIM)

    xs = x_sample.reshape(dec_b, D_MODEL)
    zs, gates_s = in_proj(xs, g_mix, w_za, w_gt, bm=dec_b)
    z4 = zs.reshape(dec_b, ZA_W // HEAD_DIM // HEADS, HEADS, HEAD_DIM)
    o_as = window_decode(z4, caches).reshape(dec_b, GROUP_W)
    kv_sample = shift_caches(z4, caches)
    a_s = zs[:, 3 * ATT_WIDTH:3 * ATT_WIDTH + D_CONV].reshape(dec_b, 1, D_CONV)
    b_s = zs[:, 3 * ATT_WIDTH + D_CONV:].reshape(dec_b, 1, D_CONV)
    c_s, conv_sample = conv_step(a_s, b_s, state_conv, conv_w, conv_b, ln_g, ln_b)
    x1s = merge_branches(o_as, c_s.reshape(dec_b, D_CONV), gates_s, xs, w_pa16, w_pb16, w_o16, tm=dec_b)
    q_s = norm_matmul(x1s, g_x, w_xq, BF16, bm=dec_b)
    xo_s = cross_attention_decode(q_s, cache_mem_kv)
    y_sample = _trunk_tail(x1s, xo_s, w_xo, g_ffn, w_rg, b_rg, w_re, b_re, w_ein, w_eout, norm_final_g,
                           bm=dec_b, routed=False)

    return (y_prompt.reshape(batch, seq, D_MODEL), y_sample.reshape(dec_b, 1, D_MODEL),
            kv_prompt[0], kv_prompt[1], kv_prompt[2], conv_prompt, mem_kv_prompt,
            kv_sample[0], kv_sample[1], kv_sample[2], conv_sample)
```
